```python
import jax, jax.numpy as jnp
from jax import lax
import numpy as np

D_MODEL = 1024
BATCH = 8
SEQ = 2048
DEPTH = 1
DEC_BATCH = 128
DEC_SEQ = 4
PAST_LEN = 16384
PAGE_SIZE = 128

D_A = D_MODEL // 2
D_B = D_MODEL // 2
CONV_A_WIDTH = 3
CONV_B_WIDTH = 31
N_GROUPS = 4
EXPERTS_PER_GROUP = 4
N_EXPERTS = N_GROUPS * EXPERTS_PER_GROUP
TOP_K = 2
D_EXPERT = 256
N_MOD = 6
EPS = 1e-6
SPLIT_SIZES = (D_A, D_A, D_A, D_B, D_B, D_MODEL, D_MODEL)
SPLIT_IDX = tuple(int(s) for s in np.cumsum(SPLIT_SIZES)[:-1])
D_IN = int(sum(SPLIT_SIZES))

kernel_name = "hybrid_shortconv_conformer_hmoe_step"


def _rmsnorm(x, g):
    xf = x.astype(jnp.float32)
    xf = xf * lax.rsqrt(jnp.mean(xf * xf, axis=-1, keepdims=True) + EPS)
    return (xf * g.astype(jnp.float32)).astype(x.dtype)


def _layernorm(x, g, b):
    xf = x.astype(jnp.float32)
    mu = jnp.mean(xf, axis=-1, keepdims=True)
    var = jnp.mean(jnp.square(xf - mu), axis=-1, keepdims=True)
    y = (xf - mu) * lax.rsqrt(var + EPS) * g.astype(jnp.float32) + b.astype(jnp.float32)
    return y.astype(x.dtype)


def _causal_dwconv(u, buf, w):
    k = w.shape[0]
    full = jnp.concatenate([buf.astype(u.dtype), u], axis=1)
    out = lax.conv_general_dilated(
        full, w[:, None, :].astype(u.dtype), window_strides=(1,), padding='VALID',
        dimension_numbers=('NWC', 'WIO', 'NWC'), feature_group_count=u.shape[-1])
    return out, full[:, full.shape[1] - (k - 1):]


def _hier_moe(h, w_group, b_group, w_expert, b_expert, w1, w3, w2):
    shp = h.shape
    t = h.reshape(-1, shp[-1])
    gl = (t @ w_group + b_group).astype(jnp.float32)
    gp = jax.nn.softmax(gl, axis=-1)
    g_idx = jnp.argmax(gl, axis=-1)
    p_g = jnp.take_along_axis(gp, g_idx[:, None], axis=-1)
    el = (t @ w_expert + b_expert).astype(jnp.float32).reshape(-1, N_GROUPS, EXPERTS_PER_GROUP)
    el = jnp.take_along_axis(el, g_idx[:, None, None], axis=1)[:, 0]
    top_v, top_i = lax.top_k(el, TOP_K)
    wts = jax.nn.softmax(top_v, axis=-1) * p_g
    ids = g_idx[:, None] * EXPERTS_PER_GROUP + top_i
    combine = jnp.sum(jax.nn.one_hot(ids, N_EXPERTS, dtype=jnp.float32) * wts[..., None], axis=1)
    a = jax.nn.silu(jnp.einsum('nd,edf->nef', t, w1)) * jnp.einsum('nd,edf->nef', t, w3)
    a = a * combine.astype(t.dtype)[..., None]
    y = jnp.einsum('nef,efd->nd', a, w2)
    return y.reshape(shp)


def _layer(x, c, buf_a, buf_b, w_ada, b_ada, norm1_g, norm2_g, w_in, conv_a_w, w_out_a,
           conv_b_w, conv_b_bias, ln_b_g, ln_b_b, w_out_b, w_o,
           w_group, b_group, w_expert, b_expert, w1, w3, w2):
    mod = jax.nn.silu(c) @ w_ada + b_ada
    sh1, sc1, gt1, sh2, sc2, gt2 = jnp.split(mod[:, None, :], N_MOD, axis=-1)
    h = _rmsnorm(x, norm1_g) * (1 + sc1) + sh1
    proj = h @ w_in
    b_a, c_a, h_a, v_b, g_b, mg_a, mg_b = jnp.split(proj, SPLIT_IDX, axis=-1)
    conv_a, new_a = _causal_dwconv(c_a * h_a, buf_a, conv_a_w)
    y_a = (b_a * conv_a) @ w_out_a
    conv_b, new_b = _causal_dwconv(v_b * jax.nn.sigmoid(g_b), buf_b, conv_b_w)
    y_b = jax.nn.silu(_layernorm(conv_b + conv_b_bias, ln_b_g, ln_b_b)) @ w_out_b
    merged = jax.nn.sigmoid(mg_a) * y_a + jax.nn.sigmoid(mg_b) * y_b
    x = x + gt1 * (merged @ w_o)
    h2 = _rmsnorm(x, norm2_g) * (1 + sc2) + sh2
    x = x + gt2 * _hier_moe(h2, w_group, b_group, w_expert, b_expert, w1, w3, w2)
    return x, new_a, new_b


def setup_inputs(seed: int = 0) -> dict:
    key = jax.random.key(seed)
    ks = jax.random.split(key, 32)
    f = jnp.float32
    nrm = lambda k, shape, s: jax.random.normal(k, shape, f) * s
    L = DEPTH
    return {
        "x_prompt": nrm(ks[0], (BATCH, SEQ, D_MODEL), 1.0),
        "x_sample": nrm(ks[1], (DEC_BATCH, DEC_SEQ, D_MODEL), 1.0),
        "c_prompt": nrm(ks[2], (BATCH, D_MODEL), 1.0),
        "c_sample": nrm(ks[3], (DEC_BATCH, D_MODEL), 1.0),
        "state_conv_a": nrm(ks[4], (L, DEC_BATCH, CONV_A_WIDTH - 1, D_A), 1.0),
        "state_conv_b": nrm(ks[5], (L, DEC_BATCH, CONV_B_WIDTH - 1, D_B), 1.0),
        "w_ada": nrm(ks[6], (L, D_MODEL, N_MOD * D_MODEL), 0.3 * D_MODEL ** -0.5),
        "b_ada": nrm(ks[7], (L, N_MOD * D_MODEL), 0.02),
        "norm1_g": 1.0 + nrm(ks[8], (L, D_MODEL), 0.02),
        "norm2_g": 1.0 + nrm(ks[9], (L, D_MODEL), 0.02),
        "w_in": nrm(ks[10], (L, D_MODEL, D_IN), D_MODEL ** -0.5),
        "conv_a_w": nrm(ks[11], (L, CONV_A_WIDTH, D_A), CONV_A_WIDTH ** -0.5),
        "w_out_a": nrm(ks[12], (L, D_A, D_MODEL), D_A ** -0.5),
        "conv_b_w": nrm(ks[13], (L, CONV_B_WIDTH, D_B), CONV_B_WIDTH ** -0.5),
        "conv_b_bias": nrm(ks[14], (L, D_B), 0.02),
        "ln_b_g": 1.0 + nrm(ks[15], (L, D_B), 0.02),
        "ln_b_b": nrm(ks[16], (L, D_B), 0.02),
        "w_out_b": nrm(ks[17], (L, D_B, D_MODEL), D_B ** -0.5),
        "w_o": nrm(ks[18], (L, D_MODEL, D_MODEL), D_MODEL ** -0.5),
        "w_group": nrm(ks[19], (L, D_MODEL, N_GROUPS), D_MODEL ** -0.5),
        "b_group": nrm(ks[20], (L, N_GROUPS), 0.01),
        "w_expert": nrm(ks[21], (L, D_MODEL, N_EXPERTS), D_MODEL ** -0.5),
        "b_expert": nrm(ks[22], (L, N_EXPERTS), 0.01),
        "w1": nrm(ks[23], (L, N_EXPERTS, D_MODEL, D_EXPERT), D_MODEL ** -0.5),
        "w3": nrm(ks[24], (L, N_EXPERTS, D_MODEL, D_EXPERT), D_MODEL ** -0.5),
        "w2": nrm(ks[25], (L, N_EXPERTS, D_EXPERT, D_MODEL), D_EXPERT ** -0.5),
        "final_norm_g": 1.0 + nrm(ks[26], (D_MODEL,), 0.02),
    }


def reference(x_prompt, x_sample, c_prompt, c_sample, state_conv_a, state_conv_b,
              w_ada, b_ada, norm1_g, norm2_g, w_in, conv_a_w, w_out_a,
              conv_b_w, conv_b_bias, ln_b_g, ln_b_b, w_out_b, w_o,
              w_group, b_group, w_expert, b_expert, w1, w3, w2, final_norm_g):
    n_p = x_prompt.shape[0]
    xp, xs = x_prompt, x_sample
    pa, pb, sa, sb = [], [], [], []
    for l in range(DEPTH):
        params = (w_ada[l], b_ada[l], norm1_g[l], norm2_g[l], w_in[l], conv_a_w[l], w_out_a[l],
                  conv_b_w[l], conv_b_bias[l], ln_b_g[l], ln_b_b[l], w_out_b[l], w_o[l],
                  w_group[l], b_group[l], w_expert[l], b_expert[l], w1[l], w3[l], w2[l])
        buf_a0 = jnp.zeros((n_p, CONV_A_WIDTH - 1, D_A), xp.dtype)
        buf_b0 = jnp.zeros((n_p, CONV_B_WIDTH - 1, D_B), xp.dtype)
        xp, na_p, nb_p = _layer(xp, c_prompt, buf_a0, buf_b0, *params)
        xs, na_s, nb_s = _layer(xs, c_sample, state_conv_a[l], state_conv_b[l], *params)
        pa.append(na_p); pb.append(nb_p); sa.append(na_s); sb.append(nb_s)
    y_prompt = _rmsnorm(xp, final_norm_g)
    y_sample = _rmsnorm(xs, final_norm_g)
    return (y_prompt, y_sample, jnp.stack(pa), jnp.stack(pb), jnp.stack(sa), jnp.stack(sb))
```

```python
import functools

import jax
import jax.numpy as jnp
from jax import lax
from jax.experimental import pallas as pl
from jax.experimental.pallas import tpu as pltpu

D_MODEL = 1024
D_A = 512
D_B = 512
CONV_A_WIDTH = 3
CONV_B_WIDTH = 31
N_GROUPS = 4
EXPERTS_PER_GROUP = 4
N_EXPERTS = 16
N_MOD = 6
EPS = 1e-6
D_IN = 3 * D_A + 2 * D_B + 2 * D_MODEL

LANES = 128
SUBLANES = 8
ROUTER_LANES = LANES
CARRY_A = SUBLANES
CARRY_B = 4 * SUBLANES
SEQ_TILE = 512
CONV_ROWS = 32
VMEM_LIMIT = 52 * 1024 * 1024

BF16 = jnp.bfloat16
F32 = jnp.float32


def _dot(a, b):
    return jnp.dot(a, b, preferred_element_type=F32)


def _rms(x, g):
    return x * lax.rsqrt(jnp.mean(x * x, axis=-1, keepdims=True) + EPS) * g


def _affine(v, scale, shift=None):
    r = scale.shape[0]
    if r != 1 and r != v.shape[0]:
        v3 = v.reshape(v.shape[0] // r, r, v.shape[-1])
        shift3 = None if shift is None else shift[None]
        return _affine(v3, scale[None], shift3).reshape(v.shape)
    out = v * scale
    return out if shift is None else out + shift


def _route(logits):
    lane_i = lax.broadcasted_iota(jnp.int32, logits.shape, 1)
    lane = lane_i.astype(F32)
    lane_group = lax.shift_right_logical(lane_i, 2).astype(F32)
    neg = -jnp.inf
    is_g = (lane_i >= N_EXPERTS) & (lane_i < N_EXPERTS + N_GROUPS)
    gl = jnp.where(is_g, logits, neg)
    gmax = jnp.max(gl, axis=-1, keepdims=True)
    g_lane = jnp.min(jnp.where(gl == gmax, lane, float(LANES)), axis=-1, keepdims=True)
    p_g = 1.0 / jnp.sum(jnp.exp(gl - gmax), axis=-1, keepdims=True)
    g_idx = g_lane - float(N_EXPERTS)
    in_grp = (lane_i < N_EXPERTS) & (lane_group == g_idx)
    el = jnp.where(in_grp, logits, neg)
    v1 = jnp.max(el, axis=-1, keepdims=True)
    i1 = jnp.min(jnp.where(el == v1, lane, float(LANES)), axis=-1, keepdims=True)
    el2 = jnp.where(lane == i1, neg, el)
    v2 = jnp.max(el2, axis=-1, keepdims=True)
    i2 = jnp.min(jnp.where(el2 == v2, lane, float(LANES)), axis=-1, keepdims=True)
    e21 = jnp.exp(v2 - v1)
    wt1 = p_g / (1.0 + e21)
    wt2 = p_g * e21 / (1.0 + e21)
    return jnp.where(lane == i1, wt1, jnp.where(lane == i2, wt2, 0.0))


def _ln_silu(v, bias, g, b):
    v = v + bias
    mu = jnp.mean(v, axis=-1, keepdims=True)
    d = v - mu
    var = jnp.mean(d * d, axis=-1, keepdims=True)
    y = d * lax.rsqrt(var + EPS) * g + b
    return y * jax.nn.sigmoid(y)


def _mixer_tail(x, hb, mod, b_a_conv, yb_in, g2, w_in_ref, woa_ref, wob_ref, wo_ref, wr_ref, br_ref,
                x1_ref, h2_ref, comb_ref):
    y_a = _dot(b_a_conv.astype(BF16), woa_ref[...])
    y_b = _dot(yb_in, wob_ref[...])
    o = 3 * D_A + 2 * D_B
    mg_a = _dot(hb, w_in_ref[:, o:o + D_MODEL])
    mg_b = _dot(hb, w_in_ref[:, o + D_MODEL:o + 2 * D_MODEL])
    merged = jax.nn.sigmoid(mg_a) * y_a + jax.nn.sigmoid(mg_b) * y_b
    gt1 = mod[:, 2 * D_MODEL:3 * D_MODEL]
    x1 = x + _affine(_dot(merged.astype(BF16), wo_ref[...]), gt1)
    sh2 = mod[:, 3 * D_MODEL:4 * D_MODEL]
    sc2 = mod[:, 4 * D_MODEL:5 * D_MODEL]
    h2 = _affine(_rms(x1, g2), 1.0 + sc2, sh2).astype(BF16)
    logits = _dot(h2, wr_ref[...]) + br_ref[...]
    x1_ref[...] = x1
    h2_ref[...] = h2
    comb_ref[...] = _route(logits)


def _mixer_prompt_kernel(x_ref, mod_ref, g1_ref, g2_ref, w_in_ref, caw_ref, woa_ref, cbw_ref, cbb_ref,
                         lng_ref, lnb_ref, wob_ref, wo_ref, wr_ref, br_ref,
                         x1_ref, h2_ref, comb_ref, na_ref, nb_ref, hist_a, hist_b, yb_in_ref):
    t = x_ref.shape[0]

    @pl.when(pl.program_id(1) == 0)
    def _():
        hist_a[0:CARRY_A, :] = jnp.zeros((CARRY_A, D_A), F32)
        hist_b[0:CARRY_B, :] = jnp.zeros((CARRY_B, D_B), F32)

    x = x_ref[...]
    mod = mod_ref[...]
    h = _affine(_rms(x, g1_ref[...]), 1.0 + mod[:, D_MODEL:2 * D_MODEL], mod[:, 0:D_MODEL])
    hb = h.astype(BF16)

    def proj(lo, width):
        return _dot(hb, w_in_ref[:, lo:lo + width])

    hist_a[CARRY_A:CARRY_A + t, :] = proj(D_A, D_A) * proj(2 * D_A, D_A)
    conv_a = jnp.zeros((t, D_A), F32)
    for k in range(CONV_A_WIDTH):
        o = CARRY_A - (CONV_A_WIDTH - 1) + k
        conv_a = conv_a + caw_ref[k:k + 1, :] * hist_a[o:o + t, :]
    b_a_conv = proj(0, D_A) * conv_a
    na_ref[...] = hist_a[CARRY_A + t - (CONV_A_WIDTH - 1):CARRY_A + t, :]

    hist_b[CARRY_B:CARRY_B + t, :] = proj(3 * D_A, D_B) * jax.nn.sigmoid(proj(3 * D_A + D_B, D_B))
    nb_ref[...] = hist_b[CARRY_B + t - (CONV_B_WIDTH - 1):CARRY_B + t, :]
    bias, lng, lnb = cbb_ref[...], lng_ref[...], lnb_ref[...]

    for r0 in range(0, t, CONV_ROWS):
        acc = jnp.zeros((CONV_ROWS, D_B), F32)
        for k in range(CONV_B_WIDTH):
            o = r0 + CARRY_B - (CONV_B_WIDTH - 1) + k
            acc = acc + cbw_ref[k:k + 1, :] * hist_b[o:o + CONV_ROWS, :]
        yb_in_ref[r0:r0 + CONV_ROWS, :] = _ln_silu(acc, bias, lng, lnb).astype(BF16)

    hist_a[0:CARRY_A, :] = hist_a[t:t + CARRY_A, :]
    hist_b[0:CARRY_B, :] = hist_b[t:t + CARRY_B, :]

    _mixer_tail(x, hb, mod, b_a_conv, yb_in_ref[...], g2_ref[...], w_in_ref, woa_ref, wob_ref, wo_ref,
                wr_ref, br_ref, x1_ref, h2_ref, comb_ref)


def _mixer_sample_kernel(x_ref, mod_ref, g1_ref, g2_ref, w_in_ref, caw_ref, woa_ref, cbw_ref, cbb_ref,
                         lng_ref, lnb_ref, wob_ref, wo_ref, wr_ref, br_ref, sa_ref, sb_ref,
                         x1_ref, h2_ref, comb_ref, na_ref, nb_ref, yb_in_ref):
    n_seq = mod_ref.shape[0]
    steps = x_ref.shape[0] // n_seq
    x = x_ref[...]
    mod = mod_ref[...]
    h = _affine(_rms(x, g1_ref[...]), 1.0 + mod[:, D_MODEL:2 * D_MODEL], mod[:, 0:D_MODEL])
    hb = h.astype(BF16)

    def proj(lo, width):
        return _dot(hb, w_in_ref[:, lo:lo + width])

    def slab(v, j):
        return v[j * n_seq:(j + 1) * n_seq, :]

    u_a = proj(D_A, D_A) * proj(2 * D_A, D_A)
    full_a = [sa_ref[j] for j in range(CONV_A_WIDTH - 1)] + [slab(u_a, j) for j in range(steps)]
    conv_a = []
    for s in range(steps):
        acc = jnp.zeros((n_seq, D_A), F32)
        for k in range(CONV_A_WIDTH):
            acc = acc + caw_ref[k:k + 1, :] * full_a[s + k]
        conv_a.append(acc)
    b_a_conv = proj(0, D_A) * jnp.concatenate(conv_a, axis=0)
    for j in range(CONV_A_WIDTH - 1):
        na_ref[j] = full_a[steps + j]

    u_b = proj(3 * D_A, D_B) * jax.nn.sigmoid(proj(3 * D_A + D_B, D_B))
    hist = CONV_B_WIDTH - 1

    def full_b(j):
        return sb_ref[j] if j < hist else slab(u_b, j - hist)

    bias, lng, lnb = cbb_ref[...], lng_ref[...], lnb_ref[...]
    for s in range(steps):
        acc = jnp.zeros((n_seq, D_B), F32)
        for k in range(CONV_B_WIDTH):
            acc = acc + cbw_ref[k:k + 1, :] * full_b(s + k)
        yb_in_ref[s * n_seq:(s + 1) * n_seq, :] = _ln_silu(acc, bias, lng, lnb).astype(BF16)
    for j in range(hist):
        nb_ref[j] = full_b(steps + j)

    _mixer_tail(x, hb, mod, b_a_conv, yb_in_ref[...], g2_ref[...], w_in_ref, woa_ref, wob_ref, wo_ref,
                wr_ref, br_ref, x1_ref, h2_ref, comb_ref)


def _moe_kernel(x1_ref, h2_ref, comb_ref, gt2_ref, w1_ref, w3_ref, w2_ref, gf_ref, y_ref):
    hb = h2_ref[...]
    comb = comb_ref[...]
    acc = jnp.zeros((hb.shape[0], D_MODEL), F32)
    for e in range(N_EXPERTS):
        g = _dot(hb, w1_ref[e])
        u = _dot(hb, w3_ref[e])
        a = g * jax.nn.sigmoid(g) * u * comb[:, e:e + 1]
        acc = acc + _dot(a.astype(BF16), w2_ref[e])
    x2 = x1_ref[...] + _affine(acc, gt2_ref[...])
    y_ref[...] = _rms(x2, gf_ref[...])


def _mod_kernel(c_ref, w_ref, b_ref, o_ref):
    c = c_ref[...]
    a = (c * jax.nn.sigmoid(c)).astype(BF16)
    o_ref[...] = _dot(a, w_ref[...].astype(BF16)) + b_ref[...]


def _const_spec(shape):
    nd = len(shape)
    return pl.BlockSpec(shape, lambda *_: (0,) * nd, pipeline_mode=pl.Buffered(1))


def _modulation(c_all, w_ada, b_ada):
    n, d = c_all.shape
    n_out = w_ada.shape[1]
    return pl.pallas_call(
        _mod_kernel,
        grid=(n_out // D_MODEL,),
        in_specs=[pl.BlockSpec((n, d), lambda j: (0, 0)),
                  pl.BlockSpec((d, D_MODEL), lambda j: (0, j)),
                  pl.BlockSpec((1, D_MODEL), lambda j: (0, j))],
        out_specs=pl.BlockSpec((n, D_MODEL), lambda j: (0, j)),
        out_shape=jax.ShapeDtypeStruct((n, n_out), F32),
        compiler_params=pltpu.CompilerParams(dimension_semantics=("arbitrary",),
                                             vmem_limit_bytes=VMEM_LIMIT),
        name="modulation",
    )(c_all, w_ada, b_ada)


def _mixer_weight_specs(p):
    names = ("g1", "g2", "w_in", "caw", "woa", "cbw", "cbb", "lng", "lnb", "wob", "wo", "wr", "br")
    return [p[k] for k in names], [_const_spec(p[k].shape) for k in names]


def _mixer_prompt(x, mod, p):
    b, s, d = x.shape
    t = SEQ_TILE
    weights, wspecs = _mixer_weight_specs(p)
    tile = lambda width: pl.BlockSpec((None, t, width), lambda i, j: (i, j, 0))
    per_seq = lambda rows, width: pl.BlockSpec((None, rows, width), lambda i, j: (i, 0, 0))
    return pl.pallas_call(
        _mixer_prompt_kernel,
        grid=(b, s // t),
        in_specs=[tile(d), per_seq(1, N_MOD * d)] + wspecs,
        out_specs=[tile(d), tile(d), tile(ROUTER_LANES),
                   per_seq(CONV_A_WIDTH - 1, D_A), per_seq(CONV_B_WIDTH - 1, D_B)],
        out_shape=[jax.ShapeDtypeStruct((b, s, d), F32),
                   jax.ShapeDtypeStruct((b, s, d), BF16),
                   jax.ShapeDtypeStruct((b, s, ROUTER_LANES), F32),
                   jax.ShapeDtypeStruct((b, CONV_A_WIDTH - 1, D_A), F32),
                   jax.ShapeDtypeStruct((b, CONV_B_WIDTH - 1, D_B), F32)],
        scratch_shapes=[pltpu.VMEM((CARRY_A + t, D_A), F32),
                        pltpu.VMEM((CARRY_B + t, D_B), F32),
                        pltpu.VMEM((t, D_B), BF16)],
        compiler_params=pltpu.CompilerParams(dimension_semantics=("arbitrary", "arbitrary"),
                                             vmem_limit_bytes=VMEM_LIMIT),
        name="mixer_prompt",
    )(x, mod.reshape(b, 1, N_MOD * d), *weights)


def _mixer_sample(x_tm, mod, sa_tm, sb_tm, p):
    n, d = x_tm.shape
    weights, wspecs = _mixer_weight_specs(p)
    full = lambda a: pl.BlockSpec(a.shape, lambda i: (0,) * a.ndim)
    return pl.pallas_call(
        _mixer_sample_kernel,
        grid=(1,),
        in_specs=[full(x_tm), full(mod)] + wspecs + [full(sa_tm), full(sb_tm)],
        out_specs=[pl.BlockSpec((n, d), lambda i: (0, 0)), pl.BlockSpec((n, d), lambda i: (0, 0)),
                   pl.BlockSpec((n, ROUTER_LANES), lambda i: (0, 0)), full(sa_tm), full(sb_tm)],
        out_shape=[jax.ShapeDtypeStruct((n, d), F32),
                   jax.ShapeDtypeStruct((n, d), BF16),
                   jax.ShapeDtypeStruct((n, ROUTER_LANES), F32),
                   jax.ShapeDtypeStruct(sa_tm.shape, F32),
                   jax.ShapeDtypeStruct(sb_tm.shape, F32)],
        scratch_shapes=[pltpu.VMEM((n, D_B), BF16)],
        compiler_params=pltpu.CompilerParams(dimension_semantics=("arbitrary",),
                                             vmem_limit_bytes=VMEM_LIMIT),
        name="mixer_sample",
    )(x_tm, mod, *weights, sa_tm, sb_tm)


def _moe(x1, h2, comb, gt2_src, gt2_spec, p, grid, tile):
    weights = [p["w1"], p["w3"], p["w2"], p["gf"]]
    return pl.pallas_call(
        _moe_kernel,
        grid=grid,
        in_specs=[tile(D_MODEL), tile(D_MODEL), tile(ROUTER_LANES), gt2_spec]
        + [_const_spec(w.shape) for w in weights],
        out_specs=tile(D_MODEL),
        out_shape=jax.ShapeDtypeStruct(x1.shape, F32),
        compiler_params=pltpu.CompilerParams(dimension_semantics=("arbitrary",) * len(grid),
                                             vmem_limit_bytes=VMEM_LIMIT),
        name="moe",
    )(x1, h2, comb, gt2_src, *weights)


def kernel(x_prompt, x_sample, c_prompt, c_sample, state_conv_a, state_conv_b, w_ada, b_ada, norm1_g,
           norm2_g, w_in, conv_a_w, w_out_a, conv_b_w, conv_b_bias, ln_b_g, ln_b_b, w_out_b, w_o,
           w_group, b_group, w_expert, b_expert, w1, w3, w2, final_norm_g):
    depth = w_ada.shape[0]
    assert depth == 1, "the MoE kernel fuses the final norm, so exactly one layer is supported"
    n_p, seq, d = x_prompt.shape
    n_s, steps, _ = x_sample.shape
    c_all = jnp.concatenate([c_prompt, c_sample], axis=0)
    xp = x_prompt
    xs = x_sample.transpose(1, 0, 2).reshape(steps * n_s, d)
    gf = final_norm_g.reshape(1, d)
    pa, pb, sa, sb = [], [], [], []
    for l in range(depth):
        pad = ROUTER_LANES - N_EXPERTS - N_GROUPS
        wr = jnp.pad(jnp.concatenate([w_expert[l], w_group[l]], axis=1), ((0, 0), (0, pad)))
        br = jnp.pad(jnp.concatenate([b_expert[l], b_group[l]]), (0, pad)).reshape(1, ROUTER_LANES)
        p = {
            "g1": norm1_g[l].reshape(1, d), "g2": norm2_g[l].reshape(1, d),
            "w_in": w_in[l].astype(BF16), "caw": conv_a_w[l], "woa": w_out_a[l].astype(BF16),
            "cbw": conv_b_w[l], "cbb": conv_b_bias[l].reshape(1, D_B),
            "lng": ln_b_g[l].reshape(1, D_B), "lnb": ln_b_b[l].reshape(1, D_B),
            "wob": w_out_b[l].astype(BF16), "wo": w_o[l].astype(BF16), "wr": wr.astype(BF16), "br": br,
            "w1": w1[l].astype(BF16), "w3": w3[l].astype(BF16), "w2": w2[l].astype(BF16),
            "gf": gf,
        }
        mod = _modulation(c_all, w_ada[l], b_ada[l].reshape(1, N_MOD * d))
        mod_p, mod_s = mod[:n_p], mod[n_p:]

        x1p, h2p, combp, na_p, nb_p = _mixer_prompt(xp, mod_p, p)
        t = SEQ_TILE
        xp = _moe(x1p, h2p, combp, mod_p.reshape(n_p, 1, N_MOD * d),
                  pl.BlockSpec((None, 1, d), lambda i, j: (i, 0, N_MOD - 1)), p, (n_p, seq // t),
                  lambda width: pl.BlockSpec((None, t, width), lambda i, j: (i, j, 0)))

        sa_tm = state_conv_a[l].transpose(1, 0, 2)
        sb_tm = state_conv_b[l].transpose(1, 0, 2)
        x1s, h2s, combs, na_s, nb_s = _mixer_sample(xs, mod_s, sa_tm, sb_tm, p)
        xs = _moe(x1s, h2s, combs, mod_s, pl.BlockSpec((n_s, d), lambda i: (0, N_MOD - 1)), p, (1,),
                  lambda width: pl.BlockSpec((steps * n_s, width), lambda i: (0, 0)))
        pa.append(na_p)
        pb.append(nb_p)
        sa.append(na_s.transpose(1, 0, 2))
        sb.append(nb_s.transpose(1, 0, 2))
    y_sample = xs.reshape(steps, n_s, d).transpose(1, 0, 2)
    return (xp, y_sample, jnp.stack(pa), jnp.stack(pb), jnp.stack(sa), jnp.stack(sb))
```

```python
import functools

import jax
import jax.numpy as jnp
from jax import lax
from jax.experimental import pallas as pl
from jax.experimental.pallas import tpu as pltpu

D_MODEL = 1024
D_A = 512
D_B = 512
CONV_A_WIDTH = 3
CONV_B_WIDTH = 31
N_GROUPS = 4
EXPERTS_PER_GROUP = 4
N_EXPERTS = 16
N_MOD = 6
EPS = 1e-6
D_IN = 3 * D_A + 2 * D_B + 2 * D_MODEL

LANES = 128
SUBLANES = 8
ROUTER_LANES = LANES
CARRY_A = SUBLANES
CARRY_B = 4 * SUBLANES
SEQ_TILE = 512
CONV_ROWS = 64
CONV_LANES = 2 * LANES
VMEM_LIMIT = 52 * 1024 * 1024

BF16 = jnp.bfloat16
F32 = jnp.float32


def _dot(a, b):
    return jnp.dot(a, b, preferred_element_type=F32)


def _rms(x, g):
    return x * lax.rsqrt(jnp.mean(x * x, axis=-1, keepdims=True) + EPS) * g


def _affine(v, scale, shift=None):
    r = scale.shape[0]
    if r != 1 and r != v.shape[0]:
        v3 = v.reshape(v.shape[0] // r, r, v.shape[-1])
        shift3 = None if shift is None else shift[None]
        return _affine(v3, scale[None], shift3).reshape(v.shape)
    out = v * scale
    return out if shift is None else out + shift


def _route(logits):
    lane_i = lax.broadcasted_iota(jnp.int32, logits.shape, 1)
    lane = lane_i.astype(F32)
    lane_group = lax.shift_right_logical(lane_i, 2).astype(F32)
    neg = -jnp.inf
    is_g = (lane_i >= N_EXPERTS) & (lane_i < N_EXPERTS + N_GROUPS)
    gl = jnp.where(is_g, logits, neg)
    gmax = jnp.max(gl, axis=-1, keepdims=True)
    g_lane = jnp.min(jnp.where(gl == gmax, lane, float(LANES)), axis=-1, keepdims=True)
    p_g = 1.0 / jnp.sum(jnp.exp(gl - gmax), axis=-1, keepdims=True)
    g_idx = g_lane - float(N_EXPERTS)
    in_grp = (lane_i < N_EXPERTS) & (lane_group == g_idx)
    el = jnp.where(in_grp, logits, neg)
    v1 = jnp.max(el, axis=-1, keepdims=True)
    i1 = jnp.min(jnp.where(el == v1, lane, float(LANES)), axis=-1, keepdims=True)
    el2 = jnp.where(lane == i1, neg, el)
    v2 = jnp.max(el2, axis=-1, keepdims=True)
    i2 = jnp.min(jnp.where(el2 == v2, lane, float(LANES)), axis=-1, keepdims=True)
    e21 = jnp.exp(v2 - v1)
    wt1 = p_g / (1.0 + e21)
    wt2 = p_g * e21 / (1.0 + e21)
    return jnp.where(lane == i1, wt1, jnp.where(lane == i2, wt2, 0.0))


def _ln_silu(v, bias, g, b):
    v = v + bias
    mu = jnp.mean(v, axis=-1, keepdims=True)
    d = v - mu
    var = jnp.mean(d * d, axis=-1, keepdims=True)
    y = d * lax.rsqrt(var + EPS) * g + b
    return y * jax.nn.sigmoid(y)


def _mixer_tail(x, hb, mod, b_a_conv, yb_in, g2, w_in_ref, woa_ref, wob_ref, wo_ref, wr_ref, br_ref,
                x1_ref, h2_ref, comb_ref):
    y_a = _dot(b_a_conv.astype(BF16), woa_ref[...])
    y_b = _dot(yb_in, wob_ref[...])
    o = 3 * D_A + 2 * D_B
    mg_a = _dot(hb, w_in_ref[:, o:o + D_MODEL])
    mg_b = _dot(hb, w_in_ref[:, o + D_MODEL:o + 2 * D_MODEL])
    merged = jax.nn.sigmoid(mg_a) * y_a + jax.nn.sigmoid(mg_b) * y_b
    gt1 = mod[:, 2 * D_MODEL:3 * D_MODEL]
    x1 = x + _affine(_dot(merged.astype(BF16), wo_ref[...]), gt1)
    sh2 = mod[:, 3 * D_MODEL:4 * D_MODEL]
    sc2 = mod[:, 4 * D_MODEL:5 * D_MODEL]
    h2 = _affine(_rms(x1, g2), 1.0 + sc2, sh2).astype(BF16)
    logits = _dot(h2, wr_ref[...]) + br_ref[...]
    x1_ref[...] = x1
    h2_ref[...] = h2
    comb_ref[...] = _route(logits)


def _mixer_prompt_kernel(x_ref, mod_ref, g1_ref, g2_ref, w_in_ref, caw_ref, woa_ref, cbw_ref, cbb_ref,
                         lng_ref, lnb_ref, wob_ref, wo_ref, wr_ref, br_ref,
                         x1_ref, h2_ref, comb_ref, na_ref, nb_ref, hist_a, hist_b, conv_b_ref):
    t = x_ref.shape[0]

    @pl.when(pl.program_id(1) == 0)
    def _():
        hist_a[0:CARRY_A, :] = jnp.zeros((CARRY_A, D_A), F32)
        hist_b[0:CARRY_B, :] = jnp.zeros((CARRY_B, D_B), F32)
        hist_b[CARRY_B + t:CARRY_B + t + SUBLANES, :] = jnp.zeros((SUBLANES, D_B), F32)

    x = x_ref[...]
    mod = mod_ref[...]
    h = _affine(_rms(x, g1_ref[...]), 1.0 + mod[:, D_MODEL:2 * D_MODEL], mod[:, 0:D_MODEL])
    hb = h.astype(BF16)

    def proj(lo, width):
        return _dot(hb, w_in_ref[:, lo:lo + width])

    hist_a[CARRY_A:CARRY_A + t, :] = proj(D_A, D_A) * proj(2 * D_A, D_A)
    conv_a = jnp.zeros((t, D_A), F32)
    for k in range(CONV_A_WIDTH):
        o = CARRY_A - (CONV_A_WIDTH - 1) + k
        conv_a = conv_a + caw_ref[k:k + 1, :] * hist_a[o:o + t, :]
    b_a_conv = proj(0, D_A) * conv_a
    na_ref[...] = hist_a[CARRY_A + t - (CONV_A_WIDTH - 1):CARRY_A + t, :]

    hist_b[CARRY_B:CARRY_B + t, :] = proj(3 * D_A, D_B) * jax.nn.sigmoid(proj(3 * D_A + D_B, D_B))
    nb_ref[...] = hist_b[CARRY_B + t - (CONV_B_WIDTH - 1):CARRY_B + t, :]

    base = CARRY_B - (CONV_B_WIDTH - 1)
    for r0 in range(0, t, CONV_ROWS):
        for c0 in range(0, D_B, CONV_LANES):
            cols = slice(c0, c0 + CONV_LANES)
            out = None
            for j in range(SUBLANES):
                part = None
                for k in range(CONV_B_WIDTH):
                    if (base + k) % SUBLANES != j:
                        continue
                    o = r0 + (base + k - j)
                    term = cbw_ref[k:k + 1, cols] * hist_b[o:o + CONV_ROWS + SUBLANES, cols]
                    part = term if part is None else part + term
                shifted = part[j:j + CONV_ROWS, :]
                out = shifted if out is None else out + shifted
            conv_b_ref[r0:r0 + CONV_ROWS, cols] = out
    yb_in = _ln_silu(conv_b_ref[...], cbb_ref[...], lng_ref[...], lnb_ref[...]).astype(BF16)

    hist_a[0:CARRY_A, :] = hist_a[t:t + CARRY_A, :]
    hist_b[0:CARRY_B, :] = hist_b[t:t + CARRY_B, :]

    _mixer_tail(x, hb, mod, b_a_conv, yb_in, g2_ref[...], w_in_ref, woa_ref, wob_ref, wo_ref,
                wr_ref, br_ref, x1_ref, h2_ref, comb_ref)


def _mixer_sample_kernel(x_ref, mod_ref, g1_ref, g2_ref, w_in_ref, caw_ref, woa_ref, cbw_ref, cbb_ref,
                         lng_ref, lnb_ref, wob_ref, wo_ref, wr_ref, br_ref, sa_ref, sb_ref,
                         x1_ref, h2_ref, comb_ref, na_ref, nb_ref, yb_in_ref):
    n_seq = mod_ref.shape[0]
    steps = x_ref.shape[0] // n_seq
    x = x_ref[...]
    mod = mod_ref[...]
    h = _affine(_rms(x, g1_ref[...]), 1.0 + mod[:, D_MODEL:2 * D_MODEL], mod[:, 0:D_MODEL])
    hb = h.astype(BF16)

    def proj(lo, width):
        return _dot(hb, w_in_ref[:, lo:lo + width])

    def slab(v, j):
        return v[j * n_seq:(j + 1) * n_seq, :]

    u_a = proj(D_A, D_A) * proj(2 * D_A, D_A)
    full_a = [sa_ref[j] for j in range(CONV_A_WIDTH - 1)] + [slab(u_a, j) for j in range(steps)]
    conv_a = []
    for s in range(steps):
        acc = jnp.zeros((n_seq, D_A), F32)
        for k in range(CONV_A_WIDTH):
            acc = acc + caw_ref[k:k + 1, :] * full_a[s + k]
        conv_a.append(acc)
    b_a_conv = proj(0, D_A) * jnp.concatenate(conv_a, axis=0)
    for j in range(CONV_A_WIDTH - 1):
        na_ref[j] = full_a[steps + j]

    u_b = proj(3 * D_A, D_B) * jax.nn.sigmoid(proj(3 * D_A + D_B, D_B))
    hist = CONV_B_WIDTH - 1

    def full_b(j):
        return sb_ref[j] if j < hist else slab(u_b, j - hist)

    bias, lng, lnb = cbb_ref[...], lng_ref[...], lnb_ref[...]
    for s in range(steps):
        acc = jnp.zeros((n_seq, D_B), F32)
        for k in range(CONV_B_WIDTH):
            acc = acc + cbw_ref[k:k + 1, :] * full_b(s + k)
        yb_in_ref[s * n_seq:(s + 1) * n_seq, :] = _ln_silu(acc, bias, lng, lnb).astype(BF16)
    for j in range(hist):
        nb_ref[j] = full_b(steps + j)

    _mixer_tail(x, hb, mod, b_a_conv, yb_in_ref[...], g2_ref[...], w_in_ref, woa_ref, wob_ref, wo_ref,
                wr_ref, br_ref, x1_ref, h2_ref, comb_ref)


def _moe_kernel(x1_ref, h2_ref, comb_ref, gt2_ref, w1_ref, w3_ref, w2_ref, gf_ref, y_ref):
    hb = h2_ref[...]
    comb = comb_ref[...]
    acc = jnp.zeros((hb.shape[0], D_MODEL), F32)
    for e in range(N_EXPERTS):
        g = _dot(hb, w1_ref[e])
        u = _dot(hb, w3_ref[e])
        a = g * jax.nn.sigmoid(g) * u * comb[:, e:e + 1]
        acc = acc + _dot(a.astype(BF16), w2_ref[e])
    x2 = x1_ref[...] + _affine(acc, gt2_ref[...])
    y_ref[...] = _rms(x2, gf_ref[...])


def _mod_kernel(c_ref, w_ref, b_ref, o_ref):
    c = c_ref[...]
    a = (c * jax.nn.sigmoid(c)).astype(BF16)
    o_ref[...] = _dot(a, w_ref[...].astype(BF16)) + b_ref[...]


def _const_spec(shape):
    nd = len(shape)
    return pl.BlockSpec(shape, lambda *_: (0,) * nd, pipeline_mode=pl.Buffered(1))


def _modulation(c_all, w_ada, b_ada):
    n, d = c_all.shape
    n_out = w_ada.shape[1]
    return pl.pallas_call(
        _mod_kernel,
        grid=(n_out // D_MODEL,),
        in_specs=[pl.BlockSpec((n, d), lambda j: (0, 0)),
                  pl.BlockSpec((d, D_MODEL), lambda j: (0, j)),
                  pl.BlockSpec((1, D_MODEL), lambda j: (0, j))],
        out_specs=pl.BlockSpec((n, D_MODEL), lambda j: (0, j)),
        out_shape=jax.ShapeDtypeStruct((n, n_out), F32),
        compiler_params=pltpu.CompilerParams(dimension_semantics=("arbitrary",),
                                             vmem_limit_bytes=VMEM_LIMIT),
        name="modulation",
    )(c_all, w_ada, b_ada)


def _mixer_weight_specs(p):
    names = ("g1", "g2", "w_in", "caw", "woa", "cbw", "cbb", "lng", "lnb", "wob", "wo", "wr", "br")
    return [p[k] for k in names], [_const_spec(p[k].shape) for k in names]


def _mixer_prompt(x, mod, p):
    b, s, d = x.shape
    t = SEQ_TILE
    weights, wspecs = _mixer_weight_specs(p)
    tile = lambda width: pl.BlockSpec((None, t, width), lambda i, j: (i, j, 0))
    per_seq = lambda rows, width: pl.BlockSpec((None, rows, width), lambda i, j: (i, 0, 0))
    return pl.pallas_call(
        _mixer_prompt_kernel,
        grid=(b, s // t),
        in_specs=[tile(d), per_seq(1, N_MOD * d)] + wspecs,
        out_specs=[tile(d), tile(d), tile(ROUTER_LANES),
                   per_seq(CONV_A_WIDTH - 1, D_A), per_seq(CONV_B_WIDTH - 1, D_B)],
        out_shape=[jax.ShapeDtypeStruct((b, s, d), F32),
                   jax.ShapeDtypeStruct((b, s, d), BF16),
                   jax.ShapeDtypeStruct((b, s, ROUTER_LANES), F32),
                   jax.ShapeDtypeStruct((b, CONV_A_WIDTH - 1, D_A), F32),
                   jax.ShapeDtypeStruct((b, CONV_B_WIDTH - 1, D_B), F32)],
        scratch_shapes=[pltpu.VMEM((CARRY_A + t, D_A), F32),
                        pltpu.VMEM((CARRY_B + t + SUBLANES, D_B), F32),
                        pltpu.VMEM((t, D_B), F32)],
        compiler_params=pltpu.CompilerParams(dimension_semantics=("arbitrary", "arbitrary"),
                                             vmem_limit_bytes=VMEM_LIMIT),
        name="mixer_prompt",
    )(x, mod.reshape(b, 1, N_MOD * d), *weights)


def _mixer_sample(x_tm, mod, sa_tm, sb_tm, p):
    n, d = x_tm.shape
    weights, wspecs = _mixer_weight_specs(p)
    full = lambda a: pl.BlockSpec(a.shape, lambda i: (0,) * a.ndim)
    return pl.pallas_call(
        _mixer_sample_kernel,
        grid=(1,),
        in_specs=[full(x_tm), full(mod)] + wspecs + [full(sa_tm), full(sb_tm)],
        out_specs=[pl.BlockSpec((n, d), lambda i: (0, 0)), pl.BlockSpec((n, d), lambda i: (0, 0)),
                   pl.BlockSpec((n, ROUTER_LANES), lambda i: (0, 0)), full(sa_tm), full(sb_tm)],
        out_shape=[jax.ShapeDtypeStruct((n, d), F32),
                   jax.ShapeDtypeStruct((n, d), BF16),
                   jax.ShapeDtypeStruct((n, ROUTER_LANES), F32),
                   jax.ShapeDtypeStruct(sa_tm.shape, F32),
                   jax.ShapeDtypeStruct(sb_tm.shape, F32)],
        scratch_shapes=[pltpu.VMEM((n, D_B), BF16)],
        compiler_params=pltpu.CompilerParams(dimension_semantics=("arbitrary",),
                                             vmem_limit_bytes=VMEM_LIMIT),
        name="mixer_sample",
    )(x_tm, mod, *weights, sa_tm, sb_tm)


def _moe(x1, h2, comb, gt2_src, gt2_spec, p, grid, tile):
    weights = [p["w1"], p["w3"], p["w2"], p["gf"]]
    return pl.pallas_call(
        _moe_kernel,
        grid=grid,
        in_specs=[tile(D_MODEL), tile(D_MODEL), tile(ROUTER_LANES), gt2_spec]
        + [_const_spec(w.shape) for w in weights],
        out_specs=tile(D_MODEL),
        out_shape=jax.ShapeDtypeStruct(x1.shape, F32),
        compiler_params=pltpu.CompilerParams(dimension_semantics=("arbitrary",) * len(grid),
                                             vmem_limit_bytes=VMEM_LIMIT),
        name="moe",
    )(x1, h2, comb, gt2_src, *weights)


def kernel(x_prompt, x_sample, c_prompt, c_sample, state_conv_a, state_conv_b, w_ada, b_ada, norm1_g,
           norm2_g, w_in, conv_a_w, w_out_a, conv_b_w, conv_b_bias, ln_b_g, ln_b_b, w_out_b, w_o,
           w_group, b_group, w_expert, b_expert, w1, w3, w2, final_norm_g):
    depth = w_ada.shape[0]
    assert depth == 1, "the MoE kernel fuses the final norm, so exactly one layer is supported"
    n_p, seq, d = x_prompt.shape
    n_s, steps, _ = x_sample.shape
    c_all = jnp.concatenate([c_prompt, c_sample], axis=0)
    xp = x_prompt
    xs = x_sample.transpose(1, 0, 2).reshape(steps * n_s, d)
    gf = final_norm_g.reshape(1, d)
    pa, pb, sa, sb = [], [], [], []
    for l in range(depth):
        pad = ROUTER_LANES - N_EXPERTS - N_GROUPS
        wr = jnp.pad(jnp.concatenate([w_expert[l], w_group[l]], axis=1), ((0, 0), (0, pad)))
        br = jnp.pad(jnp.concatenate([b_expert[l], b_group[l]]), (0, pad)).reshape(1, ROUTER_LANES)
        p = {
            "g1": norm1_g[l].reshape(1, d), "g2": norm2_g[l].reshape(1, d),
            "w_in": w_in[l].astype(BF16), "caw": conv_a_w[l], "woa": w_out_a[l].astype(BF16),
            "cbw": conv_b_w[l], "cbb": conv_b_bias[l].reshape(1, D_B),
            "lng": ln_b_g[l].reshape(1, D_B), "lnb": ln_b_b[l].reshape(1, D_B),
            "wob": w_out_b[l].astype(BF16), "wo": w_o[l].astype(BF16), "wr": wr.astype(BF16), "br": br,
            "w1": w1[l].astype(BF16), "w3": w3[l].astype(BF16), "w2": w2[l].astype(BF16),
            "gf": gf,
        }
        mod = _modulation(c_all, w_ada[l], b_ada[l].reshape(1, N_MOD * d))
        mod_p, mod_s = mod[:n_p], mod[n_p:]

        x1p, h2p, combp, na_p, nb_p = _mixer_prompt(xp, mod_p, p)
        t = SEQ_TILE
        xp = _moe(x1p, h2p, combp, mod_p.reshape(n_p, 1, N_MOD * d),
                  pl.BlockSpec((None, 1, d), lambda i, j: (i, 0, N_MOD - 1)), p, (n_p, seq // t),
                  lambda width: pl.BlockSpec((None, t, width), lambda i, j: (i, j, 0)))

        sa_tm = state_conv_a[l].transpose(1, 0, 2)
        sb_tm = state_conv_b[l].transpose(1, 0, 2)
        x1s, h2s, combs, na_s, nb_s = _mixer_sample(xs, mod_s, sa_tm, sb_tm, p)
        xs = _moe(x1s, h2s, combs, mod_s, pl.BlockSpec((n_s, d), lambda i: (0, N_MOD - 1)), p, (1,),
                  lambda width: pl.BlockSpec((steps * n_s, width), lambda i: (0, 0)))
        pa.append(na_p)
        pb.append(nb_p)
        sa.append(na_s.transpose(1, 0, 2))
        sb.append(nb_s.transpose(1, 0, 2))
    y_sample = xs.reshape(steps, n_s, d).transpose(1, 0, 2)
    return (xp, y_sample, jnp.stack(pa), jnp.stack(pb), jnp.stack(sa), jnp.stack(sb))
```

```python
import functools

import jax
import jax.numpy as jnp
from jax import lax
from jax.experimental import pallas as pl
from jax.experimental.pallas import tpu as pltpu

D_MODEL = 1024
D_A = 512
D_B = 512
CONV_A_WIDTH = 3
CONV_B_WIDTH = 31
N_GROUPS = 4
EXPERTS_PER_GROUP = 4
N_EXPERTS = 16
D_EXPERT = 256
N_MOD = 6
EPS = 1e-6
D_IN = 3 * D_A + 2 * D_B + 2 * D_MODEL

LANES = 128
SUBLANES = 8
ROUTER_LANES = LANES
CARRY_A = SUBLANES
CARRY_B = 4 * SUBLANES
SEQ_TILE = 512
CONV_ROWS = 64
CONV_LANES = 2 * LANES
SORT_SHIFT = 4
SORT_ALIGN = 1 << SORT_SHIFT
SORT_PAD = LANES
MOE_ROWS = 144
VMEM_LIMIT = 52 * 1024 * 1024

BF16 = jnp.bfloat16
F32 = jnp.float32


def _dot(a, b):
    return jnp.dot(a, b, preferred_element_type=F32)


def _rms(x, g):
    return x * lax.rsqrt(jnp.mean(x * x, axis=-1, keepdims=True) + EPS) * g


def _affine(v, scale, shift=None):
    r = scale.shape[0]
    if r != 1 and r != v.shape[0]:
        v3 = v.reshape(v.shape[0] // r, r, v.shape[-1])
        shift3 = None if shift is None else shift[None]
        return _affine(v3, scale[None], shift3).reshape(v.shape)
    out = v * scale
    return out if shift is None else out + shift


def _route(logits):
    t = logits.shape[0]
    lane_i = lax.broadcasted_iota(jnp.int32, logits.shape, 1)
    lane = lane_i.astype(F32)
    neg = -jnp.inf
    is_g = (lane_i >= N_EXPERTS) & (lane_i < N_EXPERTS + N_GROUPS)
    gl = jnp.where(is_g, logits, neg)
    gmax = jnp.max(gl, axis=-1, keepdims=True)
    g_lane = jnp.min(jnp.where(gl == gmax, lane, float(LANES)), axis=-1, keepdims=True)
    p_g = 1.0 / jnp.sum(jnp.exp(gl - gmax), axis=-1, keepdims=True)
    g_idx = g_lane - float(N_EXPERTS)
    el = logits
    for g in range(1, N_GROUPS):
        el = jnp.where(g_idx == float(g), pltpu.roll(logits, LANES - EXPERTS_PER_GROUP * g, axis=1), el)
    el = jnp.where(lane_i < EXPERTS_PER_GROUP, el, neg)
    v1 = jnp.max(el, axis=-1, keepdims=True)
    i1 = jnp.min(jnp.where(el == v1, lane, float(LANES)), axis=-1, keepdims=True)
    el2 = jnp.where(lane == i1, neg, el)
    v2 = jnp.max(el2, axis=-1, keepdims=True)
    i2 = jnp.min(jnp.where(el2 == v2, lane, float(LANES)), axis=-1, keepdims=True)
    e21 = jnp.exp(v2 - v1)
    wt1 = p_g / (1.0 + e21)
    wt2 = p_g * e21 / (1.0 + e21)
    comb = jnp.where(lane == i1, wt1, jnp.where(lane == i2, wt2, 0.0))
    onehot = jnp.where(lane == g_idx, 1.0, 0.0)
    earlier = lax.broadcasted_iota(jnp.int32, (t, t), 1) < lax.broadcasted_iota(jnp.int32, (t, t), 0)
    before = _dot(jnp.where(earlier, 1.0, 0.0).astype(BF16), onehot.astype(BF16))
    rank = jnp.sum(before * onehot, axis=-1, keepdims=True)
    count = jnp.sum(onehot, axis=0, keepdims=True)
    padded = jnp.ceil(count / SORT_ALIGN) * SORT_ALIGN
    lane1 = lax.broadcasted_iota(jnp.int32, (1, LANES), 1)
    start = jnp.zeros((1, LANES), F32)
    for g in range(N_GROUPS - 1):
        size_g = jnp.sum(jnp.where(lane1 == g, padded, 0.0), axis=-1, keepdims=True)
        start = start + jnp.where(lane1 > g, size_g, 0.0)
    dest = jnp.sum(onehot * start, axis=-1, keepdims=True) + rank
    return jnp.where(lane_i < EXPERTS_PER_GROUP, comb, dest), count


def _ln_silu(v, bias, g, b):
    v = v + bias
    mu = jnp.mean(v, axis=-1, keepdims=True)
    d = v - mu
    var = jnp.mean(d * d, axis=-1, keepdims=True)
    y = d * lax.rsqrt(var + EPS) * g + b
    return y * jax.nn.sigmoid(y)


def _mixer_tail(x, hb, mod, b_a_conv, yb_in, g2, w_in_ref, woa_ref, wob_ref, wo_ref, wr_ref, br_ref,
                x1_ref, h2_ref, route_ref, cnt_ref):
    y_a = _dot(b_a_conv.astype(BF16), woa_ref[...])
    y_b = _dot(yb_in, wob_ref[...])
    o = 3 * D_A + 2 * D_B
    mg_a = _dot(hb, w_in_ref[:, o:o + D_MODEL])
    mg_b = _dot(hb, w_in_ref[:, o + D_MODEL:o + 2 * D_MODEL])
    merged = jax.nn.sigmoid(mg_a) * y_a + jax.nn.sigmoid(mg_b) * y_b
    gt1 = mod[:, 2 * D_MODEL:3 * D_MODEL]
    x1 = x + _affine(_dot(merged.astype(BF16), wo_ref[...]), gt1)
    sh2 = mod[:, 3 * D_MODEL:4 * D_MODEL]
    sc2 = mod[:, 4 * D_MODEL:5 * D_MODEL]
    h2 = _affine(_rms(x1, g2), 1.0 + sc2, sh2).astype(BF16)
    logits = _dot(h2, wr_ref[...]) + br_ref[...]
    x1_ref[...] = x1
    h2_ref[...] = h2
    route, count = _route(logits)
    route_ref[...] = route
    cnt_ref[...] = count.astype(jnp.int32)


def _mixer_prompt_kernel(x_ref, mod_ref, g1_ref, g2_ref, w_in_ref, caw_ref, woa_ref, cbw_ref, cbb_ref,
                         lng_ref, lnb_ref, wob_ref, wo_ref, wr_ref, br_ref,
                         x1_ref, h2_ref, route_ref, cnt_ref, na_ref, nb_ref, hist_a, hist_b, conv_b_ref):
    t = x_ref.shape[0]

    @pl.when(pl.program_id(1) == 0)
    def _():
        hist_a[0:CARRY_A, :] = jnp.zeros((CARRY_A, D_A), F32)
        hist_b[0:CARRY_B, :] = jnp.zeros((CARRY_B, D_B), F32)
        hist_b[CARRY_B + t:CARRY_B + t + SUBLANES, :] = jnp.zeros((SUBLANES, D_B), F32)

    x = x_ref[...]
    mod = mod_ref[...]
    h = _affine(_rms(x, g1_ref[...]), 1.0 + mod[:, D_MODEL:2 * D_MODEL], mod[:, 0:D_MODEL])
    hb = h.astype(BF16)

    def proj(lo, width):
        return _dot(hb, w_in_ref[:, lo:lo + width])

    hist_a[CARRY_A:CARRY_A + t, :] = proj(D_A, D_A) * proj(2 * D_A, D_A)
    conv_a = jnp.zeros((t, D_A), F32)
    for k in range(CONV_A_WIDTH):
        o = CARRY_A - (CONV_A_WIDTH - 1) + k
        conv_a = conv_a + caw_ref[k:k + 1, :] * hist_a[o:o + t, :]
    b_a_conv = proj(0, D_A) * conv_a
    na_ref[...] = hist_a[CARRY_A + t - (CONV_A_WIDTH - 1):CARRY_A + t, :]

    hist_b[CARRY_B:CARRY_B + t, :] = proj(3 * D_A, D_B) * jax.nn.sigmoid(proj(3 * D_A + D_B, D_B))
    nb_ref[...] = hist_b[CARRY_B + t - (CONV_B_WIDTH - 1):CARRY_B + t, :]

    base = CARRY_B - (CONV_B_WIDTH - 1)
    for r0 in range(0, t, CONV_ROWS):
        for c0 in range(0, D_B, CONV_LANES):
            cols = slice(c0, c0 + CONV_LANES)
            out = None
            for j in range(SUBLANES):
                part = None
                for k in range(CONV_B_WIDTH):
                    if (base + k) % SUBLANES != j:
                        continue
                    o = r0 + (base + k - j)
                    term = cbw_ref[k:k + 1, cols] * hist_b[o:o + CONV_ROWS + SUBLANES, cols]
                    part = term if part is None else part + term
                shifted = part[j:j + CONV_ROWS, :]
                out = shifted if out is None else out + shifted
            conv_b_ref[r0:r0 + CONV_ROWS, cols] = out
    yb_in = _ln_silu(conv_b_ref[...], cbb_ref[...], lng_ref[...], lnb_ref[...]).astype(BF16)

    hist_a[0:CARRY_A, :] = hist_a[t:t + CARRY_A, :]
    hist_b[0:CARRY_B, :] = hist_b[t:t + CARRY_B, :]

    _mixer_tail(x, hb, mod, b_a_conv, yb_in, g2_ref[...], w_in_ref, woa_ref, wob_ref, wo_ref,
                wr_ref, br_ref, x1_ref, h2_ref, route_ref, cnt_ref)


def _mixer_sample_kernel(x_ref, mod_ref, g1_ref, g2_ref, w_in_ref, caw_ref, woa_ref, cbw_ref, cbb_ref,
                         lng_ref, lnb_ref, wob_ref, wo_ref, wr_ref, br_ref, sa_ref, sb_ref,
                         x1_ref, h2_ref, route_ref, cnt_ref, na_ref, nb_ref, yb_in_ref):
    n_seq = mod_ref.shape[0]
    steps = x_ref.shape[0] // n_seq
    x = x_ref[...]
    mod = mod_ref[...]
    h = _affine(_rms(x, g1_ref[...]), 1.0 + mod[:, D_MODEL:2 * D_MODEL], mod[:, 0:D_MODEL])
    hb = h.astype(BF16)

    def proj(lo, width):
        return _dot(hb, w_in_ref[:, lo:lo + width])

    def slab(v, j):
        return v[j * n_seq:(j + 1) * n_seq, :]

    u_a = proj(D_A, D_A) * proj(2 * D_A, D_A)
    full_a = [sa_ref[j] for j in range(CONV_A_WIDTH - 1)] + [slab(u_a, j) for j in range(steps)]
    conv_a = []
    for s in range(steps):
        acc = jnp.zeros((n_seq, D_A), F32)
        for k in range(CONV_A_WIDTH):
            acc = acc + caw_ref[k:k + 1, :] * full_a[s + k]
        conv_a.append(acc)
    b_a_conv = proj(0, D_A) * jnp.concatenate(conv_a, axis=0)
    for j in range(CONV_A_WIDTH - 1):
        na_ref[j] = full_a[steps + j]

    u_b = proj(3 * D_A, D_B) * jax.nn.sigmoid(proj(3 * D_A + D_B, D_B))
    hist = CONV_B_WIDTH - 1

    def full_b(j):
        return sb_ref[j] if j < hist else slab(u_b, j - hist)

    bias, lng, lnb = cbb_ref[...], lng_ref[...], lnb_ref[...]
    for s in range(steps):
        acc = jnp.zeros((n_seq, D_B), F32)
        for k in range(CONV_B_WIDTH):
            acc = acc + cbw_ref[k:k + 1, :] * full_b(s + k)
        yb_in_ref[s * n_seq:(s + 1) * n_seq, :] = _ln_silu(acc, bias, lng, lnb).astype(BF16)
    for j in range(hist):
        nb_ref[j] = full_b(steps + j)

    _mixer_tail(x, hb, mod, b_a_conv, yb_in_ref[...], g2_ref[...], w_in_ref, woa_ref, wob_ref, wo_ref,
                wr_ref, br_ref, x1_ref, h2_ref, route_ref, cnt_ref)


def _moe_kernel(cnt_ref, x1_ref, h2_ref, route_ref, gt2_ref, w1_ref, w3_ref, w2_ref, gf_ref, y_ref,
                h2s_ref, cs_ref, acc_ref, *, grid_rank):
    t = h2_ref.shape[0]
    tp = h2s_ref.shape[0]
    tile = pl.program_id(0)
    for axis in range(1, grid_rank):
        tile = tile * pl.num_programs(axis) + pl.program_id(axis)

    route = route_ref[...]
    lane_i = lax.broadcasted_iota(jnp.int32, route.shape, 1)
    dest_row = jnp.transpose(route)[EXPERTS_PER_GROUP:EXPERTS_PER_GROUP + 1, :]
    slot = lax.broadcasted_iota(jnp.int32, (tp, t), 0).astype(F32)
    p_sort = jnp.where(slot == dest_row, 1.0, 0.0).astype(BF16)
    h2s_ref[...] = _dot(p_sort, h2_ref[...]).astype(BF16)
    comb = jnp.where(lane_i < EXPERTS_PER_GROUP, route, 0.0)
    c_hi = comb.astype(BF16)
    rest = comb - c_hi.astype(F32)
    c_mid = rest.astype(BF16)
    c_lo = (rest - c_mid.astype(F32)).astype(BF16)
    cs_ref[...] = _dot(p_sort, c_hi) + _dot(p_sort, c_mid) + _dot(p_sort, c_lo)
    acc_ref[...] = jnp.zeros(acc_ref.shape, F32)

    start = 0
    for g in range(N_GROUPS):
        n = cnt_ref[tile * N_GROUPS + g]
        end = start + n

        def one_pass(p, carry, g=g, start=start, end=end):
            lo = start + p * MOE_ROWS
            off = pl.multiple_of(jnp.minimum(lo, tp - MOE_ROWS), SORT_ALIGN)
            rows = off + lax.broadcasted_iota(jnp.int32, (MOE_ROWS, 1), 0)
            valid = (rows >= lo) & (rows < end)
            hb = h2s_ref[pl.ds(off, MOE_ROWS), :]
            c4 = jnp.where(valid, cs_ref[pl.ds(off, MOE_ROWS), :], 0.0)
            gate = _dot(hb, w1_ref[g])
            up = _dot(hb, w3_ref[g])
            cexp = jnp.concatenate(
                [jnp.broadcast_to(c4[:, e:e + 1], (MOE_ROWS, D_EXPERT)) for e in range(EXPERTS_PER_GROUP)],
                axis=1)
            a = gate * jax.nn.sigmoid(gate) * up * cexp
            acc_ref[pl.ds(off, MOE_ROWS), :] += _dot(a.astype(BF16), w2_ref[g])
            return carry

        lax.fori_loop(0, lax.div(n + (MOE_ROWS - 1), MOE_ROWS), one_pass, 0)
        start = start + lax.shift_right_logical(n + (SORT_ALIGN - 1), SORT_SHIFT) * SORT_ALIGN

    dest_col = route_ref[:, EXPERTS_PER_GROUP:EXPERTS_PER_GROUP + 1]
    p_back = jnp.where(lax.broadcasted_iota(jnp.int32, (t, tp), 1).astype(F32) == dest_col,
                       1.0, 0.0).astype(BF16)
    y = acc_ref[...]
    y_hi = y.astype(BF16)
    y_lo = (y - y_hi.astype(F32)).astype(BF16)
    y_tok = _dot(p_back, y_hi) + _dot(p_back, y_lo)
    x2 = x1_ref[...] + _affine(y_tok, gt2_ref[...])
    y_ref[...] = _rms(x2, gf_ref[...])


def _mod_kernel(c_ref, w_ref, b_ref, o_ref):
    c = c_ref[...]
    a = (c * jax.nn.sigmoid(c)).astype(BF16)
    o_ref[...] = _dot(a, w_ref[...].astype(BF16)) + b_ref[...]


def _const_spec(shape):
    nd = len(shape)
    return pl.BlockSpec(shape, lambda *_: (0,) * nd, pipeline_mode=pl.Buffered(1))


def _modulation(c_all, w_ada, b_ada):
    n, d = c_all.shape
    n_out = w_ada.shape[1]
    return pl.pallas_call(
        _mod_kernel,
        grid=(n_out // D_MODEL,),
        in_specs=[pl.BlockSpec((n, d), lambda j: (0, 0)),
                  pl.BlockSpec((d, D_MODEL), lambda j: (0, j)),
                  pl.BlockSpec((1, D_MODEL), lambda j: (0, j))],
        out_specs=pl.BlockSpec((n, D_MODEL), lambda j: (0, j)),
        out_shape=jax.ShapeDtypeStruct((n, n_out), F32),
        compiler_params=pltpu.CompilerParams(dimension_semantics=("arbitrary",),
                                             vmem_limit_bytes=VMEM_LIMIT),
        name="modulation",
    )(c_all, w_ada, b_ada)


def _mixer_weight_specs(p):
    names = ("g1", "g2", "w_in", "caw", "woa", "cbw", "cbb", "lng", "lnb", "wob", "wo", "wr", "br")
    return [p[k] for k in names], [_const_spec(p[k].shape) for k in names]


def _mixer_prompt(x, mod, p):
    b, s, d = x.shape
    t = SEQ_TILE
    n_t = s // t
    weights, wspecs = _mixer_weight_specs(p)
    tile = lambda width: pl.BlockSpec((None, t, width), lambda i, j: (i, j, 0))
    per_seq = lambda rows, width: pl.BlockSpec((None, rows, width), lambda i, j: (i, 0, 0))
    return pl.pallas_call(
        _mixer_prompt_kernel,
        grid=(b, n_t),
        in_specs=[tile(d), per_seq(1, N_MOD * d)] + wspecs,
        out_specs=[tile(d), tile(d), tile(ROUTER_LANES),
                   pl.BlockSpec((None, 1, LANES), lambda i, j: (i * n_t + j, 0, 0)),
                   per_seq(CONV_A_WIDTH - 1, D_A), per_seq(CONV_B_WIDTH - 1, D_B)],
        out_shape=[jax.ShapeDtypeStruct((b, s, d), F32),
                   jax.ShapeDtypeStruct((b, s, d), BF16),
                   jax.ShapeDtypeStruct((b, s, ROUTER_LANES), F32),
                   jax.ShapeDtypeStruct((b * n_t, 1, LANES), jnp.int32),
                   jax.ShapeDtypeStruct((b, CONV_A_WIDTH - 1, D_A), F32),
                   jax.ShapeDtypeStruct((b, CONV_B_WIDTH - 1, D_B), F32)],
        scratch_shapes=[pltpu.VMEM((CARRY_A + t, D_A), F32),
                        pltpu.VMEM((CARRY_B + t + SUBLANES, D_B), F32),
                        pltpu.VMEM((t, D_B), F32)],
        compiler_params=pltpu.CompilerParams(dimension_semantics=("arbitrary", "arbitrary"),
                                             vmem_limit_bytes=VMEM_LIMIT),
        name="mixer_prompt",
    )(x, mod.reshape(b, 1, N_MOD * d), *weights)


def _mixer_sample(x_tm, mod, sa_tm, sb_tm, p):
    n, d = x_tm.shape
    weights, wspecs = _mixer_weight_specs(p)
    full = lambda a: pl.BlockSpec(a.shape, lambda i: (0,) * a.ndim)
    whole = lambda *shape: pl.BlockSpec(shape, lambda i: (0,) * len(shape))
    return pl.pallas_call(
        _mixer_sample_kernel,
        grid=(1,),
        in_specs=[full(x_tm), full(mod)] + wspecs + [full(sa_tm), full(sb_tm)],
        out_specs=[whole(n, d), whole(n, d), whole(n, ROUTER_LANES), whole(1, LANES),
                   full(sa_tm), full(sb_tm)],
        out_shape=[jax.ShapeDtypeStruct((n, d), F32),
                   jax.ShapeDtypeStruct((n, d), BF16),
                   jax.ShapeDtypeStruct((n, ROUTER_LANES), F32),
                   jax.ShapeDtypeStruct((1, LANES), jnp.int32),
                   jax.ShapeDtypeStruct(sa_tm.shape, F32),
                   jax.ShapeDtypeStruct(sb_tm.shape, F32)],
        scratch_shapes=[pltpu.VMEM((n, D_B), BF16)],
        compiler_params=pltpu.CompilerParams(dimension_semantics=("arbitrary",),
                                             vmem_limit_bytes=VMEM_LIMIT),
        name="mixer_sample",
    )(x_tm, mod, *weights, sa_tm, sb_tm)


def _moe(counts, x1, h2, route, gt2_src, gt2_spec, p, grid, tile, rows):
    weights = [p["w1"], p["w3"], p["w2"], p["gf"]]
    sorted_rows = rows + SORT_PAD
    grid_spec = pltpu.PrefetchScalarGridSpec(
        num_scalar_prefetch=1,
        grid=grid,
        in_specs=[tile(D_MODEL), tile(D_MODEL), tile(ROUTER_LANES), gt2_spec]
        + [_const_spec(w.shape) for w in weights],
        out_specs=tile(D_MODEL),
        scratch_shapes=[pltpu.VMEM((sorted_rows, D_MODEL), BF16),
                        pltpu.VMEM((sorted_rows, ROUTER_LANES), F32),
                        pltpu.VMEM((sorted_rows, D_MODEL), F32)],
    )
    return pl.pallas_call(
        functools.partial(_moe_kernel, grid_rank=len(grid)),
        grid_spec=grid_spec,
        out_shape=jax.ShapeDtypeStruct(x1.shape, F32),
        compiler_params=pltpu.CompilerParams(dimension_semantics=("arbitrary",) * len(grid),
                                             vmem_limit_bytes=VMEM_LIMIT),
        name="moe",
    )(counts, x1, h2, route, gt2_src, *weights)


def _group_weights(w, transpose):
    e, a, b = w.shape
    w = w.astype(BF16).reshape(N_GROUPS, EXPERTS_PER_GROUP, a, b)
    if transpose:
        return w.transpose(0, 2, 1, 3).reshape(N_GROUPS, a, EXPERTS_PER_GROUP * b)
    return w.reshape(N_GROUPS, EXPERTS_PER_GROUP * a, b)


def kernel(x_prompt, x_sample, c_prompt, c_sample, state_conv_a, state_conv_b, w_ada, b_ada, norm1_g,
           norm2_g, w_in, conv_a_w, w_out_a, conv_b_w, conv_b_bias, ln_b_g, ln_b_b, w_out_b, w_o,
           w_group, b_group, w_expert, b_expert, w1, w3, w2, final_norm_g):
    depth = w_ada.shape[0]
    assert depth == 1, "the MoE kernel fuses the final norm, so exactly one layer is supported"
    n_p, seq, d = x_prompt.shape
    n_s, steps, _ = x_sample.shape
    c_all = jnp.concatenate([c_prompt, c_sample], axis=0)
    xp = x_prompt
    xs = x_sample.transpose(1, 0, 2).reshape(steps * n_s, d)
    gf = final_norm_g.reshape(1, d)
    pa, pb, sa, sb = [], [], [], []
    for l in range(depth):
        pad = ROUTER_LANES - N_EXPERTS - N_GROUPS
        wr = jnp.pad(jnp.concatenate([w_expert[l], w_group[l]], axis=1), ((0, 0), (0, pad)))
        br = jnp.pad(jnp.concatenate([b_expert[l], b_group[l]]), (0, pad)).reshape(1, ROUTER_LANES)
        p = {
            "g1": norm1_g[l].reshape(1, d), "g2": norm2_g[l].reshape(1, d),
            "w_in": w_in[l].astype(BF16), "caw": conv_a_w[l], "woa": w_out_a[l].astype(BF16),
            "cbw": conv_b_w[l], "cbb": conv_b_bias[l].reshape(1, D_B),
            "lng": ln_b_g[l].reshape(1, D_B), "lnb": ln_b_b[l].reshape(1, D_B),
            "wob": w_out_b[l].astype(BF16), "wo": w_o[l].astype(BF16), "wr": wr.astype(BF16), "br": br,
            "w1": _group_weights(w1[l], True), "w3": _group_weights(w3[l], True),
            "w2": _group_weights(w2[l], False), "gf": gf,
        }
        mod = _modulation(c_all, w_ada[l], b_ada[l].reshape(1, N_MOD * d))
        mod_p, mod_s = mod[:n_p], mod[n_p:]

        x1p, h2p, route_p, cnt_p, na_p, nb_p = _mixer_prompt(xp, mod_p, p)
        t = SEQ_TILE
        xp = _moe(cnt_p[:, 0, :N_GROUPS].reshape(-1), x1p, h2p, route_p,
                  mod_p.reshape(n_p, 1, N_MOD * d),
                  pl.BlockSpec((None, 1, d), lambda i, j, c: (i, 0, N_MOD - 1)), p, (n_p, seq // t),
                  lambda width: pl.BlockSpec((None, t, width), lambda i, j, c: (i, j, 0)), t)

        sa_tm = state_conv_a[l].transpose(1, 0, 2)
        sb_tm = state_conv_b[l].transpose(1, 0, 2)
        x1s, h2s, route_s, cnt_s, na_s, nb_s = _mixer_sample(xs, mod_s, sa_tm, sb_tm, p)
        xs = _moe(cnt_s[0, :N_GROUPS], x1s, h2s, route_s, mod_s,
                  pl.BlockSpec((n_s, d), lambda i, c: (0, N_MOD - 1)), p, (1,),
                  lambda width: pl.BlockSpec((steps * n_s, width), lambda i, c: (0, 0)), steps * n_s)
        pa.append(na_p)
        pb.append(nb_p)
        sa.append(na_s.transpose(1, 0, 2))
        sb.append(nb_s.transpose(1, 0, 2))
    y_sample = xs.reshape(steps, n_s, d).transpose(1, 0, 2)
    return (xp, y_sample, jnp.stack(pa), jnp.stack(pb), jnp.stack(sa), jnp.stack(sb))
```

```python
import functools

import jax
import jax.numpy as jnp
from jax import lax
from jax.experimental import pallas as pl
from jax.experimental.pallas import tpu as pltpu

D_MODEL = 1024
D_A = 512
D_B = 512
CONV_A_WIDTH = 3
CONV_B_WIDTH = 31
N_GROUPS = 4
EXPERTS_PER_GROUP = 4
N_EXPERTS = 16
D_EXPERT = 256
N_MOD = 6
EPS = 1e-6
D_IN = 3 * D_A + 2 * D_B + 2 * D_MODEL

LANES = 128
SUBLANES = 8
ROUTER_LANES = LANES
CARRY_A = SUBLANES
CARRY_B = 4 * SUBLANES
SEQ_TILE = 512
CONV_ROWS = 64
CONV_LANES = 2 * LANES
SORT_SHIFT = 4
SORT_ALIGN = 1 << SORT_SHIFT
SORT_PAD = LANES
MOE_ROWS = 144
VMEM_LIMIT = 52 * 1024 * 1024

BF16 = jnp.bfloat16
F32 = jnp.float32


def _dot(a, b):
    return jnp.dot(a, b, preferred_element_type=F32)


def _rms(x, g):
    return x * lax.rsqrt(jnp.mean(x * x, axis=-1, keepdims=True) + EPS) * g


def _affine(v, scale, shift=None):
    r = scale.shape[0]
    if r != 1 and r != v.shape[0]:
        v3 = v.reshape(v.shape[0] // r, r, v.shape[-1])
        shift3 = None if shift is None else shift[None]
        return _affine(v3, scale[None], shift3).reshape(v.shape)
    out = v * scale
    return out if shift is None else out + shift


def _route(logits):
    t = logits.shape[0]
    lane_i = lax.broadcasted_iota(jnp.int32, logits.shape, 1)
    lane = lane_i.astype(F32)
    neg = -jnp.inf
    is_g = (lane_i >= N_EXPERTS) & (lane_i < N_EXPERTS + N_GROUPS)
    gl = jnp.where(is_g, logits, neg)
    gmax = jnp.max(gl, axis=-1, keepdims=True)
    g_lane = jnp.min(jnp.where(gl == gmax, lane, float(LANES)), axis=-1, keepdims=True)
    p_g = 1.0 / jnp.sum(jnp.exp(gl - gmax), axis=-1, keepdims=True)
    g_idx = g_lane - float(N_EXPERTS)
    el = logits
    for g in range(1, N_GROUPS):
        el = jnp.where(g_idx == float(g), pltpu.roll(logits, LANES - EXPERTS_PER_GROUP * g, axis=1), el)
    el = jnp.where(lane_i < EXPERTS_PER_GROUP, el, neg)
    v1 = jnp.max(el, axis=-1, keepdims=True)
    i1 = jnp.min(jnp.where(el == v1, lane, float(LANES)), axis=-1, keepdims=True)
    el2 = jnp.where(lane == i1, neg, el)
    v2 = jnp.max(el2, axis=-1, keepdims=True)
    i2 = jnp.min(jnp.where(el2 == v2, lane, float(LANES)), axis=-1, keepdims=True)
    e21 = jnp.exp(v2 - v1)
    wt1 = p_g / (1.0 + e21)
    wt2 = p_g * e21 / (1.0 + e21)
    comb = jnp.where(lane == i1, wt1, jnp.where(lane == i2, wt2, 0.0))
    onehot = jnp.where(lane == g_idx, 1.0, 0.0)
    earlier = lax.broadcasted_iota(jnp.int32, (t, t), 1) < lax.broadcasted_iota(jnp.int32, (t, t), 0)
    before = _dot(jnp.where(earlier, 1.0, 0.0).astype(BF16), onehot.astype(BF16))
    rank = jnp.sum(before * onehot, axis=-1, keepdims=True)
    count = jnp.sum(onehot, axis=0, keepdims=True)
    padded = jnp.ceil(count / SORT_ALIGN) * SORT_ALIGN
    lane1 = lax.broadcasted_iota(jnp.int32, (1, LANES), 1)
    start = jnp.zeros((1, LANES), F32)
    for g in range(N_GROUPS - 1):
        size_g = jnp.sum(jnp.where(lane1 == g, padded, 0.0), axis=-1, keepdims=True)
        start = start + jnp.where(lane1 > g, size_g, 0.0)
    dest = jnp.sum(onehot * start, axis=-1, keepdims=True) + rank
    return jnp.where(lane_i < EXPERTS_PER_GROUP, comb, dest), count


def _ln_silu(v, bias, g, b):
    v = v + bias
    mu = jnp.mean(v, axis=-1, keepdims=True)
    d = v - mu
    var = jnp.mean(d * d, axis=-1, keepdims=True)
    y = d * lax.rsqrt(var + EPS) * g + b
    return y * jax.nn.sigmoid(y)


def _mixer_tail(x, hb, mod, b_a_conv, yb_in, g2, w_in_ref, woa_ref, wob_ref, wo_ref, wr_ref, br_ref,
                x1_ref, h2_ref, route_ref, cnt_ref):
    y_a = _dot(b_a_conv.astype(BF16), woa_ref[...])
    y_b = _dot(yb_in, wob_ref[...])
    o = 3 * D_A + 2 * D_B
    mg_a = _dot(hb, w_in_ref[:, o:o + D_MODEL])
    mg_b = _dot(hb, w_in_ref[:, o + D_MODEL:o + 2 * D_MODEL])
    merged = jax.nn.sigmoid(mg_a) * y_a + jax.nn.sigmoid(mg_b) * y_b
    gt1 = mod[:, 2 * D_MODEL:3 * D_MODEL]
    x1 = x + _affine(_dot(merged.astype(BF16), wo_ref[...]), gt1)
    sh2 = mod[:, 3 * D_MODEL:4 * D_MODEL]
    sc2 = mod[:, 4 * D_MODEL:5 * D_MODEL]
    h2 = _affine(_rms(x1, g2), 1.0 + sc2, sh2).astype(BF16)
    logits = _dot(h2, wr_ref[...]) + br_ref[...]
    x1_ref[...] = x1
    h2_ref[...] = h2
    route, count = _route(logits)
    route_ref[...] = route
    cnt_ref[...] = count.astype(jnp.int32)


def _mixer_prompt_kernel(x_ref, mod_ref, g1_ref, g2_ref, w_in_ref, caw_ref, woa_ref, cbw_ref, cbb_ref,
                         lng_ref, lnb_ref, wob_ref, wo_ref, wr_ref, br_ref,
                         x1_ref, h2_ref, route_ref, cnt_ref, na_ref, nb_ref, hist_a, hist_b, conv_b_ref):
    t = x_ref.shape[0]

    @pl.when(pl.program_id(1) == 0)
    def _():
        hist_a[0:CARRY_A, :] = jnp.zeros((CARRY_A, D_A), F32)
        hist_b[0:CARRY_B, :] = jnp.zeros((CARRY_B, D_B), F32)
        hist_b[CARRY_B + t:CARRY_B + t + SUBLANES, :] = jnp.zeros((SUBLANES, D_B), F32)

    x = x_ref[...]
    mod = mod_ref[...]
    h = _affine(_rms(x, g1_ref[...]), 1.0 + mod[:, D_MODEL:2 * D_MODEL], mod[:, 0:D_MODEL])
    hb = h.astype(BF16)

    def proj(lo, width):
        return _dot(hb, w_in_ref[:, lo:lo + width])

    hist_a[CARRY_A:CARRY_A + t, :] = proj(D_A, D_A) * proj(2 * D_A, D_A)
    conv_a = jnp.zeros((t, D_A), F32)
    for k in range(CONV_A_WIDTH):
        o = CARRY_A - (CONV_A_WIDTH - 1) + k
        conv_a = conv_a + caw_ref[k:k + 1, :] * hist_a[o:o + t, :]
    b_a_conv = proj(0, D_A) * conv_a
    na_ref[...] = hist_a[CARRY_A + t - (CONV_A_WIDTH - 1):CARRY_A + t, :]

    hist_b[CARRY_B:CARRY_B + t, :] = proj(3 * D_A, D_B) * jax.nn.sigmoid(proj(3 * D_A + D_B, D_B))
    nb_ref[...] = hist_b[CARRY_B + t - (CONV_B_WIDTH - 1):CARRY_B + t, :]

    base = CARRY_B - (CONV_B_WIDTH - 1)
    for r0 in range(0, t, CONV_ROWS):
        for c0 in range(0, D_B, CONV_LANES):
            cols = slice(c0, c0 + CONV_LANES)
            out = None
            for j in range(SUBLANES):
                part = None
                for k in range(CONV_B_WIDTH):
                    if (base + k) % SUBLANES != j:
                        continue
                    o = r0 + (base + k - j)
                    term = cbw_ref[k:k + 1, cols] * hist_b[o:o + CONV_ROWS + SUBLANES, cols]
                    part = term if part is None else part + term
                shifted = part[j:j + CONV_ROWS, :]
                out = shifted if out is None else out + shifted
            conv_b_ref[r0:r0 + CONV_ROWS, cols] = out
    yb_in = _ln_silu(conv_b_ref[...], cbb_ref[...], lng_ref[...], lnb_ref[...]).astype(BF16)

    hist_a[0:CARRY_A, :] = hist_a[t:t + CARRY_A, :]
    hist_b[0:CARRY_B, :] = hist_b[t:t + CARRY_B, :]

    _mixer_tail(x, hb, mod, b_a_conv, yb_in, g2_ref[...], w_in_ref, woa_ref, wob_ref, wo_ref,
                wr_ref, br_ref, x1_ref, h2_ref, route_ref, cnt_ref)


def _mixer_sample_kernel(x_ref, mod_ref, g1_ref, g2_ref, w_in_ref, caw_ref, woa_ref, cbw_ref, cbb_ref,
                         lng_ref, lnb_ref, wob_ref, wo_ref, wr_ref, br_ref, sa_ref, sb_ref,
                         x1_ref, h2_ref, route_ref, cnt_ref, na_ref, nb_ref, yb_in_ref):
    n_seq = mod_ref.shape[0]
    steps = x_ref.shape[0] // n_seq
    x = x_ref[...]
    mod = mod_ref[...]
    h = _affine(_rms(x, g1_ref[...]), 1.0 + mod[:, D_MODEL:2 * D_MODEL], mod[:, 0:D_MODEL])
    hb = h.astype(BF16)

    def proj(lo, width):
        return _dot(hb, w_in_ref[:, lo:lo + width])

    def slab(v, j):
        return v[j * n_seq:(j + 1) * n_seq, :]

    u_a = proj(D_A, D_A) * proj(2 * D_A, D_A)
    full_a = [sa_ref[j] for j in range(CONV_A_WIDTH - 1)] + [slab(u_a, j) for j in range(steps)]
    conv_a = []
    for s in range(steps):
        acc = jnp.zeros((n_seq, D_A), F32)
        for k in range(CONV_A_WIDTH):
            acc = acc + caw_ref[k:k + 1, :] * full_a[s + k]
        conv_a.append(acc)
    b_a_conv = proj(0, D_A) * jnp.concatenate(conv_a, axis=0)
    for j in range(CONV_A_WIDTH - 1):
        na_ref[j] = full_a[steps + j]

    u_b = proj(3 * D_A, D_B) * jax.nn.sigmoid(proj(3 * D_A + D_B, D_B))
    hist = CONV_B_WIDTH - 1

    def full_b(j):
        return sb_ref[j] if j < hist else slab(u_b, j - hist)

    bias, lng, lnb = cbb_ref[...], lng_ref[...], lnb_ref[...]
    for s in range(steps):
        acc = jnp.zeros((n_seq, D_B), F32)
        for k in range(CONV_B_WIDTH):
            acc = acc + cbw_ref[k:k + 1, :] * full_b(s + k)
        yb_in_ref[s * n_seq:(s + 1) * n_seq, :] = _ln_silu(acc, bias, lng, lnb).astype(BF16)
    for j in range(hist):
        nb_ref[j] = full_b(steps + j)

    _mixer_tail(x, hb, mod, b_a_conv, yb_in_ref[...], g2_ref[...], w_in_ref, woa_ref, wob_ref, wo_ref,
                wr_ref, br_ref, x1_ref, h2_ref, route_ref, cnt_ref)


def _moe_kernel(cnt_ref, x1_ref, h2_ref, route_ref, gt2_ref, w1_ref, w3_ref, w2_ref, gf_ref, y_ref,
                h2s_ref, cs_ref, acc_ref, *, grid_rank):
    t = h2_ref.shape[0]
    tp = h2s_ref.shape[0]
    tile = pl.program_id(0)
    for axis in range(1, grid_rank):
        tile = tile * pl.num_programs(axis) + pl.program_id(axis)

    route = route_ref[...]
    lane_i = lax.broadcasted_iota(jnp.int32, route.shape, 1)
    dest_row = jnp.transpose(route)[EXPERTS_PER_GROUP:EXPERTS_PER_GROUP + 1, :]
    slot = lax.broadcasted_iota(jnp.int32, (tp, t), 0).astype(F32)
    p_sort = jnp.where(slot == dest_row, 1.0, 0.0).astype(BF16)
    h2s_ref[...] = _dot(p_sort, h2_ref[...]).astype(BF16)
    comb = jnp.where(lane_i < EXPERTS_PER_GROUP, route, 0.0)
    c_hi = comb.astype(BF16).astype(F32)
    c_mid = (comb - c_hi).astype(BF16).astype(F32)
    c_lo = comb - c_hi - c_mid
    packed = (c_hi + pltpu.roll(c_mid, EXPERTS_PER_GROUP, axis=1)
              + pltpu.roll(c_lo, 2 * EXPERTS_PER_GROUP, axis=1)).astype(BF16)
    moved = _dot(p_sort, packed)
    cs_ref[...] = (moved + pltpu.roll(moved, LANES - EXPERTS_PER_GROUP, axis=1)
                   + pltpu.roll(moved, LANES - 2 * EXPERTS_PER_GROUP, axis=1))
    acc_ref[...] = jnp.zeros(acc_ref.shape, F32)

    start = 0
    for g in range(N_GROUPS):
        n = cnt_ref[tile * N_GROUPS + g]
        end = start + n

        def one_pass(p, carry, g=g, start=start, end=end):
            lo = start + p * MOE_ROWS
            off = pl.multiple_of(jnp.minimum(lo, tp - MOE_ROWS), SORT_ALIGN)
            rows = off + lax.broadcasted_iota(jnp.int32, (MOE_ROWS, 1), 0)
            valid = (rows >= lo) & (rows < end)
            hb = h2s_ref[pl.ds(off, MOE_ROWS), :]
            c4 = jnp.where(valid, cs_ref[pl.ds(off, MOE_ROWS), :], 0.0)
            experts = range(g * EXPERTS_PER_GROUP, (g + 1) * EXPERTS_PER_GROUP)
            gate = jnp.concatenate([_dot(hb, w1_ref[e]) for e in experts], axis=1)
            up = jnp.concatenate([_dot(hb, w3_ref[e]) for e in experts], axis=1)
            cexp = jnp.concatenate(
                [jnp.broadcast_to(c4[:, e:e + 1], (MOE_ROWS, D_EXPERT)) for e in range(EXPERTS_PER_GROUP)],
                axis=1)
            a = gate * jax.nn.sigmoid(gate) * up * cexp
            acc_ref[pl.ds(off, MOE_ROWS), :] += _dot(a.astype(BF16), w2_ref[g])
            return carry

        lax.fori_loop(0, lax.div(n + (MOE_ROWS - 1), MOE_ROWS), one_pass, 0)
        start = start + lax.shift_right_logical(n + (SORT_ALIGN - 1), SORT_SHIFT) * SORT_ALIGN

    dest_col = route_ref[:, EXPERTS_PER_GROUP:EXPERTS_PER_GROUP + 1]
    p_back = jnp.where(lax.broadcasted_iota(jnp.int32, (t, tp), 1).astype(F32) == dest_col,
                       1.0, 0.0).astype(BF16)
    y = acc_ref[...]
    y_hi = y.astype(BF16)
    y_lo = (y - y_hi.astype(F32)).astype(BF16)
    y_tok = _dot(p_back, y_hi) + _dot(p_back, y_lo)
    x2 = x1_ref[...] + _affine(y_tok, gt2_ref[...])
    y_ref[...] = _rms(x2, gf_ref[...])


def _mod_kernel(c_ref, w_ref, b_ref, o_ref):
    c = c_ref[...]
    a = (c * jax.nn.sigmoid(c)).astype(BF16)
    o_ref[...] = _dot(a, w_ref[...].astype(BF16)) + b_ref[...]


def _const_spec(shape):
    nd = len(shape)
    return pl.BlockSpec(shape, lambda *_: (0,) * nd, pipeline_mode=pl.Buffered(1))


def _modulation(c_all, w_ada, b_ada):
    n, d = c_all.shape
    n_out = w_ada.shape[1]
    return pl.pallas_call(
        _mod_kernel,
        grid=(n_out // D_MODEL,),
        in_specs=[pl.BlockSpec((n, d), lambda j: (0, 0)),
                  pl.BlockSpec((d, D_MODEL), lambda j: (0, j)),
                  pl.BlockSpec((1, D_MODEL), lambda j: (0, j))],
        out_specs=pl.BlockSpec((n, D_MODEL), lambda j: (0, j)),
        out_shape=jax.ShapeDtypeStruct((n, n_out), F32),
        compiler_params=pltpu.CompilerParams(dimension_semantics=("arbitrary",),
                                             vmem_limit_bytes=VMEM_LIMIT),
        name="modulation",
    )(c_all, w_ada, b_ada)


def _mixer_weight_specs(p):
    names = ("g1", "g2", "w_in", "caw", "woa", "cbw", "cbb", "lng", "lnb", "wob", "wo", "wr", "br")
    return [p[k] for k in names], [_const_spec(p[k].shape) for k in names]


def _mixer_prompt(x, mod, p):
    b, s, d = x.shape
    t = SEQ_TILE
    n_t = s // t
    weights, wspecs = _mixer_weight_specs(p)
    tile = lambda width: pl.BlockSpec((None, t, width), lambda i, j: (i, j, 0))
    per_seq = lambda rows, width: pl.BlockSpec((None, rows, width), lambda i, j: (i, 0, 0))
    return pl.pallas_call(
        _mixer_prompt_kernel,
        grid=(b, n_t),
        in_specs=[tile(d), per_seq(1, N_MOD * d)] + wspecs,
        out_specs=[tile(d), tile(d), tile(ROUTER_LANES),
                   pl.BlockSpec((None, 1, LANES), lambda i, j: (i * n_t + j, 0, 0)),
                   per_seq(CONV_A_WIDTH - 1, D_A), per_seq(CONV_B_WIDTH - 1, D_B)],
        out_shape=[jax.ShapeDtypeStruct((b, s, d), F32),
                   jax.ShapeDtypeStruct((b, s, d), BF16),
                   jax.ShapeDtypeStruct((b, s, ROUTER_LANES), F32),
                   jax.ShapeDtypeStruct((b * n_t, 1, LANES), jnp.int32),
                   jax.ShapeDtypeStruct((b, CONV_A_WIDTH - 1, D_A), F32),
                   jax.ShapeDtypeStruct((b, CONV_B_WIDTH - 1, D_B), F32)],
        scratch_shapes=[pltpu.VMEM((CARRY_A + t, D_A), F32),
                        pltpu.VMEM((CARRY_B + t + SUBLANES, D_B), F32),
                        pltpu.VMEM((t, D_B), F32)],
        compiler_params=pltpu.CompilerParams(dimension_semantics=("arbitrary", "arbitrary"),
                                             vmem_limit_bytes=VMEM_LIMIT),
        name="mixer_prompt",
    )(x, mod.reshape(b, 1, N_MOD * d), *weights)


def _mixer_sample(x_tm, mod, sa_tm, sb_tm, p):
    n, d = x_tm.shape
    weights, wspecs = _mixer_weight_specs(p)
    full = lambda a: pl.BlockSpec(a.shape, lambda i: (0,) * a.ndim)
    whole = lambda *shape: pl.BlockSpec(shape, lambda i: (0,) * len(shape))
    return pl.pallas_call(
        _mixer_sample_kernel,
        grid=(1,),
        in_specs=[full(x_tm), full(mod)] + wspecs + [full(sa_tm), full(sb_tm)],
        out_specs=[whole(n, d), whole(n, d), whole(n, ROUTER_LANES), whole(1, LANES),
                   full(sa_tm), full(sb_tm)],
        out_shape=[jax.ShapeDtypeStruct((n, d), F32),
                   jax.ShapeDtypeStruct((n, d), BF16),
                   jax.ShapeDtypeStruct((n, ROUTER_LANES), F32),
                   jax.ShapeDtypeStruct((1, LANES), jnp.int32),
                   jax.ShapeDtypeStruct(sa_tm.shape, F32),
                   jax.ShapeDtypeStruct(sb_tm.shape, F32)],
        scratch_shapes=[pltpu.VMEM((n, D_B), BF16)],
        compiler_params=pltpu.CompilerParams(dimension_semantics=("arbitrary",),
                                             vmem_limit_bytes=VMEM_LIMIT),
        name="mixer_sample",
    )(x_tm, mod, *weights, sa_tm, sb_tm)


def _moe(counts, x1, h2, route, gt2_src, gt2_spec, p, grid, tile, rows):
    weights = [p["w1"], p["w3"], p["w2"], p["gf"]]
    sorted_rows = rows + SORT_PAD
    grid_spec = pltpu.PrefetchScalarGridSpec(
        num_scalar_prefetch=1,
        grid=grid,
        in_specs=[tile(D_MODEL), tile(D_MODEL), tile(ROUTER_LANES), gt2_spec]
        + [_const_spec(w.shape) for w in weights],
        out_specs=tile(D_MODEL),
        scratch_shapes=[pltpu.VMEM((sorted_rows, D_MODEL), BF16),
                        pltpu.VMEM((sorted_rows, ROUTER_LANES), F32),
                        pltpu.VMEM((sorted_rows, D_MODEL), F32)],
    )
    return pl.pallas_call(
        functools.partial(_moe_kernel, grid_rank=len(grid)),
        grid_spec=grid_spec,
        out_shape=jax.ShapeDtypeStruct(x1.shape, F32),
        compiler_params=pltpu.CompilerParams(dimension_semantics=("arbitrary",) * len(grid),
                                             vmem_limit_bytes=VMEM_LIMIT),
        name="moe",
    )(counts, x1, h2, route, gt2_src, *weights)


def kernel(x_prompt, x_sample, c_prompt, c_sample, state_conv_a, state_conv_b, w_ada, b_ada, norm1_g,
           norm2_g, w_in, conv_a_w, w_out_a, conv_b_w, conv_b_bias, ln_b_g, ln_b_b, w_out_b, w_o,
           w_group, b_group, w_expert, b_expert, w1, w3, w2, final_norm_g):
    depth = w_ada.shape[0]
    assert depth == 1, "the MoE kernel fuses the final norm, so exactly one layer is supported"
    n_p, seq, d = x_prompt.shape
    n_s, steps, _ = x_sample.shape
    c_all = jnp.concatenate([c_prompt, c_sample], axis=0)
    xp = x_prompt
    xs = x_sample.transpose(1, 0, 2).reshape(steps * n_s, d)
    gf = final_norm_g.reshape(1, d)
    pa, pb, sa, sb = [], [], [], []
    for l in range(depth):
        pad = ROUTER_LANES - N_EXPERTS - N_GROUPS
        wr = jnp.pad(jnp.concatenate([w_expert[l], w_group[l]], axis=1), ((0, 0), (0, pad)))
        br = jnp.pad(jnp.concatenate([b_expert[l], b_group[l]]), (0, pad)).reshape(1, ROUTER_LANES)
        p = {
            "g1": norm1_g[l].reshape(1, d), "g2": norm2_g[l].reshape(1, d),
            "w_in": w_in[l].astype(BF16), "caw": conv_a_w[l], "woa": w_out_a[l].astype(BF16),
            "cbw": conv_b_w[l], "cbb": conv_b_bias[l].reshape(1, D_B),
            "lng": ln_b_g[l].reshape(1, D_B), "lnb": ln_b_b[l].reshape(1, D_B),
            "wob": w_out_b[l].astype(BF16), "wo": w_o[l].astype(BF16), "wr": wr.astype(BF16), "br": br,
            "w1": w1[l].astype(BF16), "w3": w3[l].astype(BF16),
            "w2": w2[l].astype(BF16).reshape(N_GROUPS, EXPERTS_PER_GROUP * D_EXPERT, d), "gf": gf,
        }
        mod = _modulation(c_all, w_ada[l], b_ada[l].reshape(1, N_MOD * d))
        mod_p, mod_s = mod[:n_p], mod[n_p:]

        x1p, h2p, route_p, cnt_p, na_p, nb_p = _mixer_prompt(xp, mod_p, p)
        t = SEQ_TILE
        xp = _moe(cnt_p[:, 0, :N_GROUPS].reshape(-1), x1p, h2p, route_p,
                  mod_p.reshape(n_p, 1, N_MOD * d),
                  pl.BlockSpec((None, 1, d), lambda i, j, c: (i, 0, N_MOD - 1)), p, (n_p, seq // t),
                  lambda width: pl.BlockSpec((None, t, width), lambda i, j, c: (i, j, 0)), t)

        sa_tm = state_conv_a[l].transpose(1, 0, 2)
        sb_tm = state_conv_b[l].transpose(1, 0, 2)
        x1s, h2s, route_s, cnt_s, na_s, nb_s = _mixer_sample(xs, mod_s, sa_tm, sb_tm, p)
        xs = _moe(cnt_s[0, :N_GROUPS], x1s, h2s, route_s, mod_s,
                  pl.BlockSpec((n_s, d), lambda i, c: (0, N_MOD - 1)), p, (1,),
                  lambda width: pl.BlockSpec((steps * n_s, width), lambda i, c: (0, 0)), steps * n_s)
        pa.append(na_p)
        pb.append(nb_p)
        sa.append(na_s.transpose(1, 0, 2))
        sb.append(nb_s.transpose(1, 0, 2))
    y_sample = xs.reshape(steps, n_s, d).transpose(1, 0, 2)
    return (xp, y_sample, jnp.stack(pa), jnp.stack(pb), jnp.stack(sa), jnp.stack(sb))
```

```python
import functools

import jax
import jax.numpy as jnp
from jax import lax
from jax.experimental import pallas as pl
from jax.experimental.pallas import tpu as pltpu

D_MODEL = 1024
D_A = 512
D_B = 512
CONV_A_WIDTH = 3
CONV_B_WIDTH = 31
N_GROUPS = 4
EXPERTS_PER_GROUP = 4
N_EXPERTS = 16
D_EXPERT = 256
N_MOD = 6
EPS = 1e-6
D_IN = 3 * D_A + 2 * D_B + 2 * D_MODEL

LANES = 128
SUBLANES = 8
ROUTER_LANES = LANES
CARRY_A = SUBLANES
CARRY_B = 4 * SUBLANES
SEQ_TILE = 512
CONV_ROWS = 64
CONV_LANES = 2 * LANES
SORT_SHIFT = 4
SORT_ALIGN = 1 << SORT_SHIFT
SORT_PAD = LANES
MOE_PASS_ROWS = (128, 160, 192, 256)
VMEM_LIMIT = 52 * 1024 * 1024

BF16 = jnp.bfloat16
F32 = jnp.float32


def _dot(a, b):
    return jnp.dot(a, b, preferred_element_type=F32)


def _rms(x, g):
    return x * lax.rsqrt(jnp.mean(x * x, axis=-1, keepdims=True) + EPS) * g


def _affine(v, scale, shift=None):
    r = scale.shape[0]
    if r != 1 and r != v.shape[0]:
        v3 = v.reshape(v.shape[0] // r, r, v.shape[-1])
        shift3 = None if shift is None else shift[None]
        return _affine(v3, scale[None], shift3).reshape(v.shape)
    out = v * scale
    return out if shift is None else out + shift


def _route(logits):
    t = logits.shape[0]
    lane_i = lax.broadcasted_iota(jnp.int32, logits.shape, 1)
    lane = lane_i.astype(F32)
    neg = -jnp.inf
    is_g = (lane_i >= N_EXPERTS) & (lane_i < N_EXPERTS + N_GROUPS)
    gl = jnp.where(is_g, logits, neg)
    gmax = jnp.max(gl, axis=-1, keepdims=True)
    g_lane = jnp.min(jnp.where(gl == gmax, lane, float(LANES)), axis=-1, keepdims=True)
    p_g = 1.0 / jnp.sum(jnp.exp(gl - gmax), axis=-1, keepdims=True)
    g_idx = g_lane - float(N_EXPERTS)
    el = logits
    for g in range(1, N_GROUPS):
        el = jnp.where(g_idx == float(g), pltpu.roll(logits, LANES - EXPERTS_PER_GROUP * g, axis=1), el)
    el = jnp.where(lane_i < EXPERTS_PER_GROUP, el, neg)
    v1 = jnp.max(el, axis=-1, keepdims=True)
    i1 = jnp.min(jnp.where(el == v1, lane, float(LANES)), axis=-1, keepdims=True)
    el2 = jnp.where(lane == i1, neg, el)
    v2 = jnp.max(el2, axis=-1, keepdims=True)
    i2 = jnp.min(jnp.where(el2 == v2, lane, float(LANES)), axis=-1, keepdims=True)
    e21 = jnp.exp(v2 - v1)
    wt1 = p_g / (1.0 + e21)
    wt2 = p_g * e21 / (1.0 + e21)
    comb = jnp.where(lane == i1, wt1, jnp.where(lane == i2, wt2, 0.0))
    onehot = jnp.where(lane == g_idx, 1.0, 0.0)
    earlier = lax.broadcasted_iota(jnp.int32, (t, t), 1) < lax.broadcasted_iota(jnp.int32, (t, t), 0)
    before = _dot(jnp.where(earlier, 1.0, 0.0).astype(BF16), onehot.astype(BF16))
    rank = jnp.sum(before * onehot, axis=-1, keepdims=True)
    count = jnp.sum(onehot, axis=0, keepdims=True)
    padded = jnp.ceil(count / SORT_ALIGN) * SORT_ALIGN
    lane1 = lax.broadcasted_iota(jnp.int32, (1, LANES), 1)
    start = jnp.zeros((1, LANES), F32)
    for g in range(N_GROUPS - 1):
        size_g = jnp.sum(jnp.where(lane1 == g, padded, 0.0), axis=-1, keepdims=True)
        start = start + jnp.where(lane1 > g, size_g, 0.0)
    dest = jnp.sum(onehot * start, axis=-1, keepdims=True) + rank
    return jnp.where(lane_i < EXPERTS_PER_GROUP, comb, dest), count


def _ln_silu(v, bias, g, b):
    v = v + bias
    mu = jnp.mean(v, axis=-1, keepdims=True)
    d = v - mu
    var = jnp.mean(d * d, axis=-1, keepdims=True)
    y = d * lax.rsqrt(var + EPS) * g + b
    return y * jax.nn.sigmoid(y)


def _mixer_tail(x, hb, mod, b_a_conv, yb_in, g2, w_in_ref, woa_ref, wob_ref, wo_ref, wr_ref, br_ref,
                x1_ref, h2_ref, route_ref, cnt_ref):
    y_a = _dot(b_a_conv.astype(BF16), woa_ref[...])
    y_b = _dot(yb_in, wob_ref[...])
    o = 3 * D_A + 2 * D_B
    mg_a = _dot(hb, w_in_ref[:, o:o + D_MODEL])
    mg_b = _dot(hb, w_in_ref[:, o + D_MODEL:o + 2 * D_MODEL])
    merged = jax.nn.sigmoid(mg_a) * y_a + jax.nn.sigmoid(mg_b) * y_b
    gt1 = mod[:, 2 * D_MODEL:3 * D_MODEL]
    x1 = x + _affine(_dot(merged.astype(BF16), wo_ref[...]), gt1)
    sh2 = mod[:, 3 * D_MODEL:4 * D_MODEL]
    sc2 = mod[:, 4 * D_MODEL:5 * D_MODEL]
    h2 = _affine(_rms(x1, g2), 1.0 + sc2, sh2).astype(BF16)
    logits = _dot(h2, wr_ref[...]) + br_ref[...]
    x1_ref[...] = x1
    h2_ref[...] = h2
    route, count = _route(logits)
    route_ref[...] = route
    cnt_ref[...] = count.astype(jnp.int32)


def _mixer_prompt_kernel(x_ref, mod_ref, g1_ref, g2_ref, w_in_ref, caw_ref, woa_ref, cbw_ref, cbb_ref,
                         lng_ref, lnb_ref, wob_ref, wo_ref, wr_ref, br_ref,
                         x1_ref, h2_ref, route_ref, cnt_ref, na_ref, nb_ref, hist_a, hist_b, conv_b_ref):
    t = x_ref.shape[0]

    @pl.when(pl.program_id(1) == 0)
    def _():
        hist_a[0:CARRY_A, :] = jnp.zeros((CARRY_A, D_A), F32)
        hist_b[0:CARRY_B, :] = jnp.zeros((CARRY_B, D_B), F32)
        hist_b[CARRY_B + t:CARRY_B + t + SUBLANES, :] = jnp.zeros((SUBLANES, D_B), F32)

    x = x_ref[...]
    mod = mod_ref[...]
    h = _affine(_rms(x, g1_ref[...]), 1.0 + mod[:, D_MODEL:2 * D_MODEL], mod[:, 0:D_MODEL])
    hb = h.astype(BF16)

    def proj(lo, width):
        return _dot(hb, w_in_ref[:, lo:lo + width])

    hist_a[CARRY_A:CARRY_A + t, :] = proj(D_A, D_A) * proj(2 * D_A, D_A)
    conv_a = jnp.zeros((t, D_A), F32)
    for k in range(CONV_A_WIDTH):
        o = CARRY_A - (CONV_A_WIDTH - 1) + k
        conv_a = conv_a + caw_ref[k:k + 1, :] * hist_a[o:o + t, :]
    b_a_conv = proj(0, D_A) * conv_a
    na_ref[...] = hist_a[CARRY_A + t - (CONV_A_WIDTH - 1):CARRY_A + t, :]

    hist_b[CARRY_B:CARRY_B + t, :] = proj(3 * D_A, D_B) * jax.nn.sigmoid(proj(3 * D_A + D_B, D_B))
    nb_ref[...] = hist_b[CARRY_B + t - (CONV_B_WIDTH - 1):CARRY_B + t, :]

    base = CARRY_B - (CONV_B_WIDTH - 1)
    for r0 in range(0, t, CONV_ROWS):
        for c0 in range(0, D_B, CONV_LANES):
            cols = slice(c0, c0 + CONV_LANES)
            out = None
            for j in range(SUBLANES):
                part = None
                for k in range(CONV_B_WIDTH):
                    if (base + k) % SUBLANES != j:
                        continue
                    o = r0 + (base + k - j)
                    term = cbw_ref[k:k + 1, cols] * hist_b[o:o + CONV_ROWS + SUBLANES, cols]
                    part = term if part is None else part + term
                shifted = part[j:j + CONV_ROWS, :]
                out = shifted if out is None else out + shifted
            conv_b_ref[r0:r0 + CONV_ROWS, cols] = out
    yb_in = _ln_silu(conv_b_ref[...], cbb_ref[...], lng_ref[...], lnb_ref[...]).astype(BF16)

    hist_a[0:CARRY_A, :] = hist_a[t:t + CARRY_A, :]
    hist_b[0:CARRY_B, :] = hist_b[t:t + CARRY_B, :]

    _mixer_tail(x, hb, mod, b_a_conv, yb_in, g2_ref[...], w_in_ref, woa_ref, wob_ref, wo_ref,
                wr_ref, br_ref, x1_ref, h2_ref, route_ref, cnt_ref)


def _mixer_sample_kernel(x_ref, mod_ref, g1_ref, g2_ref, w_in_ref, caw_ref, woa_ref, cbw_ref, cbb_ref,
                         lng_ref, lnb_ref, wob_ref, wo_ref, wr_ref, br_ref, sa_ref, sb_ref,
                         x1_ref, h2_ref, route_ref, cnt_ref, na_ref, nb_ref, yb_in_ref):
    n_seq = mod_ref.shape[0]
    steps = x_ref.shape[0] // n_seq
    x = x_ref[...]
    mod = mod_ref[...]
    h = _affine(_rms(x, g1_ref[...]), 1.0 + mod[:, D_MODEL:2 * D_MODEL], mod[:, 0:D_MODEL])
    hb = h.astype(BF16)

    def proj(lo, width):
        return _dot(hb, w_in_ref[:, lo:lo + width])

    def slab(v, j):
        return v[j * n_seq:(j + 1) * n_seq, :]

    u_a = proj(D_A, D_A) * proj(2 * D_A, D_A)
    full_a = [sa_ref[j] for j in range(CONV_A_WIDTH - 1)] + [slab(u_a, j) for j in range(steps)]
    conv_a = []
    for s in range(steps):
        acc = jnp.zeros((n_seq, D_A), F32)
        for k in range(CONV_A_WIDTH):
            acc = acc + caw_ref[k:k + 1, :] * full_a[s + k]
        conv_a.append(acc)
    b_a_conv = proj(0, D_A) * jnp.concatenate(conv_a, axis=0)
    for j in range(CONV_A_WIDTH - 1):
        na_ref[j] = full_a[steps + j]

    u_b = proj(3 * D_A, D_B) * jax.nn.sigmoid(proj(3 * D_A + D_B, D_B))
    hist = CONV_B_WIDTH - 1

    def full_b(j):
        return sb_ref[j] if j < hist else slab(u_b, j - hist)

    bias, lng, lnb = cbb_ref[...], lng_ref[...], lnb_ref[...]
    for s in range(steps):
        acc = jnp.zeros((n_seq, D_B), F32)
        for k in range(CONV_B_WIDTH):
            acc = acc + cbw_ref[k:k + 1, :] * full_b(s + k)
        yb_in_ref[s * n_seq:(s + 1) * n_seq, :] = _ln_silu(acc, bias, lng, lnb).astype(BF16)
    for j in range(hist):
        nb_ref[j] = full_b(steps + j)

    _mixer_tail(x, hb, mod, b_a_conv, yb_in_ref[...], g2_ref[...], w_in_ref, woa_ref, wob_ref, wo_ref,
                wr_ref, br_ref, x1_ref, h2_ref, route_ref, cnt_ref)


def _moe_kernel(cnt_ref, x1_ref, h2_ref, route_ref, gt2_ref, w1_ref, w3_ref, w2_ref, gf_ref, y_ref,
                h2s_ref, cs_ref, acc_ref, *, grid_rank):
    t = h2_ref.shape[0]
    tp = h2s_ref.shape[0]
    tile = pl.program_id(0)
    for axis in range(1, grid_rank):
        tile = tile * pl.num_programs(axis) + pl.program_id(axis)

    route = route_ref[...]
    lane_i = lax.broadcasted_iota(jnp.int32, route.shape, 1)
    dest_row = jnp.transpose(route)[EXPERTS_PER_GROUP:EXPERTS_PER_GROUP + 1, :]
    slot = lax.broadcasted_iota(jnp.int32, (tp, t), 0).astype(F32)
    p_sort = jnp.where(slot == dest_row, 1.0, 0.0).astype(BF16)
    h2s_ref[...] = _dot(p_sort, h2_ref[...]).astype(BF16)
    comb = jnp.where(lane_i < EXPERTS_PER_GROUP, route, 0.0)
    c_hi = comb.astype(BF16).astype(F32)
    c_mid = (comb - c_hi).astype(BF16).astype(F32)
    c_lo = comb - c_hi - c_mid
    packed = (c_hi + pltpu.roll(c_mid, EXPERTS_PER_GROUP, axis=1)
              + pltpu.roll(c_lo, 2 * EXPERTS_PER_GROUP, axis=1)).astype(BF16)
    moved = _dot(p_sort, packed)
    cs_ref[...] = (moved + pltpu.roll(moved, LANES - EXPERTS_PER_GROUP, axis=1)
                   + pltpu.roll(moved, LANES - 2 * EXPERTS_PER_GROUP, axis=1))
    acc_ref[...] = jnp.zeros(acc_ref.shape, F32)

    def expert_pass(rows, g, lo, end):
        off = pl.multiple_of(jnp.minimum(lo, tp - rows), SORT_ALIGN)
        row_id = off + lax.broadcasted_iota(jnp.int32, (rows, 1), 0)
        valid = (row_id >= lo) & (row_id < end)
        hb = h2s_ref[pl.ds(off, rows), :]
        c4 = jnp.where(valid, cs_ref[pl.ds(off, rows), :], 0.0)
        experts = [g * EXPERTS_PER_GROUP + e for e in range(EXPERTS_PER_GROUP)]
        gate = jnp.concatenate([_dot(hb, w1_ref[e]) for e in experts], axis=1)
        up = jnp.concatenate([_dot(hb, w3_ref[e]) for e in experts], axis=1)
        cexp = jnp.concatenate(
            [jnp.broadcast_to(c4[:, e:e + 1], (rows, D_EXPERT)) for e in range(EXPERTS_PER_GROUP)], axis=1)
        a = gate * jax.nn.sigmoid(gate) * up * cexp
        acc_ref[pl.ds(off, rows), :] += _dot(a.astype(BF16), w2_ref[g])

    def group(g, start):
        n = cnt_ref[tile * N_GROUPS + g]
        end = start + n

        def passes(rows):
            def run():
                def body(p, carry):
                    expert_pass(rows, g, start + p * rows, end)
                    return carry
                lax.fori_loop(0, lax.div(n + (rows - 1), rows), body, 0)
            return run

        size_class = sum((n > rows).astype(jnp.int32) for rows in MOE_PASS_ROWS[:-1])
        lax.switch(size_class, [passes(rows) for rows in MOE_PASS_ROWS])
        return start + lax.shift_right_logical(n + (SORT_ALIGN - 1), SORT_SHIFT) * SORT_ALIGN

    lax.fori_loop(0, N_GROUPS, group, jnp.int32(0))

    dest_col = route_ref[:, EXPERTS_PER_GROUP:EXPERTS_PER_GROUP + 1]
    p_back = jnp.where(lax.broadcasted_iota(jnp.int32, (t, tp), 1).astype(F32) == dest_col,
                       1.0, 0.0).astype(BF16)
    y = acc_ref[...]
    y_hi = y.astype(BF16)
    y_lo = (y - y_hi.astype(F32)).astype(BF16)
    y_tok = _dot(p_back, y_hi) + _dot(p_back, y_lo)
    x2 = x1_ref[...] + _affine(y_tok, gt2_ref[...])
    y_ref[...] = _rms(x2, gf_ref[...])


def _mod_kernel(c_ref, w_ref, b_ref, o_ref):
    c = c_ref[...]
    a = (c * jax.nn.sigmoid(c)).astype(BF16)
    o_ref[...] = _dot(a, w_ref[...].astype(BF16)) + b_ref[...]


def _const_spec(shape):
    nd = len(shape)
    return pl.BlockSpec(shape, lambda *_: (0,) * nd, pipeline_mode=pl.Buffered(1))


def _modulation(c_all, w_ada, b_ada):
    n, d = c_all.shape
    n_out = w_ada.shape[1]
    return pl.pallas_call(
        _mod_kernel,
        grid=(n_out // D_MODEL,),
        in_specs=[pl.BlockSpec((n, d), lambda j: (0, 0)),
                  pl.BlockSpec((d, D_MODEL), lambda j: (0, j)),
                  pl.BlockSpec((1, D_MODEL), lambda j: (0, j))],
        out_specs=pl.BlockSpec((n, D_MODEL), lambda j: (0, j)),
        out_shape=jax.ShapeDtypeStruct((n, n_out), F32),
        compiler_params=pltpu.CompilerParams(dimension_semantics=("arbitrary",),
                                             vmem_limit_bytes=VMEM_LIMIT),
        name="modulation",
    )(c_all, w_ada, b_ada)


def _mixer_weight_specs(p):
    names = ("g1", "g2", "w_in", "caw", "woa", "cbw", "cbb", "lng", "lnb", "wob", "wo", "wr", "br")
    return [p[k] for k in names], [_const_spec(p[k].shape) for k in names]


def _mixer_prompt(x, mod, p):
    b, s, d = x.shape
    t = SEQ_TILE
    n_t = s // t
    weights, wspecs = _mixer_weight_specs(p)
    tile = lambda width: pl.BlockSpec((None, t, width), lambda i, j: (i, j, 0))
    per_seq = lambda rows, width: pl.BlockSpec((None, rows, width), lambda i, j: (i, 0, 0))
    return pl.pallas_call(
        _mixer_prompt_kernel,
        grid=(b, n_t),
        in_specs=[tile(d), per_seq(1, N_MOD * d)] + wspecs,
        out_specs=[tile(d), tile(d), tile(ROUTER_LANES),
                   pl.BlockSpec((None, 1, LANES), lambda i, j: (i * n_t + j, 0, 0)),
                   per_seq(CONV_A_WIDTH - 1, D_A), per_seq(CONV_B_WIDTH - 1, D_B)],
        out_shape=[jax.ShapeDtypeStruct((b, s, d), F32),
                   jax.ShapeDtypeStruct((b, s, d), BF16),
                   jax.ShapeDtypeStruct((b, s, ROUTER_LANES), F32),
                   jax.ShapeDtypeStruct((b * n_t, 1, LANES), jnp.int32),
                   jax.ShapeDtypeStruct((b, CONV_A_WIDTH - 1, D_A), F32),
                   jax.ShapeDtypeStruct((b, CONV_B_WIDTH - 1, D_B), F32)],
        scratch_shapes=[pltpu.VMEM((CARRY_A + t, D_A), F32),
                        pltpu.VMEM((CARRY_B + t + SUBLANES, D_B), F32),
                        pltpu.VMEM((t, D_B), F32)],
        compiler_params=pltpu.CompilerParams(dimension_semantics=("arbitrary", "arbitrary"),
                                             vmem_limit_bytes=VMEM_LIMIT),
        name="mixer_prompt",
    )(x, mod.reshape(b, 1, N_MOD * d), *weights)


def _mixer_sample(x_tm, mod, sa_tm, sb_tm, p):
    n, d = x_tm.shape
    weights, wspecs = _mixer_weight_specs(p)
    full = lambda a: pl.BlockSpec(a.shape, lambda i: (0,) * a.ndim)
    whole = lambda *shape: pl.BlockSpec(shape, lambda i: (0,) * len(shape))
    return pl.pallas_call(
        _mixer_sample_kernel,
        grid=(1,),
        in_specs=[full(x_tm), full(mod)] + wspecs + [full(sa_tm), full(sb_tm)],
        out_specs=[whole(n, d), whole(n, d), whole(n, ROUTER_LANES), whole(1, LANES),
                   full(sa_tm), full(sb_tm)],
        out_shape=[jax.ShapeDtypeStruct((n, d), F32),
                   jax.ShapeDtypeStruct((n, d), BF16),
                   jax.ShapeDtypeStruct((n, ROUTER_LANES), F32),
                   jax.ShapeDtypeStruct((1, LANES), jnp.int32),
                   jax.ShapeDtypeStruct(sa_tm.shape, F32),
                   jax.ShapeDtypeStruct(sb_tm.shape, F32)],
        scratch_shapes=[pltpu.VMEM((n, D_B), BF16)],
        compiler_params=pltpu.CompilerParams(dimension_semantics=("arbitrary",),
                                             vmem_limit_bytes=VMEM_LIMIT),
        name="mixer_sample",
    )(x_tm, mod, *weights, sa_tm, sb_tm)


def _moe(counts, x1, h2, route, gt2_src, gt2_spec, p, grid, tile, rows):
    weights = [p["w1"], p["w3"], p["w2"], p["gf"]]
    sorted_rows = rows + SORT_PAD
    grid_spec = pltpu.PrefetchScalarGridSpec(
        num_scalar_prefetch=1,
        grid=grid,
        in_specs=[tile(D_MODEL), tile(D_MODEL), tile(ROUTER_LANES), gt2_spec]
        + [_const_spec(w.shape) for w in weights],
        out_specs=tile(D_MODEL),
        scratch_shapes=[pltpu.VMEM((sorted_rows, D_MODEL), BF16),
                        pltpu.VMEM((sorted_rows, ROUTER_LANES), F32),
                        pltpu.VMEM((sorted_rows, D_MODEL), F32)],
    )
    return pl.pallas_call(
        functools.partial(_moe_kernel, grid_rank=len(grid)),
        grid_spec=grid_spec,
        out_shape=jax.ShapeDtypeStruct(x1.shape, F32),
        compiler_params=pltpu.CompilerParams(dimension_semantics=("arbitrary",) * len(grid),
                                             vmem_limit_bytes=VMEM_LIMIT),
        name="moe",
    )(counts, x1, h2, route, gt2_src, *weights)


def kernel(x_prompt, x_sample, c_prompt, c_sample, state_conv_a, state_conv_b, w_ada, b_ada, norm1_g,
           norm2_g, w_in, conv_a_w, w_out_a, conv_b_w, conv_b_bias, ln_b_g, ln_b_b, w_out_b, w_o,
           w_group, b_group, w_expert, b_expert, w1, w3, w2, final_norm_g):
    depth = w_ada.shape[0]
    assert depth == 1, "the MoE kernel fuses the final norm, so exactly one layer is supported"
    n_p, seq, d = x_prompt.shape
    n_s, steps, _ = x_sample.shape
    c_all = jnp.concatenate([c_prompt, c_sample], axis=0)
    xp = x_prompt
    xs = x_sample.transpose(1, 0, 2).reshape(steps * n_s, d)
    gf = final_norm_g.reshape(1, d)
    pa, pb, sa, sb = [], [], [], []
    for l in range(depth):
        pad = ROUTER_LANES - N_EXPERTS - N_GROUPS
        wr = jnp.pad(jnp.concatenate([w_expert[l], w_group[l]], axis=1), ((0, 0), (0, pad)))
        br = jnp.pad(jnp.concatenate([b_expert[l], b_group[l]]), (0, pad)).reshape(1, ROUTER_LANES)
        p = {
            "g1": norm1_g[l].reshape(1, d), "g2": norm2_g[l].reshape(1, d),
            "w_in": w_in[l].astype(BF16), "caw": conv_a_w[l], "woa": w_out_a[l].astype(BF16),
            "cbw": conv_b_w[l], "cbb": conv_b_bias[l].reshape(1, D_B),
            "lng": ln_b_g[l].reshape(1, D_B), "lnb": ln_b_b[l].reshape(1, D_B),
            "wob": w_out_b[l].astype(BF16), "wo": w_o[l].astype(BF16), "wr": wr.astype(BF16), "br": br,
            "w1": w1[l].astype(BF16), "w3": w3[l].astype(BF16),
            "w2": w2[l].astype(BF16).reshape(N_GROUPS, EXPERTS_PER_GROUP * D_EXPERT, d), "gf": gf,
        }
        mod = _modulation(c_all, w_ada[l], b_ada[l].reshape(1, N_MOD * d))
        mod_p, mod_s = mod[:n_p], mod[n_p:]

        x1p, h2p, route_p, cnt_p, na_p, nb_p = _mixer_prompt(xp, mod_p, p)
        t = SEQ_TILE
        xp = _moe(cnt_p[:, 0, :N_GROUPS].reshape(-1), x1p, h2p, route_p,
                  mod_p.reshape(n_p, 1, N_MOD * d),
                  pl.BlockSpec((None, 1, d), lambda i, j, c: (i, 0, N_MOD - 1)), p, (n_p, seq // t),
                  lambda width: pl.BlockSpec((None, t, width), lambda i, j, c: (i, j, 0)), t)

        sa_tm = state_conv_a[l].transpose(1, 0, 2)
        sb_tm = state_conv_b[l].transpose(1, 0, 2)
        x1s, h2s, route_s, cnt_s, na_s, nb_s = _mixer_sample(xs, mod_s, sa_tm, sb_tm, p)
        xs = _moe(cnt_s[0, :N_GROUPS], x1s, h2s, route_s, mod_s,
                  pl.BlockSpec((n_s, d), lambda i, c: (0, N_MOD - 1)), p, (1,),
                  lambda width: pl.BlockSpec((steps * n_s, width), lambda i, c: (0, 0)), steps * n_s)
        pa.append(na_p)
        pb.append(nb_p)
        sa.append(na_s.transpose(1, 0, 2))
        sb.append(nb_s.transpose(1, 0, 2))
    y_sample = xs.reshape(steps, n_s, d).transpose(1, 0, 2)
    return (xp, y_sample, jnp.stack(pa), jnp.stack(pb), jnp.stack(sa), jnp.stack(sb))
```

```python
import functools

import jax
import jax.numpy as jnp
from jax import lax
from jax.experimental import pallas as pl
from jax.experimental.pallas import tpu as pltpu

D_MODEL = 1024
D_A = 512
D_B = 512
CONV_A_WIDTH = 3
CONV_B_WIDTH = 31
N_GROUPS = 4
EXPERTS_PER_GROUP = 4
N_EXPERTS = 16
D_EXPERT = 256
N_MOD = 6
EPS = 1e-6
D_IN = 3 * D_A + 2 * D_B + 2 * D_MODEL

LANES = 128
SUBLANES = 8
ROUTER_LANES = LANES
CARRY_A = SUBLANES
CARRY_B = 4 * SUBLANES
SEQ_TILE = 512
CONV_ROWS = 64
CONV_LANES = 2 * LANES
SORT_SHIFT = 4
SORT_ALIGN = 1 << SORT_SHIFT
SORT_PAD = LANES
MOE_PASS_ROWS = (128, 160, 192, 256)
VMEM_LIMIT = 52 * 1024 * 1024

BF16 = jnp.bfloat16
F32 = jnp.float32


def _dot(a, b):
    return jnp.dot(a, b, preferred_element_type=F32)


def _rms(x, g):
    return x * lax.rsqrt(jnp.mean(x * x, axis=-1, keepdims=True) + EPS) * g


def _affine(v, scale, shift=None):
    r = scale.shape[0]
    if r != 1 and r != v.shape[0]:
        v3 = v.reshape(v.shape[0] // r, r, v.shape[-1])
        shift3 = None if shift is None else shift[None]
        return _affine(v3, scale[None], shift3).reshape(v.shape)
    out = v * scale
    return out if shift is None else out + shift


def _route(logits):
    t = logits.shape[0]
    lane_i = lax.broadcasted_iota(jnp.int32, logits.shape, 1)
    lane = lane_i.astype(F32)
    neg = -jnp.inf
    is_g = (lane_i >= N_EXPERTS) & (lane_i < N_EXPERTS + N_GROUPS)
    gl = jnp.where(is_g, logits, neg)
    gmax = jnp.max(gl, axis=-1, keepdims=True)
    g_lane = jnp.min(jnp.where(gl == gmax, lane, float(LANES)), axis=-1, keepdims=True)
    p_g = 1.0 / jnp.sum(jnp.exp(gl - gmax), axis=-1, keepdims=True)
    g_idx = g_lane - float(N_EXPERTS)
    el = logits
    for g in range(1, N_GROUPS):
        el = jnp.where(g_idx == float(g), pltpu.roll(logits, LANES - EXPERTS_PER_GROUP * g, axis=1), el)
    el = jnp.where(lane_i < EXPERTS_PER_GROUP, el, neg)
    v1 = jnp.max(el, axis=-1, keepdims=True)
    i1 = jnp.min(jnp.where(el == v1, lane, float(LANES)), axis=-1, keepdims=True)
    el2 = jnp.where(lane == i1, neg, el)
    v2 = jnp.max(el2, axis=-1, keepdims=True)
    i2 = jnp.min(jnp.where(el2 == v2, lane, float(LANES)), axis=-1, keepdims=True)
    e21 = jnp.exp(v2 - v1)
    wt1 = p_g / (1.0 + e21)
    wt2 = p_g * e21 / (1.0 + e21)
    comb = jnp.where(lane == i1, wt1, jnp.where(lane == i2, wt2, 0.0))
    onehot = jnp.where(lane == g_idx, 1.0, 0.0)
    earlier = lax.broadcasted_iota(jnp.int32, (t, t), 1) < lax.broadcasted_iota(jnp.int32, (t, t), 0)
    before = _dot(jnp.where(earlier, 1.0, 0.0).astype(BF16), onehot.astype(BF16))
    rank = jnp.sum(before * onehot, axis=-1, keepdims=True)
    count = jnp.sum(onehot, axis=0, keepdims=True)
    padded = jnp.ceil(count / SORT_ALIGN) * SORT_ALIGN
    lane1 = lax.broadcasted_iota(jnp.int32, (1, LANES), 1)
    start = jnp.zeros((1, LANES), F32)
    for g in range(N_GROUPS - 1):
        size_g = jnp.sum(jnp.where(lane1 == g, padded, 0.0), axis=-1, keepdims=True)
        start = start + jnp.where(lane1 > g, size_g, 0.0)
    dest = jnp.sum(onehot * start, axis=-1, keepdims=True) + rank
    return jnp.where(lane_i < EXPERTS_PER_GROUP, comb, dest), count


def _ln_silu(v, bias, g, b):
    v = v + bias
    mu = jnp.mean(v, axis=-1, keepdims=True)
    d = v - mu
    var = jnp.mean(d * d, axis=-1, keepdims=True)
    y = d * lax.rsqrt(var + EPS) * g + b
    return y * jax.nn.sigmoid(y)


def _mixer_tail(x, hb, mod, b_a_conv, yb_in, g2, w_in_ref, woa_ref, wob_ref, wo_ref, wr_ref, br_ref,
                x1_ref, h2_ref, route_ref, cnt_ref):
    y_a = _dot(b_a_conv.astype(BF16), woa_ref[...])
    y_b = _dot(yb_in, wob_ref[...])
    o = 3 * D_A + 2 * D_B
    mg_a = _dot(hb, w_in_ref[:, o:o + D_MODEL])
    mg_b = _dot(hb, w_in_ref[:, o + D_MODEL:o + 2 * D_MODEL])
    merged = jax.nn.sigmoid(mg_a) * y_a + jax.nn.sigmoid(mg_b) * y_b
    gt1 = mod[:, 2 * D_MODEL:3 * D_MODEL]
    x1 = x + _affine(_dot(merged.astype(BF16), wo_ref[...]), gt1)
    sh2 = mod[:, 3 * D_MODEL:4 * D_MODEL]
    sc2 = mod[:, 4 * D_MODEL:5 * D_MODEL]
    h2 = _affine(_rms(x1, g2), 1.0 + sc2, sh2).astype(BF16)
    logits = _dot(h2, wr_ref[...]) + br_ref[...]
    x1_ref[...] = x1
    h2_ref[...] = h2
    route, count = _route(logits)
    route_ref[...] = route
    cnt_ref[...] = count.astype(jnp.int32)


def _mixer_prompt_kernel(x_ref, mod_ref, g1_ref, g2_ref, w_in_ref, caw_ref, woa_ref, cbw_ref, cbb_ref,
                         lng_ref, lnb_ref, wob_ref, wo_ref, wr_ref, br_ref,
                         w1f_ref, w3f_ref, w2f_ref,
                         x1_ref, h2_ref, route_ref, cnt_ref, na_ref, nb_ref, w1b_ref, w3b_ref, w2b_ref,
                         hist_a, hist_b, conv_b_ref):
    t = x_ref.shape[0]
    w1b_ref[...] = w1f_ref[...].astype(BF16)
    w3b_ref[...] = w3f_ref[...].astype(BF16)
    w2b_ref[...] = w2f_ref[...].astype(BF16)

    @pl.when(pl.program_id(1) == 0)
    def _():
        hist_a[0:CARRY_A, :] = jnp.zeros((CARRY_A, D_A), F32)
        hist_b[0:CARRY_B, :] = jnp.zeros((CARRY_B, D_B), F32)
        hist_b[CARRY_B + t:CARRY_B + t + SUBLANES, :] = jnp.zeros((SUBLANES, D_B), F32)

    x = x_ref[...]
    mod = mod_ref[...]
    h = _affine(_rms(x, g1_ref[...]), 1.0 + mod[:, D_MODEL:2 * D_MODEL], mod[:, 0:D_MODEL])
    hb = h.astype(BF16)

    def proj(lo, width):
        return _dot(hb, w_in_ref[:, lo:lo + width])

    hist_a[CARRY_A:CARRY_A + t, :] = proj(D_A, D_A) * proj(2 * D_A, D_A)
    conv_a = jnp.zeros((t, D_A), F32)
    for k in range(CONV_A_WIDTH):
        o = CARRY_A - (CONV_A_WIDTH - 1) + k
        conv_a = conv_a + caw_ref[k:k + 1, :] * hist_a[o:o + t, :]
    b_a_conv = proj(0, D_A) * conv_a
    na_ref[...] = hist_a[CARRY_A + t - (CONV_A_WIDTH - 1):CARRY_A + t, :]

    hist_b[CARRY_B:CARRY_B + t, :] = proj(3 * D_A, D_B) * jax.nn.sigmoid(proj(3 * D_A + D_B, D_B))
    nb_ref[...] = hist_b[CARRY_B + t - (CONV_B_WIDTH - 1):CARRY_B + t, :]

    base = CARRY_B - (CONV_B_WIDTH - 1)
    for r0 in range(0, t, CONV_ROWS):
        for c0 in range(0, D_B, CONV_LANES):
            cols = slice(c0, c0 + CONV_LANES)
            out = None
            for j in range(SUBLANES):
                part = None
                for k in range(CONV_B_WIDTH):
                    if (base + k) % SUBLANES != j:
                        continue
                    o = r0 + (base + k - j)
                    term = cbw_ref[k:k + 1, cols] * hist_b[o:o + CONV_ROWS + SUBLANES, cols]
                    part = term if part is None else part + term
                shifted = part[j:j + CONV_ROWS, :]
                out = shifted if out is None else out + shifted
            conv_b_ref[r0:r0 + CONV_ROWS, cols] = out
    yb_in = _ln_silu(conv_b_ref[...], cbb_ref[...], lng_ref[...], lnb_ref[...]).astype(BF16)

    hist_a[0:CARRY_A, :] = hist_a[t:t + CARRY_A, :]
    hist_b[0:CARRY_B, :] = hist_b[t:t + CARRY_B, :]

    _mixer_tail(x, hb, mod, b_a_conv, yb_in, g2_ref[...], w_in_ref, woa_ref, wob_ref, wo_ref,
                wr_ref, br_ref, x1_ref, h2_ref, route_ref, cnt_ref)


def _mixer_sample_kernel(x_ref, mod_ref, g1_ref, g2_ref, w_in_ref, caw_ref, woa_ref, cbw_ref, cbb_ref,
                         lng_ref, lnb_ref, wob_ref, wo_ref, wr_ref, br_ref, sa_ref, sb_ref,
                         x1_ref, h2_ref, route_ref, cnt_ref, na_ref, nb_ref, yb_in_ref):
    n_seq = mod_ref.shape[0]
    steps = x_ref.shape[0] // n_seq
    x = x_ref[...]
    mod = mod_ref[...]
    h = _affine(_rms(x, g1_ref[...]), 1.0 + mod[:, D_MODEL:2 * D_MODEL], mod[:, 0:D_MODEL])
    hb = h.astype(BF16)

    def proj(lo, width):
        return _dot(hb, w_in_ref[:, lo:lo + width])

    def slab(v, j):
        return v[j * n_seq:(j + 1) * n_seq, :]

    u_a = proj(D_A, D_A) * proj(2 * D_A, D_A)
    full_a = [sa_ref[j] for j in range(CONV_A_WIDTH - 1)] + [slab(u_a, j) for j in range(steps)]
    conv_a = []
    for s in range(steps):
        acc = jnp.zeros((n_seq, D_A), F32)
        for k in range(CONV_A_WIDTH):
            acc = acc + caw_ref[k:k + 1, :] * full_a[s + k]
        conv_a.append(acc)
    b_a_conv = proj(0, D_A) * jnp.concatenate(conv_a, axis=0)
    for j in range(CONV_A_WIDTH - 1):
        na_ref[j] = full_a[steps + j]

    u_b = proj(3 * D_A, D_B) * jax.nn.sigmoid(proj(3 * D_A + D_B, D_B))
    hist = CONV_B_WIDTH - 1

    def full_b(j):
        return sb_ref[j] if j < hist else slab(u_b, j - hist)

    bias, lng, lnb = cbb_ref[...], lng_ref[...], lnb_ref[...]
    for s in range(steps):
        acc = jnp.zeros((n_seq, D_B), F32)
        for k in range(CONV_B_WIDTH):
            acc = acc + cbw_ref[k:k + 1, :] * full_b(s + k)
        yb_in_ref[s * n_seq:(s + 1) * n_seq, :] = _ln_silu(acc, bias, lng, lnb).astype(BF16)
    for j in range(hist):
        nb_ref[j] = full_b(steps + j)

    _mixer_tail(x, hb, mod, b_a_conv, yb_in_ref[...], g2_ref[...], w_in_ref, woa_ref, wob_ref, wo_ref,
                wr_ref, br_ref, x1_ref, h2_ref, route_ref, cnt_ref)


def _moe_kernel(cnt_ref, x1_ref, h2_ref, route_ref, gt2_ref, w1_ref, w3_ref, w2_ref, gf_ref, y_ref,
                h2s_ref, cs_ref, acc_ref, *, grid_rank):
    t = h2_ref.shape[0]
    tp = h2s_ref.shape[0]
    tile = pl.program_id(0)
    for axis in range(1, grid_rank):
        tile = tile * pl.num_programs(axis) + pl.program_id(axis)

    route = route_ref[...]
    lane_i = lax.broadcasted_iota(jnp.int32, route.shape, 1)
    dest_row = jnp.transpose(route)[EXPERTS_PER_GROUP:EXPERTS_PER_GROUP + 1, :]
    slot = lax.broadcasted_iota(jnp.int32, (tp, t), 0).astype(F32)
    p_sort = jnp.where(slot == dest_row, 1.0, 0.0).astype(BF16)
    h2s_ref[...] = _dot(p_sort, h2_ref[...]).astype(BF16)
    comb = jnp.where(lane_i < EXPERTS_PER_GROUP, route, 0.0)
    c_hi = comb.astype(BF16).astype(F32)
    c_mid = (comb - c_hi).astype(BF16).astype(F32)
    c_lo = comb - c_hi - c_mid
    packed = (c_hi + pltpu.roll(c_mid, EXPERTS_PER_GROUP, axis=1)
              + pltpu.roll(c_lo, 2 * EXPERTS_PER_GROUP, axis=1)).astype(BF16)
    moved = _dot(p_sort, packed)
    cs_ref[...] = (moved + pltpu.roll(moved, LANES - EXPERTS_PER_GROUP, axis=1)
                   + pltpu.roll(moved, LANES - 2 * EXPERTS_PER_GROUP, axis=1))
    acc_ref[...] = jnp.zeros(acc_ref.shape, F32)

    def expert_pass(rows, g, lo, end):
        off = pl.multiple_of(jnp.minimum(lo, tp - rows), SORT_ALIGN)
        row_id = off + lax.broadcasted_iota(jnp.int32, (rows, 1), 0)
        valid = (row_id >= lo) & (row_id < end)
        hb = h2s_ref[pl.ds(off, rows), :]
        c4 = jnp.where(valid, cs_ref[pl.ds(off, rows), :], 0.0)
        experts = [g * EXPERTS_PER_GROUP + e for e in range(EXPERTS_PER_GROUP)]
        gate = jnp.concatenate([_dot(hb, w1_ref[e]) for e in experts], axis=1)
        up = jnp.concatenate([_dot(hb, w3_ref[e]) for e in experts], axis=1)
        cexp = jnp.concatenate(
            [jnp.broadcast_to(c4[:, e:e + 1], (rows, D_EXPERT)) for e in range(EXPERTS_PER_GROUP)], axis=1)
        a = gate * jax.nn.sigmoid(gate) * up * cexp
        acc_ref[pl.ds(off, rows), :] += _dot(a.astype(BF16), w2_ref[g])

    def group(g, start):
        n = cnt_ref[tile * N_GROUPS + g]
        end = start + n

        def passes(rows):
            def run():
                def body(p, carry):
                    expert_pass(rows, g, start + p * rows, end)
                    return carry
                lax.fori_loop(0, lax.div(n + (rows - 1), rows), body, 0)
            return run

        size_class = sum((n > rows).astype(jnp.int32) for rows in MOE_PASS_ROWS[:-1])
        lax.switch(size_class, [passes(rows) for rows in MOE_PASS_ROWS])
        return start + lax.shift_right_logical(n + (SORT_ALIGN - 1), SORT_SHIFT) * SORT_ALIGN

    lax.fori_loop(0, N_GROUPS, group, jnp.int32(0))

    dest_col = route_ref[:, EXPERTS_PER_GROUP:EXPERTS_PER_GROUP + 1]
    p_back = jnp.where(lax.broadcasted_iota(jnp.int32, (t, tp), 1).astype(F32) == dest_col,
                       1.0, 0.0).astype(BF16)
    y = acc_ref[...]
    y_hi = y.astype(BF16)
    y_lo = (y - y_hi.astype(F32)).astype(BF16)
    y_tok = _dot(p_back, y_hi) + _dot(p_back, y_lo)
    x2 = x1_ref[...] + _affine(y_tok, gt2_ref[...])
    y_ref[...] = _rms(x2, gf_ref[...])


def _mod_kernel(c_ref, w_ref, b_ref, op_ref, os_ref):
    c = c_ref[...]
    a = (c * jax.nn.sigmoid(c)).astype(BF16)
    mod = _dot(a, w_ref[...].astype(BF16)) + b_ref[...]
    n_p = op_ref.shape[0]
    op_ref[...] = mod[:n_p]
    os_ref[...] = mod[n_p:]


def _const_spec(shape):
    nd = len(shape)
    return pl.BlockSpec(shape, lambda *_: (0,) * nd, pipeline_mode=pl.Buffered(1))


def _modulation(c_all, n_p, w_ada, b_ada):
    n, d = c_all.shape
    n_out = w_ada.shape[1]
    assert n_p % SUBLANES == 0
    return pl.pallas_call(
        _mod_kernel,
        grid=(n_out // D_MODEL,),
        in_specs=[pl.BlockSpec((n, d), lambda j: (0, 0)),
                  pl.BlockSpec((d, D_MODEL), lambda j: (0, j)),
                  pl.BlockSpec((1, D_MODEL), lambda j: (0, j))],
        out_specs=[pl.BlockSpec((n_p, D_MODEL), lambda j: (0, j)),
                   pl.BlockSpec((n - n_p, D_MODEL), lambda j: (0, j))],
        out_shape=[jax.ShapeDtypeStruct((n_p, n_out), F32), jax.ShapeDtypeStruct((n - n_p, n_out), F32)],
        compiler_params=pltpu.CompilerParams(dimension_semantics=("arbitrary",),
                                             vmem_limit_bytes=VMEM_LIMIT),
        name="modulation",
    )(c_all, w_ada, b_ada)


def _mixer_weight_specs(p):
    names = ("g1", "g2", "w_in", "caw", "woa", "cbw", "cbb", "lng", "lnb", "wob", "wo", "wr", "br")
    return [p[k] for k in names], [_const_spec(p[k].shape) for k in names]


def _mixer_prompt(x, mod, p, expert_weights):
    b, s, d = x.shape
    t = SEQ_TILE
    n_t = s // t
    tiles = b * n_t
    weights, wspecs = _mixer_weight_specs(p)
    tile = lambda width: pl.BlockSpec((None, t, width), lambda i, j: (i, j, 0))
    per_seq = lambda rows, width: pl.BlockSpec((None, rows, width), lambda i, j: (i, 0, 0))

    def cast_spec(w):
        experts, rows, cols = w.shape
        per_expert = tiles // experts
        assert per_expert * experts == tiles and rows % (per_expert * 2 * SUBLANES) == 0
        return pl.BlockSpec((1, rows // per_expert, cols),
                            lambda i, j: ((i * n_t + j) // per_expert, lax.rem(i * n_t + j, per_expert), 0))

    cast_specs = [cast_spec(w) for w in expert_weights]
    return pl.pallas_call(
        _mixer_prompt_kernel,
        grid=(b, n_t),
        in_specs=[tile(d), per_seq(1, N_MOD * d)] + wspecs + cast_specs,
        out_specs=[tile(d), tile(d), tile(ROUTER_LANES),
                   pl.BlockSpec((None, 1, LANES), lambda i, j: (i * n_t + j, 0, 0)),
                   per_seq(CONV_A_WIDTH - 1, D_A), per_seq(CONV_B_WIDTH - 1, D_B)] + cast_specs,
        out_shape=[jax.ShapeDtypeStruct((b, s, d), F32),
                   jax.ShapeDtypeStruct((b, s, d), BF16),
                   jax.ShapeDtypeStruct((b, s, ROUTER_LANES), F32),
                   jax.ShapeDtypeStruct((b * n_t, 1, LANES), jnp.int32),
                   jax.ShapeDtypeStruct((b, CONV_A_WIDTH - 1, D_A), F32),
                   jax.ShapeDtypeStruct((b, CONV_B_WIDTH - 1, D_B), F32)]
        + [jax.ShapeDtypeStruct(w.shape, BF16) for w in expert_weights],
        scratch_shapes=[pltpu.VMEM((CARRY_A + t, D_A), F32),
                        pltpu.VMEM((CARRY_B + t + SUBLANES, D_B), F32),
                        pltpu.VMEM((t, D_B), F32)],
        compiler_params=pltpu.CompilerParams(dimension_semantics=("arbitrary", "arbitrary"),
                                             vmem_limit_bytes=VMEM_LIMIT),
        name="mixer_prompt",
    )(x, mod.reshape(b, 1, N_MOD * d), *weights, *expert_weights)


def _mixer_sample(x_tm, mod, sa_tm, sb_tm, p):
    n, d = x_tm.shape
    weights, wspecs = _mixer_weight_specs(p)
    full = lambda a: pl.BlockSpec(a.shape, lambda i: (0,) * a.ndim)
    whole = lambda *shape: pl.BlockSpec(shape, lambda i: (0,) * len(shape))
    return pl.pallas_call(
        _mixer_sample_kernel,
        grid=(1,),
        in_specs=[full(x_tm), full(mod)] + wspecs + [full(sa_tm), full(sb_tm)],
        out_specs=[whole(n, d), whole(n, d), whole(n, ROUTER_LANES), whole(1, LANES),
                   full(sa_tm), full(sb_tm)],
        out_shape=[jax.ShapeDtypeStruct((n, d), F32),
                   jax.ShapeDtypeStruct((n, d), BF16),
                   jax.ShapeDtypeStruct((n, ROUTER_LANES), F32),
                   jax.ShapeDtypeStruct((1, LANES), jnp.int32),
                   jax.ShapeDtypeStruct(sa_tm.shape, F32),
                   jax.ShapeDtypeStruct(sb_tm.shape, F32)],
        scratch_shapes=[pltpu.VMEM((n, D_B), BF16)],
        compiler_params=pltpu.CompilerParams(dimension_semantics=("arbitrary",),
                                             vmem_limit_bytes=VMEM_LIMIT),
        name="mixer_sample",
    )(x_tm, mod, *weights, sa_tm, sb_tm)


def _moe(counts, x1, h2, route, gt2_src, gt2_spec, p, grid, tile, rows):
    weights = [p["w1"], p["w3"], p["w2"], p["gf"]]
    sorted_rows = rows + SORT_PAD
    grid_spec = pltpu.PrefetchScalarGridSpec(
        num_scalar_prefetch=1,
        grid=grid,
        in_specs=[tile(D_MODEL), tile(D_MODEL), tile(ROUTER_LANES), gt2_spec]
        + [_const_spec(w.shape) for w in weights],
        out_specs=tile(D_MODEL),
        scratch_shapes=[pltpu.VMEM((sorted_rows, D_MODEL), BF16),
                        pltpu.VMEM((sorted_rows, ROUTER_LANES), F32),
                        pltpu.VMEM((sorted_rows, D_MODEL), F32)],
    )
    return pl.pallas_call(
        functools.partial(_moe_kernel, grid_rank=len(grid)),
        grid_spec=grid_spec,
        out_shape=jax.ShapeDtypeStruct(x1.shape, F32),
        compiler_params=pltpu.CompilerParams(dimension_semantics=("arbitrary",) * len(grid),
                                             vmem_limit_bytes=VMEM_LIMIT),
        name="moe",
    )(counts, x1, h2, route, gt2_src, *weights)


def kernel(x_prompt, x_sample, c_prompt, c_sample, state_conv_a, state_conv_b, w_ada, b_ada, norm1_g,
           norm2_g, w_in, conv_a_w, w_out_a, conv_b_w, conv_b_bias, ln_b_g, ln_b_b, w_out_b, w_o,
           w_group, b_group, w_expert, b_expert, w1, w3, w2, final_norm_g):
    depth = w_ada.shape[0]
    assert depth == 1, "the MoE kernel fuses the final norm, so exactly one layer is supported"
    n_p, seq, d = x_prompt.shape
    n_s, steps, _ = x_sample.shape
    c_all = jnp.concatenate([c_prompt, c_sample], axis=0)
    xp = x_prompt
    xs = x_sample.transpose(1, 0, 2).reshape(steps * n_s, d)
    gf = final_norm_g.reshape(1, d)
    pa, pb, sa, sb = [], [], [], []
    for l in range(depth):
        pad = ROUTER_LANES - N_EXPERTS - N_GROUPS
        wr = jnp.pad(jnp.concatenate([w_expert[l], w_group[l]], axis=1), ((0, 0), (0, pad)))
        br = jnp.pad(jnp.concatenate([b_expert[l], b_group[l]]), (0, pad)).reshape(1, ROUTER_LANES)
        p = {
            "g1": norm1_g[l].reshape(1, d), "g2": norm2_g[l].reshape(1, d),
            "w_in": w_in[l].astype(BF16), "caw": conv_a_w[l], "woa": w_out_a[l].astype(BF16),
            "cbw": conv_b_w[l], "cbb": conv_b_bias[l].reshape(1, D_B),
            "lng": ln_b_g[l].reshape(1, D_B), "lnb": ln_b_b[l].reshape(1, D_B),
            "wob": w_out_b[l].astype(BF16), "wo": w_o[l].astype(BF16), "wr": wr.astype(BF16), "br": br,
            "gf": gf,
        }
        mod_p, mod_s = _modulation(c_all, n_p, w_ada[l], b_ada[l].reshape(1, N_MOD * d))

        x1p, h2p, route_p, cnt_p, na_p, nb_p, w1b, w3b, w2b = _mixer_prompt(
            xp, mod_p, p, (w1[l], w3[l], w2[l]))
        p.update(w1=w1b, w3=w3b, w2=w2b.reshape(N_GROUPS, EXPERTS_PER_GROUP * D_EXPERT, d))
        t = SEQ_TILE
        xp = _moe(cnt_p[:, 0, :N_GROUPS].reshape(-1), x1p, h2p, route_p,
                  mod_p.reshape(n_p, 1, N_MOD * d),
                  pl.BlockSpec((None, 1, d), lambda i, j, c: (i, 0, N_MOD - 1)), p, (n_p, seq // t),
                  lambda width: pl.BlockSpec((None, t, width), lambda i, j, c: (i, j, 0)), t)

        sa_tm = state_conv_a[l].transpose(1, 0, 2)
        sb_tm = state_conv_b[l].transpose(1, 0, 2)
        x1s, h2s, route_s, cnt_s, na_s, nb_s = _mixer_sample(xs, mod_s, sa_tm, sb_tm, p)
        xs = _moe(cnt_s[0, :N_GROUPS], x1s, h2s, route_s, mod_s,
                  pl.BlockSpec((n_s, d), lambda i, c: (0, N_MOD - 1)), p, (1,),
                  lambda width: pl.BlockSpec((steps * n_s, width), lambda i, c: (0, 0)), steps * n_s)
        pa.append(na_p)
        pb.append(nb_p)
        sa.append(na_s.transpose(1, 0, 2))
        sb.append(nb_s.transpose(1, 0, 2))
    y_sample = xs.reshape(steps, n_s, d).transpose(1, 0, 2)
    return (xp, y_sample, jnp.stack(pa), jnp.stack(pb), jnp.stack(sa), jnp.stack(sb))
```

```python
import functools

import jax
import jax.numpy as jnp
from jax import lax
from jax.experimental import pallas as pl
from jax.experimental.pallas import tpu as pltpu

D_MODEL = 1024
D_A = 512
D_B = 512
CONV_A_WIDTH = 3
CONV_B_WIDTH = 31
N_GROUPS = 4
EXPERTS_PER_GROUP = 4
N_EXPERTS = 16
D_EXPERT = 256
N_MOD = 6
EPS = 1e-6
D_IN = 3 * D_A + 2 * D_B + 2 * D_MODEL

LANES = 128
SUBLANES = 8
ROUTER_LANES = LANES
CARRY_A = SUBLANES
CARRY_B = 4 * SUBLANES
SEQ_TILE = 512
CONV_ROWS = 64
CONV_LANES = 2 * LANES
SORT_SHIFT = 4
SORT_ALIGN = 1 << SORT_SHIFT
SORT_PAD = LANES
MOE_PASS_ROWS = (128, 144, 160, 176, 192, 224, 256)
VMEM_LIMIT = 52 * 1024 * 1024

BF16 = jnp.bfloat16
F32 = jnp.float32


def _dot(a, b):
    return jnp.dot(a, b, preferred_element_type=F32)


def _rms(x, g):
    return x * lax.rsqrt(jnp.mean(x * x, axis=-1, keepdims=True) + EPS) * g


def _affine(v, scale, shift=None):
    r = scale.shape[0]
    if r != 1 and r != v.shape[0]:
        v3 = v.reshape(v.shape[0] // r, r, v.shape[-1])
        shift3 = None if shift is None else shift[None]
        return _affine(v3, scale[None], shift3).reshape(v.shape)
    out = v * scale
    return out if shift is None else out + shift


def _route(logits):
    t = logits.shape[0]
    lane_i = lax.broadcasted_iota(jnp.int32, logits.shape, 1)
    lane = lane_i.astype(F32)
    neg = -jnp.inf
    is_g = (lane_i >= N_EXPERTS) & (lane_i < N_EXPERTS + N_GROUPS)
    gl = jnp.where(is_g, logits, neg)
    gmax = jnp.max(gl, axis=-1, keepdims=True)
    g_lane = jnp.min(jnp.where(gl == gmax, lane, float(LANES)), axis=-1, keepdims=True)
    p_g = 1.0 / jnp.sum(jnp.exp(gl - gmax), axis=-1, keepdims=True)
    g_idx = g_lane - float(N_EXPERTS)
    el = logits
    for g in range(1, N_GROUPS):
        el = jnp.where(g_idx == float(g), pltpu.roll(logits, LANES - EXPERTS_PER_GROUP * g, axis=1), el)
    el = jnp.where(lane_i < EXPERTS_PER_GROUP, el, neg)
    v1 = jnp.max(el, axis=-1, keepdims=True)
    i1 = jnp.min(jnp.where(el == v1, lane, float(LANES)), axis=-1, keepdims=True)
    el2 = jnp.where(lane == i1, neg, el)
    v2 = jnp.max(el2, axis=-1, keepdims=True)
    i2 = jnp.min(jnp.where(el2 == v2, lane, float(LANES)), axis=-1, keepdims=True)
    e21 = jnp.exp(v2 - v1)
    wt1 = p_g / (1.0 + e21)
    wt2 = p_g * e21 / (1.0 + e21)
    comb = jnp.where(lane == i1, wt1, jnp.where(lane == i2, wt2, 0.0))
    onehot = jnp.where(lane == g_idx, 1.0, 0.0)
    earlier = lax.broadcasted_iota(jnp.int32, (t, t), 1) < lax.broadcasted_iota(jnp.int32, (t, t), 0)
    before = _dot(jnp.where(earlier, 1.0, 0.0).astype(BF16), onehot.astype(BF16))
    rank = jnp.sum(before * onehot, axis=-1, keepdims=True)
    count = jnp.sum(onehot, axis=0, keepdims=True)
    padded = jnp.ceil(count / SORT_ALIGN) * SORT_ALIGN
    lane1 = lax.broadcasted_iota(jnp.int32, (1, LANES), 1)
    start = jnp.zeros((1, LANES), F32)
    for g in range(N_GROUPS - 1):
        size_g = jnp.sum(jnp.where(lane1 == g, padded, 0.0), axis=-1, keepdims=True)
        start = start + jnp.where(lane1 > g, size_g, 0.0)
    dest = jnp.sum(onehot * start, axis=-1, keepdims=True) + rank
    return jnp.where(lane_i < EXPERTS_PER_GROUP, comb, dest), count


def _ln_silu(v, bias, g, b):
    v = v + bias
    mu = jnp.mean(v, axis=-1, keepdims=True)
    d = v - mu
    var = jnp.mean(d * d, axis=-1, keepdims=True)
    y = d * lax.rsqrt(var + EPS) * g + b
    return y * jax.nn.sigmoid(y)


def _mixer_tail(x, hb, mod, b_a_conv, yb_in, g2, w_in_ref, woa_ref, wob_ref, wo_ref, wr_ref, br_ref,
                x1_ref, h2_ref, route_ref, cnt_ref):
    y_a = _dot(b_a_conv.astype(BF16), woa_ref[...])
    y_b = _dot(yb_in, wob_ref[...])
    o = 3 * D_A + 2 * D_B
    mg_a = _dot(hb, w_in_ref[:, o:o + D_MODEL])
    mg_b = _dot(hb, w_in_ref[:, o + D_MODEL:o + 2 * D_MODEL])
    merged = jax.nn.sigmoid(mg_a) * y_a + jax.nn.sigmoid(mg_b) * y_b
    gt1 = mod[:, 2 * D_MODEL:3 * D_MODEL]
    x1 = x + _affine(_dot(merged.astype(BF16), wo_ref[...]), gt1)
    sh2 = mod[:, 3 * D_MODEL:4 * D_MODEL]
    sc2 = mod[:, 4 * D_MODEL:5 * D_MODEL]
    h2 = _affine(_rms(x1, g2), 1.0 + sc2, sh2).astype(BF16)
    logits = _dot(h2, wr_ref[...]) + br_ref[...]
    x1_ref[...] = x1
    h2_ref[...] = h2
    route, count = _route(logits)
    route_ref[...] = route
    cnt_ref[...] = count.astype(jnp.int32)


def _mixer_prompt_kernel(x_ref, mod_ref, g1_ref, g2_ref, w_in_ref, caw_ref, woa_ref, cbw_ref, cbb_ref,
                         lng_ref, lnb_ref, wob_ref, wo_ref, wr_ref, br_ref,
                         w1f_ref, w3f_ref, w2f_ref,
                         x1_ref, h2_ref, route_ref, cnt_ref, na_ref, nb_ref, w1b_ref, w3b_ref, w2b_ref,
                         hist_a, hist_b, conv_b_ref):
    t = x_ref.shape[0]
    w1b_ref[...] = w1f_ref[...].astype(BF16)
    w3b_ref[...] = w3f_ref[...].astype(BF16)
    w2b_ref[...] = w2f_ref[...].astype(BF16)

    @pl.when(pl.program_id(1) == 0)
    def _():
        hist_a[0:CARRY_A, :] = jnp.zeros((CARRY_A, D_A), F32)
        hist_b[0:CARRY_B, :] = jnp.zeros((CARRY_B, D_B), F32)
        hist_b[CARRY_B + t:CARRY_B + t + SUBLANES, :] = jnp.zeros((SUBLANES, D_B), F32)

    x = x_ref[...]
    mod = mod_ref[...]
    h = _affine(_rms(x, g1_ref[...]), 1.0 + mod[:, D_MODEL:2 * D_MODEL], mod[:, 0:D_MODEL])
    hb = h.astype(BF16)

    def proj(lo, width):
        return _dot(hb, w_in_ref[:, lo:lo + width])

    hist_a[CARRY_A:CARRY_A + t, :] = proj(D_A, D_A) * proj(2 * D_A, D_A)
    conv_a = jnp.zeros((t, D_A), F32)
    for k in range(CONV_A_WIDTH):
        o = CARRY_A - (CONV_A_WIDTH - 1) + k
        conv_a = conv_a + caw_ref[k:k + 1, :] * hist_a[o:o + t, :]
    b_a_conv = proj(0, D_A) * conv_a
    na_ref[...] = hist_a[CARRY_A + t - (CONV_A_WIDTH - 1):CARRY_A + t, :]

    hist_b[CARRY_B:CARRY_B + t, :] = proj(3 * D_A, D_B) * jax.nn.sigmoid(proj(3 * D_A + D_B, D_B))
    nb_ref[...] = hist_b[CARRY_B + t - (CONV_B_WIDTH - 1):CARRY_B + t, :]

    base = CARRY_B - (CONV_B_WIDTH - 1)
    for r0 in range(0, t, CONV_ROWS):
        for c0 in range(0, D_B, CONV_LANES):
            cols = slice(c0, c0 + CONV_LANES)
            out = None
            for j in range(SUBLANES):
                part = None
                for k in range(CONV_B_WIDTH):
                    if (base + k) % SUBLANES != j:
                        continue
                    o = r0 + (base + k - j)
                    term = cbw_ref[k:k + 1, cols] * hist_b[o:o + CONV_ROWS + SUBLANES, cols]
                    part = term if part is None else part + term
                shifted = part[j:j + CONV_ROWS, :]
                out = shifted if out is None else out + shifted
            conv_b_ref[r0:r0 + CONV_ROWS, cols] = out
    yb_in = _ln_silu(conv_b_ref[...], cbb_ref[...], lng_ref[...], lnb_ref[...]).astype(BF16)

    hist_a[0:CARRY_A, :] = hist_a[t:t + CARRY_A, :]
    hist_b[0:CARRY_B, :] = hist_b[t:t + CARRY_B, :]

    _mixer_tail(x, hb, mod, b_a_conv, yb_in, g2_ref[...], w_in_ref, woa_ref, wob_ref, wo_ref,
                wr_ref, br_ref, x1_ref, h2_ref, route_ref, cnt_ref)


def _mixer_sample_kernel(x_ref, mod_ref, g1_ref, g2_ref, w_in_ref, caw_ref, woa_ref, cbw_ref, cbb_ref,
                         lng_ref, lnb_ref, wob_ref, wo_ref, wr_ref, br_ref, sa_ref, sb_ref,
                         x1_ref, h2_ref, route_ref, cnt_ref, na_ref, nb_ref, yb_in_ref):
    n_seq = mod_ref.shape[0]
    steps = x_ref.shape[0] // n_seq
    x = x_ref[...]
    mod = mod_ref[...]
    h = _affine(_rms(x, g1_ref[...]), 1.0 + mod[:, D_MODEL:2 * D_MODEL], mod[:, 0:D_MODEL])
    hb = h.astype(BF16)

    def proj(lo, width):
        return _dot(hb, w_in_ref[:, lo:lo + width])

    def slab(v, j):
        return v[j * n_seq:(j + 1) * n_seq, :]

    u_a = proj(D_A, D_A) * proj(2 * D_A, D_A)
    full_a = [sa_ref[j] for j in range(CONV_A_WIDTH - 1)] + [slab(u_a, j) for j in range(steps)]
    conv_a = []
    for s in range(steps):
        acc = jnp.zeros((n_seq, D_A), F32)
        for k in range(CONV_A_WIDTH):
            acc = acc + caw_ref[k:k + 1, :] * full_a[s + k]
        conv_a.append(acc)
    b_a_conv = proj(0, D_A) * jnp.concatenate(conv_a, axis=0)
    for j in range(CONV_A_WIDTH - 1):
        na_ref[j] = full_a[steps + j]

    u_b = proj(3 * D_A, D_B) * jax.nn.sigmoid(proj(3 * D_A + D_B, D_B))
    hist = CONV_B_WIDTH - 1

    def full_b(j):
        return sb_ref[j] if j < hist else slab(u_b, j - hist)

    bias, lng, lnb = cbb_ref[...], lng_ref[...], lnb_ref[...]
    for s in range(steps):
        acc = jnp.zeros((n_seq, D_B), F32)
        for k in range(CONV_B_WIDTH):
            acc = acc + cbw_ref[k:k + 1, :] * full_b(s + k)
        yb_in_ref[s * n_seq:(s + 1) * n_seq, :] = _ln_silu(acc, bias, lng, lnb).astype(BF16)
    for j in range(hist):
        nb_ref[j] = full_b(steps + j)

    _mixer_tail(x, hb, mod, b_a_conv, yb_in_ref[...], g2_ref[...], w_in_ref, woa_ref, wob_ref, wo_ref,
                wr_ref, br_ref, x1_ref, h2_ref, route_ref, cnt_ref)


def _moe_kernel(cnt_ref, x1_ref, h2_ref, route_ref, gt2_ref, w1_ref, w3_ref, w2_ref, gf_ref, y_ref,
                h2s_ref, cs_ref, acc_ref, *, grid_rank):
    t = h2_ref.shape[0]
    tp = h2s_ref.shape[0]
    tile = pl.program_id(0)
    for axis in range(1, grid_rank):
        tile = tile * pl.num_programs(axis) + pl.program_id(axis)

    route = route_ref[...]
    lane_i = lax.broadcasted_iota(jnp.int32, route.shape, 1)
    dest_row = jnp.transpose(route)[EXPERTS_PER_GROUP:EXPERTS_PER_GROUP + 1, :]
    slot = lax.broadcasted_iota(jnp.int32, (tp, t), 0).astype(F32)
    p_sort = jnp.where(slot == dest_row, 1.0, 0.0).astype(BF16)
    h2s_ref[...] = _dot(p_sort, h2_ref[...]).astype(BF16)
    comb = jnp.where(lane_i < EXPERTS_PER_GROUP, route, 0.0)
    c_hi = comb.astype(BF16).astype(F32)
    c_mid = (comb - c_hi).astype(BF16).astype(F32)
    c_lo = comb - c_hi - c_mid
    packed = (c_hi + pltpu.roll(c_mid, EXPERTS_PER_GROUP, axis=1)
              + pltpu.roll(c_lo, 2 * EXPERTS_PER_GROUP, axis=1)).astype(BF16)
    moved = _dot(p_sort, packed)
    cs_ref[...] = (moved + pltpu.roll(moved, LANES - EXPERTS_PER_GROUP, axis=1)
                   + pltpu.roll(moved, LANES - 2 * EXPERTS_PER_GROUP, axis=1))
    acc_ref[...] = jnp.zeros(acc_ref.shape, F32)

    def expert_pass(rows, g, lo, end):
        off = pl.multiple_of(jnp.minimum(lo, tp - rows), SORT_ALIGN)
        row_id = off + lax.broadcasted_iota(jnp.int32, (rows, 1), 0)
        valid = (row_id >= lo) & (row_id < end)
        hb = h2s_ref[pl.ds(off, rows), :]
        c4 = jnp.where(valid, cs_ref[pl.ds(off, rows), :], 0.0)
        experts = [g * EXPERTS_PER_GROUP + e for e in range(EXPERTS_PER_GROUP)]
        gate = jnp.concatenate([_dot(hb, w1_ref[e]) for e in experts], axis=1)
        up = jnp.concatenate([_dot(hb, w3_ref[e]) for e in experts], axis=1)
        cexp = jnp.concatenate(
            [jnp.broadcast_to(c4[:, e:e + 1], (rows, D_EXPERT)) for e in range(EXPERTS_PER_GROUP)], axis=1)
        a = gate * jax.nn.sigmoid(gate) * up * cexp
        acc_ref[pl.ds(off, rows), :] += _dot(a.astype(BF16), w2_ref[g])

    def group(g, start):
        n = cnt_ref[tile * N_GROUPS + g]
        end = start + n

        def passes(rows):
            def run():
                def body(p, carry):
                    expert_pass(rows, g, start + p * rows, end)
                    return carry
                lax.fori_loop(0, lax.div(n + (rows - 1), rows), body, 0)
            return run

        size_class = sum((n > rows).astype(jnp.int32) for rows in MOE_PASS_ROWS[:-1])
        lax.switch(size_class, [passes(rows) for rows in MOE_PASS_ROWS])
        return start + lax.shift_right_logical(n + (SORT_ALIGN - 1), SORT_SHIFT) * SORT_ALIGN

    lax.fori_loop(0, N_GROUPS, group, jnp.int32(0))

    dest_col = route_ref[:, EXPERTS_PER_GROUP:EXPERTS_PER_GROUP + 1]
    p_back = jnp.where(lax.broadcasted_iota(jnp.int32, (t, tp), 1).astype(F32) == dest_col,
                       1.0, 0.0).astype(BF16)
    y = acc_ref[...]
    y_hi = y.astype(BF16)
    y_lo = (y - y_hi.astype(F32)).astype(BF16)
    y_tok = _dot(p_back, y_hi) + _dot(p_back, y_lo)
    x2 = x1_ref[...] + _affine(y_tok, gt2_ref[...])
    y_ref[...] = _rms(x2, gf_ref[...])


def _mod_kernel(c_ref, w_ref, b_ref, op_ref, os_ref):
    c = c_ref[...]
    a = (c * jax.nn.sigmoid(c)).astype(BF16)
    mod = _dot(a, w_ref[...].astype(BF16)) + b_ref[...]
    n_p = op_ref.shape[0]
    op_ref[...] = mod[:n_p]
    os_ref[...] = mod[n_p:]


def _const_spec(shape):
    nd = len(shape)
    return pl.BlockSpec(shape, lambda *_: (0,) * nd, pipeline_mode=pl.Buffered(1))


def _modulation(c_all, n_p, w_ada, b_ada):
    n, d = c_all.shape
    n_out = w_ada.shape[1]
    assert n_p % SUBLANES == 0
    return pl.pallas_call(
        _mod_kernel,
        grid=(n_out // D_MODEL,),
        in_specs=[pl.BlockSpec((n, d), lambda j: (0, 0)),
                  pl.BlockSpec((d, D_MODEL), lambda j: (0, j)),
                  pl.BlockSpec((1, D_MODEL), lambda j: (0, j))],
        out_specs=[pl.BlockSpec((n_p, D_MODEL), lambda j: (0, j)),
                   pl.BlockSpec((n - n_p, D_MODEL), lambda j: (0, j))],
        out_shape=[jax.ShapeDtypeStruct((n_p, n_out), F32), jax.ShapeDtypeStruct((n - n_p, n_out), F32)],
        compiler_params=pltpu.CompilerParams(dimension_semantics=("arbitrary",),
                                             vmem_limit_bytes=VMEM_LIMIT),
        name="modulation",
    )(c_all, w_ada, b_ada)


def _mixer_weight_specs(p):
    names = ("g1", "g2", "w_in", "caw", "woa", "cbw", "cbb", "lng", "lnb", "wob", "wo", "wr", "br")
    return [p[k] for k in names], [_const_spec(p[k].shape) for k in names]


def _mixer_prompt(x, mod, p, expert_weights):
    b, s, d = x.shape
    t = SEQ_TILE
    n_t = s // t
    tiles = b * n_t
    weights, wspecs = _mixer_weight_specs(p)
    tile = lambda width: pl.BlockSpec((None, t, width), lambda i, j: (i, j, 0))
    per_seq = lambda rows, width: pl.BlockSpec((None, rows, width), lambda i, j: (i, 0, 0))

    def cast_spec(w):
        experts, rows, cols = w.shape
        per_expert = tiles // experts
        assert per_expert * experts == tiles and rows % (per_expert * 2 * SUBLANES) == 0
        return pl.BlockSpec((1, rows // per_expert, cols),
                            lambda i, j: ((i * n_t + j) // per_expert, lax.rem(i * n_t + j, per_expert), 0))

    cast_specs = [cast_spec(w) for w in expert_weights]
    return pl.pallas_call(
        _mixer_prompt_kernel,
        grid=(b, n_t),
        in_specs=[tile(d), per_seq(1, N_MOD * d)] + wspecs + cast_specs,
        out_specs=[tile(d), tile(d), tile(ROUTER_LANES),
                   pl.BlockSpec((None, 1, LANES), lambda i, j: (i * n_t + j, 0, 0)),
                   per_seq(CONV_A_WIDTH - 1, D_A), per_seq(CONV_B_WIDTH - 1, D_B)] + cast_specs,
        out_shape=[jax.ShapeDtypeStruct((b, s, d), F32),
                   jax.ShapeDtypeStruct((b, s, d), BF16),
                   jax.ShapeDtypeStruct((b, s, ROUTER_LANES), F32),
                   jax.ShapeDtypeStruct((b * n_t, 1, LANES), jnp.int32),
                   jax.ShapeDtypeStruct((b, CONV_A_WIDTH - 1, D_A), F32),
                   jax.ShapeDtypeStruct((b, CONV_B_WIDTH - 1, D_B), F32)]
        + [jax.ShapeDtypeStruct(w.shape, BF16) for w in expert_weights],
        scratch_shapes=[pltpu.VMEM((CARRY_A + t, D_A), F32),
                        pltpu.VMEM((CARRY_B + t + SUBLANES, D_B), F32),
                        pltpu.VMEM((t, D_B), F32)],
        compiler_params=pltpu.CompilerParams(dimension_semantics=("arbitrary", "arbitrary"),
                                             vmem_limit_bytes=VMEM_LIMIT),
        name="mixer_prompt",
    )(x, mod.reshape(b, 1, N_MOD * d), *weights, *expert_weights)


def _mixer_sample(x_tm, mod, sa_tm, sb_tm, p):
    n, d = x_tm.shape
    weights, wspecs = _mixer_weight_specs(p)
    full = lambda a: pl.BlockSpec(a.shape, lambda i: (0,) * a.ndim)
    whole = lambda *shape: pl.BlockSpec(shape, lambda i: (0,) * len(shape))
    return pl.pallas_call(
        _mixer_sample_kernel,
        grid=(1,),
        in_specs=[full(x_tm), full(mod)] + wspecs + [full(sa_tm), full(sb_tm)],
        out_specs=[whole(n, d), whole(n, d), whole(n, ROUTER_LANES), whole(1, LANES),
                   full(sa_tm), full(sb_tm)],
        out_shape=[jax.ShapeDtypeStruct((n, d), F32),
                   jax.ShapeDtypeStruct((n, d), BF16),
                   jax.ShapeDtypeStruct((n, ROUTER_LANES), F32),
                   jax.ShapeDtypeStruct((1, LANES), jnp.int32),
                   jax.ShapeDtypeStruct(sa_tm.shape, F32),
                   jax.ShapeDtypeStruct(sb_tm.shape, F32)],
        scratch_shapes=[pltpu.VMEM((n, D_B), BF16)],
        compiler_params=pltpu.CompilerParams(dimension_semantics=("arbitrary",),
                                             vmem_limit_bytes=VMEM_LIMIT),
        name="mixer_sample",
    )(x_tm, mod, *weights, sa_tm, sb_tm)


def _moe(counts, x1, h2, route, gt2_src, gt2_spec, p, grid, tile, rows):
    weights = [p["w1"], p["w3"], p["w2"], p["gf"]]
    sorted_rows = rows + SORT_PAD
    grid_spec = pltpu.PrefetchScalarGridSpec(
        num_scalar_prefetch=1,
        grid=grid,
        in_specs=[tile(D_MODEL), tile(D_MODEL), tile(ROUTER_LANES), gt2_spec]
        + [_const_spec(w.shape) for w in weights],
        out_specs=tile(D_MODEL),
        scratch_shapes=[pltpu.VMEM((sorted_rows, D_MODEL), BF16),
                        pltpu.VMEM((sorted_rows, ROUTER_LANES), F32),
                        pltpu.VMEM((sorted_rows, D_MODEL), F32)],
    )
    return pl.pallas_call(
        functools.partial(_moe_kernel, grid_rank=len(grid)),
        grid_spec=grid_spec,
        out_shape=jax.ShapeDtypeStruct(x1.shape, F32),
        compiler_params=pltpu.CompilerParams(dimension_semantics=("arbitrary",) * len(grid),
                                             vmem_limit_bytes=VMEM_LIMIT),
        name="moe",
    )(counts, x1, h2, route, gt2_src, *weights)


def kernel(x_prompt, x_sample, c_prompt, c_sample, state_conv_a, state_conv_b, w_ada, b_ada, norm1_g,
           norm2_g, w_in, conv_a_w, w_out_a, conv_b_w, conv_b_bias, ln_b_g, ln_b_b, w_out_b, w_o,
           w_group, b_group, w_expert, b_expert, w1, w3, w2, final_norm_g):
    depth = w_ada.shape[0]
    assert depth == 1, "the MoE kernel fuses the final norm, so exactly one layer is supported"
    n_p, seq, d = x_prompt.shape
    n_s, steps, _ = x_sample.shape
    c_all = jnp.concatenate([c_prompt, c_sample], axis=0)
    xp = x_prompt
    xs = x_sample.transpose(1, 0, 2).reshape(steps * n_s, d)
    gf = final_norm_g.reshape(1, d)
    pa, pb, sa, sb = [], [], [], []
    for l in range(depth):
        pad = ROUTER_LANES - N_EXPERTS - N_GROUPS
        wr = jnp.pad(jnp.concatenate([w_expert[l], w_group[l]], axis=1), ((0, 0), (0, pad)))
        br = jnp.pad(jnp.concatenate([b_expert[l], b_group[l]]), (0, pad)).reshape(1, ROUTER_LANES)
        p = {
            "g1": norm1_g[l].reshape(1, d), "g2": norm2_g[l].reshape(1, d),
            "w_in": w_in[l].astype(BF16), "caw": conv_a_w[l], "woa": w_out_a[l].astype(BF16),
            "cbw": conv_b_w[l], "cbb": conv_b_bias[l].reshape(1, D_B),
            "lng": ln_b_g[l].reshape(1, D_B), "lnb": ln_b_b[l].reshape(1, D_B),
            "wob": w_out_b[l].astype(BF16), "wo": w_o[l].astype(BF16), "wr": wr.astype(BF16), "br": br,
            "gf": gf,
        }
        mod_p, mod_s = _modulation(c_all, n_p, w_ada[l], b_ada[l].reshape(1, N_MOD * d))

        x1p, h2p, route_p, cnt_p, na_p, nb_p, w1b, w3b, w2b = _mixer_prompt(
            xp, mod_p, p, (w1[l], w3[l], w2[l]))
        p.update(w1=w1b, w3=w3b, w2=w2b.reshape(N_GROUPS, EXPERTS_PER_GROUP * D_EXPERT, d))
        t = SEQ_TILE
        xp = _moe(cnt_p[:, 0, :N_GROUPS].reshape(-1), x1p, h2p, route_p,
                  mod_p.reshape(n_p, 1, N_MOD * d),
                  pl.BlockSpec((None, 1, d), lambda i, j, c: (i, 0, N_MOD - 1)), p, (n_p, seq // t),
                  lambda width: pl.BlockSpec((None, t, width), lambda i, j, c: (i, j, 0)), t)

        sa_tm = state_conv_a[l].transpose(1, 0, 2)
        sb_tm = state_conv_b[l].transpose(1, 0, 2)
        x1s, h2s, route_s, cnt_s, na_s, nb_s = _mixer_sample(xs, mod_s, sa_tm, sb_tm, p)
        xs = _moe(cnt_s[0, :N_GROUPS], x1s, h2s, route_s, mod_s,
                  pl.BlockSpec((n_s, d), lambda i, c: (0, N_MOD - 1)), p, (1,),
                  lambda width: pl.BlockSpec((steps * n_s, width), lambda i, c: (0, 0)), steps * n_s)
        pa.append(na_p)
        pb.append(nb_p)
        sa.append(na_s.transpose(1, 0, 2))
        sb.append(nb_s.transpose(1, 0, 2))
    y_sample = xs.reshape(steps, n_s, d).transpose(1, 0, 2)
    return (xp, y_sample, jnp.stack(pa), jnp.stack(pb), jnp.stack(sa), jnp.stack(sb))
```

```python
import functools

import jax
import jax.numpy as jnp
from jax import lax
from jax.experimental import pallas as pl
from jax.experimental.pallas import tpu as pltpu

D_MODEL = 1024
D_A = 512
D_B = 512
CONV_A_WIDTH = 3
CONV_B_WIDTH = 31
N_GROUPS = 4
EXPERTS_PER_GROUP = 4
N_EXPERTS = 16
D_EXPERT = 256
N_MOD = 6
EPS = 1e-6
D_IN = 3 * D_A + 2 * D_B + 2 * D_MODEL

LANES = 128
SUBLANES = 8
ROUTER_LANES = LANES
CARRY_A = SUBLANES
CARRY_B = 4 * SUBLANES
SEQ_TILE = 512
CONV_ROWS = 64
CONV_LANES = 2 * LANES
SORT_SHIFT = 4
SORT_ALIGN = 1 << SORT_SHIFT
SORT_PAD = LANES
MOE_PASS_ROWS = (128, 144, 160, 176, 192, 224, 256)
VMEM_LIMIT = 52 * 1024 * 1024

BF16 = jnp.bfloat16
F32 = jnp.float32


def _dot(a, b):
    return jnp.dot(a, b, preferred_element_type=F32)


def _rms(x, g):
    return x * lax.rsqrt(jnp.mean(x * x, axis=-1, keepdims=True) + EPS) * g


def _affine(v, scale, shift=None):
    r = scale.shape[0]
    if r != 1 and r != v.shape[0]:
        v3 = v.reshape(v.shape[0] // r, r, v.shape[-1])
        shift3 = None if shift is None else shift[None]
        return _affine(v3, scale[None], shift3).reshape(v.shape)
    out = v * scale
    return out if shift is None else out + shift


def _route(logits_t):
    t = logits_t.shape[1]
    neg = -jnp.inf
    row = lax.broadcasted_iota(jnp.int32, (SUBLANES, t), 0).astype(F32)
    live = row < float(EXPERTS_PER_GROUP)
    gl = jnp.where(live, logits_t[N_EXPERTS:N_EXPERTS + SUBLANES, :], neg)
    gmax = jnp.max(gl, axis=0, keepdims=True)
    g_idx = jnp.min(jnp.where(gl == gmax, row, float(SUBLANES)), axis=0, keepdims=True)
    p_g = 1.0 / jnp.sum(jnp.exp(gl - gmax), axis=0, keepdims=True)
    el = None
    for g in reversed(range(N_GROUPS)):
        slab = logits_t[(g // 2) * SUBLANES:(g // 2 + 1) * SUBLANES, :]
        cand = slab if g % 2 == 0 else pltpu.roll(slab, SUBLANES - EXPERTS_PER_GROUP, axis=0)
        el = cand if el is None else jnp.where(g_idx == float(g), cand, el)
    el = jnp.where(live, el, neg)
    v1 = jnp.max(el, axis=0, keepdims=True)
    i1 = jnp.min(jnp.where(el == v1, row, float(SUBLANES)), axis=0, keepdims=True)
    el2 = jnp.where(row == i1, neg, el)
    v2 = jnp.max(el2, axis=0, keepdims=True)
    i2 = jnp.min(jnp.where(el2 == v2, row, float(SUBLANES)), axis=0, keepdims=True)
    e21 = jnp.exp(v2 - v1)
    wt1 = p_g / (1.0 + e21)
    wt2 = p_g * e21 / (1.0 + e21)
    comb = jnp.where(row == i1, wt1, jnp.where(row == i2, wt2, 0.0))
    onehot = jnp.where(row == g_idx, 1.0, 0.0)
    onehot_b = jnp.concatenate([onehot, jnp.zeros_like(onehot)], axis=0).astype(BF16)
    earlier = lax.broadcasted_iota(jnp.int32, (t, t), 0) < lax.broadcasted_iota(jnp.int32, (t, t), 1)
    before = _dot(onehot_b, jnp.where(earlier, 1.0, 0.0).astype(BF16))[0:SUBLANES, :]
    rank = jnp.sum(before * onehot, axis=0, keepdims=True)
    count = jnp.sum(onehot, axis=1, keepdims=True)
    padded = jnp.ceil(count / SORT_ALIGN) * SORT_ALIGN
    row1 = lax.broadcasted_iota(jnp.int32, (SUBLANES, 1), 0)
    start = jnp.zeros((SUBLANES, 1), F32)
    for g in range(N_GROUPS - 1):
        start = start + jnp.where(row1 > g, padded[g:g + 1, :], 0.0)
    dest = jnp.sum(onehot * start, axis=0, keepdims=True) + rank
    route_t = jnp.concatenate([jnp.where(live, comb, dest), jnp.broadcast_to(dest, (LANES - SUBLANES, t))],
                              axis=0)
    onehot_all = jnp.concatenate([onehot, jnp.zeros((LANES - SUBLANES, t), F32)], axis=0).astype(BF16)
    count_row = lax.dot_general(jnp.ones((2 * SUBLANES, t), BF16), onehot_all, (((1,), (1,)), ((), ())),
                                preferred_element_type=F32)[0:1, :]
    return jnp.transpose(route_t), count_row


def _ln_silu(v, bias, g, b):
    v = v + bias
    mu = jnp.mean(v, axis=-1, keepdims=True)
    d = v - mu
    var = jnp.mean(d * d, axis=-1, keepdims=True)
    y = d * lax.rsqrt(var + EPS) * g + b
    return y * jax.nn.sigmoid(y)


def _mixer_tail(x, hb, mod, b_a_conv, yb_in, g2, w_in_ref, woa_ref, wob_ref, wo_ref, wr_ref, br_ref,
                x1_ref, h2_ref, route_ref, cnt_ref):
    y_a = _dot(b_a_conv.astype(BF16), woa_ref[...])
    y_b = _dot(yb_in, wob_ref[...])
    o = 3 * D_A + 2 * D_B
    mg_a = _dot(hb, w_in_ref[:, o:o + D_MODEL])
    mg_b = _dot(hb, w_in_ref[:, o + D_MODEL:o + 2 * D_MODEL])
    merged = jax.nn.sigmoid(mg_a) * y_a + jax.nn.sigmoid(mg_b) * y_b
    gt1 = mod[:, 2 * D_MODEL:3 * D_MODEL]
    x1 = x + _affine(_dot(merged.astype(BF16), wo_ref[...]), gt1)
    sh2 = mod[:, 3 * D_MODEL:4 * D_MODEL]
    sc2 = mod[:, 4 * D_MODEL:5 * D_MODEL]
    h2 = _affine(_rms(x1, g2), 1.0 + sc2, sh2).astype(BF16)
    logits_t = lax.dot_general(wr_ref[...], h2, (((1,), (1,)), ((), ())), preferred_element_type=F32)
    logits_t = logits_t + jnp.tile(br_ref[...], (1, h2.shape[0] // LANES))
    x1_ref[...] = x1
    h2_ref[...] = h2
    route, count = _route(logits_t)
    route_ref[...] = route
    cnt_ref[...] = count.astype(jnp.int32)


def _mixer_prompt_kernel(x_ref, mod_ref, g1_ref, g2_ref, w_in_ref, caw_ref, woa_ref, cbw_ref, cbb_ref,
                         lng_ref, lnb_ref, wob_ref, wo_ref, wr_ref, br_ref,
                         w1f_ref, w3f_ref, w2f_ref,
                         x1_ref, h2_ref, route_ref, cnt_ref, na_ref, nb_ref, w1b_ref, w3b_ref, w2b_ref,
                         hist_a, hist_b, conv_b_ref):
    t = x_ref.shape[0]
    w1b_ref[...] = w1f_ref[...].astype(BF16)
    w3b_ref[...] = w3f_ref[...].astype(BF16)
    w2b_ref[...] = w2f_ref[...].astype(BF16)

    @pl.when(pl.program_id(1) == 0)
    def _():
        hist_a[0:CARRY_A, :] = jnp.zeros((CARRY_A, D_A), F32)
        hist_b[0:CARRY_B, :] = jnp.zeros((CARRY_B, D_B), F32)
        hist_b[CARRY_B + t:CARRY_B + t + SUBLANES, :] = jnp.zeros((SUBLANES, D_B), F32)

    x = x_ref[...]
    mod = mod_ref[...]
    h = _affine(_rms(x, g1_ref[...]), 1.0 + mod[:, D_MODEL:2 * D_MODEL], mod[:, 0:D_MODEL])
    hb = h.astype(BF16)

    def proj(lo, width):
        return _dot(hb, w_in_ref[:, lo:lo + width])

    hist_a[CARRY_A:CARRY_A + t, :] = proj(D_A, D_A) * proj(2 * D_A, D_A)
    conv_a = jnp.zeros((t, D_A), F32)
    for k in range(CONV_A_WIDTH):
        o = CARRY_A - (CONV_A_WIDTH - 1) + k
        conv_a = conv_a + caw_ref[k:k + 1, :] * hist_a[o:o + t, :]
    b_a_conv = proj(0, D_A) * conv_a
    na_ref[...] = hist_a[CARRY_A + t - (CONV_A_WIDTH - 1):CARRY_A + t, :]

    hist_b[CARRY_B:CARRY_B + t, :] = proj(3 * D_A, D_B) * jax.nn.sigmoid(proj(3 * D_A + D_B, D_B))
    nb_ref[...] = hist_b[CARRY_B + t - (CONV_B_WIDTH - 1):CARRY_B + t, :]

    base = CARRY_B - (CONV_B_WIDTH - 1)
    for r0 in range(0, t, CONV_ROWS):
        for c0 in range(0, D_B, CONV_LANES):
            cols = slice(c0, c0 + CONV_LANES)
            out = None
            for j in range(SUBLANES):
                part = None
                for k in range(CONV_B_WIDTH):
                    if (base + k) % SUBLANES != j:
                        continue
                    o = r0 + (base + k - j)
                    term = cbw_ref[k:k + 1, cols] * hist_b[o:o + CONV_ROWS + SUBLANES, cols]
                    part = term if part is None else part + term
                shifted = part[j:j + CONV_ROWS, :]
                out = shifted if out is None else out + shifted
            conv_b_ref[r0:r0 + CONV_ROWS, cols] = out
    yb_in = _ln_silu(conv_b_ref[...], cbb_ref[...], lng_ref[...], lnb_ref[...]).astype(BF16)

    hist_a[0:CARRY_A, :] = hist_a[t:t + CARRY_A, :]
    hist_b[0:CARRY_B, :] = hist_b[t:t + CARRY_B, :]

    _mixer_tail(x, hb, mod, b_a_conv, yb_in, g2_ref[...], w_in_ref, woa_ref, wob_ref, wo_ref,
                wr_ref, br_ref, x1_ref, h2_ref, route_ref, cnt_ref)


def _mixer_sample_kernel(x_ref, mod_ref, g1_ref, g2_ref, w_in_ref, caw_ref, woa_ref, cbw_ref, cbb_ref,
                         lng_ref, lnb_ref, wob_ref, wo_ref, wr_ref, br_ref, sa_ref, sb_ref,
                         x1_ref, h2_ref, route_ref, cnt_ref, na_ref, nb_ref, yb_in_ref):
    n_seq = mod_ref.shape[0]
    steps = x_ref.shape[0] // n_seq
    x = x_ref[...]
    mod = mod_ref[...]
    h = _affine(_rms(x, g1_ref[...]), 1.0 + mod[:, D_MODEL:2 * D_MODEL], mod[:, 0:D_MODEL])
    hb = h.astype(BF16)

    def proj(lo, width):
        return _dot(hb, w_in_ref[:, lo:lo + width])

    def slab(v, j):
        return v[j * n_seq:(j + 1) * n_seq, :]

    u_a = proj(D_A, D_A) * proj(2 * D_A, D_A)
    full_a = [sa_ref[j] for j in range(CONV_A_WIDTH - 1)] + [slab(u_a, j) for j in range(steps)]
    conv_a = []
    for s in range(steps):
        acc = jnp.zeros((n_seq, D_A), F32)
        for k in range(CONV_A_WIDTH):
            acc = acc + caw_ref[k:k + 1, :] * full_a[s + k]
        conv_a.append(acc)
    b_a_conv = proj(0, D_A) * jnp.concatenate(conv_a, axis=0)
    for j in range(CONV_A_WIDTH - 1):
        na_ref[j] = full_a[steps + j]

    u_b = proj(3 * D_A, D_B) * jax.nn.sigmoid(proj(3 * D_A + D_B, D_B))
    hist = CONV_B_WIDTH - 1

    def full_b(j):
        return sb_ref[j] if j < hist else slab(u_b, j - hist)

    bias, lng, lnb = cbb_ref[...], lng_ref[...], lnb_ref[...]
    for s in range(steps):
        acc = jnp.zeros((n_seq, D_B), F32)
        for k in range(CONV_B_WIDTH):
            acc = acc + cbw_ref[k:k + 1, :] * full_b(s + k)
        yb_in_ref[s * n_seq:(s + 1) * n_seq, :] = _ln_silu(acc, bias, lng, lnb).astype(BF16)
    for j in range(hist):
        nb_ref[j] = full_b(steps + j)

    _mixer_tail(x, hb, mod, b_a_conv, yb_in_ref[...], g2_ref[...], w_in_ref, woa_ref, wob_ref, wo_ref,
                wr_ref, br_ref, x1_ref, h2_ref, route_ref, cnt_ref)


def _moe_kernel(cnt_ref, x1_ref, h2_ref, route_ref, gt2_ref, w1_ref, w3_ref, w2_ref, gf_ref, y_ref,
                h2s_ref, cs_ref, acc_ref, *, grid_rank):
    t = h2_ref.shape[0]
    tp = h2s_ref.shape[0]
    tile = pl.program_id(0)
    for axis in range(1, grid_rank):
        tile = tile * pl.num_programs(axis) + pl.program_id(axis)

    route = route_ref[...]
    lane_i = lax.broadcasted_iota(jnp.int32, route.shape, 1)
    dest_row = jnp.transpose(route)[EXPERTS_PER_GROUP:EXPERTS_PER_GROUP + 1, :]
    slot = lax.broadcasted_iota(jnp.int32, (tp, t), 0).astype(F32)
    p_sort = jnp.where(slot == dest_row, 1.0, 0.0).astype(BF16)
    h2s_ref[...] = _dot(p_sort, h2_ref[...]).astype(BF16)
    comb = jnp.where(lane_i < EXPERTS_PER_GROUP, route, 0.0)
    c_hi = comb.astype(BF16).astype(F32)
    c_mid = (comb - c_hi).astype(BF16).astype(F32)
    c_lo = comb - c_hi - c_mid
    packed = (c_hi + pltpu.roll(c_mid, EXPERTS_PER_GROUP, axis=1)
              + pltpu.roll(c_lo, 2 * EXPERTS_PER_GROUP, axis=1)).astype(BF16)
    moved = _dot(p_sort, packed)
    cs_ref[...] = (moved + pltpu.roll(moved, LANES - EXPERTS_PER_GROUP, axis=1)
                   + pltpu.roll(moved, LANES - 2 * EXPERTS_PER_GROUP, axis=1))
    acc_ref[...] = jnp.zeros(acc_ref.shape, F32)

    def expert_pass(rows, g, lo, end):
        off = pl.multiple_of(jnp.minimum(lo, tp - rows), SORT_ALIGN)
        row_id = off + lax.broadcasted_iota(jnp.int32, (rows, 1), 0)
        valid = (row_id >= lo) & (row_id < end)
        hb = h2s_ref[pl.ds(off, rows), :]
        c4 = jnp.where(valid, cs_ref[pl.ds(off, rows), :], 0.0)
        experts = [g * EXPERTS_PER_GROUP + e for e in range(EXPERTS_PER_GROUP)]
        gate = jnp.concatenate([_dot(hb, w1_ref[e]) for e in experts], axis=1)
        up = jnp.concatenate([_dot(hb, w3_ref[e]) for e in experts], axis=1)
        cexp = jnp.concatenate(
            [jnp.broadcast_to(c4[:, e:e + 1], (rows, D_EXPERT)) for e in range(EXPERTS_PER_GROUP)], axis=1)
        a = gate * jax.nn.sigmoid(gate) * up * cexp
        acc_ref[pl.ds(off, rows), :] += _dot(a.astype(BF16), w2_ref[g])

    def group(g, start):
        n = cnt_ref[tile * N_GROUPS + g]
        end = start + n

        def passes(rows):
            def run():
                def body(p, carry):
                    expert_pass(rows, g, start + p * rows, end)
                    return carry
                lax.fori_loop(0, lax.div(n + (rows - 1), rows), body, 0)
            return run

        size_class = sum((n > rows).astype(jnp.int32) for rows in MOE_PASS_ROWS[:-1])
        lax.switch(size_class, [passes(rows) for rows in MOE_PASS_ROWS])
        return start + lax.shift_right_logical(n + (SORT_ALIGN - 1), SORT_SHIFT) * SORT_ALIGN

    lax.fori_loop(0, N_GROUPS, group, jnp.int32(0))

    dest_col = route_ref[:, EXPERTS_PER_GROUP:EXPERTS_PER_GROUP + 1]
    p_back = jnp.where(lax.broadcasted_iota(jnp.int32, (t, tp), 1).astype(F32) == dest_col,
                       1.0, 0.0).astype(BF16)
    y = acc_ref[...]
    y_hi = y.astype(BF16)
    y_lo = (y - y_hi.astype(F32)).astype(BF16)
    y_tok = _dot(p_back, y_hi) + _dot(p_back, y_lo)
    x2 = x1_ref[...] + _affine(y_tok, gt2_ref[...])
    y_ref[...] = _rms(x2, gf_ref[...])


def _mod_kernel(c_ref, w_ref, b_ref, op_ref, os_ref):
    c = c_ref[...]
    a = (c * jax.nn.sigmoid(c)).astype(BF16)
    mod = _dot(a, w_ref[...].astype(BF16)) + b_ref[...]
    n_p = op_ref.shape[0]
    op_ref[...] = mod[:n_p]
    os_ref[...] = mod[n_p:]


def _const_spec(shape):
    nd = len(shape)
    return pl.BlockSpec(shape, lambda *_: (0,) * nd, pipeline_mode=pl.Buffered(1))


def _modulation(c_all, n_p, w_ada, b_ada):
    n, d = c_all.shape
    n_out = w_ada.shape[1]
    assert n_p % SUBLANES == 0
    return pl.pallas_call(
        _mod_kernel,
        grid=(n_out // D_MODEL,),
        in_specs=[pl.BlockSpec((n, d), lambda j: (0, 0)),
                  pl.BlockSpec((d, D_MODEL), lambda j: (0, j)),
                  pl.BlockSpec((1, D_MODEL), lambda j: (0, j))],
        out_specs=[pl.BlockSpec((n_p, D_MODEL), lambda j: (0, j)),
                   pl.BlockSpec((n - n_p, D_MODEL), lambda j: (0, j))],
        out_shape=[jax.ShapeDtypeStruct((n_p, n_out), F32), jax.ShapeDtypeStruct((n - n_p, n_out), F32)],
        compiler_params=pltpu.CompilerParams(dimension_semantics=("arbitrary",),
                                             vmem_limit_bytes=VMEM_LIMIT),
        name="modulation",
    )(c_all, w_ada, b_ada)


def _mixer_weight_specs(p):
    names = ("g1", "g2", "w_in", "caw", "woa", "cbw", "cbb", "lng", "lnb", "wob", "wo", "wr", "br")
    return [p[k] for k in names], [_const_spec(p[k].shape) for k in names]


def _mixer_prompt(x, mod, p, expert_weights):
    b, s, d = x.shape
    t = SEQ_TILE
    n_t = s // t
    tiles = b * n_t
    weights, wspecs = _mixer_weight_specs(p)
    tile = lambda width: pl.BlockSpec((None, t, width), lambda i, j: (i, j, 0))
    per_seq = lambda rows, width: pl.BlockSpec((None, rows, width), lambda i, j: (i, 0, 0))

    def cast_spec(w):
        experts, rows, cols = w.shape
        per_expert = tiles // experts
        assert per_expert * experts == tiles and rows % (per_expert * 2 * SUBLANES) == 0
        return pl.BlockSpec((1, rows // per_expert, cols),
                            lambda i, j: ((i * n_t + j) // per_expert, lax.rem(i * n_t + j, per_expert), 0))

    cast_specs = [cast_spec(w) for w in expert_weights]
    return pl.pallas_call(
        _mixer_prompt_kernel,
        grid=(b, n_t),
        in_specs=[tile(d), per_seq(1, N_MOD * d)] + wspecs + cast_specs,
        out_specs=[tile(d), tile(d), tile(ROUTER_LANES),
                   pl.BlockSpec((None, 1, LANES), lambda i, j: (i * n_t + j, 0, 0)),
                   per_seq(CONV_A_WIDTH - 1, D_A), per_seq(CONV_B_WIDTH - 1, D_B)] + cast_specs,
        out_shape=[jax.ShapeDtypeStruct((b, s, d), F32),
                   jax.ShapeDtypeStruct((b, s, d), BF16),
                   jax.ShapeDtypeStruct((b, s, ROUTER_LANES), F32),
                   jax.ShapeDtypeStruct((b * n_t, 1, LANES), jnp.int32),
                   jax.ShapeDtypeStruct((b, CONV_A_WIDTH - 1, D_A), F32),
                   jax.ShapeDtypeStruct((b, CONV_B_WIDTH - 1, D_B), F32)]
        + [jax.ShapeDtypeStruct(w.shape, BF16) for w in expert_weights],
        scratch_shapes=[pltpu.VMEM((CARRY_A + t, D_A), F32),
                        pltpu.VMEM((CARRY_B + t + SUBLANES, D_B), F32),
                        pltpu.VMEM((t, D_B), F32)],
        compiler_params=pltpu.CompilerParams(dimension_semantics=("arbitrary", "arbitrary"),
                                             vmem_limit_bytes=VMEM_LIMIT),
        name="mixer_prompt",
    )(x, mod.reshape(b, 1, N_MOD * d), *weights, *expert_weights)


def _mixer_sample(x_tm, mod, sa_tm, sb_tm, p):
    n, d = x_tm.shape
    weights, wspecs = _mixer_weight_specs(p)
    full = lambda a: pl.BlockSpec(a.shape, lambda i: (0,) * a.ndim)
    whole = lambda *shape: pl.BlockSpec(shape, lambda i: (0,) * len(shape))
    return pl.pallas_call(
        _mixer_sample_kernel,
        grid=(1,),
        in_specs=[full(x_tm), full(mod)] + wspecs + [full(sa_tm), full(sb_tm)],
        out_specs=[whole(n, d), whole(n, d), whole(n, ROUTER_LANES), whole(1, LANES),
                   full(sa_tm), full(sb_tm)],
        out_shape=[jax.ShapeDtypeStruct((n, d), F32),
                   jax.ShapeDtypeStruct((n, d), BF16),
                   jax.ShapeDtypeStruct((n, ROUTER_LANES), F32),
                   jax.ShapeDtypeStruct((1, LANES), jnp.int32),
                   jax.ShapeDtypeStruct(sa_tm.shape, F32),
                   jax.ShapeDtypeStruct(sb_tm.shape, F32)],
        scratch_shapes=[pltpu.VMEM((n, D_B), BF16)],
        compiler_params=pltpu.CompilerParams(dimension_semantics=("arbitrary",),
                                             vmem_limit_bytes=VMEM_LIMIT),
        name="mixer_sample",
    )(x_tm, mod, *weights, sa_tm, sb_tm)


def _moe(counts, x1, h2, route, gt2_src, gt2_spec, p, grid, tile, rows):
    weights = [p["w1"], p["w3"], p["w2"], p["gf"]]
    sorted_rows = rows + SORT_PAD
    grid_spec = pltpu.PrefetchScalarGridSpec(
        num_scalar_prefetch=1,
        grid=grid,
        in_specs=[tile(D_MODEL), tile(D_MODEL), tile(ROUTER_LANES), gt2_spec]
        + [_const_spec(w.shape) for w in weights],
        out_specs=tile(D_MODEL),
        scratch_shapes=[pltpu.VMEM((sorted_rows, D_MODEL), BF16),
                        pltpu.VMEM((sorted_rows, ROUTER_LANES), F32),
                        pltpu.VMEM((sorted_rows, D_MODEL), F32)],
    )
    return pl.pallas_call(
        functools.partial(_moe_kernel, grid_rank=len(grid)),
        grid_spec=grid_spec,
        out_shape=jax.ShapeDtypeStruct(x1.shape, F32),
        compiler_params=pltpu.CompilerParams(dimension_semantics=("arbitrary",) * len(grid),
                                             vmem_limit_bytes=VMEM_LIMIT),
        name="moe",
    )(counts, x1, h2, route, gt2_src, *weights)


def kernel(x_prompt, x_sample, c_prompt, c_sample, state_conv_a, state_conv_b, w_ada, b_ada, norm1_g,
           norm2_g, w_in, conv_a_w, w_out_a, conv_b_w, conv_b_bias, ln_b_g, ln_b_b, w_out_b, w_o,
           w_group, b_group, w_expert, b_expert, w1, w3, w2, final_norm_g):
    depth = w_ada.shape[0]
    assert depth == 1, "the MoE kernel fuses the final norm, so exactly one layer is supported"
    n_p, seq, d = x_prompt.shape
    n_s, steps, _ = x_sample.shape
    c_all = jnp.concatenate([c_prompt, c_sample], axis=0)
    xp = x_prompt
    xs = x_sample.transpose(1, 0, 2).reshape(steps * n_s, d)
    gf = final_norm_g.reshape(1, d)
    pa, pb, sa, sb = [], [], [], []
    for l in range(depth):
        pad = ROUTER_LANES - N_EXPERTS - N_GROUPS
        wr = jnp.pad(jnp.concatenate([w_expert[l], w_group[l]], axis=1).T, ((0, pad), (0, 0)))
        br = jnp.pad(jnp.concatenate([b_expert[l], b_group[l]]), (0, pad))
        br = jnp.broadcast_to(br[:, None], (ROUTER_LANES, LANES))
        p = {
            "g1": norm1_g[l].reshape(1, d), "g2": norm2_g[l].reshape(1, d),
            "w_in": w_in[l].astype(BF16), "caw": conv_a_w[l], "woa": w_out_a[l].astype(BF16),
            "cbw": conv_b_w[l], "cbb": conv_b_bias[l].reshape(1, D_B),
            "lng": ln_b_g[l].reshape(1, D_B), "lnb": ln_b_b[l].reshape(1, D_B),
            "wob": w_out_b[l].astype(BF16), "wo": w_o[l].astype(BF16), "wr": wr.astype(BF16), "br": br,
            "gf": gf,
        }
        mod_p, mod_s = _modulation(c_all, n_p, w_ada[l], b_ada[l].reshape(1, N_MOD * d))

        x1p, h2p, route_p, cnt_p, na_p, nb_p, w1b, w3b, w2b = _mixer_prompt(
            xp, mod_p, p, (w1[l], w3[l], w2[l]))
        p.update(w1=w1b, w3=w3b, w2=w2b.reshape(N_GROUPS, EXPERTS_PER_GROUP * D_EXPERT, d))
        t = SEQ_TILE
        xp = _moe(cnt_p[:, 0, :N_GROUPS].reshape(-1), x1p, h2p, route_p,
                  mod_p.reshape(n_p, 1, N_MOD * d),
                  pl.BlockSpec((None, 1, d), lambda i, j, c: (i, 0, N_MOD - 1)), p, (n_p, seq // t),
                  lambda width: pl.BlockSpec((None, t, width), lambda i, j, c: (i, j, 0)), t)

        sa_tm = state_conv_a[l].transpose(1, 0, 2)
        sb_tm = state_conv_b[l].transpose(1, 0, 2)
        x1s, h2s, route_s, cnt_s, na_s, nb_s = _mixer_sample(xs, mod_s, sa_tm, sb_tm, p)
        xs = _moe(cnt_s[0, :N_GROUPS], x1s, h2s, route_s, mod_s,
                  pl.BlockSpec((n_s, d), lambda i, c: (0, N_MOD - 1)), p, (1,),
                  lambda width: pl.BlockSpec((steps * n_s, width), lambda i, c: (0, 0)), steps * n_s)
        pa.append(na_p)
        pb.append(nb_p)
        sa.append(na_s.transpose(1, 0, 2))
        sb.append(nb_s.transpose(1, 0, 2))
    y_sample = xs.reshape(steps, n_s, d).transpose(1, 0, 2)
    return (xp, y_sample, jnp.stack(pa), jnp.stack(pb), jnp.stack(sa), jnp.stack(sb))
```

```python
import functools

import jax
import jax.numpy as jnp
from jax import lax
from jax.experimental import pallas as pl
from jax.experimental.pallas import tpu as pltpu

D_MODEL = 1024
D_A = 512
D_B = 512
CONV_A_WIDTH = 3
CONV_B_WIDTH = 31
N_GROUPS = 4
EXPERTS_PER_GROUP = 4
N_EXPERTS = 16
D_EXPERT = 256
N_MOD = 6
EPS = 1e-6
D_IN = 3 * D_A + 2 * D_B + 2 * D_MODEL

LANES = 128
SUBLANES = 8
ROUTER_LANES = LANES
SEQ_TILE = 512
CONV_ROWS = 64
CONV_LANES = 2 * LANES
SORT_SHIFT = 4
SORT_ALIGN = 1 << SORT_SHIFT
SORT_PAD = LANES
MOE_PASS_ROWS = (128, 144, 160, 176, 192, 224, 256)
VMEM_LIMIT = 52 * 1024 * 1024

BF16 = jnp.bfloat16
F32 = jnp.float32


def _dot(a, b):
    return jnp.dot(a, b, preferred_element_type=F32)


def _rms(x, g):
    return x * lax.rsqrt(jnp.mean(x * x, axis=-1, keepdims=True) + EPS) * g


def _affine(v, scale, shift=None):
    r = scale.shape[0]
    if r != 1 and r != v.shape[0]:
        v3 = v.reshape(v.shape[0] // r, r, v.shape[-1])
        shift3 = None if shift is None else shift[None]
        return _affine(v3, scale[None], shift3).reshape(v.shape)
    out = v * scale
    return out if shift is None else out + shift


def _per_sequence(v, scale, shift=None):
    n = scale.shape[0]
    v3 = v.reshape(n, v.shape[0] // n, v.shape[-1])
    out = v3 * scale[:, None, :]
    if shift is not None:
        out = out + shift[:, None, :]
    return out.reshape(v.shape)


def _tap(v, w8):
    return (v.reshape(v.shape[0] // SUBLANES, SUBLANES, v.shape[-1]) * w8[None]).reshape(v.shape)


def _route(logits_t):
    t = logits_t.shape[1]
    neg = -jnp.inf
    row = lax.broadcasted_iota(jnp.int32, (SUBLANES, t), 0).astype(F32)
    live = row < float(EXPERTS_PER_GROUP)
    gl = jnp.where(live, logits_t[N_EXPERTS:N_EXPERTS + SUBLANES, :], neg)
    gmax = jnp.max(gl, axis=0, keepdims=True)
    g_idx = jnp.min(jnp.where(gl == gmax, row, float(SUBLANES)), axis=0, keepdims=True)
    p_g = 1.0 / jnp.sum(jnp.exp(gl - gmax), axis=0, keepdims=True)
    el = None
    for g in reversed(range(N_GROUPS)):
        slab = logits_t[(g // 2) * SUBLANES:(g // 2 + 1) * SUBLANES, :]
        cand = slab if g % 2 == 0 else pltpu.roll(slab, SUBLANES - EXPERTS_PER_GROUP, axis=0)
        el = cand if el is None else jnp.where(g_idx == float(g), cand, el)
    el = jnp.where(live, el, neg)
    v1 = jnp.max(el, axis=0, keepdims=True)
    i1 = jnp.min(jnp.where(el == v1, row, float(SUBLANES)), axis=0, keepdims=True)
    el2 = jnp.where(row == i1, neg, el)
    v2 = jnp.max(el2, axis=0, keepdims=True)
    i2 = jnp.min(jnp.where(el2 == v2, row, float(SUBLANES)), axis=0, keepdims=True)
    e21 = jnp.exp(v2 - v1)
    wt1 = p_g / (1.0 + e21)
    wt2 = p_g * e21 / (1.0 + e21)
    comb = jnp.where(row == i1, wt1, jnp.where(row == i2, wt2, 0.0))
    onehot = jnp.where(row == g_idx, 1.0, 0.0)
    onehot_b = jnp.concatenate([onehot, jnp.zeros_like(onehot)], axis=0).astype(BF16)
    earlier = lax.broadcasted_iota(jnp.int32, (t, t), 0) < lax.broadcasted_iota(jnp.int32, (t, t), 1)
    before = _dot(onehot_b, jnp.where(earlier, 1.0, 0.0).astype(BF16))[0:SUBLANES, :]
    rank = jnp.sum(before * onehot, axis=0, keepdims=True)
    count = jnp.sum(onehot, axis=1, keepdims=True)
    padded = jnp.ceil(count / SORT_ALIGN) * SORT_ALIGN
    row1 = lax.broadcasted_iota(jnp.int32, (SUBLANES, 1), 0)
    start = jnp.zeros((SUBLANES, 1), F32)
    for g in range(N_GROUPS - 1):
        start = start + jnp.where(row1 > g, padded[g:g + 1, :], 0.0)
    dest = jnp.sum(onehot * start, axis=0, keepdims=True) + rank
    route_t = jnp.concatenate([jnp.where(live, comb, dest), jnp.broadcast_to(dest, (LANES - SUBLANES, t))],
                              axis=0)
    onehot_all = jnp.concatenate([onehot, jnp.zeros((LANES - SUBLANES, t), F32)], axis=0).astype(BF16)
    count_row = lax.dot_general(jnp.ones((2 * SUBLANES, t), BF16), onehot_all, (((1,), (1,)), ((), ())),
                                preferred_element_type=F32)[0:1, :]
    return jnp.transpose(route_t), count_row


def _ln_silu(v, bias, g, b):
    v = v + bias
    mu = jnp.mean(v, axis=-1, keepdims=True)
    d = v - mu
    var = jnp.mean(d * d, axis=-1, keepdims=True)
    y = d * lax.rsqrt(var + EPS) * g + b
    return y * jax.nn.sigmoid(y)


def _mixer_tail(x, hb, mod, b_a_conv, yb_in, g2, w_in_ref, woa_ref, wob_ref, wo_ref, wr_ref, br_ref,
                x1_ref, h2_ref, route_ref, cnt_ref, affine):
    y_a = _dot(b_a_conv.astype(BF16), woa_ref[...])
    y_b = _dot(yb_in, wob_ref[...])
    o = 3 * D_A + 2 * D_B
    mg_a = _dot(hb, w_in_ref[:, o:o + D_MODEL])
    mg_b = _dot(hb, w_in_ref[:, o + D_MODEL:o + 2 * D_MODEL])
    merged = jax.nn.sigmoid(mg_a) * y_a + jax.nn.sigmoid(mg_b) * y_b
    gt1 = mod[:, 2 * D_MODEL:3 * D_MODEL]
    x1 = x + affine(_dot(merged.astype(BF16), wo_ref[...]), gt1)
    sh2 = mod[:, 3 * D_MODEL:4 * D_MODEL]
    sc2 = mod[:, 4 * D_MODEL:5 * D_MODEL]
    h2 = affine(_rms(x1, g2), 1.0 + sc2, sh2).astype(BF16)
    logits_t = lax.dot_general(wr_ref[...], h2, (((1,), (1,)), ((), ())), preferred_element_type=F32)
    logits_t = logits_t + jnp.tile(br_ref[...], (1, h2.shape[0] // LANES))
    x1_ref[...] = x1.reshape(x1_ref.shape)
    h2_ref[...] = h2.reshape(h2_ref.shape)
    route, count = _route(logits_t)
    route_ref[...] = route.reshape(route_ref.shape)
    cnt_ref[...] = count.astype(jnp.int32)


def _mixer_prompt_kernel(x_ref, mod_ref, g1_ref, g2_ref, w_in_ref, caw_ref, woa_ref, cbw_ref, cbb_ref,
                         lng_ref, lnb_ref, wob_ref, wo_ref, wr_ref, br_ref,
                         w1f_ref, w3f_ref, w2f_ref,
                         x1_ref, h2_ref, route_ref, cnt_ref, na_ref, nb_ref, w1b_ref, w3b_ref, w2b_ref,
                         hist_a, hist_b, conv_b_ref):
    n_seq, steps, d = x_ref.shape
    t = n_seq * steps
    rows_a = (CONV_A_WIDTH - 1) * n_seq
    rows_b = (CONV_B_WIDTH - 1) * n_seq
    w1b_ref[...] = w1f_ref[...].astype(BF16)
    w3b_ref[...] = w3f_ref[...].astype(BF16)
    w2b_ref[...] = w2f_ref[...].astype(BF16)

    @pl.when(pl.program_id(0) == 0)
    def _():
        hist_a[0:rows_a, :] = jnp.zeros((rows_a, D_A), F32)
        hist_b[0:rows_b, :] = jnp.zeros((rows_b, D_B), F32)

    def time_major(v):
        return jnp.swapaxes(v.reshape(n_seq, steps, v.shape[-1]), 0, 1).reshape(v.shape)

    def sequence_major(v):
        return jnp.swapaxes(v.reshape(steps, n_seq, v.shape[-1]), 0, 1).reshape(v.shape)

    x = x_ref[...].reshape(t, d)
    mod = mod_ref[...]
    h = _per_sequence(_rms(x, g1_ref[...]), 1.0 + mod[:, D_MODEL:2 * D_MODEL], mod[:, 0:D_MODEL])
    hb = h.astype(BF16)

    def proj(lo, width):
        return _dot(hb, w_in_ref[:, lo:lo + width])

    hist_a[rows_a:rows_a + t, :] = time_major(proj(D_A, D_A) * proj(2 * D_A, D_A))
    conv_a = None
    for k in range(CONV_A_WIDTH):
        term = _tap(hist_a[k * n_seq:k * n_seq + t, :], caw_ref[k])
        conv_a = term if conv_a is None else conv_a + term
    b_a_conv = proj(0, D_A) * sequence_major(conv_a)
    na_ref[...] = hist_a[t:t + rows_a, :].reshape(na_ref.shape)

    hist_b[rows_b:rows_b + t, :] = time_major(proj(3 * D_A, D_B) * jax.nn.sigmoid(proj(3 * D_A + D_B, D_B)))
    nb_ref[...] = hist_b[t:t + rows_b, :].reshape(nb_ref.shape)
    for r0 in range(0, t, CONV_ROWS):
        for c0 in range(0, D_B, CONV_LANES):
            cols = slice(c0, c0 + CONV_LANES)
            acc = None
            for k in range(CONV_B_WIDTH):
                o = r0 + k * n_seq
                term = _tap(hist_b[o:o + CONV_ROWS, cols], cbw_ref[k][:, cols])
                acc = term if acc is None else acc + term
            conv_b_ref[r0:r0 + CONV_ROWS, cols] = acc
    yb_in = _ln_silu(sequence_major(conv_b_ref[...]), cbb_ref[...], lng_ref[...], lnb_ref[...]).astype(BF16)

    hist_a[0:rows_a, :] = hist_a[t:t + rows_a, :]
    hist_b[0:rows_b, :] = hist_b[t:t + rows_b, :]

    _mixer_tail(x, hb, mod, b_a_conv, yb_in, g2_ref[...], w_in_ref, woa_ref, wob_ref, wo_ref,
                wr_ref, br_ref, x1_ref, h2_ref, route_ref, cnt_ref, _per_sequence)


def _mixer_sample_kernel(x_ref, mod_ref, g1_ref, g2_ref, w_in_ref, caw_ref, woa_ref, cbw_ref, cbb_ref,
                         lng_ref, lnb_ref, wob_ref, wo_ref, wr_ref, br_ref, sa_ref, sb_ref,
                         x1_ref, h2_ref, route_ref, cnt_ref, na_ref, nb_ref, yb_in_ref):
    n_seq = mod_ref.shape[0]
    steps = x_ref.shape[0] // n_seq
    x = x_ref[...]
    mod = mod_ref[...]
    h = _affine(_rms(x, g1_ref[...]), 1.0 + mod[:, D_MODEL:2 * D_MODEL], mod[:, 0:D_MODEL])
    hb = h.astype(BF16)

    def proj(lo, width):
        return _dot(hb, w_in_ref[:, lo:lo + width])

    def slab(v, j):
        return v[j * n_seq:(j + 1) * n_seq, :]

    u_a = proj(D_A, D_A) * proj(2 * D_A, D_A)
    full_a = [sa_ref[j] for j in range(CONV_A_WIDTH - 1)] + [slab(u_a, j) for j in range(steps)]
    conv_a = []
    for s in range(steps):
        acc = jnp.zeros((n_seq, D_A), F32)
        for k in range(CONV_A_WIDTH):
            acc = acc + _tap(full_a[s + k], caw_ref[k])
        conv_a.append(acc)
    b_a_conv = proj(0, D_A) * jnp.concatenate(conv_a, axis=0)
    for j in range(CONV_A_WIDTH - 1):
        na_ref[j] = full_a[steps + j]

    u_b = proj(3 * D_A, D_B) * jax.nn.sigmoid(proj(3 * D_A + D_B, D_B))
    hist = CONV_B_WIDTH - 1

    def full_b(j):
        return sb_ref[j] if j < hist else slab(u_b, j - hist)

    bias, lng, lnb = cbb_ref[...], lng_ref[...], lnb_ref[...]
    for s in range(steps):
        acc = jnp.zeros((n_seq, D_B), F32)
        for k in range(CONV_B_WIDTH):
            acc = acc + _tap(full_b(s + k), cbw_ref[k])
        yb_in_ref[s * n_seq:(s + 1) * n_seq, :] = _ln_silu(acc, bias, lng, lnb).astype(BF16)
    for j in range(hist):
        nb_ref[j] = full_b(steps + j)

    _mixer_tail(x, hb, mod, b_a_conv, yb_in_ref[...], g2_ref[...], w_in_ref, woa_ref, wob_ref, wo_ref,
                wr_ref, br_ref, x1_ref, h2_ref, route_ref, cnt_ref, _affine)


def _moe_kernel(cnt_ref, x1_ref, h2_ref, route_ref, gt2_ref, w1_ref, w3_ref, w2_ref, gf_ref, y_ref,
                h2s_ref, cs_ref, acc_ref, *, affine):
    def rows(ref):
        return ref[...].reshape(-1, ref.shape[-1])

    tp = h2s_ref.shape[0]
    tile = pl.program_id(0)
    route = rows(route_ref)
    t = route.shape[0]
    lane_i = lax.broadcasted_iota(jnp.int32, route.shape, 1)
    dest_row = jnp.transpose(route)[EXPERTS_PER_GROUP:EXPERTS_PER_GROUP + 1, :]
    slot = lax.broadcasted_iota(jnp.int32, (tp, t), 0).astype(F32)
    p_sort = jnp.where(slot == dest_row, 1.0, 0.0).astype(BF16)
    h2s_ref[...] = _dot(p_sort, rows(h2_ref)).astype(BF16)
    comb = jnp.where(lane_i < EXPERTS_PER_GROUP, route, 0.0)
    c_hi = comb.astype(BF16).astype(F32)
    c_mid = (comb - c_hi).astype(BF16).astype(F32)
    c_lo = comb - c_hi - c_mid
    packed = (c_hi + pltpu.roll(c_mid, EXPERTS_PER_GROUP, axis=1)
              + pltpu.roll(c_lo, 2 * EXPERTS_PER_GROUP, axis=1)).astype(BF16)
    moved = _dot(p_sort, packed)
    cs_ref[...] = (moved + pltpu.roll(moved, LANES - EXPERTS_PER_GROUP, axis=1)
                   + pltpu.roll(moved, LANES - 2 * EXPERTS_PER_GROUP, axis=1))
    acc_ref[...] = jnp.zeros(acc_ref.shape, F32)

    def expert_pass(rows, g, lo, end):
        off = pl.multiple_of(jnp.minimum(lo, tp - rows), SORT_ALIGN)
        row_id = off + lax.broadcasted_iota(jnp.int32, (rows, 1), 0)
        valid = (row_id >= lo) & (row_id < end)
        hb = h2s_ref[pl.ds(off, rows), :]
        c4 = jnp.where(valid, cs_ref[pl.ds(off, rows), :], 0.0)
        experts = [g * EXPERTS_PER_GROUP + e for e in range(EXPERTS_PER_GROUP)]
        gate = jnp.concatenate([_dot(hb, w1_ref[e]) for e in experts], axis=1)
        up = jnp.concatenate([_dot(hb, w3_ref[e]) for e in experts], axis=1)
        cexp = jnp.concatenate(
            [jnp.broadcast_to(c4[:, e:e + 1], (rows, D_EXPERT)) for e in range(EXPERTS_PER_GROUP)], axis=1)
        a = gate * jax.nn.sigmoid(gate) * up * cexp
        acc_ref[pl.ds(off, rows), :] += _dot(a.astype(BF16), w2_ref[g])

    def group(g, start):
        n = cnt_ref[tile * N_GROUPS + g]
        end = start + n

        def passes(rows):
            def run():
                def body(p, carry):
                    expert_pass(rows, g, start + p * rows, end)
                    return carry
                lax.fori_loop(0, lax.div(n + (rows - 1), rows), body, 0)
            return run

        size_class = sum((n > rows).astype(jnp.int32) for rows in MOE_PASS_ROWS[:-1])
        lax.switch(size_class, [passes(rows) for rows in MOE_PASS_ROWS])
        return start + lax.shift_right_logical(n + (SORT_ALIGN - 1), SORT_SHIFT) * SORT_ALIGN

    lax.fori_loop(0, N_GROUPS, group, jnp.int32(0))

    dest_col = route[:, EXPERTS_PER_GROUP:EXPERTS_PER_GROUP + 1]
    p_back = jnp.where(lax.broadcasted_iota(jnp.int32, (t, tp), 1).astype(F32) == dest_col,
                       1.0, 0.0).astype(BF16)
    y = acc_ref[...]
    y_hi = y.astype(BF16)
    y_lo = (y - y_hi.astype(F32)).astype(BF16)
    y_tok = _dot(p_back, y_hi) + _dot(p_back, y_lo)
    x2 = rows(x1_ref) + affine(y_tok, gt2_ref[...])
    y_ref[...] = _rms(x2, gf_ref[...]).reshape(y_ref.shape)


def _mod_kernel(c_ref, w_ref, b_ref, op_ref, os_ref):
    c = c_ref[...]
    a = (c * jax.nn.sigmoid(c)).astype(BF16)
    mod = _dot(a, w_ref[...].astype(BF16)) + b_ref[...]
    n_p = op_ref.shape[0]
    op_ref[...] = mod[:n_p]
    os_ref[...] = mod[n_p:]


def _const_spec(shape):
    nd = len(shape)
    return pl.BlockSpec(shape, lambda *_: (0,) * nd, pipeline_mode=pl.Buffered(1))


def _modulation(c_all, n_p, w_ada, b_ada):
    n, d = c_all.shape
    n_out = w_ada.shape[1]
    assert n_p % SUBLANES == 0
    return pl.pallas_call(
        _mod_kernel,
        grid=(n_out // D_MODEL,),
        in_specs=[pl.BlockSpec((n, d), lambda j: (0, 0)),
                  pl.BlockSpec((d, D_MODEL), lambda j: (0, j)),
                  pl.BlockSpec((1, D_MODEL), lambda j: (0, j))],
        out_specs=[pl.BlockSpec((n_p, D_MODEL), lambda j: (0, j)),
                   pl.BlockSpec((n - n_p, D_MODEL), lambda j: (0, j))],
        out_shape=[jax.ShapeDtypeStruct((n_p, n_out), F32), jax.ShapeDtypeStruct((n - n_p, n_out), F32)],
        compiler_params=pltpu.CompilerParams(dimension_semantics=("arbitrary",),
                                             vmem_limit_bytes=VMEM_LIMIT),
        name="modulation",
    )(c_all, w_ada, b_ada)


def _mixer_weight_specs(p):
    names = ("g1", "g2", "w_in", "caw", "woa", "cbw", "cbb", "lng", "lnb", "wob", "wo", "wr", "br")
    return [p[k] for k in names], [_const_spec(p[k].shape) for k in names]


def _mixer_prompt(x, mod, p, expert_weights):
    b, s, d = x.shape
    assert b == SUBLANES, "the time-major conv layout puts one sequence on each sublane"
    steps = SEQ_TILE // b
    tiles = s // steps
    weights, wspecs = _mixer_weight_specs(p)
    tile = lambda width: pl.BlockSpec((b, steps, width), lambda c: (0, c, 0))
    whole = lambda *shape: pl.BlockSpec(shape, lambda c: (0,) * len(shape))

    def cast_spec(w):
        experts, rows, cols = w.shape
        per_expert = tiles // experts
        assert per_expert * experts == tiles and rows % (per_expert * 2 * SUBLANES) == 0
        return pl.BlockSpec((1, rows // per_expert, cols), lambda c: (c // per_expert, lax.rem(c, per_expert), 0))

    cast_specs = [cast_spec(w) for w in expert_weights]
    return pl.pallas_call(
        _mixer_prompt_kernel,
        grid=(tiles,),
        in_specs=[tile(d), whole(b, N_MOD * d)] + wspecs + cast_specs,
        out_specs=[tile(d), tile(d), tile(ROUTER_LANES),
                   pl.BlockSpec((None, 1, LANES), lambda c: (c, 0, 0)),
                   whole(CONV_A_WIDTH - 1, b, D_A), whole(CONV_B_WIDTH - 1, b, D_B)] + cast_specs,
        out_shape=[jax.ShapeDtypeStruct((b, s, d), F32),
                   jax.ShapeDtypeStruct((b, s, d), BF16),
                   jax.ShapeDtypeStruct((b, s, ROUTER_LANES), F32),
                   jax.ShapeDtypeStruct((tiles, 1, LANES), jnp.int32),
                   jax.ShapeDtypeStruct((CONV_A_WIDTH - 1, b, D_A), F32),
                   jax.ShapeDtypeStruct((CONV_B_WIDTH - 1, b, D_B), F32)]
        + [jax.ShapeDtypeStruct(w.shape, BF16) for w in expert_weights],
        scratch_shapes=[pltpu.VMEM(((CONV_A_WIDTH - 1) * b + SEQ_TILE, D_A), F32),
                        pltpu.VMEM(((CONV_B_WIDTH - 1) * b + SEQ_TILE, D_B), F32),
                        pltpu.VMEM((SEQ_TILE, D_B), F32)],
        compiler_params=pltpu.CompilerParams(dimension_semantics=("arbitrary",),
                                             vmem_limit_bytes=VMEM_LIMIT),
        name="mixer_prompt",
    )(x, mod, *weights, *expert_weights)


def _mixer_sample(x_tm, mod, sa_tm, sb_tm, p):
    n, d = x_tm.shape
    weights, wspecs = _mixer_weight_specs(p)
    full = lambda a: pl.BlockSpec(a.shape, lambda i: (0,) * a.ndim)
    whole = lambda *shape: pl.BlockSpec(shape, lambda i: (0,) * len(shape))
    return pl.pallas_call(
        _mixer_sample_kernel,
        grid=(1,),
        in_specs=[full(x_tm), full(mod)] + wspecs + [full(sa_tm), full(sb_tm)],
        out_specs=[whole(n, d), whole(n, d), whole(n, ROUTER_LANES), whole(1, LANES),
                   full(sa_tm), full(sb_tm)],
        out_shape=[jax.ShapeDtypeStruct((n, d), F32),
                   jax.ShapeDtypeStruct((n, d), BF16),
                   jax.ShapeDtypeStruct((n, ROUTER_LANES), F32),
                   jax.ShapeDtypeStruct((1, LANES), jnp.int32),
                   jax.ShapeDtypeStruct(sa_tm.shape, F32),
                   jax.ShapeDtypeStruct(sb_tm.shape, F32)],
        scratch_shapes=[pltpu.VMEM((n, D_B), BF16)],
        compiler_params=pltpu.CompilerParams(dimension_semantics=("arbitrary",),
                                             vmem_limit_bytes=VMEM_LIMIT),
        name="mixer_sample",
    )(x_tm, mod, *weights, sa_tm, sb_tm)


def _moe(counts, x1, h2, route, gt2_src, gt2_spec, p, grid, tile, rows, affine):
    weights = [p["w1"], p["w3"], p["w2"], p["gf"]]
    sorted_rows = rows + SORT_PAD
    grid_spec = pltpu.PrefetchScalarGridSpec(
        num_scalar_prefetch=1,
        grid=grid,
        in_specs=[tile(D_MODEL), tile(D_MODEL), tile(ROUTER_LANES), gt2_spec]
        + [_const_spec(w.shape) for w in weights],
        out_specs=tile(D_MODEL),
        scratch_shapes=[pltpu.VMEM((sorted_rows, D_MODEL), BF16),
                        pltpu.VMEM((sorted_rows, ROUTER_LANES), F32),
                        pltpu.VMEM((sorted_rows, D_MODEL), F32)],
    )
    return pl.pallas_call(
        functools.partial(_moe_kernel, affine=affine),
        grid_spec=grid_spec,
        out_shape=jax.ShapeDtypeStruct(x1.shape, F32),
        compiler_params=pltpu.CompilerParams(dimension_semantics=("arbitrary",) * len(grid),
                                             vmem_limit_bytes=VMEM_LIMIT),
        name="moe",
    )(counts, x1, h2, route, gt2_src, *weights)


def kernel(x_prompt, x_sample, c_prompt, c_sample, state_conv_a, state_conv_b, w_ada, b_ada, norm1_g,
           norm2_g, w_in, conv_a_w, w_out_a, conv_b_w, conv_b_bias, ln_b_g, ln_b_b, w_out_b, w_o,
           w_group, b_group, w_expert, b_expert, w1, w3, w2, final_norm_g):
    depth = w_ada.shape[0]
    assert depth == 1, "the MoE kernel fuses the final norm, so exactly one layer is supported"
    n_p, seq, d = x_prompt.shape
    n_s, steps, _ = x_sample.shape
    c_all = jnp.concatenate([c_prompt, c_sample], axis=0)
    xp = x_prompt
    xs = x_sample.transpose(1, 0, 2).reshape(steps * n_s, d)
    gf = final_norm_g.reshape(1, d)
    pa, pb, sa, sb = [], [], [], []
    for l in range(depth):
        pad = ROUTER_LANES - N_EXPERTS - N_GROUPS
        wr = jnp.pad(jnp.concatenate([w_expert[l], w_group[l]], axis=1).T, ((0, pad), (0, 0)))
        br = jnp.pad(jnp.concatenate([b_expert[l], b_group[l]]), (0, pad))
        br = jnp.broadcast_to(br[:, None], (ROUTER_LANES, LANES))
        p = {
            "g1": norm1_g[l].reshape(1, d), "g2": norm2_g[l].reshape(1, d),
            "w_in": w_in[l].astype(BF16), "woa": w_out_a[l].astype(BF16),
            "caw": jnp.broadcast_to(conv_a_w[l][:, None, :], (CONV_A_WIDTH, SUBLANES, D_A)),
            "cbw": jnp.broadcast_to(conv_b_w[l][:, None, :], (CONV_B_WIDTH, SUBLANES, D_B)),
            "cbb": conv_b_bias[l].reshape(1, D_B),
            "lng": ln_b_g[l].reshape(1, D_B), "lnb": ln_b_b[l].reshape(1, D_B),
            "wob": w_out_b[l].astype(BF16), "wo": w_o[l].astype(BF16), "wr": wr.astype(BF16), "br": br,
            "gf": gf,
        }
        mod_p, mod_s = _modulation(c_all, n_p, w_ada[l], b_ada[l].reshape(1, N_MOD * d))

        x1p, h2p, route_p, cnt_p, na_p, nb_p, w1b, w3b, w2b = _mixer_prompt(
            xp, mod_p, p, (w1[l], w3[l], w2[l]))
        p.update(w1=w1b, w3=w3b, w2=w2b.reshape(N_GROUPS, EXPERTS_PER_GROUP * D_EXPERT, d))
        tile_steps = SEQ_TILE // n_p
        xp = _moe(cnt_p[:, 0, :N_GROUPS].reshape(-1), x1p, h2p, route_p, mod_p,
                  pl.BlockSpec((n_p, d), lambda c, cnt: (0, N_MOD - 1)), p, (seq // tile_steps,),
                  lambda width: pl.BlockSpec((n_p, tile_steps, width), lambda c, cnt: (0, c, 0)), SEQ_TILE,
                  _per_sequence)

        sa_tm = state_conv_a[l].transpose(1, 0, 2)
        sb_tm = state_conv_b[l].transpose(1, 0, 2)
        x1s, h2s, route_s, cnt_s, na_s, nb_s = _mixer_sample(xs, mod_s, sa_tm, sb_tm, p)
        xs = _moe(cnt_s[0, :N_GROUPS], x1s, h2s, route_s, mod_s,
                  pl.BlockSpec((n_s, d), lambda i, c: (0, N_MOD - 1)), p, (1,),
                  lambda width: pl.BlockSpec((steps * n_s, width), lambda i, c: (0, 0)), steps * n_s, _affine)
        pa.append(na_p.transpose(1, 0, 2))
        pb.append(nb_p.transpose(1, 0, 2))
        sa.append(na_s.transpose(1, 0, 2))
        sb.append(nb_s.transpose(1, 0, 2))
    y_sample = xs.reshape(steps, n_s, d).transpose(1, 0, 2)
    return (xp, y_sample, jnp.stack(pa), jnp.stack(pb), jnp.stack(sa), jnp.stack(sb))
```

```python
import functools

import jax
import jax.numpy as jnp
from jax import lax
from jax.experimental import pallas as pl
from jax.experimental.pallas import tpu as pltpu

D_MODEL = 1024
D_A = 512
D_B = 512
CONV_A_WIDTH = 3
CONV_B_WIDTH = 31
N_GROUPS = 4
EXPERTS_PER_GROUP = 4
N_EXPERTS = 16
D_EXPERT = 256
N_MOD = 6
EPS = 1e-6
D_IN = 3 * D_A + 2 * D_B + 2 * D_MODEL

LANES = 128
SUBLANES = 8
ROUTER_LANES = LANES
SEQ_TILE = 512
CONV_ROWS = 64
CONV_LANES = 2 * LANES
SORT_SHIFT = 4
SORT_ALIGN = 1 << SORT_SHIFT
MOE_PASS_ROWS = (128, 144, 160, 176, 192, 224, 256)
VMEM_LIMIT = 52 * 1024 * 1024

BF16 = jnp.bfloat16
F32 = jnp.float32


def _dot(a, b):
    return jnp.dot(a, b, preferred_element_type=F32)


def _rms(x, g):
    return x * lax.rsqrt(jnp.mean(x * x, axis=-1, keepdims=True) + EPS) * g


def _affine(v, scale, shift=None):
    r = scale.shape[0]
    if r != 1 and r != v.shape[0]:
        v3 = v.reshape(v.shape[0] // r, r, v.shape[-1])
        shift3 = None if shift is None else shift[None]
        return _affine(v3, scale[None], shift3).reshape(v.shape)
    out = v * scale
    return out if shift is None else out + shift


def _per_sequence(v, scale, shift=None):
    n = scale.shape[0]
    v3 = v.reshape(n, v.shape[0] // n, v.shape[-1])
    out = v3 * scale[:, None, :]
    if shift is not None:
        out = out + shift[:, None, :]
    return out.reshape(v.shape)


def _tap(v, w8):
    return (v.reshape(v.shape[0] // SUBLANES, SUBLANES, v.shape[-1]) * w8[None]).reshape(v.shape)


def _route(logits_t):
    t = logits_t.shape[1]
    neg = -jnp.inf
    row = lax.broadcasted_iota(jnp.int32, (SUBLANES, t), 0).astype(F32)
    live = row < float(EXPERTS_PER_GROUP)
    gl = jnp.where(live, logits_t[N_EXPERTS:N_EXPERTS + SUBLANES, :], neg)
    gmax = jnp.max(gl, axis=0, keepdims=True)
    g_idx = jnp.min(jnp.where(gl == gmax, row, float(SUBLANES)), axis=0, keepdims=True)
    p_g = 1.0 / jnp.sum(jnp.exp(gl - gmax), axis=0, keepdims=True)
    el = None
    for g in reversed(range(N_GROUPS)):
        slab = logits_t[(g // 2) * SUBLANES:(g // 2 + 1) * SUBLANES, :]
        cand = slab if g % 2 == 0 else pltpu.roll(slab, SUBLANES - EXPERTS_PER_GROUP, axis=0)
        el = cand if el is None else jnp.where(g_idx == float(g), cand, el)
    el = jnp.where(live, el, neg)
    v1 = jnp.max(el, axis=0, keepdims=True)
    i1 = jnp.min(jnp.where(el == v1, row, float(SUBLANES)), axis=0, keepdims=True)
    el2 = jnp.where(row == i1, neg, el)
    v2 = jnp.max(el2, axis=0, keepdims=True)
    i2 = jnp.min(jnp.where(el2 == v2, row, float(SUBLANES)), axis=0, keepdims=True)
    e21 = jnp.exp(v2 - v1)
    wt1 = p_g / (1.0 + e21)
    wt2 = p_g * e21 / (1.0 + e21)
    comb = jnp.where(row == i1, wt1, jnp.where(row == i2, wt2, 0.0))
    onehot = jnp.where(row == g_idx, 1.0, 0.0)
    onehot_b = jnp.concatenate([onehot, jnp.zeros_like(onehot)], axis=0).astype(BF16)
    earlier = lax.broadcasted_iota(jnp.int32, (t, t), 0) < lax.broadcasted_iota(jnp.int32, (t, t), 1)
    before = _dot(onehot_b, jnp.where(earlier, 1.0, 0.0).astype(BF16))[0:SUBLANES, :]
    rank = jnp.sum(before * onehot, axis=0, keepdims=True)
    count = jnp.sum(onehot, axis=1, keepdims=True)
    row1 = lax.broadcasted_iota(jnp.int32, (SUBLANES, 1), 0)
    start = jnp.zeros((SUBLANES, 1), F32)
    for g in range(N_GROUPS - 1):
        start = start + jnp.where(row1 > g, count[g:g + 1, :], 0.0)
    dest = jnp.sum(onehot * start, axis=0, keepdims=True) + rank
    route_t = jnp.concatenate([jnp.where(live, comb, dest), jnp.broadcast_to(dest, (LANES - SUBLANES, t))],
                              axis=0)
    onehot_all = jnp.concatenate([onehot, jnp.zeros((LANES - SUBLANES, t), F32)], axis=0).astype(BF16)
    count_row = lax.dot_general(jnp.ones((2 * SUBLANES, t), BF16), onehot_all, (((1,), (1,)), ((), ())),
                                preferred_element_type=F32)[0:1, :]
    return jnp.transpose(route_t), count_row


def _ln_silu(v, bias, g, b):
    v = v + bias
    mu = jnp.mean(v, axis=-1, keepdims=True)
    d = v - mu
    var = jnp.mean(d * d, axis=-1, keepdims=True)
    y = d * lax.rsqrt(var + EPS) * g + b
    return y * jax.nn.sigmoid(y)


def _mixer_tail(x, hb, mod, b_a_conv, yb_in, g2, w_in_ref, woa_ref, wob_ref, wo_ref, wr_ref, br_ref,
                x1_ref, h2_ref, route_ref, cnt_ref, affine):
    y_a = _dot(b_a_conv.astype(BF16), woa_ref[...])
    y_b = _dot(yb_in, wob_ref[...])
    o = 3 * D_A + 2 * D_B
    mg_a = _dot(hb, w_in_ref[:, o:o + D_MODEL])
    mg_b = _dot(hb, w_in_ref[:, o + D_MODEL:o + 2 * D_MODEL])
    merged = jax.nn.sigmoid(mg_a) * y_a + jax.nn.sigmoid(mg_b) * y_b
    gt1 = mod[:, 2 * D_MODEL:3 * D_MODEL]
    x1 = x + affine(_dot(merged.astype(BF16), wo_ref[...]), gt1)
    sh2 = mod[:, 3 * D_MODEL:4 * D_MODEL]
    sc2 = mod[:, 4 * D_MODEL:5 * D_MODEL]
    h2 = affine(_rms(x1, g2), 1.0 + sc2, sh2).astype(BF16)
    logits_t = lax.dot_general(wr_ref[...], h2, (((1,), (1,)), ((), ())), preferred_element_type=F32)
    logits_t = logits_t + jnp.tile(br_ref[...], (1, h2.shape[0] // LANES))
    x1_ref[...] = x1.reshape(x1_ref.shape)
    h2_ref[...] = h2.reshape(h2_ref.shape)
    route, count = _route(logits_t)
    route_ref[...] = route.reshape(route_ref.shape)
    cnt_ref[...] = count.astype(jnp.int32)


def _mixer_prompt_kernel(x_ref, mod_ref, g1_ref, g2_ref, w_in_ref, caw_ref, woa_ref, cbw_ref, cbb_ref,
                         lng_ref, lnb_ref, wob_ref, wo_ref, wr_ref, br_ref,
                         w1f_ref, w3f_ref, w2f_ref,
                         x1_ref, h2_ref, route_ref, cnt_ref, na_ref, nb_ref, w1b_ref, w3b_ref, w2b_ref,
                         hist_a, hist_b, conv_b_ref):
    n_seq, steps, d = x_ref.shape
    t = n_seq * steps
    rows_a = (CONV_A_WIDTH - 1) * n_seq
    rows_b = (CONV_B_WIDTH - 1) * n_seq
    w1b_ref[...] = w1f_ref[...].astype(BF16)
    w3b_ref[...] = w3f_ref[...].astype(BF16)
    w2b_ref[...] = w2f_ref[...].astype(BF16)

    @pl.when(pl.program_id(0) == 0)
    def _():
        hist_a[0:rows_a, :] = jnp.zeros((rows_a, D_A), F32)
        hist_b[0:rows_b, :] = jnp.zeros((rows_b, D_B), F32)

    def time_major(v):
        return jnp.swapaxes(v.reshape(n_seq, steps, v.shape[-1]), 0, 1).reshape(v.shape)

    def sequence_major(v):
        return jnp.swapaxes(v.reshape(steps, n_seq, v.shape[-1]), 0, 1).reshape(v.shape)

    x = x_ref[...].reshape(t, d)
    mod = mod_ref[...]
    h = _per_sequence(_rms(x, g1_ref[...]), 1.0 + mod[:, D_MODEL:2 * D_MODEL], mod[:, 0:D_MODEL])
    hb = h.astype(BF16)

    def proj(lo, width):
        return _dot(hb, w_in_ref[:, lo:lo + width])

    hist_a[rows_a:rows_a + t, :] = time_major(proj(D_A, D_A) * proj(2 * D_A, D_A))
    conv_a = None
    for k in range(CONV_A_WIDTH):
        term = _tap(hist_a[k * n_seq:k * n_seq + t, :], caw_ref[k])
        conv_a = term if conv_a is None else conv_a + term
    b_a_conv = proj(0, D_A) * sequence_major(conv_a)
    na_ref[...] = hist_a[t:t + rows_a, :].reshape(na_ref.shape)

    hist_b[rows_b:rows_b + t, :] = time_major(proj(3 * D_A, D_B) * jax.nn.sigmoid(proj(3 * D_A + D_B, D_B)))
    nb_ref[...] = hist_b[t:t + rows_b, :].reshape(nb_ref.shape)
    for r0 in range(0, t, CONV_ROWS):
        for c0 in range(0, D_B, CONV_LANES):
            cols = slice(c0, c0 + CONV_LANES)
            acc = None
            for k in range(CONV_B_WIDTH):
                o = r0 + k * n_seq
                term = _tap(hist_b[o:o + CONV_ROWS, cols], cbw_ref[k][:, cols])
                acc = term if acc is None else acc + term
            conv_b_ref[r0:r0 + CONV_ROWS, cols] = acc
    yb_in = _ln_silu(sequence_major(conv_b_ref[...]), cbb_ref[...], lng_ref[...], lnb_ref[...]).astype(BF16)

    hist_a[0:rows_a, :] = hist_a[t:t + rows_a, :]
    hist_b[0:rows_b, :] = hist_b[t:t + rows_b, :]

    _mixer_tail(x, hb, mod, b_a_conv, yb_in, g2_ref[...], w_in_ref, woa_ref, wob_ref, wo_ref,
                wr_ref, br_ref, x1_ref, h2_ref, route_ref, cnt_ref, _per_sequence)


def _mixer_sample_kernel(x_ref, mod_ref, g1_ref, g2_ref, w_in_ref, caw_ref, woa_ref, cbw_ref, cbb_ref,
                         lng_ref, lnb_ref, wob_ref, wo_ref, wr_ref, br_ref, sa_ref, sb_ref,
                         x1_ref, h2_ref, route_ref, cnt_ref, na_ref, nb_ref, yb_in_ref):
    n_seq = mod_ref.shape[0]
    steps = x_ref.shape[0] // n_seq
    x = x_ref[...]
    mod = mod_ref[...]
    h = _affine(_rms(x, g1_ref[...]), 1.0 + mod[:, D_MODEL:2 * D_MODEL], mod[:, 0:D_MODEL])
    hb = h.astype(BF16)

    def proj(lo, width):
        return _dot(hb, w_in_ref[:, lo:lo + width])

    def slab(v, j):
        return v[j * n_seq:(j + 1) * n_seq, :]

    u_a = proj(D_A, D_A) * proj(2 * D_A, D_A)
    full_a = [sa_ref[j] for j in range(CONV_A_WIDTH - 1)] + [slab(u_a, j) for j in range(steps)]
    conv_a = []
    for s in range(steps):
        acc = jnp.zeros((n_seq, D_A), F32)
        for k in range(CONV_A_WIDTH):
            acc = acc + _tap(full_a[s + k], caw_ref[k])
        conv_a.append(acc)
    b_a_conv = proj(0, D_A) * jnp.concatenate(conv_a, axis=0)
    for j in range(CONV_A_WIDTH - 1):
        na_ref[j] = full_a[steps + j]

    u_b = proj(3 * D_A, D_B) * jax.nn.sigmoid(proj(3 * D_A + D_B, D_B))
    hist = CONV_B_WIDTH - 1

    def full_b(j):
        return sb_ref[j] if j < hist else slab(u_b, j - hist)

    bias, lng, lnb = cbb_ref[...], lng_ref[...], lnb_ref[...]
    for s in range(steps):
        acc = jnp.zeros((n_seq, D_B), F32)
        for k in range(CONV_B_WIDTH):
            acc = acc + _tap(full_b(s + k), cbw_ref[k])
        yb_in_ref[s * n_seq:(s + 1) * n_seq, :] = _ln_silu(acc, bias, lng, lnb).astype(BF16)
    for j in range(hist):
        nb_ref[j] = full_b(steps + j)

    _mixer_tail(x, hb, mod, b_a_conv, yb_in_ref[...], g2_ref[...], w_in_ref, woa_ref, wob_ref, wo_ref,
                wr_ref, br_ref, x1_ref, h2_ref, route_ref, cnt_ref, _affine)


def _moe_kernel(cnt_ref, x1_ref, h2_ref, route_ref, gt2_ref, w1_ref, w3_ref, w2_ref, gf_ref, y_ref,
                h2s_ref, cs_ref, acc_ref, *, affine):
    def rows(ref):
        return ref[...].reshape(-1, ref.shape[-1])

    tp = h2s_ref.shape[0]
    tile = pl.program_id(0)
    route = rows(route_ref)
    t = route.shape[0]
    lane_i = lax.broadcasted_iota(jnp.int32, route.shape, 1)
    dest_row = jnp.transpose(route)[EXPERTS_PER_GROUP:EXPERTS_PER_GROUP + 1, :]
    slot = lax.broadcasted_iota(jnp.int32, (tp, t), 0).astype(F32)
    p_sort = jnp.where(slot == dest_row, 1.0, 0.0).astype(BF16)
    h2s_ref[...] = _dot(p_sort, rows(h2_ref)).astype(BF16)
    comb = jnp.where(lane_i < EXPERTS_PER_GROUP, route, 0.0)
    c_hi = comb.astype(BF16).astype(F32)
    c_mid = (comb - c_hi).astype(BF16).astype(F32)
    c_lo = comb - c_hi - c_mid
    packed = (c_hi + pltpu.roll(c_mid, EXPERTS_PER_GROUP, axis=1)
              + pltpu.roll(c_lo, 2 * EXPERTS_PER_GROUP, axis=1)).astype(BF16)
    moved = _dot(p_sort, packed)
    cs_ref[...] = (moved + pltpu.roll(moved, LANES - EXPERTS_PER_GROUP, axis=1)
                   + pltpu.roll(moved, LANES - 2 * EXPERTS_PER_GROUP, axis=1))
    acc_ref[...] = jnp.zeros(acc_ref.shape, F32)

    def expert_pass(rows, g, window, lo, end):
        off = pl.multiple_of(jnp.minimum(window, tp - rows), SORT_ALIGN)
        row_id = off + lax.broadcasted_iota(jnp.int32, (rows, 1), 0)
        valid = (row_id >= jnp.maximum(lo, window)) & (row_id < end)
        hb = h2s_ref[pl.ds(off, rows), :]
        c4 = jnp.where(valid, cs_ref[pl.ds(off, rows), :], 0.0)
        experts = [g * EXPERTS_PER_GROUP + e for e in range(EXPERTS_PER_GROUP)]
        gate = jnp.concatenate([_dot(hb, w1_ref[e]) for e in experts], axis=1)
        up = jnp.concatenate([_dot(hb, w3_ref[e]) for e in experts], axis=1)
        cexp = jnp.concatenate(
            [jnp.broadcast_to(c4[:, e:e + 1], (rows, D_EXPERT)) for e in range(EXPERTS_PER_GROUP)], axis=1)
        a = gate * jax.nn.sigmoid(gate) * up * cexp
        acc_ref[pl.ds(off, rows), :] += _dot(a.astype(BF16), w2_ref[g])

    def group(g, start):
        end = start + cnt_ref[tile * N_GROUPS + g]
        base = lax.shift_right_logical(start, SORT_SHIFT) * SORT_ALIGN
        span = end - base

        def passes(rows):
            def run():
                def body(p, carry):
                    expert_pass(rows, g, base + p * rows, start, end)
                    return carry
                lax.fori_loop(0, jnp.where(end > start, lax.div(span + (rows - 1), rows), 0), body, 0)
            return run

        size_class = sum((span > rows).astype(jnp.int32) for rows in MOE_PASS_ROWS[:-1])
        lax.switch(size_class, [passes(rows) for rows in MOE_PASS_ROWS])
        return end

    lax.fori_loop(0, N_GROUPS, group, jnp.int32(0))

    dest_col = route[:, EXPERTS_PER_GROUP:EXPERTS_PER_GROUP + 1]
    p_back = jnp.where(lax.broadcasted_iota(jnp.int32, (t, tp), 1).astype(F32) == dest_col,
                       1.0, 0.0).astype(BF16)
    y = acc_ref[...]
    y_hi = y.astype(BF16)
    y_lo = (y - y_hi.astype(F32)).astype(BF16)
    y_tok = _dot(p_back, y_hi) + _dot(p_back, y_lo)
    x2 = rows(x1_ref) + affine(y_tok, gt2_ref[...])
    y_ref[...] = _rms(x2, gf_ref[...]).reshape(y_ref.shape)


def _mod_kernel(c_ref, w_ref, b_ref, op_ref, os_ref):
    c = c_ref[...]
    a = (c * jax.nn.sigmoid(c)).astype(BF16)
    mod = _dot(a, w_ref[...].astype(BF16)) + b_ref[...]
    n_p = op_ref.shape[0]
    op_ref[...] = mod[:n_p]
    os_ref[...] = mod[n_p:]


def _const_spec(shape):
    nd = len(shape)
    return pl.BlockSpec(shape, lambda *_: (0,) * nd, pipeline_mode=pl.Buffered(1))


def _modulation(c_all, n_p, w_ada, b_ada):
    n, d = c_all.shape
    n_out = w_ada.shape[1]
    assert n_p % SUBLANES == 0
    return pl.pallas_call(
        _mod_kernel,
        grid=(n_out // D_MODEL,),
        in_specs=[pl.BlockSpec((n, d), lambda j: (0, 0)),
                  pl.BlockSpec((d, D_MODEL), lambda j: (0, j)),
                  pl.BlockSpec((1, D_MODEL), lambda j: (0, j))],
        out_specs=[pl.BlockSpec((n_p, D_MODEL), lambda j: (0, j)),
                   pl.BlockSpec((n - n_p, D_MODEL), lambda j: (0, j))],
        out_shape=[jax.ShapeDtypeStruct((n_p, n_out), F32), jax.ShapeDtypeStruct((n - n_p, n_out), F32)],
        compiler_params=pltpu.CompilerParams(dimension_semantics=("arbitrary",),
                                             vmem_limit_bytes=VMEM_LIMIT),
        name="modulation",
    )(c_all, w_ada, b_ada)


def _mixer_weight_specs(p):
    names = ("g1", "g2", "w_in", "caw", "woa", "cbw", "cbb", "lng", "lnb", "wob", "wo", "wr", "br")
    return [p[k] for k in names], [_const_spec(p[k].shape) for k in names]


def _mixer_prompt(x, mod, p, expert_weights):
    b, s, d = x.shape
    assert b == SUBLANES, "the time-major conv layout puts one sequence on each sublane"
    steps = SEQ_TILE // b
    tiles = s // steps
    weights, wspecs = _mixer_weight_specs(p)
    tile = lambda width: pl.BlockSpec((b, steps, width), lambda c: (0, c, 0))
    whole = lambda *shape: pl.BlockSpec(shape, lambda c: (0,) * len(shape))

    def cast_spec(w):
        experts, rows, cols = w.shape
        per_expert = tiles // experts
        assert per_expert * experts == tiles and rows % (per_expert * 2 * SUBLANES) == 0
        return pl.BlockSpec((1, rows // per_expert, cols), lambda c: (c // per_expert, lax.rem(c, per_expert), 0))

    cast_specs = [cast_spec(w) for w in expert_weights]
    return pl.pallas_call(
        _mixer_prompt_kernel,
        grid=(tiles,),
        in_specs=[tile(d), whole(b, N_MOD * d)] + wspecs + cast_specs,
        out_specs=[tile(d), tile(d), tile(ROUTER_LANES),
                   pl.BlockSpec((None, 1, LANES), lambda c: (c, 0, 0)),
                   whole(CONV_A_WIDTH - 1, b, D_A), whole(CONV_B_WIDTH - 1, b, D_B)] + cast_specs,
        out_shape=[jax.ShapeDtypeStruct((b, s, d), F32),
                   jax.ShapeDtypeStruct((b, s, d), BF16),
                   jax.ShapeDtypeStruct((b, s, ROUTER_LANES), F32),
                   jax.ShapeDtypeStruct((tiles, 1, LANES), jnp.int32),
                   jax.ShapeDtypeStruct((CONV_A_WIDTH - 1, b, D_A), F32),
                   jax.ShapeDtypeStruct((CONV_B_WIDTH - 1, b, D_B), F32)]
        + [jax.ShapeDtypeStruct(w.shape, BF16) for w in expert_weights],
        scratch_shapes=[pltpu.VMEM(((CONV_A_WIDTH - 1) * b + SEQ_TILE, D_A), F32),
                        pltpu.VMEM(((CONV_B_WIDTH - 1) * b + SEQ_TILE, D_B), F32),
                        pltpu.VMEM((SEQ_TILE, D_B), F32)],
        compiler_params=pltpu.CompilerParams(dimension_semantics=("arbitrary",),
                                             vmem_limit_bytes=VMEM_LIMIT),
        name="mixer_prompt",
    )(x, mod, *weights, *expert_weights)


def _mixer_sample(x_tm, mod, sa_tm, sb_tm, p):
    n, d = x_tm.shape
    weights, wspecs = _mixer_weight_specs(p)
    full = lambda a: pl.BlockSpec(a.shape, lambda i: (0,) * a.ndim)
    whole = lambda *shape: pl.BlockSpec(shape, lambda i: (0,) * len(shape))
    return pl.pallas_call(
        _mixer_sample_kernel,
        grid=(1,),
        in_specs=[full(x_tm), full(mod)] + wspecs + [full(sa_tm), full(sb_tm)],
        out_specs=[whole(n, d), whole(n, d), whole(n, ROUTER_LANES), whole(1, LANES),
                   full(sa_tm), full(sb_tm)],
        out_shape=[jax.ShapeDtypeStruct((n, d), F32),
                   jax.ShapeDtypeStruct((n, d), BF16),
                   jax.ShapeDtypeStruct((n, ROUTER_LANES), F32),
                   jax.ShapeDtypeStruct((1, LANES), jnp.int32),
                   jax.ShapeDtypeStruct(sa_tm.shape, F32),
                   jax.ShapeDtypeStruct(sb_tm.shape, F32)],
        scratch_shapes=[pltpu.VMEM((n, D_B), BF16)],
        compiler_params=pltpu.CompilerParams(dimension_semantics=("arbitrary",),
                                             vmem_limit_bytes=VMEM_LIMIT),
        name="mixer_sample",
    )(x_tm, mod, *weights, sa_tm, sb_tm)


def _moe(counts, x1, h2, route, gt2_src, gt2_spec, p, grid, tile, rows, affine):
    weights = [p["w1"], p["w3"], p["w2"], p["gf"]]
    grid_spec = pltpu.PrefetchScalarGridSpec(
        num_scalar_prefetch=1,
        grid=grid,
        in_specs=[tile(D_MODEL), tile(D_MODEL), tile(ROUTER_LANES), gt2_spec]
        + [_const_spec(w.shape) for w in weights],
        out_specs=tile(D_MODEL),
        scratch_shapes=[pltpu.VMEM((rows, D_MODEL), BF16),
                        pltpu.VMEM((rows, ROUTER_LANES), F32),
                        pltpu.VMEM((rows, D_MODEL), F32)],
    )
    return pl.pallas_call(
        functools.partial(_moe_kernel, affine=affine),
        grid_spec=grid_spec,
        out_shape=jax.ShapeDtypeStruct(x1.shape, F32),
        compiler_params=pltpu.CompilerParams(dimension_semantics=("arbitrary",) * len(grid),
                                             vmem_limit_bytes=VMEM_LIMIT),
        name="moe",
    )(counts, x1, h2, route, gt2_src, *weights)


def kernel(x_prompt, x_sample, c_prompt, c_sample, state_conv_a, state_conv_b, w_ada, b_ada, norm1_g,
           norm2_g, w_in, conv_a_w, w_out_a, conv_b_w, conv_b_bias, ln_b_g, ln_b_b, w_out_b, w_o,
           w_group, b_group, w_expert, b_expert, w1, w3, w2, final_norm_g):
    depth = w_ada.shape[0]
    assert depth == 1, "the MoE kernel fuses the final norm, so exactly one layer is supported"
    n_p, seq, d = x_prompt.shape
    n_s, steps, _ = x_sample.shape
    c_all = jnp.concatenate([c_prompt, c_sample], axis=0)
    xp = x_prompt
    xs = x_sample.transpose(1, 0, 2).reshape(steps * n_s, d)
    gf = final_norm_g.reshape(1, d)
    pa, pb, sa, sb = [], [], [], []
    for l in range(depth):
        pad = ROUTER_LANES - N_EXPERTS - N_GROUPS
        wr = jnp.pad(jnp.concatenate([w_expert[l], w_group[l]], axis=1).T, ((0, pad), (0, 0)))
        br = jnp.pad(jnp.concatenate([b_expert[l], b_group[l]]), (0, pad))
        br = jnp.broadcast_to(br[:, None], (ROUTER_LANES, LANES))
        p = {
            "g1": norm1_g[l].reshape(1, d), "g2": norm2_g[l].reshape(1, d),
            "w_in": w_in[l].astype(BF16), "woa": w_out_a[l].astype(BF16),
            "caw": jnp.broadcast_to(conv_a_w[l][:, None, :], (CONV_A_WIDTH, SUBLANES, D_A)),
            "cbw": jnp.broadcast_to(conv_b_w[l][:, None, :], (CONV_B_WIDTH, SUBLANES, D_B)),
            "cbb": conv_b_bias[l].reshape(1, D_B),
            "lng": ln_b_g[l].reshape(1, D_B), "lnb": ln_b_b[l].reshape(1, D_B),
            "wob": w_out_b[l].astype(BF16), "wo": w_o[l].astype(BF16), "wr": wr.astype(BF16), "br": br,
            "gf": gf,
        }
        mod_p, mod_s = _modulation(c_all, n_p, w_ada[l], b_ada[l].reshape(1, N_MOD * d))

        x1p, h2p, route_p, cnt_p, na_p, nb_p, w1b, w3b, w2b = _mixer_prompt(
            xp, mod_p, p, (w1[l], w3[l], w2[l]))
        p.update(w1=w1b, w3=w3b, w2=w2b.reshape(N_GROUPS, EXPERTS_PER_GROUP * D_EXPERT, d))
        tile_steps = SEQ_TILE // n_p
        xp = _moe(cnt_p[:, 0, :N_GROUPS].reshape(-1), x1p, h2p, route_p, mod_p,
                  pl.BlockSpec((n_p, d), lambda c, cnt: (0, N_MOD - 1)), p, (seq // tile_steps,),
                  lambda width: pl.BlockSpec((n_p, tile_steps, width), lambda c, cnt: (0, c, 0)), SEQ_TILE,
                  _per_sequence)

        sa_tm = state_conv_a[l].transpose(1, 0, 2)
        sb_tm = state_conv_b[l].transpose(1, 0, 2)
        x1s, h2s, route_s, cnt_s, na_s, nb_s = _mixer_sample(xs, mod_s, sa_tm, sb_tm, p)
        xs = _moe(cnt_s[0, :N_GROUPS], x1s, h2s, route_s, mod_s,
                  pl.BlockSpec((n_s, d), lambda i, c: (0, N_MOD - 1)), p, (1,),
                  lambda width: pl.BlockSpec((steps * n_s, width), lambda i, c: (0, 0)), steps * n_s, _affine)
        pa.append(na_p.transpose(1, 0, 2))
        pb.append(nb_p.transpose(1, 0, 2))
        sa.append(na_s.transpose(1, 0, 2))
        sb.append(nb_s.transpose(1, 0, 2))
    y_sample = xs.reshape(steps, n_s, d).transpose(1, 0, 2)
    return (xp, y_sample, jnp.stack(pa), jnp.stack(pb), jnp.stack(sa), jnp.stack(sb))
```

```python
import functools

import jax
import jax.numpy as jnp
from jax import lax
from jax.experimental import pallas as pl
from jax.experimental.pallas import tpu as pltpu

D_MODEL = 1024
D_A = 512
D_B = 512
CONV_A_WIDTH = 3
CONV_B_WIDTH = 31
N_GROUPS = 4
EXPERTS_PER_GROUP = 4
N_EXPERTS = 16
D_EXPERT = 256
N_MOD = 6
EPS = 1e-6
D_IN = 3 * D_A + 2 * D_B + 2 * D_MODEL

LANES = 128
SUBLANES = 8
ROUTER_LANES = LANES
SEQ_TILE = 512
CONV_ROWS = 64
CONV_LANES = 2 * LANES
SORT_SHIFT = 4
SORT_ALIGN = 1 << SORT_SHIFT
MOE_PASS_ROWS = (128, 144, 160, 176, 192, 224, 256)
VMEM_LIMIT = 52 * 1024 * 1024
MOD_BLOCK = 2 * D_MODEL

BF16 = jnp.bfloat16
F32 = jnp.float32


def _dot(a, b):
    return jnp.dot(a, b, preferred_element_type=F32)


def _unit_rms(x):
    return x * lax.rsqrt(jnp.mean(x * x, axis=-1, keepdims=True) + EPS)


def _affine(v, scale, shift=None):
    r = scale.shape[0]
    if r != 1 and r != v.shape[0]:
        v3 = v.reshape(v.shape[0] // r, r, v.shape[-1])
        shift3 = None if shift is None else shift[None]
        return _affine(v3, scale[None], shift3).reshape(v.shape)
    out = v * scale
    return out if shift is None else out + shift


def _per_sequence(v, scale, shift=None):
    n = scale.shape[0]
    v3 = v.reshape(n, v.shape[0] // n, v.shape[-1])
    out = v3 * scale[:, None, :]
    if shift is not None:
        out = out + shift[:, None, :]
    return out.reshape(v.shape)


def _tap(v, w8):
    return (v.reshape(v.shape[0] // SUBLANES, SUBLANES, v.shape[-1]) * w8[None]).reshape(v.shape)


def _route(logits_t):
    t = logits_t.shape[1]
    neg = -jnp.inf
    row = lax.broadcasted_iota(jnp.int32, (SUBLANES, t), 0).astype(F32)
    live = row < float(EXPERTS_PER_GROUP)
    gl = jnp.where(live, logits_t[N_EXPERTS:N_EXPERTS + SUBLANES, :], neg)
    gmax = jnp.max(gl, axis=0, keepdims=True)
    g_idx = jnp.min(jnp.where(gl == gmax, row, float(SUBLANES)), axis=0, keepdims=True)
    p_g = 1.0 / jnp.sum(jnp.exp(gl - gmax), axis=0, keepdims=True)
    el = None
    for g in reversed(range(N_GROUPS)):
        slab = logits_t[(g // 2) * SUBLANES:(g // 2 + 1) * SUBLANES, :]
        cand = slab if g % 2 == 0 else pltpu.roll(slab, SUBLANES - EXPERTS_PER_GROUP, axis=0)
        el = cand if el is None else jnp.where(g_idx == float(g), cand, el)
    el = jnp.where(live, el, neg)
    v1 = jnp.max(el, axis=0, keepdims=True)
    i1 = jnp.min(jnp.where(el == v1, row, float(SUBLANES)), axis=0, keepdims=True)
    el2 = jnp.where(row == i1, neg, el)
    v2 = jnp.max(el2, axis=0, keepdims=True)
    i2 = jnp.min(jnp.where(el2 == v2, row, float(SUBLANES)), axis=0, keepdims=True)
    e21 = jnp.exp(v2 - v1)
    wt1 = p_g / (1.0 + e21)
    wt2 = p_g * e21 / (1.0 + e21)
    comb = jnp.where(row == i1, wt1, jnp.where(row == i2, wt2, 0.0))
    onehot = jnp.where(row == g_idx, 1.0, 0.0)
    onehot_b = jnp.concatenate([onehot, jnp.zeros_like(onehot)], axis=0).astype(BF16)
    earlier = lax.broadcasted_iota(jnp.int32, (t, t), 0) < lax.broadcasted_iota(jnp.int32, (t, t), 1)
    before = _dot(onehot_b, jnp.where(earlier, 1.0, 0.0).astype(BF16))[0:SUBLANES, :]
    rank = jnp.sum(before * onehot, axis=0, keepdims=True)
    count = jnp.sum(onehot, axis=1, keepdims=True)
    row1 = lax.broadcasted_iota(jnp.int32, (SUBLANES, 1), 0)
    start = jnp.zeros((SUBLANES, 1), F32)
    for g in range(N_GROUPS - 1):
        start = start + jnp.where(row1 > g, count[g:g + 1, :], 0.0)
    dest = jnp.sum(onehot * start, axis=0, keepdims=True) + rank
    route_t = jnp.concatenate([jnp.where(live, comb, dest), jnp.broadcast_to(dest, (LANES - SUBLANES, t))],
                              axis=0)
    onehot_all = jnp.concatenate([onehot, jnp.zeros((LANES - SUBLANES, t), F32)], axis=0).astype(BF16)
    count_row = lax.dot_general(jnp.ones((2 * SUBLANES, t), BF16), onehot_all, (((1,), (1,)), ((), ())),
                                preferred_element_type=F32)[0:1, :]
    return jnp.transpose(route_t), count_row


def _ln_silu(v, bias, g, b):
    v = v + bias
    mu = jnp.mean(v, axis=-1, keepdims=True)
    d = v - mu
    var = jnp.mean(d * d, axis=-1, keepdims=True)
    y = d * lax.rsqrt(var + EPS) * g + b
    return y * jax.nn.sigmoid(y)


def _mixer_tail(x, hb, mod, b_a_conv, yb_in, g2, w_in_ref, woa_ref, wob_ref, wo_ref, wr_ref, br_ref,
                x1_ref, h2_ref, route_ref, cnt_ref, affine):
    y_a = _dot(b_a_conv.astype(BF16), woa_ref[...])
    y_b = _dot(yb_in, wob_ref[...])
    o = 3 * D_A + 2 * D_B
    mg_a = _dot(hb, w_in_ref[:, o:o + D_MODEL])
    mg_b = _dot(hb, w_in_ref[:, o + D_MODEL:o + 2 * D_MODEL])
    merged = jax.nn.sigmoid(mg_a) * y_a + jax.nn.sigmoid(mg_b) * y_b
    gt1 = mod[:, 2 * D_MODEL:3 * D_MODEL]
    x1 = x + affine(_dot(merged.astype(BF16), wo_ref[...]), gt1)
    sh2 = mod[:, 3 * D_MODEL:4 * D_MODEL]
    sc2 = mod[:, 4 * D_MODEL:5 * D_MODEL]
    h2 = affine(_unit_rms(x1), g2 * (1.0 + sc2), sh2).astype(BF16)
    logits_t = lax.dot_general(wr_ref[...], h2, (((1,), (1,)), ((), ())), preferred_element_type=F32)
    logits_t = logits_t + jnp.tile(br_ref[...], (1, h2.shape[0] // LANES))
    x1_ref[...] = x1.reshape(x1_ref.shape)
    h2_ref[...] = h2.reshape(h2_ref.shape)
    route, count = _route(logits_t)
    route_ref[...] = route.reshape(route_ref.shape)
    cnt_ref[...] = count.astype(jnp.int32)


def _mixer_prompt_kernel(x_ref, mod_ref, g1_ref, g2_ref, w_in_ref, caw_ref, woa_ref, cbw_ref, cbb_ref,
                         lng_ref, lnb_ref, wob_ref, wo_ref, wr_ref, br_ref,
                         w1f_ref, w3f_ref, w2f_ref,
                         x1_ref, h2_ref, route_ref, cnt_ref, na_ref, nb_ref, w1b_ref, w3b_ref, w2b_ref,
                         hist_a, hist_b, conv_b_ref):
    n_seq, steps, d = x_ref.shape
    t = n_seq * steps
    rows_a = (CONV_A_WIDTH - 1) * n_seq
    rows_b = (CONV_B_WIDTH - 1) * n_seq
    w1b_ref[...] = w1f_ref[...].astype(BF16)
    w3b_ref[...] = w3f_ref[...].astype(BF16)
    w2b_ref[...] = w2f_ref[...].astype(BF16)

    @pl.when(pl.program_id(0) == 0)
    def _():
        hist_a[0:rows_a, :] = jnp.zeros((rows_a, D_A), F32)
        hist_b[0:rows_b, :] = jnp.zeros((rows_b, D_B), F32)

    def time_major(v):
        return jnp.swapaxes(v.reshape(n_seq, steps, v.shape[-1]), 0, 1).reshape(v.shape)

    def sequence_major(v):
        return jnp.swapaxes(v.reshape(steps, n_seq, v.shape[-1]), 0, 1).reshape(v.shape)

    x = x_ref[...].reshape(t, d)
    mod = mod_ref[...]
    h = _per_sequence(_unit_rms(x), g1_ref[...] * (1.0 + mod[:, D_MODEL:2 * D_MODEL]), mod[:, 0:D_MODEL])
    hb = h.astype(BF16)

    def proj(lo, width):
        return _dot(hb, w_in_ref[:, lo:lo + width])

    hist_a[rows_a:rows_a + t, :] = time_major(proj(D_A, D_A) * proj(2 * D_A, D_A))
    conv_a = None
    for k in range(CONV_A_WIDTH):
        term = _tap(hist_a[k * n_seq:k * n_seq + t, :], caw_ref[k])
        conv_a = term if conv_a is None else conv_a + term
    b_a_conv = proj(0, D_A) * sequence_major(conv_a)
    na_ref[...] = hist_a[t:t + rows_a, :].reshape(na_ref.shape)

    hist_b[rows_b:rows_b + t, :] = time_major(proj(3 * D_A, D_B) * jax.nn.sigmoid(proj(3 * D_A + D_B, D_B)))
    nb_ref[...] = hist_b[t:t + rows_b, :].reshape(nb_ref.shape)
    for r0 in range(0, t, CONV_ROWS):
        for c0 in range(0, D_B, CONV_LANES):
            cols = slice(c0, c0 + CONV_LANES)
            acc = None
            for k in range(CONV_B_WIDTH):
                o = r0 + k * n_seq
                term = _tap(hist_b[o:o + CONV_ROWS, cols], cbw_ref[k][:, cols])
                acc = term if acc is None else acc + term
            conv_b_ref[r0:r0 + CONV_ROWS, cols] = acc
    yb_in = _ln_silu(sequence_major(conv_b_ref[...]), cbb_ref[...], lng_ref[...], lnb_ref[...]).astype(BF16)

    hist_a[0:rows_a, :] = hist_a[t:t + rows_a, :]
    hist_b[0:rows_b, :] = hist_b[t:t + rows_b, :]

    _mixer_tail(x, hb, mod, b_a_conv, yb_in, g2_ref[...], w_in_ref, woa_ref, wob_ref, wo_ref,
                wr_ref, br_ref, x1_ref, h2_ref, route_ref, cnt_ref, _per_sequence)


def _mixer_sample_kernel(x_ref, mod_ref, g1_ref, g2_ref, w_in_ref, caw_ref, woa_ref, cbw_ref, cbb_ref,
                         lng_ref, lnb_ref, wob_ref, wo_ref, wr_ref, br_ref, sa_ref, sb_ref,
                         x1_ref, h2_ref, route_ref, cnt_ref, na_ref, nb_ref, yb_in_ref):
    n_seq = mod_ref.shape[0]
    steps = x_ref.shape[0] // n_seq
    x = x_ref[...]
    mod = mod_ref[...]
    h = _affine(_unit_rms(x), g1_ref[...] * (1.0 + mod[:, D_MODEL:2 * D_MODEL]), mod[:, 0:D_MODEL])
    hb = h.astype(BF16)

    def proj(lo, width):
        return _dot(hb, w_in_ref[:, lo:lo + width])

    def slab(v, j):
        return v[j * n_seq:(j + 1) * n_seq, :]

    u_a = proj(D_A, D_A) * proj(2 * D_A, D_A)
    full_a = [sa_ref[j] for j in range(CONV_A_WIDTH - 1)] + [slab(u_a, j) for j in range(steps)]
    conv_a = []
    for s in range(steps):
        acc = jnp.zeros((n_seq, D_A), F32)
        for k in range(CONV_A_WIDTH):
            acc = acc + _tap(full_a[s + k], caw_ref[k])
        conv_a.append(acc)
    b_a_conv = proj(0, D_A) * jnp.concatenate(conv_a, axis=0)
    for j in range(CONV_A_WIDTH - 1):
        na_ref[j] = full_a[steps + j]

    u_b = proj(3 * D_A, D_B) * jax.nn.sigmoid(proj(3 * D_A + D_B, D_B))
    hist = CONV_B_WIDTH - 1

    def full_b(j):
        return sb_ref[j] if j < hist else slab(u_b, j - hist)

    bias, lng, lnb = cbb_ref[...], lng_ref[...], lnb_ref[...]
    for s in range(steps):
        acc = jnp.zeros((n_seq, D_B), F32)
        for k in range(CONV_B_WIDTH):
            acc = acc + _tap(full_b(s + k), cbw_ref[k])
        yb_in_ref[s * n_seq:(s + 1) * n_seq, :] = _ln_silu(acc, bias, lng, lnb).astype(BF16)
    for j in range(hist):
        nb_ref[j] = full_b(steps + j)

    _mixer_tail(x, hb, mod, b_a_conv, yb_in_ref[...], g2_ref[...], w_in_ref, woa_ref, wob_ref, wo_ref,
                wr_ref, br_ref, x1_ref, h2_ref, route_ref, cnt_ref, _affine)


def _moe_kernel(cnt_ref, x1_ref, h2_ref, route_ref, gt2_ref, w1_ref, w3_ref, w2_ref, gf_ref, y_ref,
                h2s_ref, cs_ref, acc_ref, *, affine):
    def rows(ref):
        return ref[...].reshape(-1, ref.shape[-1])

    tp = h2s_ref.shape[0]
    tile = pl.program_id(0)
    route = rows(route_ref)
    t = route.shape[0]
    lane_i = lax.broadcasted_iota(jnp.int32, route.shape, 1)
    dest_row = jnp.transpose(route)[EXPERTS_PER_GROUP:EXPERTS_PER_GROUP + 1, :]
    slot = lax.broadcasted_iota(jnp.int32, (tp, t), 0).astype(F32)
    p_sort = jnp.where(slot == dest_row, 1.0, 0.0).astype(BF16)
    h2s_ref[...] = _dot(p_sort, rows(h2_ref)).astype(BF16)
    comb = jnp.where(lane_i < EXPERTS_PER_GROUP, route, 0.0)
    c_hi = comb.astype(BF16).astype(F32)
    c_mid = (comb - c_hi).astype(BF16).astype(F32)
    c_lo = comb - c_hi - c_mid
    packed = (c_hi + pltpu.roll(c_mid, EXPERTS_PER_GROUP, axis=1)
              + pltpu.roll(c_lo, 2 * EXPERTS_PER_GROUP, axis=1)).astype(BF16)
    moved = _dot(p_sort, packed)
    cs_ref[...] = (moved + pltpu.roll(moved, LANES - EXPERTS_PER_GROUP, axis=1)
                   + pltpu.roll(moved, LANES - 2 * EXPERTS_PER_GROUP, axis=1))
    acc_ref[...] = jnp.zeros(acc_ref.shape, F32)

    def expert_pass(rows, g, window, lo, end):
        off = pl.multiple_of(jnp.minimum(window, tp - rows), SORT_ALIGN)
        row_id = off + lax.broadcasted_iota(jnp.int32, (rows, 1), 0)
        valid = (row_id >= jnp.maximum(lo, window)) & (row_id < end)
        hb = h2s_ref[pl.ds(off, rows), :]
        c4 = jnp.where(valid, cs_ref[pl.ds(off, rows), :], 0.0)
        experts = [g * EXPERTS_PER_GROUP + e for e in range(EXPERTS_PER_GROUP)]
        gate = jnp.concatenate([_dot(hb, w1_ref[e]) for e in experts], axis=1)
        up = jnp.concatenate([_dot(hb, w3_ref[e]) for e in experts], axis=1)
        cexp = jnp.concatenate(
            [jnp.broadcast_to(c4[:, e:e + 1], (rows, D_EXPERT)) for e in range(EXPERTS_PER_GROUP)], axis=1)
        a = gate * jax.nn.sigmoid(gate) * up * cexp
        acc_ref[pl.ds(off, rows), :] += _dot(a.astype(BF16), w2_ref[g])

    def group(g, start):
        end = start + cnt_ref[tile * N_GROUPS + g]
        base = lax.shift_right_logical(start, SORT_SHIFT) * SORT_ALIGN
        span = end - base

        def passes(rows):
            def run():
                def body(p, carry):
                    expert_pass(rows, g, base + p * rows, start, end)
                    return carry
                lax.fori_loop(0, jnp.where(end > start, lax.div(span + (rows - 1), rows), 0), body, 0)
            return run

        size_class = sum((span > rows).astype(jnp.int32) for rows in MOE_PASS_ROWS[:-1])
        lax.switch(size_class, [passes(rows) for rows in MOE_PASS_ROWS])
        return end

    lax.fori_loop(0, N_GROUPS, group, jnp.int32(0))

    dest_col = route[:, EXPERTS_PER_GROUP:EXPERTS_PER_GROUP + 1]
    p_back = jnp.where(lax.broadcasted_iota(jnp.int32, (t, tp), 1).astype(F32) == dest_col,
                       1.0, 0.0).astype(BF16)
    y = acc_ref[...]
    y_hi = y.astype(BF16)
    y_lo = (y - y_hi.astype(F32)).astype(BF16)
    y_tok = _dot(p_back, y_hi) + _dot(p_back, y_lo)
    x2 = rows(x1_ref) + affine(y_tok, gt2_ref[...])
    y_ref[...] = (_unit_rms(x2) * gf_ref[...]).reshape(y_ref.shape)


def _mod_kernel(c_ref, w_ref, b_ref, op_ref, os_ref):
    c = c_ref[...]
    a = (c * jax.nn.sigmoid(c)).astype(BF16)
    mod = _dot(a, w_ref[...].astype(BF16)) + b_ref[...]
    n_p = op_ref.shape[0]
    op_ref[...] = mod[:n_p]
    os_ref[...] = mod[n_p:]


def _const_spec(shape):
    nd = len(shape)
    return pl.BlockSpec(shape, lambda *_: (0,) * nd, pipeline_mode=pl.Buffered(1))


def _modulation(c_all, n_p, w_ada, b_ada):
    n, d = c_all.shape
    n_out = w_ada.shape[1]
    assert n_p % SUBLANES == 0
    return pl.pallas_call(
        _mod_kernel,
        grid=(n_out // MOD_BLOCK,),
        in_specs=[pl.BlockSpec((n, d), lambda j: (0, 0)),
                  pl.BlockSpec((d, MOD_BLOCK), lambda j: (0, j)),
                  pl.BlockSpec((1, MOD_BLOCK), lambda j: (0, j))],
        out_specs=[pl.BlockSpec((n_p, MOD_BLOCK), lambda j: (0, j)),
                   pl.BlockSpec((n - n_p, MOD_BLOCK), lambda j: (0, j))],
        out_shape=[jax.ShapeDtypeStruct((n_p, n_out), F32), jax.ShapeDtypeStruct((n - n_p, n_out), F32)],
        compiler_params=pltpu.CompilerParams(dimension_semantics=("arbitrary",),
                                             vmem_limit_bytes=VMEM_LIMIT),
        name="modulation",
    )(c_all, w_ada, b_ada)


def _mixer_weight_specs(p):
    names = ("g1", "g2", "w_in", "caw", "woa", "cbw", "cbb", "lng", "lnb", "wob", "wo", "wr", "br")
    return [p[k] for k in names], [_const_spec(p[k].shape) for k in names]


def _mixer_prompt(x, mod, p, expert_weights):
    b, s, d = x.shape
    assert b == SUBLANES, "the time-major conv layout puts one sequence on each sublane"
    steps = SEQ_TILE // b
    tiles = s // steps
    weights, wspecs = _mixer_weight_specs(p)
    tile = lambda width: pl.BlockSpec((b, steps, width), lambda c: (0, c, 0))
    whole = lambda *shape: pl.BlockSpec(shape, lambda c: (0,) * len(shape))

    def cast_spec(w):
        experts, rows, cols = w.shape
        per_expert = tiles // experts
        assert per_expert * experts == tiles and rows % (per_expert * 2 * SUBLANES) == 0
        return pl.BlockSpec((1, rows // per_expert, cols), lambda c: (c // per_expert, lax.rem(c, per_expert), 0))

    cast_specs = [cast_spec(w) for w in expert_weights]
    return pl.pallas_call(
        _mixer_prompt_kernel,
        grid=(tiles,),
        in_specs=[tile(d), whole(b, N_MOD * d)] + wspecs + cast_specs,
        out_specs=[tile(d), tile(d), tile(ROUTER_LANES),
                   pl.BlockSpec((None, 1, LANES), lambda c: (c, 0, 0)),
                   whole(CONV_A_WIDTH - 1, b, D_A), whole(CONV_B_WIDTH - 1, b, D_B)] + cast_specs,
        out_shape=[jax.ShapeDtypeStruct((b, s, d), F32),
                   jax.ShapeDtypeStruct((b, s, d), BF16),
                   jax.ShapeDtypeStruct((b, s, ROUTER_LANES), F32),
                   jax.ShapeDtypeStruct((tiles, 1, LANES), jnp.int32),
                   jax.ShapeDtypeStruct((CONV_A_WIDTH - 1, b, D_A), F32),
                   jax.ShapeDtypeStruct((CONV_B_WIDTH - 1, b, D_B), F32)]
        + [jax.ShapeDtypeStruct(w.shape, BF16) for w in expert_weights],
        scratch_shapes=[pltpu.VMEM(((CONV_A_WIDTH - 1) * b + SEQ_TILE, D_A), F32),
                        pltpu.VMEM(((CONV_B_WIDTH - 1) * b + SEQ_TILE, D_B), F32),
                        pltpu.VMEM((SEQ_TILE, D_B), F32)],
        compiler_params=pltpu.CompilerParams(dimension_semantics=("arbitrary",),
                                             vmem_limit_bytes=VMEM_LIMIT),
        name="mixer_prompt",
    )(x, mod, *weights, *expert_weights)


def _mixer_sample(x_tm, mod, sa_tm, sb_tm, p):
    n, d = x_tm.shape
    weights, wspecs = _mixer_weight_specs(p)
    full = lambda a: pl.BlockSpec(a.shape, lambda i: (0,) * a.ndim)
    whole = lambda *shape: pl.BlockSpec(shape, lambda i: (0,) * len(shape))
    return pl.pallas_call(
        _mixer_sample_kernel,
        grid=(1,),
        in_specs=[full(x_tm), full(mod)] + wspecs + [full(sa_tm), full(sb_tm)],
        out_specs=[whole(n, d), whole(n, d), whole(n, ROUTER_LANES), whole(1, LANES),
                   full(sa_tm), full(sb_tm)],
        out_shape=[jax.ShapeDtypeStruct((n, d), F32),
                   jax.ShapeDtypeStruct((n, d), BF16),
                   jax.ShapeDtypeStruct((n, ROUTER_LANES), F32),
                   jax.ShapeDtypeStruct((1, LANES), jnp.int32),
                   jax.ShapeDtypeStruct(sa_tm.shape, F32),
                   jax.ShapeDtypeStruct(sb_tm.shape, F32)],
        scratch_shapes=[pltpu.VMEM((n, D_B), BF16)],
        compiler_params=pltpu.CompilerParams(dimension_semantics=("arbitrary",),
                                             vmem_limit_bytes=VMEM_LIMIT),
        name="mixer_sample",
    )(x_tm, mod, *weights, sa_tm, sb_tm)


def _moe(counts, x1, h2, route, gt2_src, gt2_spec, p, grid, tile, rows, affine):
    weights = [p["w1"], p["w3"], p["w2"], p["gf"]]
    grid_spec = pltpu.PrefetchScalarGridSpec(
        num_scalar_prefetch=1,
        grid=grid,
        in_specs=[tile(D_MODEL), tile(D_MODEL), tile(ROUTER_LANES), gt2_spec]
        + [_const_spec(w.shape) for w in weights],
        out_specs=tile(D_MODEL),
        scratch_shapes=[pltpu.VMEM((rows, D_MODEL), BF16),
                        pltpu.VMEM((rows, ROUTER_LANES), F32),
                        pltpu.VMEM((rows, D_MODEL), F32)],
    )
    return pl.pallas_call(
        functools.partial(_moe_kernel, affine=affine),
        grid_spec=grid_spec,
        out_shape=jax.ShapeDtypeStruct(x1.shape, F32),
        compiler_params=pltpu.CompilerParams(dimension_semantics=("arbitrary",) * len(grid),
                                             vmem_limit_bytes=VMEM_LIMIT),
        name="moe",
    )(counts, x1, h2, route, gt2_src, *weights)


def kernel(x_prompt, x_sample, c_prompt, c_sample, state_conv_a, state_conv_b, w_ada, b_ada, norm1_g,
           norm2_g, w_in, conv_a_w, w_out_a, conv_b_w, conv_b_bias, ln_b_g, ln_b_b, w_out_b, w_o,
           w_group, b_group, w_expert, b_expert, w1, w3, w2, final_norm_g):
    depth = w_ada.shape[0]
    assert depth == 1, "the MoE kernel fuses the final norm, so exactly one layer is supported"
    n_p, seq, d = x_prompt.shape
    n_s, steps, _ = x_sample.shape
    c_all = jnp.concatenate([c_prompt, c_sample], axis=0)
    xp = x_prompt
    xs = x_sample.transpose(1, 0, 2).reshape(steps * n_s, d)
    gf = final_norm_g.reshape(1, d)
    pa, pb, sa, sb = [], [], [], []
    for l in range(depth):
        pad = ROUTER_LANES - N_EXPERTS - N_GROUPS
        wr = jnp.pad(jnp.concatenate([w_expert[l], w_group[l]], axis=1).T, ((0, pad), (0, 0)))
        br = jnp.pad(jnp.concatenate([b_expert[l], b_group[l]]), (0, pad))
        br = jnp.broadcast_to(br[:, None], (ROUTER_LANES, LANES))
        p = {
            "g1": norm1_g[l].reshape(1, d), "g2": norm2_g[l].reshape(1, d),
            "w_in": w_in[l].astype(BF16), "woa": w_out_a[l].astype(BF16),
            "caw": jnp.broadcast_to(conv_a_w[l][:, None, :], (CONV_A_WIDTH, SUBLANES, D_A)),
            "cbw": jnp.broadcast_to(conv_b_w[l][:, None, :], (CONV_B_WIDTH, SUBLANES, D_B)),
            "cbb": conv_b_bias[l].reshape(1, D_B),
            "lng": ln_b_g[l].reshape(1, D_B), "lnb": ln_b_b[l].reshape(1, D_B),
            "wob": w_out_b[l].astype(BF16), "wo": w_o[l].astype(BF16), "wr": wr.astype(BF16), "br": br,
            "gf": gf,
        }
        mod_p, mod_s = _modulation(c_all, n_p, w_ada[l], b_ada[l].reshape(1, N_MOD * d))

        x1p, h2p, route_p, cnt_p, na_p, nb_p, w1b, w3b, w2b = _mixer_prompt(
            xp, mod_p, p, (w1[l], w3[l], w2[l]))
        p.update(w1=w1b, w3=w3b, w2=w2b.reshape(N_GROUPS, EXPERTS_PER_GROUP * D_EXPERT, d))
        tile_steps = SEQ_TILE // n_p
        xp = _moe(cnt_p[:, 0, :N_GROUPS].reshape(-1), x1p, h2p, route_p, mod_p,
                  pl.BlockSpec((n_p, d), lambda c, cnt: (0, N_MOD - 1)), p, (seq // tile_steps,),
                  lambda width: pl.BlockSpec((n_p, tile_steps, width), lambda c, cnt: (0, c, 0)), SEQ_TILE,
                  _per_sequence)

        sa_tm = state_conv_a[l].transpose(1, 0, 2)
        sb_tm = state_conv_b[l].transpose(1, 0, 2)
        x1s, h2s, route_s, cnt_s, na_s, nb_s = _mixer_sample(xs, mod_s, sa_tm, sb_tm, p)
        xs = _moe(cnt_s[0, :N_GROUPS], x1s, h2s, route_s, mod_s,
                  pl.BlockSpec((n_s, d), lambda i, c: (0, N_MOD - 1)), p, (1,),
                  lambda width: pl.BlockSpec((steps * n_s, width), lambda i, c: (0, 0)), steps * n_s, _affine)
        pa.append(na_p.transpose(1, 0, 2))
        pb.append(nb_p.transpose(1, 0, 2))
        sa.append(na_s.transpose(1, 0, 2))
        sb.append(nb_s.transpose(1, 0, 2))
    y_sample = xs.reshape(steps, n_s, d).transpose(1, 0, 2)
    return (xp, y_sample, jnp.stack(pa), jnp.stack(pb), jnp.stack(sa), jnp.stack(sb))
```

```python
import functools

import jax
import jax.numpy as jnp
from jax import lax
from jax.experimental import pallas as pl
from jax.experimental.pallas import tpu as pltpu

D_MODEL = 1024
D_A = 512
D_B = 512
CONV_A_WIDTH = 3
CONV_B_WIDTH = 31
N_GROUPS = 4
EXPERTS_PER_GROUP = 4
N_EXPERTS = 16
D_EXPERT = 256
N_MOD = 6
EPS = 1e-6
D_IN = 3 * D_A + 2 * D_B + 2 * D_MODEL

LANES = 128
SUBLANES = 8
ROUTER_LANES = LANES
SEQ_TILE = 512
CONV_ROWS = 64
CONV_LANES = 2 * LANES
SORT_SHIFT = 4
SORT_ALIGN = 1 << SORT_SHIFT
MOE_PASS_ROWS = (128, 144, 160, 176, 192, 224, 256)
VMEM_LIMIT = 52 * 1024 * 1024
MOD_BLOCK = 2 * D_MODEL

BF16 = jnp.bfloat16
F32 = jnp.float32


def _dot(a, b):
    return jnp.dot(a, b, preferred_element_type=F32)


def _unit_rms(x):
    return x * lax.rsqrt(jnp.mean(x * x, axis=-1, keepdims=True) + EPS)


def _affine(v, scale, shift=None):
    r = scale.shape[0]
    if r != 1 and r != v.shape[0]:
        v3 = v.reshape(v.shape[0] // r, r, v.shape[-1])
        shift3 = None if shift is None else shift[None]
        return _affine(v3, scale[None], shift3).reshape(v.shape)
    out = v * scale
    return out if shift is None else out + shift


def _per_sequence(v, scale, shift=None):
    n = scale.shape[0]
    v3 = v.reshape(n, v.shape[0] // n, v.shape[-1])
    out = v3 * scale[:, None, :]
    if shift is not None:
        out = out + shift[:, None, :]
    return out.reshape(v.shape)


def _tap(v, w8):
    return (v.reshape(v.shape[0] // SUBLANES, SUBLANES, v.shape[-1]) * w8[None]).reshape(v.shape)


def _route(logits_t, tri_ref):
    t = logits_t.shape[1]
    neg = -jnp.inf
    row = lax.broadcasted_iota(jnp.int32, (SUBLANES, t), 0).astype(F32)
    live = row < float(EXPERTS_PER_GROUP)
    gl = jnp.where(live, logits_t[N_EXPERTS:N_EXPERTS + SUBLANES, :], neg)
    gmax = jnp.max(gl, axis=0, keepdims=True)
    g_idx = jnp.min(jnp.where(gl == gmax, row, float(SUBLANES)), axis=0, keepdims=True)
    p_g = 1.0 / jnp.sum(jnp.exp(gl - gmax), axis=0, keepdims=True)
    el = None
    for g in reversed(range(N_GROUPS)):
        slab = logits_t[(g // 2) * SUBLANES:(g // 2 + 1) * SUBLANES, :]
        cand = slab if g % 2 == 0 else pltpu.roll(slab, SUBLANES - EXPERTS_PER_GROUP, axis=0)
        el = cand if el is None else jnp.where(g_idx == float(g), cand, el)
    el = jnp.where(live, el, neg)
    v1 = jnp.max(el, axis=0, keepdims=True)
    i1 = jnp.min(jnp.where(el == v1, row, float(SUBLANES)), axis=0, keepdims=True)
    el2 = jnp.where(row == i1, neg, el)
    v2 = jnp.max(el2, axis=0, keepdims=True)
    i2 = jnp.min(jnp.where(el2 == v2, row, float(SUBLANES)), axis=0, keepdims=True)
    e21 = jnp.exp(v2 - v1)
    wt1 = p_g / (1.0 + e21)
    wt2 = p_g * e21 / (1.0 + e21)
    comb = jnp.where(row == i1, wt1, jnp.where(row == i2, wt2, 0.0))
    onehot = jnp.where(row == g_idx, 1.0, 0.0)
    onehot_b = jnp.concatenate([onehot, jnp.zeros_like(onehot)], axis=0).astype(BF16)
    before = _dot(onehot_b, tri_ref[...])[0:SUBLANES, :]
    rank = jnp.sum(before * onehot, axis=0, keepdims=True)
    count = jnp.sum(onehot, axis=1, keepdims=True)
    row1 = lax.broadcasted_iota(jnp.int32, (SUBLANES, 1), 0)
    start = jnp.zeros((SUBLANES, 1), F32)
    for g in range(N_GROUPS - 1):
        start = start + jnp.where(row1 > g, count[g:g + 1, :], 0.0)
    dest = jnp.sum(onehot * start, axis=0, keepdims=True) + rank
    route_t = jnp.concatenate([jnp.where(live, comb, dest), jnp.broadcast_to(dest, (LANES - SUBLANES, t))],
                              axis=0)
    onehot_all = jnp.concatenate([onehot, jnp.zeros((LANES - SUBLANES, t), F32)], axis=0).astype(BF16)
    count_row = lax.dot_general(jnp.ones((2 * SUBLANES, t), BF16), onehot_all, (((1,), (1,)), ((), ())),
                                preferred_element_type=F32)[0:1, :]
    return jnp.transpose(route_t), count_row


def _ln_silu(v, bias, g, b):
    v = v + bias
    mu = jnp.mean(v, axis=-1, keepdims=True)
    d = v - mu
    var = jnp.mean(d * d, axis=-1, keepdims=True)
    y = d * lax.rsqrt(var + EPS) * g + b
    return y * jax.nn.sigmoid(y)


def _mixer_tail(x, hb, mod, b_a_conv, yb_in, g2, w_in_ref, woa_ref, wob_ref, wo_ref, wr_ref, br_ref, tri_ref,
                x1_ref, h2_ref, route_ref, cnt_ref, affine):
    y_a = _dot(b_a_conv.astype(BF16), woa_ref[...])
    y_b = _dot(yb_in, wob_ref[...])
    o = 3 * D_A + 2 * D_B
    mg_a = _dot(hb, w_in_ref[:, o:o + D_MODEL])
    mg_b = _dot(hb, w_in_ref[:, o + D_MODEL:o + 2 * D_MODEL])
    merged = jax.nn.sigmoid(mg_a) * y_a + jax.nn.sigmoid(mg_b) * y_b
    gt1 = mod[:, 2 * D_MODEL:3 * D_MODEL]
    x1 = x + affine(_dot(merged.astype(BF16), wo_ref[...]), gt1)
    sh2 = mod[:, 3 * D_MODEL:4 * D_MODEL]
    sc2 = mod[:, 4 * D_MODEL:5 * D_MODEL]
    h2 = affine(_unit_rms(x1), g2 * (1.0 + sc2), sh2).astype(BF16)
    logits_t = lax.dot_general(wr_ref[...], h2, (((1,), (1,)), ((), ())), preferred_element_type=F32)
    logits_t = logits_t + jnp.tile(br_ref[...], (1, h2.shape[0] // LANES))
    x1_ref[...] = x1.reshape(x1_ref.shape)
    h2_ref[...] = h2.reshape(h2_ref.shape)
    route, count = _route(logits_t, tri_ref)
    route_ref[...] = route.reshape(route_ref.shape)
    cnt_ref[...] = count.astype(jnp.int32)


def _mixer_prompt_kernel(x_ref, mod_ref, g1_ref, g2_ref, w_in_ref, caw_ref, woa_ref, cbw_ref, cbb_ref,
                         lng_ref, lnb_ref, wob_ref, wo_ref, wr_ref, br_ref, tri_ref,
                         w1f_ref, w3f_ref, w2f_ref,
                         x1_ref, h2_ref, route_ref, cnt_ref, na_ref, nb_ref, w1b_ref, w3b_ref, w2b_ref,
                         hist_a, hist_b, conv_b_ref):
    n_seq, steps, d = x_ref.shape
    t = n_seq * steps
    rows_a = (CONV_A_WIDTH - 1) * n_seq
    rows_b = (CONV_B_WIDTH - 1) * n_seq
    w1b_ref[...] = w1f_ref[...].astype(BF16)
    w3b_ref[...] = w3f_ref[...].astype(BF16)
    w2b_ref[...] = w2f_ref[...].astype(BF16)

    @pl.when(pl.program_id(0) == 0)
    def _():
        hist_a[0:rows_a, :] = jnp.zeros((rows_a, D_A), F32)
        hist_b[0:rows_b, :] = jnp.zeros((rows_b, D_B), F32)

    def time_major(v):
        return jnp.swapaxes(v.reshape(n_seq, steps, v.shape[-1]), 0, 1).reshape(v.shape)

    def sequence_major(v):
        return jnp.swapaxes(v.reshape(steps, n_seq, v.shape[-1]), 0, 1).reshape(v.shape)

    x = x_ref[...].reshape(t, d)
    mod = mod_ref[...]
    h = _per_sequence(_unit_rms(x), g1_ref[...] * (1.0 + mod[:, D_MODEL:2 * D_MODEL]), mod[:, 0:D_MODEL])
    hb = h.astype(BF16)

    def proj(lo, width):
        return _dot(hb, w_in_ref[:, lo:lo + width])

    hist_a[rows_a:rows_a + t, :] = time_major(proj(D_A, D_A) * proj(2 * D_A, D_A))
    conv_a = None
    for k in range(CONV_A_WIDTH):
        term = _tap(hist_a[k * n_seq:k * n_seq + t, :], caw_ref[k])
        conv_a = term if conv_a is None else conv_a + term
    b_a_conv = proj(0, D_A) * sequence_major(conv_a)
    na_ref[...] = hist_a[t:t + rows_a, :].reshape(na_ref.shape)

    hist_b[rows_b:rows_b + t, :] = time_major(proj(3 * D_A, D_B) * jax.nn.sigmoid(proj(3 * D_A + D_B, D_B)))
    nb_ref[...] = hist_b[t:t + rows_b, :].reshape(nb_ref.shape)
    for r0 in range(0, t, CONV_ROWS):
        for c0 in range(0, D_B, CONV_LANES):
            cols = slice(c0, c0 + CONV_LANES)
            acc = None
            for k in range(CONV_B_WIDTH):
                o = r0 + k * n_seq
                term = _tap(hist_b[o:o + CONV_ROWS, cols], cbw_ref[k][:, cols])
                acc = term if acc is None else acc + term
            conv_b_ref[r0:r0 + CONV_ROWS, cols] = acc
    yb_in = _ln_silu(sequence_major(conv_b_ref[...]), cbb_ref[...], lng_ref[...], lnb_ref[...]).astype(BF16)

    hist_a[0:rows_a, :] = hist_a[t:t + rows_a, :]
    hist_b[0:rows_b, :] = hist_b[t:t + rows_b, :]

    _mixer_tail(x, hb, mod, b_a_conv, yb_in, g2_ref[...], w_in_ref, woa_ref, wob_ref, wo_ref,
                wr_ref, br_ref, tri_ref, x1_ref, h2_ref, route_ref, cnt_ref, _per_sequence)


def _mixer_sample_kernel(x_ref, mod_ref, g1_ref, g2_ref, w_in_ref, caw_ref, woa_ref, cbw_ref, cbb_ref,
                         lng_ref, lnb_ref, wob_ref, wo_ref, wr_ref, br_ref, tri_ref, sa_ref, sb_ref,
                         x1_ref, h2_ref, route_ref, cnt_ref, na_ref, nb_ref, yb_in_ref):
    n_seq = mod_ref.shape[0]
    steps = x_ref.shape[0] // n_seq
    x = x_ref[...]
    mod = mod_ref[...]
    h = _affine(_unit_rms(x), g1_ref[...] * (1.0 + mod[:, D_MODEL:2 * D_MODEL]), mod[:, 0:D_MODEL])
    hb = h.astype(BF16)

    def proj(lo, width):
        return _dot(hb, w_in_ref[:, lo:lo + width])

    def slab(v, j):
        return v[j * n_seq:(j + 1) * n_seq, :]

    u_a = proj(D_A, D_A) * proj(2 * D_A, D_A)
    full_a = [sa_ref[j] for j in range(CONV_A_WIDTH - 1)] + [slab(u_a, j) for j in range(steps)]
    conv_a = []
    for s in range(steps):
        acc = jnp.zeros((n_seq, D_A), F32)
        for k in range(CONV_A_WIDTH):
            acc = acc + _tap(full_a[s + k], caw_ref[k])
        conv_a.append(acc)
    b_a_conv = proj(0, D_A) * jnp.concatenate(conv_a, axis=0)
    for j in range(CONV_A_WIDTH - 1):
        na_ref[j] = full_a[steps + j]

    u_b = proj(3 * D_A, D_B) * jax.nn.sigmoid(proj(3 * D_A + D_B, D_B))
    hist = CONV_B_WIDTH - 1

    def full_b(j):
        return sb_ref[j] if j < hist else slab(u_b, j - hist)

    bias, lng, lnb = cbb_ref[...], lng_ref[...], lnb_ref[...]
    for s in range(steps):
        acc = jnp.zeros((n_seq, D_B), F32)
        for k in range(CONV_B_WIDTH):
            acc = acc + _tap(full_b(s + k), cbw_ref[k])
        yb_in_ref[s * n_seq:(s + 1) * n_seq, :] = _ln_silu(acc, bias, lng, lnb).astype(BF16)
    for j in range(hist):
        nb_ref[j] = full_b(steps + j)

    _mixer_tail(x, hb, mod, b_a_conv, yb_in_ref[...], g2_ref[...], w_in_ref, woa_ref, wob_ref, wo_ref,
                wr_ref, br_ref, tri_ref, x1_ref, h2_ref, route_ref, cnt_ref, _affine)


def _moe_kernel(cnt_ref, x1_ref, h2_ref, route_ref, gt2_ref, w1_ref, w3_ref, w2_ref, gf_ref, slot_ref, y_ref,
                h2s_ref, cs_ref, acc_ref, *, affine):
    def rows(ref):
        return ref[...].reshape(-1, ref.shape[-1])

    tp = h2s_ref.shape[0]
    tile = pl.program_id(0)
    route = rows(route_ref)
    t = route.shape[0]
    lane_i = lax.broadcasted_iota(jnp.int32, route.shape, 1)
    dest_row = jnp.transpose(route)[EXPERTS_PER_GROUP:EXPERTS_PER_GROUP + 1, :]
    p_sort = jnp.where(slot_ref[0] == dest_row, 1.0, 0.0).astype(BF16)
    h2s_ref[...] = _dot(p_sort, rows(h2_ref)).astype(BF16)
    comb = jnp.where(lane_i < EXPERTS_PER_GROUP, route, 0.0)
    c_hi = comb.astype(BF16).astype(F32)
    c_mid = (comb - c_hi).astype(BF16).astype(F32)
    c_lo = comb - c_hi - c_mid
    packed = (c_hi + pltpu.roll(c_mid, EXPERTS_PER_GROUP, axis=1)
              + pltpu.roll(c_lo, 2 * EXPERTS_PER_GROUP, axis=1)).astype(BF16)
    moved = _dot(p_sort, packed)
    cs_ref[...] = (moved + pltpu.roll(moved, LANES - EXPERTS_PER_GROUP, axis=1)
                   + pltpu.roll(moved, LANES - 2 * EXPERTS_PER_GROUP, axis=1))
    acc_ref[...] = jnp.zeros(acc_ref.shape, F32)

    def expert_pass(rows, g, window, lo, end):
        off = pl.multiple_of(jnp.minimum(window, tp - rows), SORT_ALIGN)
        row_id = off + lax.broadcasted_iota(jnp.int32, (rows, 1), 0)
        valid = (row_id >= jnp.maximum(lo, window)) & (row_id < end)
        hb = h2s_ref[pl.ds(off, rows), :]
        c4 = jnp.where(valid, cs_ref[pl.ds(off, rows), :], 0.0)
        experts = [g * EXPERTS_PER_GROUP + e for e in range(EXPERTS_PER_GROUP)]
        gate = jnp.concatenate([_dot(hb, w1_ref[e]) for e in experts], axis=1)
        up = jnp.concatenate([_dot(hb, w3_ref[e]) for e in experts], axis=1)
        cexp = jnp.concatenate(
            [jnp.broadcast_to(c4[:, e:e + 1], (rows, D_EXPERT)) for e in range(EXPERTS_PER_GROUP)], axis=1)
        a = gate * jax.nn.sigmoid(gate) * up * cexp
        acc_ref[pl.ds(off, rows), :] += _dot(a.astype(BF16), w2_ref[g])

    def group(g, start):
        end = start + cnt_ref[tile * N_GROUPS + g]
        base = lax.shift_right_logical(start, SORT_SHIFT) * SORT_ALIGN
        span = end - base

        def passes(rows):
            def run():
                def body(p, carry):
                    expert_pass(rows, g, base + p * rows, start, end)
                    return carry
                lax.fori_loop(0, jnp.where(end > start, lax.div(span + (rows - 1), rows), 0), body, 0)
            return run

        size_class = sum((span > rows).astype(jnp.int32) for rows in MOE_PASS_ROWS[:-1])
        lax.switch(size_class, [passes(rows) for rows in MOE_PASS_ROWS])
        return end

    lax.fori_loop(0, N_GROUPS, group, jnp.int32(0))

    dest_col = route[:, EXPERTS_PER_GROUP:EXPERTS_PER_GROUP + 1]
    p_back = jnp.where(slot_ref[1] == dest_col, 1.0, 0.0).astype(BF16)
    y = acc_ref[...]
    y_hi = y.astype(BF16)
    y_lo = (y - y_hi.astype(F32)).astype(BF16)
    y_tok = _dot(p_back, y_hi) + _dot(p_back, y_lo)
    x2 = rows(x1_ref) + affine(y_tok, gt2_ref[...])
    y_ref[...] = (_unit_rms(x2) * gf_ref[...]).reshape(y_ref.shape)


def _mod_kernel(c_ref, w_ref, b_ref, op_ref, os_ref):
    c = c_ref[...]
    a = (c * jax.nn.sigmoid(c)).astype(BF16)
    mod = _dot(a, w_ref[...].astype(BF16)) + b_ref[...]
    n_p = op_ref.shape[0]
    op_ref[...] = mod[:n_p]
    os_ref[...] = mod[n_p:]


def _const_spec(shape):
    nd = len(shape)
    return pl.BlockSpec(shape, lambda *_: (0,) * nd, pipeline_mode=pl.Buffered(1))


def _modulation(c_all, n_p, w_ada, b_ada):
    n, d = c_all.shape
    n_out = w_ada.shape[1]
    assert n_p % SUBLANES == 0
    return pl.pallas_call(
        _mod_kernel,
        grid=(n_out // MOD_BLOCK,),
        in_specs=[pl.BlockSpec((n, d), lambda j: (0, 0)),
                  pl.BlockSpec((d, MOD_BLOCK), lambda j: (0, j)),
                  pl.BlockSpec((1, MOD_BLOCK), lambda j: (0, j))],
        out_specs=[pl.BlockSpec((n_p, MOD_BLOCK), lambda j: (0, j)),
                   pl.BlockSpec((n - n_p, MOD_BLOCK), lambda j: (0, j))],
        out_shape=[jax.ShapeDtypeStruct((n_p, n_out), F32), jax.ShapeDtypeStruct((n - n_p, n_out), F32)],
        compiler_params=pltpu.CompilerParams(dimension_semantics=("arbitrary",),
                                             vmem_limit_bytes=VMEM_LIMIT),
        name="modulation",
    )(c_all, w_ada, b_ada)


def _mixer_weight_specs(p):
    names = ("g1", "g2", "w_in", "caw", "woa", "cbw", "cbb", "lng", "lnb", "wob", "wo", "wr", "br", "tri")
    return [p[k] for k in names], [_const_spec(p[k].shape) for k in names]


def _mixer_prompt(x, mod, p, expert_weights):
    b, s, d = x.shape
    assert b == SUBLANES, "the time-major conv layout puts one sequence on each sublane"
    steps = SEQ_TILE // b
    tiles = s // steps
    weights, wspecs = _mixer_weight_specs(p)
    tile = lambda width: pl.BlockSpec((b, steps, width), lambda c: (0, c, 0))
    whole = lambda *shape: pl.BlockSpec(shape, lambda c: (0,) * len(shape))

    def cast_spec(w):
        experts, rows, cols = w.shape
        per_expert = tiles // experts
        assert per_expert * experts == tiles and rows % (per_expert * 2 * SUBLANES) == 0
        return pl.BlockSpec((1, rows // per_expert, cols), lambda c: (c // per_expert, lax.rem(c, per_expert), 0))

    cast_specs = [cast_spec(w) for w in expert_weights]
    return pl.pallas_call(
        _mixer_prompt_kernel,
        grid=(tiles,),
        in_specs=[tile(d), whole(b, N_MOD * d)] + wspecs + cast_specs,
        out_specs=[tile(d), tile(d), tile(ROUTER_LANES),
                   pl.BlockSpec((None, 1, LANES), lambda c: (c, 0, 0)),
                   whole(CONV_A_WIDTH - 1, b, D_A), whole(CONV_B_WIDTH - 1, b, D_B)] + cast_specs,
        out_shape=[jax.ShapeDtypeStruct((b, s, d), F32),
                   jax.ShapeDtypeStruct((b, s, d), BF16),
                   jax.ShapeDtypeStruct((b, s, ROUTER_LANES), F32),
                   jax.ShapeDtypeStruct((tiles, 1, LANES), jnp.int32),
                   jax.ShapeDtypeStruct((CONV_A_WIDTH - 1, b, D_A), F32),
                   jax.ShapeDtypeStruct((CONV_B_WIDTH - 1, b, D_B), F32)]
        + [jax.ShapeDtypeStruct(w.shape, BF16) for w in expert_weights],
        scratch_shapes=[pltpu.VMEM(((CONV_A_WIDTH - 1) * b + SEQ_TILE, D_A), F32),
                        pltpu.VMEM(((CONV_B_WIDTH - 1) * b + SEQ_TILE, D_B), F32),
                        pltpu.VMEM((SEQ_TILE, D_B), F32)],
        compiler_params=pltpu.CompilerParams(dimension_semantics=("arbitrary",),
                                             vmem_limit_bytes=VMEM_LIMIT),
        name="mixer_prompt",
    )(x, mod, *weights, *expert_weights)


def _mixer_sample(x_tm, mod, sa_tm, sb_tm, p):
    n, d = x_tm.shape
    weights, wspecs = _mixer_weight_specs(p)
    full = lambda a: pl.BlockSpec(a.shape, lambda i: (0,) * a.ndim)
    whole = lambda *shape: pl.BlockSpec(shape, lambda i: (0,) * len(shape))
    return pl.pallas_call(
        _mixer_sample_kernel,
        grid=(1,),
        in_specs=[full(x_tm), full(mod)] + wspecs + [full(sa_tm), full(sb_tm)],
        out_specs=[whole(n, d), whole(n, d), whole(n, ROUTER_LANES), whole(1, LANES),
                   full(sa_tm), full(sb_tm)],
        out_shape=[jax.ShapeDtypeStruct((n, d), F32),
                   jax.ShapeDtypeStruct((n, d), BF16),
                   jax.ShapeDtypeStruct((n, ROUTER_LANES), F32),
                   jax.ShapeDtypeStruct((1, LANES), jnp.int32),
                   jax.ShapeDtypeStruct(sa_tm.shape, F32),
                   jax.ShapeDtypeStruct(sb_tm.shape, F32)],
        scratch_shapes=[pltpu.VMEM((n, D_B), BF16)],
        compiler_params=pltpu.CompilerParams(dimension_semantics=("arbitrary",),
                                             vmem_limit_bytes=VMEM_LIMIT),
        name="mixer_sample",
    )(x_tm, mod, *weights, sa_tm, sb_tm)


def _moe(counts, x1, h2, route, gt2_src, gt2_spec, p, grid, tile, rows, affine):
    slot = jnp.stack([lax.broadcasted_iota(F32, (rows, rows), 0), lax.broadcasted_iota(F32, (rows, rows), 1)])
    weights = [p["w1"], p["w3"], p["w2"], p["gf"], slot]
    grid_spec = pltpu.PrefetchScalarGridSpec(
        num_scalar_prefetch=1,
        grid=grid,
        in_specs=[tile(D_MODEL), tile(D_MODEL), tile(ROUTER_LANES), gt2_spec]
        + [_const_spec(w.shape) for w in weights],
        out_specs=tile(D_MODEL),
        scratch_shapes=[pltpu.VMEM((rows, D_MODEL), BF16),
                        pltpu.VMEM((rows, ROUTER_LANES), F32),
                        pltpu.VMEM((rows, D_MODEL), F32)],
    )
    return pl.pallas_call(
        functools.partial(_moe_kernel, affine=affine),
        grid_spec=grid_spec,
        out_shape=jax.ShapeDtypeStruct(x1.shape, F32),
        compiler_params=pltpu.CompilerParams(dimension_semantics=("arbitrary",) * len(grid),
                                             vmem_limit_bytes=VMEM_LIMIT),
        name="moe",
    )(counts, x1, h2, route, gt2_src, *weights)


def kernel(x_prompt, x_sample, c_prompt, c_sample, state_conv_a, state_conv_b, w_ada, b_ada, norm1_g,
           norm2_g, w_in, conv_a_w, w_out_a, conv_b_w, conv_b_bias, ln_b_g, ln_b_b, w_out_b, w_o,
           w_group, b_group, w_expert, b_expert, w1, w3, w2, final_norm_g):
    depth = w_ada.shape[0]
    assert depth == 1, "the MoE kernel fuses the final norm, so exactly one layer is supported"
    n_p, seq, d = x_prompt.shape
    n_s, steps, _ = x_sample.shape
    c_all = jnp.concatenate([c_prompt, c_sample], axis=0)
    xp = x_prompt
    xs = x_sample.transpose(1, 0, 2).reshape(steps * n_s, d)
    gf = final_norm_g.reshape(1, d)
    pa, pb, sa, sb = [], [], [], []
    for l in range(depth):
        pad = ROUTER_LANES - N_EXPERTS - N_GROUPS
        wr = jnp.pad(jnp.concatenate([w_expert[l], w_group[l]], axis=1).T, ((0, pad), (0, 0)))
        br = jnp.pad(jnp.concatenate([b_expert[l], b_group[l]]), (0, pad))
        br = jnp.broadcast_to(br[:, None], (ROUTER_LANES, LANES))
        p = {
            "g1": norm1_g[l].reshape(1, d), "g2": norm2_g[l].reshape(1, d),
            "w_in": w_in[l].astype(BF16), "woa": w_out_a[l].astype(BF16),
            "caw": jnp.broadcast_to(conv_a_w[l][:, None, :], (CONV_A_WIDTH, SUBLANES, D_A)),
            "cbw": jnp.broadcast_to(conv_b_w[l][:, None, :], (CONV_B_WIDTH, SUBLANES, D_B)),
            "cbb": conv_b_bias[l].reshape(1, D_B),
            "lng": ln_b_g[l].reshape(1, D_B), "lnb": ln_b_b[l].reshape(1, D_B),
            "wob": w_out_b[l].astype(BF16), "wo": w_o[l].astype(BF16), "wr": wr.astype(BF16), "br": br,
            "tri": jnp.triu(jnp.ones((SEQ_TILE, SEQ_TILE), BF16), k=1),
            "gf": gf,
        }
        mod_p, mod_s = _modulation(c_all, n_p, w_ada[l], b_ada[l].reshape(1, N_MOD * d))

        x1p, h2p, route_p, cnt_p, na_p, nb_p, w1b, w3b, w2b = _mixer_prompt(
            xp, mod_p, p, (w1[l], w3[l], w2[l]))
        p.update(w1=w1b, w3=w3b, w2=w2b.reshape(N_GROUPS, EXPERTS_PER_GROUP * D_EXPERT, d))
        tile_steps = SEQ_TILE // n_p
        xp = _moe(cnt_p[:, 0, :N_GROUPS].reshape(-1), x1p, h2p, route_p, mod_p,
                  pl.BlockSpec((n_p, d), lambda c, cnt: (0, N_MOD - 1)), p, (seq // tile_steps,),
                  lambda width: pl.BlockSpec((n_p, tile_steps, width), lambda c, cnt: (0, c, 0)), SEQ_TILE,
                  _per_sequence)

        sa_tm = state_conv_a[l].transpose(1, 0, 2)
        sb_tm = state_conv_b[l].transpose(1, 0, 2)
        x1s, h2s, route_s, cnt_s, na_s, nb_s = _mixer_sample(xs, mod_s, sa_tm, sb_tm, p)
        xs = _moe(cnt_s[0, :N_GROUPS], x1s, h2s, route_s, mod_s,
                  pl.BlockSpec((n_s, d), lambda i, c: (0, N_MOD - 1)), p, (1,),
                  lambda width: pl.BlockSpec((steps * n_s, width), lambda i, c: (0, 0)), steps * n_s, _affine)
        pa.append(na_p.transpose(1, 0, 2))
        pb.append(nb_p.transpose(1, 0, 2))
        sa.append(na_s.transpose(1, 0, 2))
        sb.append(nb_s.transpose(1, 0, 2))
    y_sample = xs.reshape(steps, n_s, d).transpose(1, 0, 2)
    return (xp, y_sample, jnp.stack(pa), jnp.stack(pb), jnp.stack(sa), jnp.stack(sb))
```

```python
import functools

import jax
import jax.numpy as jnp
from jax import lax
from jax.experimental import pallas as pl
from jax.experimental.pallas import tpu as pltpu

D_MODEL = 1024
D_A = 512
D_B = 512
CONV_A_WIDTH = 3
CONV_B_WIDTH = 31
N_GROUPS = 4
EXPERTS_PER_GROUP = 4
N_EXPERTS = 16
D_EXPERT = 256
N_MOD = 6
EPS = 1e-6
D_IN = 3 * D_A + 2 * D_B + 2 * D_MODEL

LANES = 128
SUBLANES = 8
ROUTER_LANES = LANES
SEQ_TILE = 512
CONV_ROWS = 64
CONV_LANES = 2 * LANES
SORT_SHIFT = 4
SORT_ALIGN = 1 << SORT_SHIFT
MOE_PASS_ROWS = (128, 144, 160, 176, 192, 224, 256)
VMEM_LIMIT = 52 * 1024 * 1024
MOD_BLOCK = 2 * D_MODEL

BF16 = jnp.bfloat16
F32 = jnp.float32


def _dot(a, b):
    return jnp.dot(a, b, preferred_element_type=F32)


def _unit_rms(x):
    return x * lax.rsqrt(jnp.mean(x * x, axis=-1, keepdims=True) + EPS)


def _affine(v, scale, shift=None):
    r = scale.shape[0]
    if r != 1 and r != v.shape[0]:
        v3 = v.reshape(v.shape[0] // r, r, v.shape[-1])
        shift3 = None if shift is None else shift[None]
        return _affine(v3, scale[None], shift3).reshape(v.shape)
    out = v * scale
    return out if shift is None else out + shift


def _per_sequence(v, scale, shift=None):
    n = scale.shape[0]
    v3 = v.reshape(n, v.shape[0] // n, v.shape[-1])
    out = v3 * scale[:, None, :]
    if shift is not None:
        out = out + shift[:, None, :]
    return out.reshape(v.shape)


def _tap(v, w8):
    return (v.reshape(v.shape[0] // SUBLANES, SUBLANES, v.shape[-1]) * w8[None]).reshape(v.shape)


def _route(logits_t):
    t = logits_t.shape[1]
    neg = -jnp.inf
    row = lax.broadcasted_iota(jnp.int32, (SUBLANES, t), 0).astype(F32)
    live = row < float(EXPERTS_PER_GROUP)
    gl = jnp.where(live, logits_t[N_EXPERTS:N_EXPERTS + SUBLANES, :], neg)
    gmax = jnp.max(gl, axis=0, keepdims=True)
    g_idx = jnp.min(jnp.where(gl == gmax, row, float(SUBLANES)), axis=0, keepdims=True)
    p_g = 1.0 / jnp.sum(jnp.exp(gl - gmax), axis=0, keepdims=True)
    el = None
    for g in reversed(range(N_GROUPS)):
        slab = logits_t[(g // 2) * SUBLANES:(g // 2 + 1) * SUBLANES, :]
        cand = slab if g % 2 == 0 else pltpu.roll(slab, SUBLANES - EXPERTS_PER_GROUP, axis=0)
        el = cand if el is None else jnp.where(g_idx == float(g), cand, el)
    el = jnp.where(live, el, neg)
    v1 = jnp.max(el, axis=0, keepdims=True)
    i1 = jnp.min(jnp.where(el == v1, row, float(SUBLANES)), axis=0, keepdims=True)
    el2 = jnp.where(row == i1, neg, el)
    v2 = jnp.max(el2, axis=0, keepdims=True)
    i2 = jnp.min(jnp.where(el2 == v2, row, float(SUBLANES)), axis=0, keepdims=True)
    e21 = jnp.exp(v2 - v1)
    wt1 = p_g / (1.0 + e21)
    wt2 = p_g * e21 / (1.0 + e21)
    comb = jnp.where(row == i1, wt1, jnp.where(row == i2, wt2, 0.0))
    onehot = jnp.where(row == g_idx, 1.0, 0.0)
    onehot_b = jnp.concatenate([onehot, jnp.zeros_like(onehot)], axis=0).astype(BF16)
    earlier = lax.broadcasted_iota(jnp.int32, (t, t), 0) < lax.broadcasted_iota(jnp.int32, (t, t), 1)
    before = _dot(onehot_b, jnp.where(earlier, 1.0, 0.0).astype(BF16))[0:SUBLANES, :]
    rank = jnp.sum(before * onehot, axis=0, keepdims=True)
    count = jnp.sum(onehot, axis=1, keepdims=True)
    row1 = lax.broadcasted_iota(jnp.int32, (SUBLANES, 1), 0)
    start = jnp.zeros((SUBLANES, 1), F32)
    for g in range(N_GROUPS - 1):
        start = start + jnp.where(row1 > g, count[g:g + 1, :], 0.0)
    dest = jnp.sum(onehot * start, axis=0, keepdims=True) + rank
    route_t = jnp.concatenate([jnp.where(live, comb, dest), jnp.broadcast_to(dest, (LANES - SUBLANES, t))],
                              axis=0)
    onehot_all = jnp.concatenate([onehot, jnp.zeros((LANES - SUBLANES, t), F32)], axis=0).astype(BF16)
    count_row = lax.dot_general(jnp.ones((2 * SUBLANES, t), BF16), onehot_all, (((1,), (1,)), ((), ())),
                                preferred_element_type=F32)[0:1, :]
    return jnp.transpose(route_t), count_row


def _ln_silu(v, bias, g, b):
    v = v + bias
    mu = jnp.mean(v, axis=-1, keepdims=True)
    d = v - mu
    var = jnp.mean(d * d, axis=-1, keepdims=True)
    y = d * lax.rsqrt(var + EPS) * g + b
    return y * jax.nn.sigmoid(y)


def _mixer_tail(x, hb, mod, b_a_conv, yb_in, g2, w_in_ref, woa_ref, wob_ref, wo_ref, wr_ref, br_ref,
                x1_ref, h2_ref, route_ref, cnt_ref, affine):
    y_a = _dot(b_a_conv.astype(BF16), woa_ref[...])
    y_b = _dot(yb_in, wob_ref[...])
    o = 3 * D_A + 2 * D_B
    mg_a = _dot(hb, w_in_ref[:, o:o + D_MODEL])
    mg_b = _dot(hb, w_in_ref[:, o + D_MODEL:o + 2 * D_MODEL])
    merged = jax.nn.sigmoid(mg_a) * y_a + jax.nn.sigmoid(mg_b) * y_b
    gt1 = mod[:, 2 * D_MODEL:3 * D_MODEL]
    x1 = x + affine(_dot(merged.astype(BF16), wo_ref[...]), gt1)
    sh2 = mod[:, 3 * D_MODEL:4 * D_MODEL]
    sc2 = mod[:, 4 * D_MODEL:5 * D_MODEL]
    h2 = affine(_unit_rms(x1), g2 * (1.0 + sc2), sh2).astype(BF16)
    logits_t = lax.dot_general(wr_ref[...], h2, (((1,), (1,)), ((), ())), preferred_element_type=F32)
    logits_t = logits_t + jnp.tile(br_ref[...], (1, h2.shape[0] // LANES))
    x1_ref[...] = x1.reshape(x1_ref.shape)
    h2_ref[...] = h2.reshape(h2_ref.shape)
    route, count = _route(logits_t)
    route_ref[...] = route.reshape(route_ref.shape)
    cnt_ref[...] = count.astype(jnp.int32)


def _mixer_prompt_kernel(x_ref, mod_ref, g1_ref, g2_ref, w_in_ref, caw_ref, woa_ref, cbw_ref, cbb_ref,
                         lng_ref, lnb_ref, wob_ref, wo_ref, wr_ref, br_ref,
                         w1f_ref, w3f_ref, w2f_ref,
                         x1_ref, h2_ref, route_ref, cnt_ref, na_ref, nb_ref, w1b_ref, w3b_ref, w2b_ref,
                         hist_a, hist_b, conv_b_ref):
    n_seq, steps, d = x_ref.shape
    t = n_seq * steps
    rows_a = (CONV_A_WIDTH - 1) * n_seq
    rows_b = (CONV_B_WIDTH - 1) * n_seq
    w1b_ref[...] = w1f_ref[...].astype(BF16)
    w3b_ref[...] = w3f_ref[...].astype(BF16)
    w2b_ref[...] = w2f_ref[...].astype(BF16)

    @pl.when(pl.program_id(0) == 0)
    def _():
        hist_a[0:rows_a, :] = jnp.zeros((rows_a, D_A), F32)
        hist_b[0:rows_b, :] = jnp.zeros((rows_b, D_B), F32)

    def time_major(v):
        return jnp.swapaxes(v.reshape(n_seq, steps, v.shape[-1]), 0, 1).reshape(v.shape)

    def sequence_major(v):
        return jnp.swapaxes(v.reshape(steps, n_seq, v.shape[-1]), 0, 1).reshape(v.shape)

    x = x_ref[...].reshape(t, d)
    mod = mod_ref[...]
    h = _per_sequence(_unit_rms(x), g1_ref[...] * (1.0 + mod[:, D_MODEL:2 * D_MODEL]), mod[:, 0:D_MODEL])
    hb = h.astype(BF16)

    def proj(lo, width):
        return _dot(hb, w_in_ref[:, lo:lo + width])

    hist_a[rows_a:rows_a + t, :] = time_major(proj(D_A, D_A) * proj(2 * D_A, D_A))
    conv_a = None
    for k in range(CONV_A_WIDTH):
        term = _tap(hist_a[k * n_seq:k * n_seq + t, :], caw_ref[k])
        conv_a = term if conv_a is None else conv_a + term
    b_a_conv = proj(0, D_A) * sequence_major(conv_a)
    na_ref[...] = hist_a[t:t + rows_a, :].reshape(na_ref.shape)

    hist_b[rows_b:rows_b + t, :] = time_major(proj(3 * D_A, D_B) * jax.nn.sigmoid(proj(3 * D_A + D_B, D_B)))
    nb_ref[...] = hist_b[t:t + rows_b, :].reshape(nb_ref.shape)
    for r0 in range(0, t, CONV_ROWS):
        for c0 in range(0, D_B, CONV_LANES):
            cols = slice(c0, c0 + CONV_LANES)
            acc = None
            for k in range(CONV_B_WIDTH):
                o = r0 + k * n_seq
                term = _tap(hist_b[o:o + CONV_ROWS, cols], cbw_ref[k][:, cols])
                acc = term if acc is None else acc + term
            conv_b_ref[r0:r0 + CONV_ROWS, cols] = acc
    yb_in = _ln_silu(sequence_major(conv_b_ref[...]), cbb_ref[...], lng_ref[...], lnb_ref[...]).astype(BF16)

    hist_a[0:rows_a, :] = hist_a[t:t + rows_a, :]
    hist_b[0:rows_b, :] = hist_b[t:t + rows_b, :]

    _mixer_tail(x, hb, mod, b_a_conv, yb_in, g2_ref[...], w_in_ref, woa_ref, wob_ref, wo_ref,
                wr_ref, br_ref, x1_ref, h2_ref, route_ref, cnt_ref, _per_sequence)


def _mixer_sample_kernel(x_ref, mod_ref, g1_ref, g2_ref, w_in_ref, caw_ref, woa_ref, cbw_ref, cbb_ref,
                         lng_ref, lnb_ref, wob_ref, wo_ref, wr_ref, br_ref, sa_ref, sb_ref,
                         x1_ref, h2_ref, route_ref, cnt_ref, na_ref, nb_ref, yb_in_ref):
    n_seq = mod_ref.shape[0]
    steps = x_ref.shape[0] // n_seq
    x = x_ref[...]
    mod = mod_ref[...]
    h = _affine(_unit_rms(x), g1_ref[...] * (1.0 + mod[:, D_MODEL:2 * D_MODEL]), mod[:, 0:D_MODEL])
    hb = h.astype(BF16)

    def proj(lo, width):
        return _dot(hb, w_in_ref[:, lo:lo + width])

    def slab(v, j):
        return v[j * n_seq:(j + 1) * n_seq, :]

    u_a = proj(D_A, D_A) * proj(2 * D_A, D_A)
    full_a = [sa_ref[j] for j in range(CONV_A_WIDTH - 1)] + [slab(u_a, j) for j in range(steps)]
    conv_a = []
    for s in range(steps):
        acc = jnp.zeros((n_seq, D_A), F32)
        for k in range(CONV_A_WIDTH):
            acc = acc + _tap(full_a[s + k], caw_ref[k])
        conv_a.append(acc)
    b_a_conv = proj(0, D_A) * jnp.concatenate(conv_a, axis=0)
    for j in range(CONV_A_WIDTH - 1):
        na_ref[j] = full_a[steps + j]

    u_b = proj(3 * D_A, D_B) * jax.nn.sigmoid(proj(3 * D_A + D_B, D_B))
    hist = CONV_B_WIDTH - 1

    def full_b(j):
        return sb_ref[j] if j < hist else slab(u_b, j - hist)

    bias, lng, lnb = cbb_ref[...], lng_ref[...], lnb_ref[...]
    for s in range(steps):
        acc = jnp.zeros((n_seq, D_B), F32)
        for k in range(CONV_B_WIDTH):
            acc = acc + _tap(full_b(s + k), cbw_ref[k])
        yb_in_ref[s * n_seq:(s + 1) * n_seq, :] = _ln_silu(acc, bias, lng, lnb).astype(BF16)
    for j in range(hist):
        nb_ref[j] = full_b(steps + j)

    _mixer_tail(x, hb, mod, b_a_conv, yb_in_ref[...], g2_ref[...], w_in_ref, woa_ref, wob_ref, wo_ref,
                wr_ref, br_ref, x1_ref, h2_ref, route_ref, cnt_ref, _affine)


def _moe_kernel(cnt_ref, x1_ref, h2_ref, route_ref, gt2_ref, w1_ref, w3_ref, w2_ref, gf_ref, y_ref,
                h2s_ref, cs_ref, acc_ref, *, affine):
    def rows(ref):
        return ref[...].reshape(-1, ref.shape[-1])

    tp = h2s_ref.shape[0]
    tile = pl.program_id(0)
    route = rows(route_ref)
    t = route.shape[0]
    lane_i = lax.broadcasted_iota(jnp.int32, route.shape, 1)
    dest_row = jnp.transpose(route)[EXPERTS_PER_GROUP:EXPERTS_PER_GROUP + 1, :]
    slot = lax.broadcasted_iota(jnp.int32, (tp, t), 0).astype(F32)
    p_sort = jnp.where(slot == dest_row, 1.0, 0.0).astype(BF16)
    h2s_ref[...] = _dot(p_sort, rows(h2_ref)).astype(BF16)
    comb = jnp.where(lane_i < EXPERTS_PER_GROUP, route, 0.0)
    c_hi = comb.astype(BF16).astype(F32)
    c_mid = (comb - c_hi).astype(BF16).astype(F32)
    c_lo = comb - c_hi - c_mid
    packed = (c_hi + pltpu.roll(c_mid, EXPERTS_PER_GROUP, axis=1)
              + pltpu.roll(c_lo, 2 * EXPERTS_PER_GROUP, axis=1)).astype(BF16)
    moved = _dot(p_sort, packed)
    cs_ref[...] = (moved + pltpu.roll(moved, LANES - EXPERTS_PER_GROUP, axis=1)
                   + pltpu.roll(moved, LANES - 2 * EXPERTS_PER_GROUP, axis=1))
    acc_ref[...] = jnp.zeros(acc_ref.shape, F32)

    def expert_pass(rows, g, window, lo, end):
        off = pl.multiple_of(jnp.minimum(window, tp - rows), SORT_ALIGN)
        row_id = off + lax.broadcasted_iota(jnp.int32, (rows, 1), 0)
        valid = (row_id >= jnp.maximum(lo, window)) & (row_id < end)
        hb = h2s_ref[pl.ds(off, rows), :]
        c4 = jnp.where(valid, cs_ref[pl.ds(off, rows), :], 0.0)
        experts = [g * EXPERTS_PER_GROUP + e for e in range(EXPERTS_PER_GROUP)]
        gate = jnp.concatenate([_dot(hb, w1_ref[e]) for e in experts], axis=1)
        up = jnp.concatenate([_dot(hb, w3_ref[e]) for e in experts], axis=1)
        cexp = jnp.concatenate(
            [jnp.broadcast_to(c4[:, e:e + 1], (rows, D_EXPERT)) for e in range(EXPERTS_PER_GROUP)], axis=1)
        a = gate * jax.nn.sigmoid(gate) * up * cexp
        acc_ref[pl.ds(off, rows), :] += _dot(a.astype(BF16), w2_ref[g])

    def group(g, start):
        end = start + cnt_ref[tile * N_GROUPS + g]
        base = lax.shift_right_logical(start, SORT_SHIFT) * SORT_ALIGN
        span = end - base

        def passes(rows):
            def run():
                def body(p, carry):
                    expert_pass(rows, g, base + p * rows, start, end)
                    return carry
                lax.fori_loop(0, jnp.where(end > start, lax.div(span + (rows - 1), rows), 0), body, 0)
            return run

        size_class = sum((span > rows).astype(jnp.int32) for rows in MOE_PASS_ROWS[:-1])
        lax.switch(size_class, [passes(rows) for rows in MOE_PASS_ROWS])
        return end

    lax.fori_loop(0, N_GROUPS, group, jnp.int32(0))

    dest_col = route[:, EXPERTS_PER_GROUP:EXPERTS_PER_GROUP + 1]
    p_back = jnp.where(lax.broadcasted_iota(jnp.int32, (t, tp), 1).astype(F32) == dest_col,
                       1.0, 0.0).astype(BF16)
    y = acc_ref[...]
    y_hi = y.astype(BF16)
    y_lo = (y - y_hi.astype(F32)).astype(BF16)
    y_tok = _dot(p_back, y_hi) + _dot(p_back, y_lo)
    x2 = rows(x1_ref) + affine(y_tok, gt2_ref[...])
    y_ref[...] = (_unit_rms(x2) * gf_ref[...]).reshape(y_ref.shape)


def _mod_kernel(c_ref, w_ref, b_ref, win_ref, op_ref, os_ref, winb_ref):
    winb_ref[...] = win_ref[...].astype(BF16)
    c = c_ref[...]
    a = (c * jax.nn.sigmoid(c)).astype(BF16)
    mod = _dot(a, w_ref[...].astype(BF16)) + b_ref[...]
    n_p = op_ref.shape[0]
    op_ref[...] = mod[:n_p]
    os_ref[...] = mod[n_p:]


def _const_spec(shape):
    nd = len(shape)
    return pl.BlockSpec(shape, lambda *_: (0,) * nd, pipeline_mode=pl.Buffered(1))


def _modulation(c_all, n_p, w_ada, b_ada, w_in):
    n, d = c_all.shape
    n_out = w_ada.shape[1]
    steps = n_out // MOD_BLOCK
    cast_cols = w_in.shape[1] // steps
    assert n_p % SUBLANES == 0 and cast_cols * steps == w_in.shape[1] and cast_cols % LANES == 0
    cast_spec = pl.BlockSpec((w_in.shape[0], cast_cols), lambda j: (0, j))
    return pl.pallas_call(
        _mod_kernel,
        grid=(steps,),
        in_specs=[pl.BlockSpec((n, d), lambda j: (0, 0)),
                  pl.BlockSpec((d, MOD_BLOCK), lambda j: (0, j)),
                  pl.BlockSpec((1, MOD_BLOCK), lambda j: (0, j)),
                  cast_spec],
        out_specs=[pl.BlockSpec((n_p, MOD_BLOCK), lambda j: (0, j)),
                   pl.BlockSpec((n - n_p, MOD_BLOCK), lambda j: (0, j)),
                   cast_spec],
        out_shape=[jax.ShapeDtypeStruct((n_p, n_out), F32), jax.ShapeDtypeStruct((n - n_p, n_out), F32),
                   jax.ShapeDtypeStruct(w_in.shape, BF16)],
        compiler_params=pltpu.CompilerParams(dimension_semantics=("arbitrary",),
                                             vmem_limit_bytes=VMEM_LIMIT),
        name="modulation",
    )(c_all, w_ada, b_ada, w_in)


def _mixer_weight_specs(p):
    names = ("g1", "g2", "w_in", "caw", "woa", "cbw", "cbb", "lng", "lnb", "wob", "wo", "wr", "br")
    return [p[k] for k in names], [_const_spec(p[k].shape) for k in names]


def _mixer_prompt(x, mod, p, expert_weights):
    b, s, d = x.shape
    assert b == SUBLANES, "the time-major conv layout puts one sequence on each sublane"
    steps = SEQ_TILE // b
    tiles = s // steps
    weights, wspecs = _mixer_weight_specs(p)
    tile = lambda width: pl.BlockSpec((b, steps, width), lambda c: (0, c, 0))
    whole = lambda *shape: pl.BlockSpec(shape, lambda c: (0,) * len(shape))

    def cast_spec(w):
        experts, rows, cols = w.shape
        per_expert = tiles // experts
        assert per_expert * experts == tiles and rows % (per_expert * 2 * SUBLANES) == 0
        return pl.BlockSpec((1, rows // per_expert, cols), lambda c: (c // per_expert, lax.rem(c, per_expert), 0))

    cast_specs = [cast_spec(w) for w in expert_weights]
    return pl.pallas_call(
        _mixer_prompt_kernel,
        grid=(tiles,),
        in_specs=[tile(d), whole(b, N_MOD * d)] + wspecs + cast_specs,
        out_specs=[tile(d), tile(d), tile(ROUTER_LANES),
                   pl.BlockSpec((None, 1, LANES), lambda c: (c, 0, 0)),
                   whole(CONV_A_WIDTH - 1, b, D_A), whole(CONV_B_WIDTH - 1, b, D_B)] + cast_specs,
        out_shape=[jax.ShapeDtypeStruct((b, s, d), F32),
                   jax.ShapeDtypeStruct((b, s, d), BF16),
                   jax.ShapeDtypeStruct((b, s, ROUTER_LANES), F32),
                   jax.ShapeDtypeStruct((tiles, 1, LANES), jnp.int32),
                   jax.ShapeDtypeStruct((CONV_A_WIDTH - 1, b, D_A), F32),
                   jax.ShapeDtypeStruct((CONV_B_WIDTH - 1, b, D_B), F32)]
        + [jax.ShapeDtypeStruct(w.shape, BF16) for w in expert_weights],
        scratch_shapes=[pltpu.VMEM(((CONV_A_WIDTH - 1) * b + SEQ_TILE, D_A), F32),
                        pltpu.VMEM(((CONV_B_WIDTH - 1) * b + SEQ_TILE, D_B), F32),
                        pltpu.VMEM((SEQ_TILE, D_B), F32)],
        compiler_params=pltpu.CompilerParams(dimension_semantics=("arbitrary",),
                                             vmem_limit_bytes=VMEM_LIMIT),
        name="mixer_prompt",
    )(x, mod, *weights, *expert_weights)


def _mixer_sample(x_tm, mod, sa_tm, sb_tm, p):
    n, d = x_tm.shape
    weights, wspecs = _mixer_weight_specs(p)
    full = lambda a: pl.BlockSpec(a.shape, lambda i: (0,) * a.ndim)
    whole = lambda *shape: pl.BlockSpec(shape, lambda i: (0,) * len(shape))
    return pl.pallas_call(
        _mixer_sample_kernel,
        grid=(1,),
        in_specs=[full(x_tm), full(mod)] + wspecs + [full(sa_tm), full(sb_tm)],
        out_specs=[whole(n, d), whole(n, d), whole(n, ROUTER_LANES), whole(1, LANES),
                   full(sa_tm), full(sb_tm)],
        out_shape=[jax.ShapeDtypeStruct((n, d), F32),
                   jax.ShapeDtypeStruct((n, d), BF16),
                   jax.ShapeDtypeStruct((n, ROUTER_LANES), F32),
                   jax.ShapeDtypeStruct((1, LANES), jnp.int32),
                   jax.ShapeDtypeStruct(sa_tm.shape, F32),
                   jax.ShapeDtypeStruct(sb_tm.shape, F32)],
        scratch_shapes=[pltpu.VMEM((n, D_B), BF16)],
        compiler_params=pltpu.CompilerParams(dimension_semantics=("arbitrary",),
                                             vmem_limit_bytes=VMEM_LIMIT),
        name="mixer_sample",
    )(x_tm, mod, *weights, sa_tm, sb_tm)


def _moe(counts, x1, h2, route, gt2_src, gt2_spec, p, grid, tile, rows, affine):
    weights = [p["w1"], p["w3"], p["w2"], p["gf"]]
    grid_spec = pltpu.PrefetchScalarGridSpec(
        num_scalar_prefetch=1,
        grid=grid,
        in_specs=[tile(D_MODEL), tile(D_MODEL), tile(ROUTER_LANES), gt2_spec]
        + [_const_spec(w.shape) for w in weights],
        out_specs=tile(D_MODEL),
        scratch_shapes=[pltpu.VMEM((rows, D_MODEL), BF16),
                        pltpu.VMEM((rows, ROUTER_LANES), F32),
                        pltpu.VMEM((rows, D_MODEL), F32)],
    )
    return pl.pallas_call(
        functools.partial(_moe_kernel, affine=affine),
        grid_spec=grid_spec,
        out_shape=jax.ShapeDtypeStruct(x1.shape, F32),
        compiler_params=pltpu.CompilerParams(dimension_semantics=("arbitrary",) * len(grid),
                                             vmem_limit_bytes=VMEM_LIMIT),
        name="moe",
    )(counts, x1, h2, route, gt2_src, *weights)


def kernel(x_prompt, x_sample, c_prompt, c_sample, state_conv_a, state_conv_b, w_ada, b_ada, norm1_g,
           norm2_g, w_in, conv_a_w, w_out_a, conv_b_w, conv_b_bias, ln_b_g, ln_b_b, w_out_b, w_o,
           w_group, b_group, w_expert, b_expert, w1, w3, w2, final_norm_g):
    depth = w_ada.shape[0]
    assert depth == 1, "the MoE kernel fuses the final norm, so exactly one layer is supported"
    n_p, seq, d = x_prompt.shape
    n_s, steps, _ = x_sample.shape
    c_all = jnp.concatenate([c_prompt, c_sample], axis=0)
    xp = x_prompt
    xs = x_sample.transpose(1, 0, 2).reshape(steps * n_s, d)
    gf = final_norm_g.reshape(1, d)
    pa, pb, sa, sb = [], [], [], []
    for l in range(depth):
        pad = ROUTER_LANES - N_EXPERTS - N_GROUPS
        wr = jnp.pad(jnp.concatenate([w_expert[l], w_group[l]], axis=1).T, ((0, pad), (0, 0)))
        br = jnp.pad(jnp.concatenate([b_expert[l], b_group[l]]), (0, pad))
        br = jnp.broadcast_to(br[:, None], (ROUTER_LANES, LANES))
        p = {
            "g1": norm1_g[l].reshape(1, d), "g2": norm2_g[l].reshape(1, d),
            "woa": w_out_a[l].astype(BF16),
            "caw": jnp.broadcast_to(conv_a_w[l][:, None, :], (CONV_A_WIDTH, SUBLANES, D_A)),
            "cbw": jnp.broadcast_to(conv_b_w[l][:, None, :], (CONV_B_WIDTH, SUBLANES, D_B)),
            "cbb": conv_b_bias[l].reshape(1, D_B),
            "lng": ln_b_g[l].reshape(1, D_B), "lnb": ln_b_b[l].reshape(1, D_B),
            "wob": w_out_b[l].astype(BF16), "wo": w_o[l].astype(BF16), "wr": wr.astype(BF16), "br": br,
            "gf": gf,
        }
        mod_p, mod_s, p["w_in"] = _modulation(c_all, n_p, w_ada[l], b_ada[l].reshape(1, N_MOD * d), w_in[l])

        x1p, h2p, route_p, cnt_p, na_p, nb_p, w1b, w3b, w2b = _mixer_prompt(
            xp, mod_p, p, (w1[l], w3[l], w2[l]))
        p.update(w1=w1b, w3=w3b, w2=w2b.reshape(N_GROUPS, EXPERTS_PER_GROUP * D_EXPERT, d))
        tile_steps = SEQ_TILE // n_p
        xp = _moe(cnt_p[:, 0, :N_GROUPS].reshape(-1), x1p, h2p, route_p, mod_p,
                  pl.BlockSpec((n_p, d), lambda c, cnt: (0, N_MOD - 1)), p, (seq // tile_steps,),
                  lambda width: pl.BlockSpec((n_p, tile_steps, width), lambda c, cnt: (0, c, 0)), SEQ_TILE,
                  _per_sequence)

        sa_tm = state_conv_a[l].transpose(1, 0, 2)
        sb_tm = state_conv_b[l].transpose(1, 0, 2)
        x1s, h2s, route_s, cnt_s, na_s, nb_s = _mixer_sample(xs, mod_s, sa_tm, sb_tm, p)
        xs = _moe(cnt_s[0, :N_GROUPS], x1s, h2s, route_s, mod_s,
                  pl.BlockSpec((n_s, d), lambda i, c: (0, N_MOD - 1)), p, (1,),
                  lambda width: pl.BlockSpec((steps * n_s, width), lambda i, c: (0, 0)), steps * n_s, _affine)
        pa.append(na_p.transpose(1, 0, 2))
        pb.append(nb_p.transpose(1, 0, 2))
        sa.append(na_s.transpose(1, 0, 2))
        sb.append(nb_s.transpose(1, 0, 2))
    y_sample = xs.reshape(steps, n_s, d).transpose(1, 0, 2)
    return (xp, y_sample, jnp.stack(pa), jnp.stack(pb), jnp.stack(sa), jnp.stack(sb))
```

```python
import functools

import jax
import jax.numpy as jnp
from jax import lax
from jax.experimental import pallas as pl
from jax.experimental.pallas import tpu as pltpu

D_MODEL = 1024
D_A = 512
D_B = 512
CONV_A_WIDTH = 3
CONV_B_WIDTH = 31
N_GROUPS = 4
EXPERTS_PER_GROUP = 4
N_EXPERTS = 16
D_EXPERT = 256
N_MOD = 6
EPS = 1e-6

LANES = 128
SUBLANES = 8
ROUTER_LANES = LANES
SEQ_TILE = 512
CONV_ROWS = 64
CONV_LANES = 2 * LANES
SORT_SHIFT = 4
SORT_ALIGN = 1 << SORT_SHIFT
MOE_PASS_ROWS = (128, 144, 160, 176, 192, 224, 256)
VMEM_LIMIT = 52 * 1024 * 1024
MOD_BLOCK = 2 * D_MODEL

BF16 = jnp.bfloat16
F32 = jnp.float32

assert 2 * EXPERTS_PER_GROUP == SUBLANES and N_GROUPS <= SUBLANES and N_EXPERTS == N_GROUPS * EXPERTS_PER_GROUP


def _dot(a, b):
    return jnp.dot(a, b, preferred_element_type=F32)


def _unit_rms(x):
    return x * lax.rsqrt(jnp.mean(x * x, axis=-1, keepdims=True) + EPS)


def _affine(v, scale, shift=None):
    r = scale.shape[0]
    if r != 1 and r != v.shape[0]:
        v3 = v.reshape(v.shape[0] // r, r, v.shape[-1])
        shift3 = None if shift is None else shift[None]
        return _affine(v3, scale[None], shift3).reshape(v.shape)
    out = v * scale
    return out if shift is None else out + shift


def _per_sequence(v, scale, shift=None):
    n = scale.shape[0]
    v3 = v.reshape(n, v.shape[0] // n, v.shape[-1])
    out = v3 * scale[:, None, :]
    if shift is not None:
        out = out + shift[:, None, :]
    return out.reshape(v.shape)


def _tap(v, w8):
    return (v.reshape(v.shape[0] // SUBLANES, SUBLANES, v.shape[-1]) * w8[None]).reshape(v.shape)


def _route(logits_t):
    t = logits_t.shape[1]
    neg = -jnp.inf
    row = lax.broadcasted_iota(jnp.int32, (SUBLANES, t), 0).astype(F32)
    live = row < float(EXPERTS_PER_GROUP)
    gl = jnp.where(live, logits_t[N_EXPERTS:N_EXPERTS + SUBLANES, :], neg)
    gmax = jnp.max(gl, axis=0, keepdims=True)
    g_idx = jnp.min(jnp.where(gl == gmax, row, float(SUBLANES)), axis=0, keepdims=True)
    p_g = 1.0 / jnp.sum(jnp.exp(gl - gmax), axis=0, keepdims=True)
    el = None
    for g in reversed(range(N_GROUPS)):
        slab = logits_t[(g // 2) * SUBLANES:(g // 2 + 1) * SUBLANES, :]
        cand = slab if g % 2 == 0 else pltpu.roll(slab, SUBLANES - EXPERTS_PER_GROUP, axis=0)
        el = cand if el is None else jnp.where(g_idx == float(g), cand, el)
    el = jnp.where(live, el, neg)
    v1 = jnp.max(el, axis=0, keepdims=True)
    i1 = jnp.min(jnp.where(el == v1, row, float(SUBLANES)), axis=0, keepdims=True)
    el2 = jnp.where(row == i1, neg, el)
    v2 = jnp.max(el2, axis=0, keepdims=True)
    i2 = jnp.min(jnp.where(el2 == v2, row, float(SUBLANES)), axis=0, keepdims=True)
    e21 = jnp.exp(v2 - v1)
    wt1 = p_g / (1.0 + e21)
    wt2 = p_g * e21 / (1.0 + e21)
    comb = jnp.where(row == i1, wt1, jnp.where(row == i2, wt2, 0.0))
    onehot = jnp.where(row == g_idx, 1.0, 0.0)
    onehot_b = jnp.concatenate([onehot, jnp.zeros_like(onehot)], axis=0).astype(BF16)
    earlier = lax.broadcasted_iota(jnp.int32, (t, t), 0) < lax.broadcasted_iota(jnp.int32, (t, t), 1)
    before = _dot(onehot_b, jnp.where(earlier, 1.0, 0.0).astype(BF16))[0:SUBLANES, :]
    rank = jnp.sum(before * onehot, axis=0, keepdims=True)
    count = jnp.sum(onehot, axis=1, keepdims=True)
    row1 = lax.broadcasted_iota(jnp.int32, (SUBLANES, 1), 0)
    start = jnp.zeros((SUBLANES, 1), F32)
    for g in range(N_GROUPS - 1):
        start = start + jnp.where(row1 > g, count[g:g + 1, :], 0.0)
    dest = jnp.sum(onehot * start, axis=0, keepdims=True) + rank
    route_t = jnp.concatenate([jnp.where(live, comb, dest), jnp.broadcast_to(dest, (LANES - SUBLANES, t))],
                              axis=0)
    onehot_all = jnp.concatenate([onehot, jnp.zeros((LANES - SUBLANES, t), F32)], axis=0).astype(BF16)
    count_row = lax.dot_general(jnp.ones((2 * SUBLANES, t), BF16), onehot_all, (((1,), (1,)), ((), ())),
                                preferred_element_type=F32)[0:1, :]
    return jnp.transpose(route_t), count_row


def _ln_silu(v, bias, g, b):
    v = v + bias
    mu = jnp.mean(v, axis=-1, keepdims=True)
    d = v - mu
    var = jnp.mean(d * d, axis=-1, keepdims=True)
    y = d * lax.rsqrt(var + EPS) * g + b
    return y * jax.nn.sigmoid(y)


def _mixer_tail(x, hb, mod, b_a_conv, yb_in, g2, w_in_ref, woa_ref, wob_ref, wo_ref, wr_ref, br_ref,
                x1_ref, h2_ref, route_ref, cnt_ref, affine):
    y_a = _dot(b_a_conv.astype(BF16), woa_ref[...])
    y_b = _dot(yb_in, wob_ref[...])
    o = 3 * D_A + 2 * D_B
    mg_a = _dot(hb, w_in_ref[:, o:o + D_MODEL])
    mg_b = _dot(hb, w_in_ref[:, o + D_MODEL:o + 2 * D_MODEL])
    merged = jax.nn.sigmoid(mg_a) * y_a + jax.nn.sigmoid(mg_b) * y_b
    gt1 = mod[:, 2 * D_MODEL:3 * D_MODEL]
    x1 = x + affine(_dot(merged.astype(BF16), wo_ref[...]), gt1)
    sh2 = mod[:, 3 * D_MODEL:4 * D_MODEL]
    sc2 = mod[:, 4 * D_MODEL:5 * D_MODEL]
    h2 = affine(_unit_rms(x1), g2 * (1.0 + sc2), sh2).astype(BF16)
    logits_t = lax.dot_general(wr_ref[...], h2, (((1,), (1,)), ((), ())), preferred_element_type=F32)
    logits_t = logits_t + jnp.tile(br_ref[...], (1, h2.shape[0] // LANES))
    x1_ref[...] = x1.reshape(x1_ref.shape)
    h2_ref[...] = h2.reshape(h2_ref.shape)
    route, count = _route(logits_t)
    route_ref[...] = route.reshape(route_ref.shape)
    cnt_ref[...] = count.astype(jnp.int32)


def _mixer_prompt_kernel(x_ref, mod_ref, g1_ref, g2_ref, w_in_ref, caw_ref, woa_ref, cbw_ref, cbb_ref,
                         lng_ref, lnb_ref, wob_ref, wo_ref, wr_ref, br_ref,
                         w1f_ref, w3f_ref, w2f_ref,
                         x1_ref, h2_ref, route_ref, cnt_ref, na_ref, nb_ref, w1b_ref, w3b_ref, w2b_ref,
                         hist_a, hist_b, conv_b_ref):
    n_seq, steps, d = x_ref.shape
    t = n_seq * steps
    rows_a = (CONV_A_WIDTH - 1) * n_seq
    rows_b = (CONV_B_WIDTH - 1) * n_seq
    w1b_ref[...] = w1f_ref[...].astype(BF16)
    w3b_ref[...] = w3f_ref[...].astype(BF16)
    w2b_ref[...] = w2f_ref[...].astype(BF16)

    @pl.when(pl.program_id(0) == 0)
    def _():
        hist_a[0:rows_a, :] = jnp.zeros((rows_a, D_A), F32)
        hist_b[0:rows_b, :] = jnp.zeros((rows_b, D_B), F32)

    def time_major(v):
        return jnp.swapaxes(v.reshape(n_seq, steps, v.shape[-1]), 0, 1).reshape(v.shape)

    def sequence_major(v):
        return jnp.swapaxes(v.reshape(steps, n_seq, v.shape[-1]), 0, 1).reshape(v.shape)

    x = x_ref[...].reshape(t, d)
    mod = mod_ref[...]
    h = _per_sequence(_unit_rms(x), g1_ref[...] * (1.0 + mod[:, D_MODEL:2 * D_MODEL]), mod[:, 0:D_MODEL])
    hb = h.astype(BF16)

    def proj(lo, width):
        return _dot(hb, w_in_ref[:, lo:lo + width])

    hist_a[rows_a:rows_a + t, :] = time_major(proj(D_A, D_A) * proj(2 * D_A, D_A))
    conv_a = None
    for k in range(CONV_A_WIDTH):
        term = _tap(hist_a[k * n_seq:k * n_seq + t, :], caw_ref[k])
        conv_a = term if conv_a is None else conv_a + term
    b_a_conv = proj(0, D_A) * sequence_major(conv_a)
    na_ref[...] = hist_a[t:t + rows_a, :].reshape(na_ref.shape)

    hist_b[rows_b:rows_b + t, :] = time_major(proj(3 * D_A, D_B) * jax.nn.sigmoid(proj(3 * D_A + D_B, D_B)))
    nb_ref[...] = hist_b[t:t + rows_b, :].reshape(nb_ref.shape)
    for r0 in range(0, t, CONV_ROWS):
        for c0 in range(0, D_B, CONV_LANES):
            cols = slice(c0, c0 + CONV_LANES)
            acc = None
            for k in range(CONV_B_WIDTH):
                o = r0 + k * n_seq
                term = _tap(hist_b[o:o + CONV_ROWS, cols], cbw_ref[k][:, cols])
                acc = term if acc is None else acc + term
            conv_b_ref[r0:r0 + CONV_ROWS, cols] = acc
    yb_in = _ln_silu(sequence_major(conv_b_ref[...]), cbb_ref[...], lng_ref[...], lnb_ref[...]).astype(BF16)

    hist_a[0:rows_a, :] = hist_a[t:t + rows_a, :]
    hist_b[0:rows_b, :] = hist_b[t:t + rows_b, :]

    _mixer_tail(x, hb, mod, b_a_conv, yb_in, g2_ref[...], w_in_ref, woa_ref, wob_ref, wo_ref,
                wr_ref, br_ref, x1_ref, h2_ref, route_ref, cnt_ref, _per_sequence)


def _mixer_sample_kernel(x_ref, mod_ref, g1_ref, g2_ref, w_in_ref, caw_ref, woa_ref, cbw_ref, cbb_ref,
                         lng_ref, lnb_ref, wob_ref, wo_ref, wr_ref, br_ref, sa_ref, sb_ref,
                         x1_ref, h2_ref, route_ref, cnt_ref, na_ref, nb_ref, yb_in_ref):
    n_seq = mod_ref.shape[0]
    steps = x_ref.shape[0] // n_seq
    x = x_ref[...]
    mod = mod_ref[...]
    h = _affine(_unit_rms(x), g1_ref[...] * (1.0 + mod[:, D_MODEL:2 * D_MODEL]), mod[:, 0:D_MODEL])
    hb = h.astype(BF16)

    def proj(lo, width):
        return _dot(hb, w_in_ref[:, lo:lo + width])

    def slab(v, j):
        return v[j * n_seq:(j + 1) * n_seq, :]

    u_a = proj(D_A, D_A) * proj(2 * D_A, D_A)
    full_a = [sa_ref[j] for j in range(CONV_A_WIDTH - 1)] + [slab(u_a, j) for j in range(steps)]
    conv_a = []
    for s in range(steps):
        acc = jnp.zeros((n_seq, D_A), F32)
        for k in range(CONV_A_WIDTH):
            acc = acc + _tap(full_a[s + k], caw_ref[k])
        conv_a.append(acc)
    b_a_conv = proj(0, D_A) * jnp.concatenate(conv_a, axis=0)
    for j in range(CONV_A_WIDTH - 1):
        na_ref[j] = full_a[steps + j]

    u_b = proj(3 * D_A, D_B) * jax.nn.sigmoid(proj(3 * D_A + D_B, D_B))
    hist = CONV_B_WIDTH - 1

    def full_b(j):
        return sb_ref[j] if j < hist else slab(u_b, j - hist)

    bias, lng, lnb = cbb_ref[...], lng_ref[...], lnb_ref[...]
    for s in range(steps):
        acc = jnp.zeros((n_seq, D_B), F32)
        for k in range(CONV_B_WIDTH):
            acc = acc + _tap(full_b(s + k), cbw_ref[k])
        yb_in_ref[s * n_seq:(s + 1) * n_seq, :] = _ln_silu(acc, bias, lng, lnb).astype(BF16)
    for j in range(hist):
        nb_ref[j] = full_b(steps + j)

    _mixer_tail(x, hb, mod, b_a_conv, yb_in_ref[...], g2_ref[...], w_in_ref, woa_ref, wob_ref, wo_ref,
                wr_ref, br_ref, x1_ref, h2_ref, route_ref, cnt_ref, _affine)


def _moe_kernel(cnt_ref, x1_ref, h2_ref, route_ref, gt2_ref, w1_ref, w3_ref, w2_ref, gf_ref, y_ref,
                h2s_ref, cs_ref, acc_ref, *, affine):
    def rows(ref):
        return ref[...].reshape(-1, ref.shape[-1])

    tp = h2s_ref.shape[0]
    tile = pl.program_id(0)
    route = rows(route_ref)
    t = route.shape[0]
    lane_i = lax.broadcasted_iota(jnp.int32, route.shape, 1)
    dest_row = jnp.transpose(route)[EXPERTS_PER_GROUP:EXPERTS_PER_GROUP + 1, :]
    slot = lax.broadcasted_iota(jnp.int32, (tp, t), 0).astype(F32)
    p_sort = jnp.where(slot == dest_row, 1.0, 0.0).astype(BF16)
    h2s_ref[...] = _dot(p_sort, rows(h2_ref)).astype(BF16)
    comb = jnp.where(lane_i < EXPERTS_PER_GROUP, route, 0.0)
    c_hi = comb.astype(BF16).astype(F32)
    c_mid = (comb - c_hi).astype(BF16).astype(F32)
    c_lo = comb - c_hi - c_mid
    packed = (c_hi + pltpu.roll(c_mid, EXPERTS_PER_GROUP, axis=1)
              + pltpu.roll(c_lo, 2 * EXPERTS_PER_GROUP, axis=1)).astype(BF16)
    moved = _dot(p_sort, packed)
    cs_ref[...] = (moved + pltpu.roll(moved, LANES - EXPERTS_PER_GROUP, axis=1)
                   + pltpu.roll(moved, LANES - 2 * EXPERTS_PER_GROUP, axis=1))
    acc_ref[...] = jnp.zeros(acc_ref.shape, F32)

    def expert_pass(rows, g, window, lo, end):
        off = pl.multiple_of(jnp.minimum(window, tp - rows), SORT_ALIGN)
        row_id = off + lax.broadcasted_iota(jnp.int32, (rows, 1), 0)
        valid = (row_id >= jnp.maximum(lo, window)) & (row_id < end)
        hb = h2s_ref[pl.ds(off, rows), :]
        c4 = jnp.where(valid, cs_ref[pl.ds(off, rows), :], 0.0)
        experts = [g * EXPERTS_PER_GROUP + e for e in range(EXPERTS_PER_GROUP)]
        gate = jnp.concatenate([_dot(hb, w1_ref[e]) for e in experts], axis=1)
        up = jnp.concatenate([_dot(hb, w3_ref[e]) for e in experts], axis=1)
        cexp = jnp.concatenate(
            [jnp.broadcast_to(c4[:, e:e + 1], (rows, D_EXPERT)) for e in range(EXPERTS_PER_GROUP)], axis=1)
        a = gate * jax.nn.sigmoid(gate) * up * cexp
        acc_ref[pl.ds(off, rows), :] += _dot(a.astype(BF16), w2_ref[g])

    def group(g, start):
        end = start + cnt_ref[tile * N_GROUPS + g]
        base = lax.shift_right_logical(start, SORT_SHIFT) * SORT_ALIGN
        span = end - base

        def passes(rows):
            def run():
                def body(p, carry):
                    expert_pass(rows, g, base + p * rows, start, end)
                    return carry
                lax.fori_loop(0, jnp.where(end > start, lax.div(span + (rows - 1), rows), 0), body, 0)
            return run

        size_class = sum((span > rows).astype(jnp.int32) for rows in MOE_PASS_ROWS[:-1])
        lax.switch(size_class, [passes(rows) for rows in MOE_PASS_ROWS])
        return end

    lax.fori_loop(0, N_GROUPS, group, jnp.int32(0))

    dest_col = route[:, EXPERTS_PER_GROUP:EXPERTS_PER_GROUP + 1]
    p_back = jnp.where(lax.broadcasted_iota(jnp.int32, (t, tp), 1).astype(F32) == dest_col,
                       1.0, 0.0).astype(BF16)
    y = acc_ref[...]
    y_hi = y.astype(BF16)
    y_lo = (y - y_hi.astype(F32)).astype(BF16)
    y_tok = _dot(p_back, y_hi) + _dot(p_back, y_lo)
    x2 = rows(x1_ref) + affine(y_tok, gt2_ref[...])
    y_ref[...] = (_unit_rms(x2) * gf_ref[...]).reshape(y_ref.shape)


def _mod_kernel(c_ref, w_ref, b_ref, win_ref, op_ref, os_ref, winb_ref):
    winb_ref[...] = win_ref[...].astype(BF16)
    c = c_ref[...]
    a = (c * jax.nn.sigmoid(c)).astype(BF16)
    mod = _dot(a, w_ref[...].astype(BF16)) + b_ref[...]
    n_p = op_ref.shape[0]
    op_ref[...] = mod[:n_p]
    os_ref[...] = mod[n_p:]


def _const_spec(shape):
    nd = len(shape)
    return pl.BlockSpec(shape, lambda *_: (0,) * nd, pipeline_mode=pl.Buffered(1))


def _modulation(c_all, n_p, w_ada, b_ada, w_in):
    n, d = c_all.shape
    n_out = w_ada.shape[1]
    steps = n_out // MOD_BLOCK
    cast_cols = w_in.shape[1] // steps
    assert n_p % SUBLANES == 0 and cast_cols * steps == w_in.shape[1] and cast_cols % LANES == 0
    cast_spec = pl.BlockSpec((w_in.shape[0], cast_cols), lambda j: (0, j))
    return pl.pallas_call(
        _mod_kernel,
        grid=(steps,),
        in_specs=[pl.BlockSpec((n, d), lambda j: (0, 0)),
                  pl.BlockSpec((d, MOD_BLOCK), lambda j: (0, j)),
                  pl.BlockSpec((1, MOD_BLOCK), lambda j: (0, j)),
                  cast_spec],
        out_specs=[pl.BlockSpec((n_p, MOD_BLOCK), lambda j: (0, j)),
                   pl.BlockSpec((n - n_p, MOD_BLOCK), lambda j: (0, j)),
                   cast_spec],
        out_shape=[jax.ShapeDtypeStruct((n_p, n_out), F32), jax.ShapeDtypeStruct((n - n_p, n_out), F32),
                   jax.ShapeDtypeStruct(w_in.shape, BF16)],
        compiler_params=pltpu.CompilerParams(dimension_semantics=("arbitrary",),
                                             vmem_limit_bytes=VMEM_LIMIT),
        name="modulation",
    )(c_all, w_ada, b_ada, w_in)


def _mixer_weight_specs(p):
    names = ("g1", "g2", "w_in", "caw", "woa", "cbw", "cbb", "lng", "lnb", "wob", "wo", "wr", "br")
    return [p[k] for k in names], [_const_spec(p[k].shape) for k in names]


def _mixer_prompt(x, mod, p, expert_weights):
    b, s, d = x.shape
    assert b == SUBLANES, "the time-major conv layout puts one sequence on each sublane"
    steps = SEQ_TILE // b
    tiles = s // steps
    weights, wspecs = _mixer_weight_specs(p)
    tile = lambda width: pl.BlockSpec((b, steps, width), lambda c: (0, c, 0))
    whole = lambda *shape: pl.BlockSpec(shape, lambda c: (0,) * len(shape))

    def cast_spec(w):
        experts, rows, cols = w.shape
        per_expert = tiles // experts
        assert per_expert * experts == tiles and rows % (per_expert * 2 * SUBLANES) == 0
        return pl.BlockSpec((1, rows // per_expert, cols), lambda c: (c // per_expert, lax.rem(c, per_expert), 0))

    cast_specs = [cast_spec(w) for w in expert_weights]
    return pl.pallas_call(
        _mixer_prompt_kernel,
        grid=(tiles,),
        in_specs=[tile(d), whole(b, N_MOD * d)] + wspecs + cast_specs,
        out_specs=[tile(d), tile(d), tile(ROUTER_LANES),
                   pl.BlockSpec((None, 1, LANES), lambda c: (c, 0, 0)),
                   whole(CONV_A_WIDTH - 1, b, D_A), whole(CONV_B_WIDTH - 1, b, D_B)] + cast_specs,
        out_shape=[jax.ShapeDtypeStruct((b, s, d), F32),
                   jax.ShapeDtypeStruct((b, s, d), BF16),
                   jax.ShapeDtypeStruct((b, s, ROUTER_LANES), F32),
                   jax.ShapeDtypeStruct((tiles, 1, LANES), jnp.int32),
                   jax.ShapeDtypeStruct((CONV_A_WIDTH - 1, b, D_A), F32),
                   jax.ShapeDtypeStruct((CONV_B_WIDTH - 1, b, D_B), F32)]
        + [jax.ShapeDtypeStruct(w.shape, BF16) for w in expert_weights],
        scratch_shapes=[pltpu.VMEM(((CONV_A_WIDTH - 1) * b + SEQ_TILE, D_A), F32),
                        pltpu.VMEM(((CONV_B_WIDTH - 1) * b + SEQ_TILE, D_B), F32),
                        pltpu.VMEM((SEQ_TILE, D_B), F32)],
        compiler_params=pltpu.CompilerParams(dimension_semantics=("arbitrary",),
                                             vmem_limit_bytes=VMEM_LIMIT),
        name="mixer_prompt",
    )(x, mod, *weights, *expert_weights)


def _mixer_sample(x_tm, mod, sa_tm, sb_tm, p):
    n, d = x_tm.shape
    weights, wspecs = _mixer_weight_specs(p)
    full = lambda a: pl.BlockSpec(a.shape, lambda i: (0,) * a.ndim)
    whole = lambda *shape: pl.BlockSpec(shape, lambda i: (0,) * len(shape))
    return pl.pallas_call(
        _mixer_sample_kernel,
        grid=(1,),
        in_specs=[full(x_tm), full(mod)] + wspecs + [full(sa_tm), full(sb_tm)],
        out_specs=[whole(n, d), whole(n, d), whole(n, ROUTER_LANES), whole(1, LANES),
                   full(sa_tm), full(sb_tm)],
        out_shape=[jax.ShapeDtypeStruct((n, d), F32),
                   jax.ShapeDtypeStruct((n, d), BF16),
                   jax.ShapeDtypeStruct((n, ROUTER_LANES), F32),
                   jax.ShapeDtypeStruct((1, LANES), jnp.int32),
                   jax.ShapeDtypeStruct(sa_tm.shape, F32),
                   jax.ShapeDtypeStruct(sb_tm.shape, F32)],
        scratch_shapes=[pltpu.VMEM((n, D_B), BF16)],
        compiler_params=pltpu.CompilerParams(dimension_semantics=("arbitrary",),
                                             vmem_limit_bytes=VMEM_LIMIT),
        name="mixer_sample",
    )(x_tm, mod, *weights, sa_tm, sb_tm)


def _moe(counts, x1, h2, route, gt2_src, gt2_spec, p, grid, tile, rows, affine):
    weights = [p["w1"], p["w3"], p["w2"], p["gf"]]
    grid_spec = pltpu.PrefetchScalarGridSpec(
        num_scalar_prefetch=1,
        grid=grid,
        in_specs=[tile(D_MODEL), tile(D_MODEL), tile(ROUTER_LANES), gt2_spec]
        + [_const_spec(w.shape) for w in weights],
        out_specs=tile(D_MODEL),
        scratch_shapes=[pltpu.VMEM((rows, D_MODEL), BF16),
                        pltpu.VMEM((rows, ROUTER_LANES), F32),
                        pltpu.VMEM((rows, D_MODEL), F32)],
    )
    return pl.pallas_call(
        functools.partial(_moe_kernel, affine=affine),
        grid_spec=grid_spec,
        out_shape=jax.ShapeDtypeStruct(x1.shape, F32),
        compiler_params=pltpu.CompilerParams(dimension_semantics=("arbitrary",) * len(grid),
                                             vmem_limit_bytes=VMEM_LIMIT),
        name="moe",
    )(counts, x1, h2, route, gt2_src, *weights)


def kernel(x_prompt, x_sample, c_prompt, c_sample, state_conv_a, state_conv_b, w_ada, b_ada, norm1_g,
           norm2_g, w_in, conv_a_w, w_out_a, conv_b_w, conv_b_bias, ln_b_g, ln_b_b, w_out_b, w_o,
           w_group, b_group, w_expert, b_expert, w1, w3, w2, final_norm_g):
    depth = w_ada.shape[0]
    assert depth == 1, "the MoE kernel fuses the final norm, so exactly one layer is supported"
    n_p, seq, d = x_prompt.shape
    n_s, steps, _ = x_sample.shape
    c_all = jnp.concatenate([c_prompt, c_sample], axis=0)
    xp = x_prompt
    xs = x_sample.transpose(1, 0, 2).reshape(steps * n_s, d)
    gf = final_norm_g.reshape(1, d)
    pa, pb, sa, sb = [], [], [], []
    for l in range(depth):
        pad = ROUTER_LANES - N_EXPERTS - N_GROUPS
        wr = jnp.pad(jnp.concatenate([w_expert[l], w_group[l]], axis=1).T, ((0, pad), (0, 0)))
        br = jnp.pad(jnp.concatenate([b_expert[l], b_group[l]]), (0, pad))
        br = jnp.broadcast_to(br[:, None], (ROUTER_LANES, LANES))
        p = {
            "g1": norm1_g[l].reshape(1, d), "g2": norm2_g[l].reshape(1, d),
            "woa": w_out_a[l].astype(BF16),
            "caw": jnp.broadcast_to(conv_a_w[l][:, None, :], (CONV_A_WIDTH, SUBLANES, D_A)),
            "cbw": jnp.broadcast_to(conv_b_w[l][:, None, :], (CONV_B_WIDTH, SUBLANES, D_B)),
            "cbb": conv_b_bias[l].reshape(1, D_B),
            "lng": ln_b_g[l].reshape(1, D_B), "lnb": ln_b_b[l].reshape(1, D_B),
            "wob": w_out_b[l].astype(BF16), "wo": w_o[l].astype(BF16), "wr": wr.astype(BF16), "br": br,
            "gf": gf,
        }
        mod_p, mod_s, p["w_in"] = _modulation(c_all, n_p, w_ada[l], b_ada[l].reshape(1, N_MOD * d), w_in[l])

        x1p, h2p, route_p, cnt_p, na_p, nb_p, w1b, w3b, w2b = _mixer_prompt(
            xp, mod_p, p, (w1[l], w3[l], w2[l]))
        p.update(w1=w1b, w3=w3b, w2=w2b.reshape(N_GROUPS, EXPERTS_PER_GROUP * D_EXPERT, d))
        tile_steps = SEQ_TILE // n_p
        xp = _moe(cnt_p[:, 0, :N_GROUPS].reshape(-1), x1p, h2p, route_p, mod_p,
                  pl.BlockSpec((n_p, d), lambda c, cnt: (0, N_MOD - 1)), p, (seq // tile_steps,),
                  lambda width: pl.BlockSpec((n_p, tile_steps, width), lambda c, cnt: (0, c, 0)), SEQ_TILE,
                  _per_sequence)

        sa_tm = state_conv_a[l].transpose(1, 0, 2)
        sb_tm = state_conv_b[l].transpose(1, 0, 2)
        x1s, h2s, route_s, cnt_s, na_s, nb_s = _mixer_sample(xs, mod_s, sa_tm, sb_tm, p)
        xs = _moe(cnt_s[0, :N_GROUPS], x1s, h2s, route_s, mod_s,
                  pl.BlockSpec((n_s, d), lambda i, c: (0, N_MOD - 1)), p, (1,),
                  lambda width: pl.BlockSpec((steps * n_s, width), lambda i, c: (0, 0)), steps * n_s, _affine)
        pa.append(na_p.transpose(1, 0, 2))
        pb.append(nb_p.transpose(1, 0, 2))
        sa.append(na_s.transpose(1, 0, 2))
        sb.append(nb_s.transpose(1, 0, 2))
    y_sample = xs.reshape(steps, n_s, d).transpose(1, 0, 2)
    return (xp, y_sample, jnp.stack(pa), jnp.stack(pb), jnp.stack(sa), jnp.stack(sb))
```

```python
import functools

import jax
import jax.numpy as jnp
from jax import lax
from jax.experimental import pallas as pl
from jax.experimental.pallas import tpu as pltpu

D_MODEL = 1024
D_A = 512
D_B = 512
CONV_A_WIDTH = 3
CONV_B_WIDTH = 31
N_GROUPS = 4
EXPERTS_PER_GROUP = 4
N_EXPERTS = 16
D_EXPERT = 256
N_MOD = 6
EPS = 1e-6

LANES = 128
SUBLANES = 8
ROUTER_LANES = LANES
SEQ_TILE = 512
CONV_ROWS = 64
CONV_LANES = 2 * LANES
SORT_SHIFT = 4
SORT_ALIGN = 1 << SORT_SHIFT
MOE_PASS_ROWS = (128, 144, 160, 176, 192, 224, 256)
VMEM_LIMIT = 52 * 1024 * 1024
MOD_BLOCK = 2 * D_MODEL

BF16 = jnp.bfloat16
F32 = jnp.float32

assert 2 * EXPERTS_PER_GROUP == SUBLANES and N_GROUPS <= SUBLANES and N_EXPERTS == N_GROUPS * EXPERTS_PER_GROUP


def _dot(a, b):
    return jnp.dot(a, b, preferred_element_type=F32)


def _unit_rms(x):
    return x * lax.rsqrt(jnp.mean(x * x, axis=-1, keepdims=True) + EPS)


def _sigmoid(v):
    return 0.5 * jnp.tanh(0.5 * v) + 0.5


def _affine(v, scale, shift=None):
    r = scale.shape[0]
    if r != 1 and r != v.shape[0]:
        v3 = v.reshape(v.shape[0] // r, r, v.shape[-1])
        shift3 = None if shift is None else shift[None]
        return _affine(v3, scale[None], shift3).reshape(v.shape)
    out = v * scale
    return out if shift is None else out + shift


def _per_sequence(v, scale, shift=None):
    n = scale.shape[0]
    v3 = v.reshape(n, v.shape[0] // n, v.shape[-1])
    out = v3 * scale[:, None, :]
    if shift is not None:
        out = out + shift[:, None, :]
    return out.reshape(v.shape)


def _tap(v, w8):
    return (v.reshape(v.shape[0] // SUBLANES, SUBLANES, v.shape[-1]) * w8[None]).reshape(v.shape)


def _route(logits_t):
    t = logits_t.shape[1]
    neg = -jnp.inf
    row = lax.broadcasted_iota(jnp.int32, (SUBLANES, t), 0).astype(F32)
    live = row < float(EXPERTS_PER_GROUP)
    gl = jnp.where(live, logits_t[N_EXPERTS:N_EXPERTS + SUBLANES, :], neg)
    gmax = jnp.max(gl, axis=0, keepdims=True)
    g_idx = jnp.min(jnp.where(gl == gmax, row, float(SUBLANES)), axis=0, keepdims=True)
    p_g = 1.0 / jnp.sum(jnp.exp(gl - gmax), axis=0, keepdims=True)
    el = None
    for g in reversed(range(N_GROUPS)):
        slab = logits_t[(g // 2) * SUBLANES:(g // 2 + 1) * SUBLANES, :]
        cand = slab if g % 2 == 0 else pltpu.roll(slab, SUBLANES - EXPERTS_PER_GROUP, axis=0)
        el = cand if el is None else jnp.where(g_idx == float(g), cand, el)
    el = jnp.where(live, el, neg)
    v1 = jnp.max(el, axis=0, keepdims=True)
    i1 = jnp.min(jnp.where(el == v1, row, float(SUBLANES)), axis=0, keepdims=True)
    el2 = jnp.where(row == i1, neg, el)
    v2 = jnp.max(el2, axis=0, keepdims=True)
    i2 = jnp.min(jnp.where(el2 == v2, row, float(SUBLANES)), axis=0, keepdims=True)
    e21 = jnp.exp(v2 - v1)
    wt1 = p_g / (1.0 + e21)
    wt2 = p_g * e21 / (1.0 + e21)
    comb = jnp.where(row == i1, wt1, jnp.where(row == i2, wt2, 0.0))
    onehot = jnp.where(row == g_idx, 1.0, 0.0)
    onehot_b = jnp.concatenate([onehot, jnp.zeros_like(onehot)], axis=0).astype(BF16)
    earlier = lax.broadcasted_iota(jnp.int32, (t, t), 0) < lax.broadcasted_iota(jnp.int32, (t, t), 1)
    before = _dot(onehot_b, jnp.where(earlier, 1.0, 0.0).astype(BF16))[0:SUBLANES, :]
    rank = jnp.sum(before * onehot, axis=0, keepdims=True)
    count = jnp.sum(onehot, axis=1, keepdims=True)
    row1 = lax.broadcasted_iota(jnp.int32, (SUBLANES, 1), 0)
    start = jnp.zeros((SUBLANES, 1), F32)
    for g in range(N_GROUPS - 1):
        start = start + jnp.where(row1 > g, count[g:g + 1, :], 0.0)
    dest = jnp.sum(onehot * start, axis=0, keepdims=True) + rank
    route_t = jnp.concatenate([jnp.where(live, comb, dest), jnp.broadcast_to(dest, (LANES - SUBLANES, t))],
                              axis=0)
    onehot_all = jnp.concatenate([onehot, jnp.zeros((LANES - SUBLANES, t), F32)], axis=0).astype(BF16)
    count_row = lax.dot_general(jnp.ones((2 * SUBLANES, t), BF16), onehot_all, (((1,), (1,)), ((), ())),
                                preferred_element_type=F32)[0:1, :]
    return jnp.transpose(route_t), count_row


def _ln_silu(v, bias, g, b):
    v = v + bias
    mu = jnp.mean(v, axis=-1, keepdims=True)
    d = v - mu
    var = jnp.mean(d * d, axis=-1, keepdims=True)
    y = d * lax.rsqrt(var + EPS) * g + b
    return y * _sigmoid(y)


def _mixer_tail(x, hb, mod, b_a_conv, yb_in, g2, w_in_ref, woa_ref, wob_ref, wo_ref, wr_ref, br_ref,
                x1_ref, h2_ref, route_ref, cnt_ref, affine):
    y_a = _dot(b_a_conv.astype(BF16), woa_ref[...])
    y_b = _dot(yb_in, wob_ref[...])
    o = 3 * D_A + 2 * D_B
    mg_a = _dot(hb, w_in_ref[:, o:o + D_MODEL])
    mg_b = _dot(hb, w_in_ref[:, o + D_MODEL:o + 2 * D_MODEL])
    merged = _sigmoid(mg_a) * y_a + _sigmoid(mg_b) * y_b
    gt1 = mod[:, 2 * D_MODEL:3 * D_MODEL]
    x1 = x + affine(_dot(merged.astype(BF16), wo_ref[...]), gt1)
    sh2 = mod[:, 3 * D_MODEL:4 * D_MODEL]
    sc2 = mod[:, 4 * D_MODEL:5 * D_MODEL]
    h2 = affine(_unit_rms(x1), g2 * (1.0 + sc2), sh2).astype(BF16)
    logits_t = lax.dot_general(wr_ref[...], h2, (((1,), (1,)), ((), ())), preferred_element_type=F32)
    logits_t = logits_t + jnp.tile(br_ref[...], (1, h2.shape[0] // LANES))
    x1_ref[...] = x1.reshape(x1_ref.shape)
    h2_ref[...] = h2.reshape(h2_ref.shape)
    route, count = _route(logits_t)
    route_ref[...] = route.reshape(route_ref.shape)
    cnt_ref[...] = count.astype(jnp.int32)


def _mixer_prompt_kernel(x_ref, mod_ref, g1_ref, g2_ref, w_in_ref, caw_ref, woa_ref, cbw_ref, cbb_ref,
                         lng_ref, lnb_ref, wob_ref, wo_ref, wr_ref, br_ref,
                         w1f_ref, w3f_ref, w2f_ref,
                         x1_ref, h2_ref, route_ref, cnt_ref, na_ref, nb_ref, w1b_ref, w3b_ref, w2b_ref,
                         hist_a, hist_b, conv_b_ref):
    n_seq, steps, d = x_ref.shape
    t = n_seq * steps
    rows_a = (CONV_A_WIDTH - 1) * n_seq
    rows_b = (CONV_B_WIDTH - 1) * n_seq
    w1b_ref[...] = w1f_ref[...].astype(BF16)
    w3b_ref[...] = w3f_ref[...].astype(BF16)
    w2b_ref[...] = w2f_ref[...].astype(BF16)

    @pl.when(pl.program_id(0) == 0)
    def _():
        hist_a[0:rows_a, :] = jnp.zeros((rows_a, D_A), F32)
        hist_b[0:rows_b, :] = jnp.zeros((rows_b, D_B), F32)

    def time_major(v):
        return jnp.swapaxes(v.reshape(n_seq, steps, v.shape[-1]), 0, 1).reshape(v.shape)

    def sequence_major(v):
        return jnp.swapaxes(v.reshape(steps, n_seq, v.shape[-1]), 0, 1).reshape(v.shape)

    x = x_ref[...].reshape(t, d)
    mod = mod_ref[...]
    h = _per_sequence(_unit_rms(x), g1_ref[...] * (1.0 + mod[:, D_MODEL:2 * D_MODEL]), mod[:, 0:D_MODEL])
    hb = h.astype(BF16)

    def proj(lo, width):
        return _dot(hb, w_in_ref[:, lo:lo + width])

    hist_a[rows_a:rows_a + t, :] = time_major(proj(D_A, D_A) * proj(2 * D_A, D_A))
    conv_a = None
    for k in range(CONV_A_WIDTH):
        term = _tap(hist_a[k * n_seq:k * n_seq + t, :], caw_ref[k])
        conv_a = term if conv_a is None else conv_a + term
    b_a_conv = proj(0, D_A) * sequence_major(conv_a)
    na_ref[...] = hist_a[t:t + rows_a, :].reshape(na_ref.shape)

    hist_b[rows_b:rows_b + t, :] = time_major(proj(3 * D_A, D_B) * _sigmoid(proj(3 * D_A + D_B, D_B)))
    nb_ref[...] = hist_b[t:t + rows_b, :].reshape(nb_ref.shape)
    for r0 in range(0, t, CONV_ROWS):
        for c0 in range(0, D_B, CONV_LANES):
            cols = slice(c0, c0 + CONV_LANES)
            acc = None
            for k in range(CONV_B_WIDTH):
                o = r0 + k * n_seq
                term = _tap(hist_b[o:o + CONV_ROWS, cols], cbw_ref[k][:, cols])
                acc = term if acc is None else acc + term
            conv_b_ref[r0:r0 + CONV_ROWS, cols] = acc
    yb_in = _ln_silu(sequence_major(conv_b_ref[...]), cbb_ref[...], lng_ref[...], lnb_ref[...]).astype(BF16)

    hist_a[0:rows_a, :] = hist_a[t:t + rows_a, :]
    hist_b[0:rows_b, :] = hist_b[t:t + rows_b, :]

    _mixer_tail(x, hb, mod, b_a_conv, yb_in, g2_ref[...], w_in_ref, woa_ref, wob_ref, wo_ref,
                wr_ref, br_ref, x1_ref, h2_ref, route_ref, cnt_ref, _per_sequence)


def _mixer_sample_kernel(x_ref, mod_ref, g1_ref, g2_ref, w_in_ref, caw_ref, woa_ref, cbw_ref, cbb_ref,
                         lng_ref, lnb_ref, wob_ref, wo_ref, wr_ref, br_ref, sa_ref, sb_ref,
                         x1_ref, h2_ref, route_ref, cnt_ref, na_ref, nb_ref, yb_in_ref):
    n_seq = mod_ref.shape[0]
    steps = x_ref.shape[0] // n_seq
    x = x_ref[...]
    mod = mod_ref[...]
    h = _affine(_unit_rms(x), g1_ref[...] * (1.0 + mod[:, D_MODEL:2 * D_MODEL]), mod[:, 0:D_MODEL])
    hb = h.astype(BF16)

    def proj(lo, width):
        return _dot(hb, w_in_ref[:, lo:lo + width])

    def slab(v, j):
        return v[j * n_seq:(j + 1) * n_seq, :]

    u_a = proj(D_A, D_A) * proj(2 * D_A, D_A)
    full_a = [sa_ref[j] for j in range(CONV_A_WIDTH - 1)] + [slab(u_a, j) for j in range(steps)]
    conv_a = []
    for s in range(steps):
        acc = jnp.zeros((n_seq, D_A), F32)
        for k in range(CONV_A_WIDTH):
            acc = acc + _tap(full_a[s + k], caw_ref[k])
        conv_a.append(acc)
    b_a_conv = proj(0, D_A) * jnp.concatenate(conv_a, axis=0)
    for j in range(CONV_A_WIDTH - 1):
        na_ref[j] = full_a[steps + j]

    u_b = proj(3 * D_A, D_B) * _sigmoid(proj(3 * D_A + D_B, D_B))
    hist = CONV_B_WIDTH - 1

    def full_b(j):
        return sb_ref[j] if j < hist else slab(u_b, j - hist)

    bias, lng, lnb = cbb_ref[...], lng_ref[...], lnb_ref[...]
    for s in range(steps):
        acc = jnp.zeros((n_seq, D_B), F32)
        for k in range(CONV_B_WIDTH):
            acc = acc + _tap(full_b(s + k), cbw_ref[k])
        yb_in_ref[s * n_seq:(s + 1) * n_seq, :] = _ln_silu(acc, bias, lng, lnb).astype(BF16)
    for j in range(hist):
        nb_ref[j] = full_b(steps + j)

    _mixer_tail(x, hb, mod, b_a_conv, yb_in_ref[...], g2_ref[...], w_in_ref, woa_ref, wob_ref, wo_ref,
                wr_ref, br_ref, x1_ref, h2_ref, route_ref, cnt_ref, _affine)


def _moe_kernel(cnt_ref, x1_ref, h2_ref, route_ref, gt2_ref, w1_ref, w3_ref, w2_ref, gf_ref, y_ref,
                h2s_ref, cs_ref, acc_ref, *, affine):
    def rows(ref):
        return ref[...].reshape(-1, ref.shape[-1])

    tp = h2s_ref.shape[0]
    tile = pl.program_id(0)
    route = rows(route_ref)
    t = route.shape[0]
    lane_i = lax.broadcasted_iota(jnp.int32, route.shape, 1)
    dest_row = jnp.transpose(route)[EXPERTS_PER_GROUP:EXPERTS_PER_GROUP + 1, :]
    slot = lax.broadcasted_iota(jnp.int32, (tp, t), 0).astype(F32)
    p_sort = jnp.where(slot == dest_row, 1.0, 0.0).astype(BF16)
    h2s_ref[...] = _dot(p_sort, rows(h2_ref)).astype(BF16)
    comb = jnp.where(lane_i < EXPERTS_PER_GROUP, route, 0.0)
    c_hi = comb.astype(BF16).astype(F32)
    c_mid = (comb - c_hi).astype(BF16).astype(F32)
    c_lo = comb - c_hi - c_mid
    packed = (c_hi + pltpu.roll(c_mid, EXPERTS_PER_GROUP, axis=1)
              + pltpu.roll(c_lo, 2 * EXPERTS_PER_GROUP, axis=1)).astype(BF16)
    moved = _dot(p_sort, packed)
    cs_ref[...] = (moved + pltpu.roll(moved, LANES - EXPERTS_PER_GROUP, axis=1)
                   + pltpu.roll(moved, LANES - 2 * EXPERTS_PER_GROUP, axis=1))
    acc_ref[...] = jnp.zeros(acc_ref.shape, F32)

    def expert_pass(rows, g, window, lo, end):
        off = pl.multiple_of(jnp.minimum(window, tp - rows), SORT_ALIGN)
        row_id = off + lax.broadcasted_iota(jnp.int32, (rows, 1), 0)
        valid = (row_id >= jnp.maximum(lo, window)) & (row_id < end)
        hb = h2s_ref[pl.ds(off, rows), :]
        c4 = jnp.where(valid, cs_ref[pl.ds(off, rows), :], 0.0)
        experts = [g * EXPERTS_PER_GROUP + e for e in range(EXPERTS_PER_GROUP)]
        gate = jnp.concatenate([_dot(hb, w1_ref[e]) for e in experts], axis=1)
        up = jnp.concatenate([_dot(hb, w3_ref[e]) for e in experts], axis=1)
        cexp = jnp.concatenate(
            [jnp.broadcast_to(c4[:, e:e + 1], (rows, D_EXPERT)) for e in range(EXPERTS_PER_GROUP)], axis=1)
        a = gate * _sigmoid(gate) * up * cexp
        acc_ref[pl.ds(off, rows), :] += _dot(a.astype(BF16), w2_ref[g])

    def group(g, start):
        end = start + cnt_ref[tile * N_GROUPS + g]
        base = lax.shift_right_logical(start, SORT_SHIFT) * SORT_ALIGN
        span = end - base

        def passes(rows):
            def run():
                def body(p, carry):
                    expert_pass(rows, g, base + p * rows, start, end)
                    return carry
                lax.fori_loop(0, jnp.where(end > start, lax.div(span + (rows - 1), rows), 0), body, 0)
            return run

        size_class = sum((span > rows).astype(jnp.int32) for rows in MOE_PASS_ROWS[:-1])
        lax.switch(size_class, [passes(rows) for rows in MOE_PASS_ROWS])
        return end

    lax.fori_loop(0, N_GROUPS, group, jnp.int32(0))

    dest_col = route[:, EXPERTS_PER_GROUP:EXPERTS_PER_GROUP + 1]
    p_back = jnp.where(lax.broadcasted_iota(jnp.int32, (t, tp), 1).astype(F32) == dest_col,
                       1.0, 0.0).astype(BF16)
    y = acc_ref[...]
    y_hi = y.astype(BF16)
    y_lo = (y - y_hi.astype(F32)).astype(BF16)
    y_tok = _dot(p_back, y_hi) + _dot(p_back, y_lo)
    x2 = rows(x1_ref) + affine(y_tok, gt2_ref[...])
    y_ref[...] = (_unit_rms(x2) * gf_ref[...]).reshape(y_ref.shape)


def _mod_kernel(c_ref, w_ref, b_ref, win_ref, op_ref, os_ref, winb_ref):
    winb_ref[...] = win_ref[...].astype(BF16)
    c = c_ref[...]
    a = (c * _sigmoid(c)).astype(BF16)
    mod = _dot(a, w_ref[...].astype(BF16)) + b_ref[...]
    n_p = op_ref.shape[0]
    op_ref[...] = mod[:n_p]
    os_ref[...] = mod[n_p:]


def _const_spec(shape):
    nd = len(shape)
    return pl.BlockSpec(shape, lambda *_: (0,) * nd, pipeline_mode=pl.Buffered(1))


def _modulation(c_all, n_p, w_ada, b_ada, w_in):
    n, d = c_all.shape
    n_out = w_ada.shape[1]
    steps = n_out // MOD_BLOCK
    cast_cols = w_in.shape[1] // steps
    assert n_p % SUBLANES == 0 and cast_cols * steps == w_in.shape[1] and cast_cols % LANES == 0
    cast_spec = pl.BlockSpec((w_in.shape[0], cast_cols), lambda j: (0, j))
    return pl.pallas_call(
        _mod_kernel,
        grid=(steps,),
        in_specs=[pl.BlockSpec((n, d), lambda j: (0, 0)),
                  pl.BlockSpec((d, MOD_BLOCK), lambda j: (0, j)),
                  pl.BlockSpec((1, MOD_BLOCK), lambda j: (0, j)),
                  cast_spec],
        out_specs=[pl.BlockSpec((n_p, MOD_BLOCK), lambda j: (0, j)),
                   pl.BlockSpec((n - n_p, MOD_BLOCK), lambda j: (0, j)),
                   cast_spec],
        out_shape=[jax.ShapeDtypeStruct((n_p, n_out), F32), jax.ShapeDtypeStruct((n - n_p, n_out), F32),
                   jax.ShapeDtypeStruct(w_in.shape, BF16)],
        compiler_params=pltpu.CompilerParams(dimension_semantics=("arbitrary",),
                                             vmem_limit_bytes=VMEM_LIMIT),
        name="modulation",
    )(c_all, w_ada, b_ada, w_in)


def _mixer_weight_specs(p):
    names = ("g1", "g2", "w_in", "caw", "woa", "cbw", "cbb", "lng", "lnb", "wob", "wo", "wr", "br")
    return [p[k] for k in names], [_const_spec(p[k].shape) for k in names]


def _mixer_prompt(x, mod, p, expert_weights):
    b, s, d = x.shape
    assert b == SUBLANES, "the time-major conv layout puts one sequence on each sublane"
    steps = SEQ_TILE // b
    tiles = s // steps
    weights, wspecs = _mixer_weight_specs(p)
    tile = lambda width: pl.BlockSpec((b, steps, width), lambda c: (0, c, 0))
    whole = lambda *shape: pl.BlockSpec(shape, lambda c: (0,) * len(shape))

    def cast_spec(w):
        experts, rows, cols = w.shape
        per_expert = tiles // experts
        assert per_expert * experts == tiles and rows % (per_expert * 2 * SUBLANES) == 0
        return pl.BlockSpec((1, rows // per_expert, cols), lambda c: (c // per_expert, lax.rem(c, per_expert), 0))

    cast_specs = [cast_spec(w) for w in expert_weights]
    return pl.pallas_call(
        _mixer_prompt_kernel,
        grid=(tiles,),
        in_specs=[tile(d), whole(b, N_MOD * d)] + wspecs + cast_specs,
        out_specs=[tile(d), tile(d), tile(ROUTER_LANES),
                   pl.BlockSpec((None, 1, LANES), lambda c: (c, 0, 0)),
                   whole(CONV_A_WIDTH - 1, b, D_A), whole(CONV_B_WIDTH - 1, b, D_B)] + cast_specs,
        out_shape=[jax.ShapeDtypeStruct((b, s, d), F32),
                   jax.ShapeDtypeStruct((b, s, d), BF16),
                   jax.ShapeDtypeStruct((b, s, ROUTER_LANES), F32),
                   jax.ShapeDtypeStruct((tiles, 1, LANES), jnp.int32),
                   jax.ShapeDtypeStruct((CONV_A_WIDTH - 1, b, D_A), F32),
                   jax.ShapeDtypeStruct((CONV_B_WIDTH - 1, b, D_B), F32)]
        + [jax.ShapeDtypeStruct(w.shape, BF16) for w in expert_weights],
        scratch_shapes=[pltpu.VMEM(((CONV_A_WIDTH - 1) * b + SEQ_TILE, D_A), F32),
                        pltpu.VMEM(((CONV_B_WIDTH - 1) * b + SEQ_TILE, D_B), F32),
                        pltpu.VMEM((SEQ_TILE, D_B), F32)],
        compiler_params=pltpu.CompilerParams(dimension_semantics=("arbitrary",),
                                             vmem_limit_bytes=VMEM_LIMIT),
        name="mixer_prompt",
    )(x, mod, *weights, *expert_weights)


def _mixer_sample(x_tm, mod, sa_tm, sb_tm, p):
    n, d = x_tm.shape
    weights, wspecs = _mixer_weight_specs(p)
    full = lambda a: pl.BlockSpec(a.shape, lambda i: (0,) * a.ndim)
    whole = lambda *shape: pl.BlockSpec(shape, lambda i: (0,) * len(shape))
    return pl.pallas_call(
        _mixer_sample_kernel,
        grid=(1,),
        in_specs=[full(x_tm), full(mod)] + wspecs + [full(sa_tm), full(sb_tm)],
        out_specs=[whole(n, d), whole(n, d), whole(n, ROUTER_LANES), whole(1, LANES),
                   full(sa_tm), full(sb_tm)],
        out_shape=[jax.ShapeDtypeStruct((n, d), F32),
                   jax.ShapeDtypeStruct((n, d), BF16),
                   jax.ShapeDtypeStruct((n, ROUTER_LANES), F32),
                   jax.ShapeDtypeStruct((1, LANES), jnp.int32),
                   jax.ShapeDtypeStruct(sa_tm.shape, F32),
                   jax.ShapeDtypeStruct(sb_tm.shape, F32)],
        scratch_shapes=[pltpu.VMEM((n, D_B), BF16)],
        compiler_params=pltpu.CompilerParams(dimension_semantics=("arbitrary",),
                                             vmem_limit_bytes=VMEM_LIMIT),
        name="mixer_sample",
    )(x_tm, mod, *weights, sa_tm, sb_tm)


def _moe(counts, x1, h2, route, gt2_src, gt2_spec, p, grid, tile, rows, affine):
    weights = [p["w1"], p["w3"], p["w2"], p["gf"]]
    grid_spec = pltpu.PrefetchScalarGridSpec(
        num_scalar_prefetch=1,
        grid=grid,
        in_specs=[tile(D_MODEL), tile(D_MODEL), tile(ROUTER_LANES), gt2_spec]
        + [_const_spec(w.shape) for w in weights],
        out_specs=tile(D_MODEL),
        scratch_shapes=[pltpu.VMEM((rows, D_MODEL), BF16),
                        pltpu.VMEM((rows, ROUTER_LANES), F32),
                        pltpu.VMEM((rows, D_MODEL), F32)],
    )
    return pl.pallas_call(
        functools.partial(_moe_kernel, affine=affine),
        grid_spec=grid_spec,
        out_shape=jax.ShapeDtypeStruct(x1.shape, F32),
        compiler_params=pltpu.CompilerParams(dimension_semantics=("arbitrary",) * len(grid),
                                             vmem_limit_bytes=VMEM_LIMIT),
        name="moe",
    )(counts, x1, h2, route, gt2_src, *weights)


def kernel(x_prompt, x_sample, c_prompt, c_sample, state_conv_a, state_conv_b, w_ada, b_ada, norm1_g,
           norm2_g, w_in, conv_a_w, w_out_a, conv_b_w, conv_b_bias, ln_b_g, ln_b_b, w_out_b, w_o,
           w_group, b_group, w_expert, b_expert, w1, w3, w2, final_norm_g):
    depth = w_ada.shape[0]
    assert depth == 1, "the MoE kernel fuses the final norm, so exactly one layer is supported"
    n_p, seq, d = x_prompt.shape
    n_s, steps, _ = x_sample.shape
    c_all = jnp.concatenate([c_prompt, c_sample], axis=0)
    xp = x_prompt
    xs = x_sample.transpose(1, 0, 2).reshape(steps * n_s, d)
    gf = final_norm_g.reshape(1, d)
    pa, pb, sa, sb = [], [], [], []
    for l in range(depth):
        pad = ROUTER_LANES - N_EXPERTS - N_GROUPS
        wr = jnp.pad(jnp.concatenate([w_expert[l], w_group[l]], axis=1).T, ((0, pad), (0, 0)))
        br = jnp.pad(jnp.concatenate([b_expert[l], b_group[l]]), (0, pad))
        br = jnp.broadcast_to(br[:, None], (ROUTER_LANES, LANES))
        p = {
            "g1": norm1_g[l].reshape(1, d), "g2": norm2_g[l].reshape(1, d),
            "woa": w_out_a[l].astype(BF16),
            "caw": jnp.broadcast_to(conv_a_w[l][:, None, :], (CONV_A_WIDTH, SUBLANES, D_A)),
            "cbw": jnp.broadcast_to(conv_b_w[l][:, None, :], (CONV_B_WIDTH, SUBLANES, D_B)),
            "cbb": conv_b_bias[l].reshape(1, D_B),
            "lng": ln_b_g[l].reshape(1, D_B), "lnb": ln_b_b[l].reshape(1, D_B),
            "wob": w_out_b[l].astype(BF16), "wo": w_o[l].astype(BF16), "wr": wr.astype(BF16), "br": br,
            "gf": gf,
        }
        mod_p, mod_s, p["w_in"] = _modulation(c_all, n_p, w_ada[l], b_ada[l].reshape(1, N_MOD * d), w_in[l])

        x1p, h2p, route_p, cnt_p, na_p, nb_p, w1b, w3b, w2b = _mixer_prompt(
            xp, mod_p, p, (w1[l], w3[l], w2[l]))
        p.update(w1=w1b, w3=w3b, w2=w2b.reshape(N_GROUPS, EXPERTS_PER_GROUP * D_EXPERT, d))
        tile_steps = SEQ_TILE // n_p
        xp = _moe(cnt_p[:, 0, :N_GROUPS].reshape(-1), x1p, h2p, route_p, mod_p,
                  pl.BlockSpec((n_p, d), lambda c, cnt: (0, N_MOD - 1)), p, (seq // tile_steps,),
                  lambda width: pl.BlockSpec((n_p, tile_steps, width), lambda c, cnt: (0, c, 0)), SEQ_TILE,
                  _per_sequence)

        sa_tm = state_conv_a[l].transpose(1, 0, 2)
        sb_tm = state_conv_b[l].transpose(1, 0, 2)
        x1s, h2s, route_s, cnt_s, na_s, nb_s = _mixer_sample(xs, mod_s, sa_tm, sb_tm, p)
        xs = _moe(cnt_s[0, :N_GROUPS], x1s, h2s, route_s, mod_s,
                  pl.BlockSpec((n_s, d), lambda i, c: (0, N_MOD - 1)), p, (1,),
                  lambda width: pl.BlockSpec((steps * n_s, width), lambda i, c: (0, 0)), steps * n_s, _affine)
        pa.append(na_p.transpose(1, 0, 2))
        pb.append(nb_p.transpose(1, 0, 2))
        sa.append(na_s.transpose(1, 0, 2))
        sb.append(nb_s.transpose(1, 0, 2))
    y_sample = xs.reshape(steps, n_s, d).transpose(1, 0, 2)
    return (xp, y_sample, jnp.stack(pa), jnp.stack(pb), jnp.stack(sa), jnp.stack(sb))
```

```python
import jax
import jax.numpy as jnp
from jax import lax
from jax.experimental import pallas as pl
from jax.experimental.pallas import tpu as pltpu

D_MODEL = 1024
D_A = 512
D_B = 512
CONV_A_WIDTH = 3
CONV_B_WIDTH = 31
N_GROUPS = 4
EXPERTS_PER_GROUP = 4
N_EXPERTS = 16
D_EXPERT = 256
N_MOD = 6
EPS = 1e-6

LANES = 128
SUBLANES = 8
ROUTER_LANES = LANES
SEQ_TILE = 512
CONV_ROWS = 64
CONV_LANES = 2 * LANES
SORT_SHIFT = 4
SORT_ALIGN = 1 << SORT_SHIFT
MOE_PASS_ROWS = (128, 144, 160, 176, 192, 224, 256)
VMEM_LIMIT = 52 * 1024 * 1024
MOD_BLOCK = 2 * D_MODEL

BF16 = jnp.bfloat16
F32 = jnp.float32

assert 2 * EXPERTS_PER_GROUP == SUBLANES and N_GROUPS <= SUBLANES and N_EXPERTS == N_GROUPS * EXPERTS_PER_GROUP


def _dot(a, b):
    return jnp.dot(a, b, preferred_element_type=F32)


def _unit_rms(x):
    return x * lax.rsqrt(jnp.mean(x * x, axis=-1, keepdims=True) + EPS)


def _sigmoid(v):
    return 0.5 * jnp.tanh(0.5 * v) + 0.5


def _affine(v, scale, shift=None):
    r = scale.shape[0]
    if r != 1 and r != v.shape[0]:
        v3 = v.reshape(v.shape[0] // r, r, v.shape[-1])
        shift3 = None if shift is None else shift[None]
        return _affine(v3, scale[None], shift3).reshape(v.shape)
    out = v * scale
    return out if shift is None else out + shift


def _per_sequence(v, scale, shift=None):
    n = scale.shape[0]
    v3 = v.reshape(n, v.shape[0] // n, v.shape[-1])
    out = v3 * scale[:, None, :]
    if shift is not None:
        out = out + shift[:, None, :]
    return out.reshape(v.shape)


def _tap(v, w8):
    return (v.reshape(v.shape[0] // SUBLANES, SUBLANES, v.shape[-1]) * w8[None]).reshape(v.shape)


def _route(logits_t):
    t = logits_t.shape[1]
    neg = -jnp.inf
    row = lax.broadcasted_iota(jnp.int32, (SUBLANES, t), 0).astype(F32)
    live = row < float(EXPERTS_PER_GROUP)
    gl = jnp.where(live, logits_t[N_EXPERTS:N_EXPERTS + SUBLANES, :], neg)
    gmax = jnp.max(gl, axis=0, keepdims=True)
    g_idx = jnp.min(jnp.where(gl == gmax, row, float(SUBLANES)), axis=0, keepdims=True)
    p_g = 1.0 / jnp.sum(jnp.exp(gl - gmax), axis=0, keepdims=True)
    el = None
    for g in reversed(range(N_GROUPS)):
        slab = logits_t[(g // 2) * SUBLANES:(g // 2 + 1) * SUBLANES, :]
        cand = slab if g % 2 == 0 else pltpu.roll(slab, SUBLANES - EXPERTS_PER_GROUP, axis=0)
        el = cand if el is None else jnp.where(g_idx == float(g), cand, el)
    el = jnp.where(live, el, neg)
    v1 = jnp.max(el, axis=0, keepdims=True)
    i1 = jnp.min(jnp.where(el == v1, row, float(SUBLANES)), axis=0, keepdims=True)
    el2 = jnp.where(row == i1, neg, el)
    v2 = jnp.max(el2, axis=0, keepdims=True)
    i2 = jnp.min(jnp.where(el2 == v2, row, float(SUBLANES)), axis=0, keepdims=True)
    e21 = jnp.exp(v2 - v1)
    wt1 = p_g / (1.0 + e21)
    wt2 = p_g * e21 / (1.0 + e21)
    comb = jnp.where(row == i1, wt1, jnp.where(row == i2, wt2, 0.0))
    onehot = jnp.where(row == g_idx, 1.0, 0.0)
    onehot_b = jnp.concatenate([onehot, jnp.zeros_like(onehot)], axis=0).astype(BF16)
    earlier = lax.broadcasted_iota(jnp.int32, (t, t), 0) < lax.broadcasted_iota(jnp.int32, (t, t), 1)
    before = _dot(onehot_b, jnp.where(earlier, 1.0, 0.0).astype(BF16))[0:SUBLANES, :]
    rank = jnp.sum(before * onehot, axis=0, keepdims=True)
    count = jnp.sum(onehot, axis=1, keepdims=True)
    row1 = lax.broadcasted_iota(jnp.int32, (SUBLANES, 1), 0)
    start = jnp.zeros((SUBLANES, 1), F32)
    for g in range(N_GROUPS - 1):
        start = start + jnp.where(row1 > g, count[g:g + 1, :], 0.0)
    dest = jnp.sum(onehot * start, axis=0, keepdims=True) + rank
    route_t = jnp.concatenate([jnp.where(live, comb, dest), jnp.broadcast_to(dest, (LANES - SUBLANES, t))],
                              axis=0)
    onehot_all = jnp.concatenate([onehot, jnp.zeros((LANES - SUBLANES, t), F32)], axis=0).astype(BF16)
    count_row = lax.dot_general(jnp.ones((2 * SUBLANES, t), BF16), onehot_all, (((1,), (1,)), ((), ())),
                                preferred_element_type=F32)[0:1, :]
    return jnp.transpose(route_t), count_row


def _ln_silu(v, bias, g, b):
    v = v + bias
    mu = jnp.mean(v, axis=-1, keepdims=True)
    d = v - mu
    var = jnp.mean(d * d, axis=-1, keepdims=True)
    y = d * lax.rsqrt(var + EPS) * g + b
    return y * _sigmoid(y)


def _mixer_tail(x, hb, mod, b_a_conv, yb_in, g2, w_in_ref, woa_ref, wob_ref, wo_ref, wr_ref, br_ref,
                x1_ref, h2_ref, route_ref, cnt_ref, affine):
    y_a = _dot(b_a_conv.astype(BF16), woa_ref[...])
    y_b = _dot(yb_in, wob_ref[...])
    o = 3 * D_A + 2 * D_B
    mg_a = _dot(hb, w_in_ref[:, o:o + D_MODEL])
    mg_b = _dot(hb, w_in_ref[:, o + D_MODEL:o + 2 * D_MODEL])
    merged = _sigmoid(mg_a) * y_a + _sigmoid(mg_b) * y_b
    gt1 = mod[:, 2 * D_MODEL:3 * D_MODEL]
    x1 = x + affine(_dot(merged.astype(BF16), wo_ref[...]), gt1)
    sh2 = mod[:, 3 * D_MODEL:4 * D_MODEL]
    sc2 = mod[:, 4 * D_MODEL:5 * D_MODEL]
    h2 = affine(_unit_rms(x1), g2 * (1.0 + sc2), sh2).astype(BF16)
    logits_t = lax.dot_general(wr_ref[...], h2, (((1,), (1,)), ((), ())), preferred_element_type=F32)
    logits_t = logits_t + jnp.tile(br_ref[...], (1, h2.shape[0] // LANES))
    x1_ref[...] = x1.reshape(x1_ref.shape)
    h2_ref[...] = h2.reshape(h2_ref.shape)
    route, count = _route(logits_t)
    route_ref[...] = route.reshape(route_ref.shape)
    cnt_ref[...] = count.astype(jnp.int32)


def _mixer_prompt_kernel(x_ref, mod_ref, g1_ref, g2_ref, w_in_ref, caw_ref, woa_ref, cbw_ref, cbb_ref,
                         lng_ref, lnb_ref, wob_ref, wo_ref, wr_ref, br_ref,
                         w1f_ref, w3f_ref, w2f_ref,
                         x1_ref, h2_ref, route_ref, cnt_ref, na_ref, nb_ref, w1b_ref, w3b_ref, w2b_ref,
                         hist_a, hist_b, conv_b_ref):
    n_seq, steps, d = x_ref.shape
    t = n_seq * steps
    rows_a = (CONV_A_WIDTH - 1) * n_seq
    rows_b = (CONV_B_WIDTH - 1) * n_seq
    w1b_ref[...] = w1f_ref[...].astype(BF16)
    w3b_ref[...] = w3f_ref[...].astype(BF16)
    w2b_ref[...] = w2f_ref[...].astype(BF16)

    @pl.when(pl.program_id(0) == 0)
    def _():
        hist_a[0:rows_a, :] = jnp.zeros((rows_a, D_A), F32)
        hist_b[0:rows_b, :] = jnp.zeros((rows_b, D_B), F32)

    def time_major(v):
        return jnp.swapaxes(v.reshape(n_seq, steps, v.shape[-1]), 0, 1).reshape(v.shape)

    def sequence_major(v):
        return jnp.swapaxes(v.reshape(steps, n_seq, v.shape[-1]), 0, 1).reshape(v.shape)

    x = x_ref[...].reshape(t, d)
    mod = mod_ref[...]
    h = _per_sequence(_unit_rms(x), g1_ref[...] * (1.0 + mod[:, D_MODEL:2 * D_MODEL]), mod[:, 0:D_MODEL])
    hb = h.astype(BF16)

    def proj(lo, width):
        return _dot(hb, w_in_ref[:, lo:lo + width])

    hist_a[rows_a:rows_a + t, :] = time_major(proj(D_A, D_A) * proj(2 * D_A, D_A))
    conv_a = None
    for k in range(CONV_A_WIDTH):
        term = _tap(hist_a[k * n_seq:k * n_seq + t, :], caw_ref[k])
        conv_a = term if conv_a is None else conv_a + term
    b_a_conv = proj(0, D_A) * sequence_major(conv_a)
    na_ref[...] = hist_a[t:t + rows_a, :].reshape(na_ref.shape)

    hist_b[rows_b:rows_b + t, :] = time_major(proj(3 * D_A, D_B) * _sigmoid(proj(3 * D_A + D_B, D_B)))
    nb_ref[...] = hist_b[t:t + rows_b, :].reshape(nb_ref.shape)
    for r0 in range(0, t, CONV_ROWS):
        for c0 in range(0, D_B, CONV_LANES):
            cols = slice(c0, c0 + CONV_LANES)
            acc = None
            for k in range(CONV_B_WIDTH):
                o = r0 + k * n_seq
                term = _tap(hist_b[o:o + CONV_ROWS, cols], cbw_ref[k][:, cols])
                acc = term if acc is None else acc + term
            conv_b_ref[r0:r0 + CONV_ROWS, cols] = acc
    yb_in = _ln_silu(sequence_major(conv_b_ref[...]), cbb_ref[...], lng_ref[...], lnb_ref[...]).astype(BF16)

    hist_a[0:rows_a, :] = hist_a[t:t + rows_a, :]
    hist_b[0:rows_b, :] = hist_b[t:t + rows_b, :]

    _mixer_tail(x, hb, mod, b_a_conv, yb_in, g2_ref[...], w_in_ref, woa_ref, wob_ref, wo_ref,
                wr_ref, br_ref, x1_ref, h2_ref, route_ref, cnt_ref, _per_sequence)


def _mixer_sample_kernel(x_ref, mod_ref, g1_ref, g2_ref, w_in_ref, caw_ref, woa_ref, cbw_ref, cbb_ref,
                         lng_ref, lnb_ref, wob_ref, wo_ref, wr_ref, br_ref, sa_ref, sb_ref,
                         x1_ref, h2_ref, route_ref, cnt_ref, na_ref, nb_ref, yb_in_ref):
    n_seq = mod_ref.shape[0]
    steps = x_ref.shape[0] // n_seq
    x = x_ref[...]
    mod = mod_ref[...]
    h = _affine(_unit_rms(x), g1_ref[...] * (1.0 + mod[:, D_MODEL:2 * D_MODEL]), mod[:, 0:D_MODEL])
    hb = h.astype(BF16)

    def proj(lo, width):
        return _dot(hb, w_in_ref[:, lo:lo + width])

    def slab(v, j):
        return v[j * n_seq:(j + 1) * n_seq, :]

    u_a = proj(D_A, D_A) * proj(2 * D_A, D_A)
    full_a = [sa_ref[j] for j in range(CONV_A_WIDTH - 1)] + [slab(u_a, j) for j in range(steps)]
    conv_a = []
    for s in range(steps):
        acc = jnp.zeros((n_seq, D_A), F32)
        for k in range(CONV_A_WIDTH):
            acc = acc + _tap(full_a[s + k], caw_ref[k])
        conv_a.append(acc)
    b_a_conv = proj(0, D_A) * jnp.concatenate(conv_a, axis=0)
    for j in range(CONV_A_WIDTH - 1):
        na_ref[j] = full_a[steps + j]

    u_b = proj(3 * D_A, D_B) * _sigmoid(proj(3 * D_A + D_B, D_B))
    hist = CONV_B_WIDTH - 1

    def full_b(j):
        return sb_ref[j] if j < hist else slab(u_b, j - hist)

    bias, lng, lnb = cbb_ref[...], lng_ref[...], lnb_ref[...]
    for s in range(steps):
        acc = jnp.zeros((n_seq, D_B), F32)
        for k in range(CONV_B_WIDTH):
            acc = acc + _tap(full_b(s + k), cbw_ref[k])
        yb_in_ref[s * n_seq:(s + 1) * n_seq, :] = _ln_silu(acc, bias, lng, lnb).astype(BF16)
    for j in range(hist):
        nb_ref[j] = full_b(steps + j)

    _mixer_tail(x, hb, mod, b_a_conv, yb_in_ref[...], g2_ref[...], w_in_ref, woa_ref, wob_ref, wo_ref,
                wr_ref, br_ref, x1_ref, h2_ref, route_ref, cnt_ref, _affine)


def _moe_kernel(cnt_ref, x1p_ref, h2p_ref, routep_ref, gt2p_ref, x1s_ref, h2s_ref, routes_ref, gt2s_ref,
                w1_ref, w3_ref, w2_ref, gf_ref, yp_ref, ys_ref, h2sort_ref, cs_ref, acc_ref):
    shared = (w1_ref, w3_ref, w2_ref, gf_ref)
    scratch = (h2sort_ref, cs_ref, acc_ref)
    last = pl.num_programs(0) - 1

    @pl.when(pl.program_id(0) < last)
    def _():
        _moe_tile(cnt_ref, x1p_ref, h2p_ref, routep_ref, gt2p_ref, *shared, yp_ref, *scratch, affine=_per_sequence)

    @pl.when(pl.program_id(0) == last)
    def _():
        _moe_tile(cnt_ref, x1s_ref, h2s_ref, routes_ref, gt2s_ref, *shared, ys_ref, *scratch, affine=_affine)


def _moe_tile(cnt_ref, x1_ref, h2_ref, route_ref, gt2_ref, w1_ref, w3_ref, w2_ref, gf_ref, y_ref,
              h2s_ref, cs_ref, acc_ref, *, affine):
    def rows(ref):
        return ref[...].reshape(-1, ref.shape[-1])

    tp = h2s_ref.shape[0]
    tile = pl.program_id(0)
    route = rows(route_ref)
    t = route.shape[0]
    lane_i = lax.broadcasted_iota(jnp.int32, route.shape, 1)
    dest_row = jnp.transpose(route)[EXPERTS_PER_GROUP:EXPERTS_PER_GROUP + 1, :]
    slot = lax.broadcasted_iota(jnp.int32, (tp, t), 0).astype(F32)
    p_sort = jnp.where(slot == dest_row, 1.0, 0.0).astype(BF16)
    h2s_ref[...] = _dot(p_sort, rows(h2_ref)).astype(BF16)
    comb = jnp.where(lane_i < EXPERTS_PER_GROUP, route, 0.0)
    c_hi = comb.astype(BF16).astype(F32)
    c_mid = (comb - c_hi).astype(BF16).astype(F32)
    c_lo = comb - c_hi - c_mid
    packed = (c_hi + pltpu.roll(c_mid, EXPERTS_PER_GROUP, axis=1)
              + pltpu.roll(c_lo, 2 * EXPERTS_PER_GROUP, axis=1)).astype(BF16)
    moved = _dot(p_sort, packed)
    cs_ref[...] = (moved + pltpu.roll(moved, LANES - EXPERTS_PER_GROUP, axis=1)
                   + pltpu.roll(moved, LANES - 2 * EXPERTS_PER_GROUP, axis=1))
    acc_ref[...] = jnp.zeros(acc_ref.shape, F32)

    def expert_pass(rows, g, window, lo, end):
        off = pl.multiple_of(jnp.minimum(window, tp - rows), SORT_ALIGN)
        row_id = off + lax.broadcasted_iota(jnp.int32, (rows, 1), 0)
        valid = (row_id >= jnp.maximum(lo, window)) & (row_id < end)
        hb = h2s_ref[pl.ds(off, rows), :]
        c4 = jnp.where(valid, cs_ref[pl.ds(off, rows), :], 0.0)
        experts = [g * EXPERTS_PER_GROUP + e for e in range(EXPERTS_PER_GROUP)]
        gate = jnp.concatenate([_dot(hb, w1_ref[e]) for e in experts], axis=1)
        up = jnp.concatenate([_dot(hb, w3_ref[e]) for e in experts], axis=1)
        cexp = jnp.concatenate(
            [jnp.broadcast_to(c4[:, e:e + 1], (rows, D_EXPERT)) for e in range(EXPERTS_PER_GROUP)], axis=1)
        a = gate * _sigmoid(gate) * up * cexp
        acc_ref[pl.ds(off, rows), :] += _dot(a.astype(BF16), w2_ref[g])

    def group(g, start):
        end = start + cnt_ref[tile * N_GROUPS + g]
        base = lax.shift_right_logical(start, SORT_SHIFT) * SORT_ALIGN
        span = end - base

        def passes(rows):
            def run():
                def body(p, carry):
                    expert_pass(rows, g, base + p * rows, start, end)
                    return carry
                lax.fori_loop(0, jnp.where(end > start, lax.div(span + (rows - 1), rows), 0), body, 0)
            return run

        size_class = sum((span > rows).astype(jnp.int32) for rows in MOE_PASS_ROWS[:-1])
        lax.switch(size_class, [passes(rows) for rows in MOE_PASS_ROWS])
        return end

    lax.fori_loop(0, N_GROUPS, group, jnp.int32(0))

    dest_col = route[:, EXPERTS_PER_GROUP:EXPERTS_PER_GROUP + 1]
    p_back = jnp.where(lax.broadcasted_iota(jnp.int32, (t, tp), 1).astype(F32) == dest_col,
                       1.0, 0.0).astype(BF16)
    y = acc_ref[...]
    y_hi = y.astype(BF16)
    y_lo = (y - y_hi.astype(F32)).astype(BF16)
    y_tok = _dot(p_back, y_hi) + _dot(p_back, y_lo)
    x2 = rows(x1_ref) + affine(y_tok, gt2_ref[...])
    y_ref[...] = (_unit_rms(x2) * gf_ref[...]).reshape(y_ref.shape)


def _mod_kernel(c_ref, w_ref, b_ref, win_ref, op_ref, os_ref, winb_ref):
    winb_ref[...] = win_ref[...].astype(BF16)
    c = c_ref[...]
    a = (c * _sigmoid(c)).astype(BF16)
    mod = _dot(a, w_ref[...].astype(BF16)) + b_ref[...]
    n_p = op_ref.shape[0]
    op_ref[...] = mod[:n_p]
    os_ref[...] = mod[n_p:]


def _const_spec(shape):
    nd = len(shape)
    return pl.BlockSpec(shape, lambda *_: (0,) * nd, pipeline_mode=pl.Buffered(1))


def _modulation(c_all, n_p, w_ada, b_ada, w_in):
    n, d = c_all.shape
    n_out = w_ada.shape[1]
    steps = n_out // MOD_BLOCK
    cast_cols = w_in.shape[1] // steps
    assert n_p % SUBLANES == 0 and cast_cols * steps == w_in.shape[1] and cast_cols % LANES == 0
    cast_spec = pl.BlockSpec((w_in.shape[0], cast_cols), lambda j: (0, j))
    return pl.pallas_call(
        _mod_kernel,
        grid=(steps,),
        in_specs=[pl.BlockSpec((n, d), lambda j: (0, 0)),
                  pl.BlockSpec((d, MOD_BLOCK), lambda j: (0, j)),
                  pl.BlockSpec((1, MOD_BLOCK), lambda j: (0, j)),
                  cast_spec],
        out_specs=[pl.BlockSpec((n_p, MOD_BLOCK), lambda j: (0, j)),
                   pl.BlockSpec((n - n_p, MOD_BLOCK), lambda j: (0, j)),
                   cast_spec],
        out_shape=[jax.ShapeDtypeStruct((n_p, n_out), F32), jax.ShapeDtypeStruct((n - n_p, n_out), F32),
                   jax.ShapeDtypeStruct(w_in.shape, BF16)],
        compiler_params=pltpu.CompilerParams(dimension_semantics=("arbitrary",),
                                             vmem_limit_bytes=VMEM_LIMIT),
        name="modulation",
    )(c_all, w_ada, b_ada, w_in)


def _mixer_weight_specs(p):
    names = ("g1", "g2", "w_in", "caw", "woa", "cbw", "cbb", "lng", "lnb", "wob", "wo", "wr", "br")
    return [p[k] for k in names], [_const_spec(p[k].shape) for k in names]


def _mixer_prompt(x, mod, p, expert_weights):
    b, s, d = x.shape
    assert b == SUBLANES, "the time-major conv layout puts one sequence on each sublane"
    steps = SEQ_TILE // b
    tiles = s // steps
    weights, wspecs = _mixer_weight_specs(p)
    tile = lambda width: pl.BlockSpec((b, steps, width), lambda c: (0, c, 0))
    whole = lambda *shape: pl.BlockSpec(shape, lambda c: (0,) * len(shape))

    def cast_spec(w):
        experts, rows, cols = w.shape
        per_expert = tiles // experts
        assert per_expert * experts == tiles and rows % (per_expert * 2 * SUBLANES) == 0
        return pl.BlockSpec((1, rows // per_expert, cols), lambda c: (c // per_expert, lax.rem(c, per_expert), 0))

    cast_specs = [cast_spec(w) for w in expert_weights]
    return pl.pallas_call(
        _mixer_prompt_kernel,
        grid=(tiles,),
        in_specs=[tile(d), whole(b, N_MOD * d)] + wspecs + cast_specs,
        out_specs=[tile(d), tile(d), tile(ROUTER_LANES),
                   pl.BlockSpec((None, 1, LANES), lambda c: (c, 0, 0)),
                   whole(CONV_A_WIDTH - 1, b, D_A), whole(CONV_B_WIDTH - 1, b, D_B)] + cast_specs,
        out_shape=[jax.ShapeDtypeStruct((b, s, d), F32),
                   jax.ShapeDtypeStruct((b, s, d), BF16),
                   jax.ShapeDtypeStruct((b, s, ROUTER_LANES), F32),
                   jax.ShapeDtypeStruct((tiles, 1, LANES), jnp.int32),
                   jax.ShapeDtypeStruct((CONV_A_WIDTH - 1, b, D_A), F32),
                   jax.ShapeDtypeStruct((CONV_B_WIDTH - 1, b, D_B), F32)]
        + [jax.ShapeDtypeStruct(w.shape, BF16) for w in expert_weights],
        scratch_shapes=[pltpu.VMEM(((CONV_A_WIDTH - 1) * b + SEQ_TILE, D_A), F32),
                        pltpu.VMEM(((CONV_B_WIDTH - 1) * b + SEQ_TILE, D_B), F32),
                        pltpu.VMEM((SEQ_TILE, D_B), F32)],
        compiler_params=pltpu.CompilerParams(dimension_semantics=("arbitrary",),
                                             vmem_limit_bytes=VMEM_LIMIT),
        name="mixer_prompt",
    )(x, mod, *weights, *expert_weights)


def _mixer_sample(x_tm, mod, sa_tm, sb_tm, p):
    n, d = x_tm.shape
    weights, wspecs = _mixer_weight_specs(p)
    full = lambda a: pl.BlockSpec(a.shape, lambda i: (0,) * a.ndim)
    whole = lambda *shape: pl.BlockSpec(shape, lambda i: (0,) * len(shape))
    return pl.pallas_call(
        _mixer_sample_kernel,
        grid=(1,),
        in_specs=[full(x_tm), full(mod)] + wspecs + [full(sa_tm), full(sb_tm)],
        out_specs=[whole(n, d), whole(n, d), whole(n, ROUTER_LANES), whole(1, LANES),
                   full(sa_tm), full(sb_tm)],
        out_shape=[jax.ShapeDtypeStruct((n, d), F32),
                   jax.ShapeDtypeStruct((n, d), BF16),
                   jax.ShapeDtypeStruct((n, ROUTER_LANES), F32),
                   jax.ShapeDtypeStruct((1, LANES), jnp.int32),
                   jax.ShapeDtypeStruct(sa_tm.shape, F32),
                   jax.ShapeDtypeStruct(sb_tm.shape, F32)],
        scratch_shapes=[pltpu.VMEM((n, D_B), BF16)],
        compiler_params=pltpu.CompilerParams(dimension_semantics=("arbitrary",),
                                             vmem_limit_bytes=VMEM_LIMIT),
        name="mixer_sample",
    )(x_tm, mod, *weights, sa_tm, sb_tm)


def _moe(counts, prompt, mod_p, sample, mod_s, p):
    b, s, d = prompt[0].shape
    n_s = mod_s.shape[0]
    steps = SEQ_TILE // b
    tiles = s // steps
    assert sample[0].shape == (SEQ_TILE, d) and SEQ_TILE % n_s == 0
    weights = [p["w1"], p["w3"], p["w2"], p["gf"]]
    tile = lambda width: pl.BlockSpec((b, steps, width), lambda c, cnt: (0, jnp.minimum(c, tiles - 1), 0))
    whole = lambda rows, width: pl.BlockSpec((rows, width), lambda c, cnt: (0, 0), pipeline_mode=pl.Buffered(1))
    gate = lambda rows: pl.BlockSpec((rows, d), lambda c, cnt: (0, N_MOD - 1), pipeline_mode=pl.Buffered(1))
    grid_spec = pltpu.PrefetchScalarGridSpec(
        num_scalar_prefetch=1,
        grid=(tiles + 1,),
        in_specs=[tile(d), tile(d), tile(ROUTER_LANES), gate(b),
                  whole(SEQ_TILE, d), whole(SEQ_TILE, d), whole(SEQ_TILE, ROUTER_LANES), gate(n_s)]
        + [_const_spec(w.shape) for w in weights],
        out_specs=[tile(d), pl.BlockSpec((SEQ_TILE, d), lambda c, cnt: (0, 0))],
        scratch_shapes=[pltpu.VMEM((SEQ_TILE, d), BF16),
                        pltpu.VMEM((SEQ_TILE, ROUTER_LANES), F32),
                        pltpu.VMEM((SEQ_TILE, d), F32)],
    )
    return pl.pallas_call(
        _moe_kernel,
        grid_spec=grid_spec,
        out_shape=[jax.ShapeDtypeStruct((b, s, d), F32), jax.ShapeDtypeStruct((SEQ_TILE, d), F32)],
        compiler_params=pltpu.CompilerParams(dimension_semantics=("arbitrary",),
                                             vmem_limit_bytes=VMEM_LIMIT),
        name="moe",
    )(counts, *prompt, mod_p, *sample, mod_s, *weights)


def kernel(x_prompt, x_sample, c_prompt, c_sample, state_conv_a, state_conv_b, w_ada, b_ada, norm1_g,
           norm2_g, w_in, conv_a_w, w_out_a, conv_b_w, conv_b_bias, ln_b_g, ln_b_b, w_out_b, w_o,
           w_group, b_group, w_expert, b_expert, w1, w3, w2, final_norm_g):
    depth = w_ada.shape[0]
    assert depth == 1, "the MoE kernel fuses the final norm, so exactly one layer is supported"
    n_p, _, d = x_prompt.shape
    n_s, steps, _ = x_sample.shape
    c_all = jnp.concatenate([c_prompt, c_sample], axis=0)
    xp = x_prompt
    xs = x_sample.transpose(1, 0, 2).reshape(steps * n_s, d)
    gf = final_norm_g.reshape(1, d)
    pa, pb, sa, sb = [], [], [], []
    for l in range(depth):
        pad = ROUTER_LANES - N_EXPERTS - N_GROUPS
        wr = jnp.pad(jnp.concatenate([w_expert[l], w_group[l]], axis=1).T, ((0, pad), (0, 0)))
        br = jnp.pad(jnp.concatenate([b_expert[l], b_group[l]]), (0, pad))
        br = jnp.broadcast_to(br[:, None], (ROUTER_LANES, LANES))
        p = {
            "g1": norm1_g[l].reshape(1, d), "g2": norm2_g[l].reshape(1, d),
            "woa": w_out_a[l].astype(BF16),
            "caw": jnp.broadcast_to(conv_a_w[l][:, None, :], (CONV_A_WIDTH, SUBLANES, D_A)),
            "cbw": jnp.broadcast_to(conv_b_w[l][:, None, :], (CONV_B_WIDTH, SUBLANES, D_B)),
            "cbb": conv_b_bias[l].reshape(1, D_B),
            "lng": ln_b_g[l].reshape(1, D_B), "lnb": ln_b_b[l].reshape(1, D_B),
            "wob": w_out_b[l].astype(BF16), "wo": w_o[l].astype(BF16), "wr": wr.astype(BF16), "br": br,
            "gf": gf,
        }
        mod_p, mod_s, p["w_in"] = _modulation(c_all, n_p, w_ada[l], b_ada[l].reshape(1, N_MOD * d), w_in[l])

        x1p, h2p, route_p, cnt_p, na_p, nb_p, w1b, w3b, w2b = _mixer_prompt(
            xp, mod_p, p, (w1[l], w3[l], w2[l]))
        p.update(w1=w1b, w3=w3b, w2=w2b.reshape(N_GROUPS, EXPERTS_PER_GROUP * D_EXPERT, d))

        sa_tm = state_conv_a[l].transpose(1, 0, 2)
        sb_tm = state_conv_b[l].transpose(1, 0, 2)
        x1s, h2s, route_s, cnt_s, na_s, nb_s = _mixer_sample(xs, mod_s, sa_tm, sb_tm, p)
        counts = jnp.concatenate([cnt_p[:, 0, :N_GROUPS].reshape(-1), cnt_s[0, :N_GROUPS]])
        xp, xs = _moe(counts, (x1p, h2p, route_p), mod_p, (x1s, h2s, route_s), mod_s, p)
        pa.append(na_p.transpose(1, 0, 2))
        pb.append(nb_p.transpose(1, 0, 2))
        sa.append(na_s.transpose(1, 0, 2))
        sb.append(nb_s.transpose(1, 0, 2))
    y_sample = xs.reshape(steps, n_s, d).transpose(1, 0, 2)
    return (xp, y_sample, jnp.stack(pa), jnp.stack(pb), jnp.stack(sa), jnp.stack(sb))
```

```python
import jax
import jax.numpy as jnp
from jax import lax
from jax.experimental import pallas as pl
from jax.experimental.pallas import tpu as pltpu

D_MODEL = 1024
D_A = 512
D_B = 512
CONV_A_WIDTH = 3
CONV_B_WIDTH = 31
N_GROUPS = 4
EXPERTS_PER_GROUP = 4
N_EXPERTS = 16
D_EXPERT = 256
N_MOD = 6
EPS = 1e-6

LANES = 128
SUBLANES = 8
ROUTER_LANES = LANES
SEQ_TILE = 512
CONV_ROWS = 64
CONV_LANES = 2 * LANES
SORT_SHIFT = 4
SORT_ALIGN = 1 << SORT_SHIFT
MOE_PASS_ROWS = (128, 144, 160, 176, 192, 224, 256)
VMEM_LIMIT = 52 * 1024 * 1024
MOD_BLOCK = D_MODEL // 2

BF16 = jnp.bfloat16
F32 = jnp.float32

assert 2 * EXPERTS_PER_GROUP == SUBLANES and N_GROUPS <= SUBLANES and N_EXPERTS == N_GROUPS * EXPERTS_PER_GROUP


def _dot(a, b):
    return jnp.dot(a, b, preferred_element_type=F32)


def _unit_rms(x):
    return x * lax.rsqrt(jnp.mean(x * x, axis=-1, keepdims=True) + EPS)


def _sigmoid(v):
    return 0.5 * jnp.tanh(0.5 * v) + 0.5


def _affine(v, scale, shift=None):
    r = scale.shape[0]
    if r != 1 and r != v.shape[0]:
        v3 = v.reshape(v.shape[0] // r, r, v.shape[-1])
        shift3 = None if shift is None else shift[None]
        return _affine(v3, scale[None], shift3).reshape(v.shape)
    out = v * scale
    return out if shift is None else out + shift


def _per_sequence(v, scale, shift=None):
    n = scale.shape[0]
    v3 = v.reshape(n, v.shape[0] // n, v.shape[-1])
    out = v3 * scale[:, None, :]
    if shift is not None:
        out = out + shift[:, None, :]
    return out.reshape(v.shape)


def _tap(v, w8):
    return (v.reshape(v.shape[0] // SUBLANES, SUBLANES, v.shape[-1]) * w8[None]).reshape(v.shape)


def _route(logits_t):
    t = logits_t.shape[1]
    neg = -jnp.inf
    row = lax.broadcasted_iota(jnp.int32, (SUBLANES, t), 0).astype(F32)
    live = row < float(EXPERTS_PER_GROUP)
    gl = jnp.where(live, logits_t[N_EXPERTS:N_EXPERTS + SUBLANES, :], neg)
    gmax = jnp.max(gl, axis=0, keepdims=True)
    g_idx = jnp.min(jnp.where(gl == gmax, row, float(SUBLANES)), axis=0, keepdims=True)
    p_g = 1.0 / jnp.sum(jnp.exp(gl - gmax), axis=0, keepdims=True)
    el = None
    for g in reversed(range(N_GROUPS)):
        slab = logits_t[(g // 2) * SUBLANES:(g // 2 + 1) * SUBLANES, :]
        cand = slab if g % 2 == 0 else pltpu.roll(slab, SUBLANES - EXPERTS_PER_GROUP, axis=0)
        el = cand if el is None else jnp.where(g_idx == float(g), cand, el)
    el = jnp.where(live, el, neg)
    v1 = jnp.max(el, axis=0, keepdims=True)
    i1 = jnp.min(jnp.where(el == v1, row, float(SUBLANES)), axis=0, keepdims=True)
    el2 = jnp.where(row == i1, neg, el)
    v2 = jnp.max(el2, axis=0, keepdims=True)
    i2 = jnp.min(jnp.where(el2 == v2, row, float(SUBLANES)), axis=0, keepdims=True)
    e21 = jnp.exp(v2 - v1)
    wt1 = p_g / (1.0 + e21)
    wt2 = p_g * e21 / (1.0 + e21)
    comb = jnp.where(row == i1, wt1, jnp.where(row == i2, wt2, 0.0))
    onehot = jnp.where(row == g_idx, 1.0, 0.0)
    onehot_b = jnp.concatenate([onehot, jnp.zeros_like(onehot)], axis=0).astype(BF16)
    earlier = lax.broadcasted_iota(jnp.int32, (t, t), 0) < lax.broadcasted_iota(jnp.int32, (t, t), 1)
    before = _dot(onehot_b, jnp.where(earlier, 1.0, 0.0).astype(BF16))[0:SUBLANES, :]
    rank = jnp.sum(before * onehot, axis=0, keepdims=True)
    count = jnp.sum(onehot, axis=1, keepdims=True)
    row1 = lax.broadcasted_iota(jnp.int32, (SUBLANES, 1), 0)
    start = jnp.zeros((SUBLANES, 1), F32)
    for g in range(N_GROUPS - 1):
        start = start + jnp.where(row1 > g, count[g:g + 1, :], 0.0)
    dest = jnp.sum(onehot * start, axis=0, keepdims=True) + rank
    route_t = jnp.concatenate([jnp.where(live, comb, dest), jnp.broadcast_to(dest, (LANES - SUBLANES, t))],
                              axis=0)
    onehot_all = jnp.concatenate([onehot, jnp.zeros((LANES - SUBLANES, t), F32)], axis=0).astype(BF16)
    count_row = lax.dot_general(jnp.ones((2 * SUBLANES, t), BF16), onehot_all, (((1,), (1,)), ((), ())),
                                preferred_element_type=F32)[0:1, :]
    return jnp.transpose(route_t), count_row


def _ln_silu(v, bias, g, b):
    v = v + bias
    mu = jnp.mean(v, axis=-1, keepdims=True)
    d = v - mu
    var = jnp.mean(d * d, axis=-1, keepdims=True)
    y = d * lax.rsqrt(var + EPS) * g + b
    return y * _sigmoid(y)


def _mixer_tail(x, hb, mod, b_a_conv, yb_in, g2, w_in_ref, woa_ref, wob_ref, wo_ref, wr_ref, br_ref,
                x1_ref, h2_ref, route_ref, cnt_ref, affine):
    y_a = _dot(b_a_conv.astype(BF16), woa_ref[...])
    y_b = _dot(yb_in, wob_ref[...])
    o = 3 * D_A + 2 * D_B
    mg_a = _dot(hb, w_in_ref[:, o:o + D_MODEL])
    mg_b = _dot(hb, w_in_ref[:, o + D_MODEL:o + 2 * D_MODEL])
    merged = _sigmoid(mg_a) * y_a + _sigmoid(mg_b) * y_b
    gt1 = mod[:, 2 * D_MODEL:3 * D_MODEL]
    x1 = x + affine(_dot(merged.astype(BF16), wo_ref[...]), gt1)
    sh2 = mod[:, 3 * D_MODEL:4 * D_MODEL]
    sc2 = mod[:, 4 * D_MODEL:5 * D_MODEL]
    h2 = affine(_unit_rms(x1), g2 * (1.0 + sc2), sh2).astype(BF16)
    logits_t = lax.dot_general(wr_ref[...], h2, (((1,), (1,)), ((), ())), preferred_element_type=F32)
    logits_t = logits_t + jnp.tile(br_ref[...], (1, h2.shape[0] // LANES))
    x1_ref[...] = x1.reshape(x1_ref.shape)
    h2_ref[...] = h2.reshape(h2_ref.shape)
    route, count = _route(logits_t)
    route_ref[...] = route.reshape(route_ref.shape)
    cnt_ref[...] = count.astype(jnp.int32)


def _mixer_prompt_kernel(x_ref, mod_ref, g1_ref, g2_ref, w_in_ref, caw_ref, woa_ref, cbw_ref, cbb_ref,
                         lng_ref, lnb_ref, wob_ref, wo_ref, wr_ref, br_ref,
                         w1f_ref, w3f_ref, w2f_ref,
                         x1_ref, h2_ref, route_ref, cnt_ref, na_ref, nb_ref, w1b_ref, w3b_ref, w2b_ref,
                         hist_a, hist_b, conv_b_ref):
    n_seq, steps, d = x_ref.shape
    t = n_seq * steps
    rows_a = (CONV_A_WIDTH - 1) * n_seq
    rows_b = (CONV_B_WIDTH - 1) * n_seq
    w1b_ref[...] = w1f_ref[...].astype(BF16)
    w3b_ref[...] = w3f_ref[...].astype(BF16)
    w2b_ref[...] = w2f_ref[...].astype(BF16)

    @pl.when(pl.program_id(0) == 0)
    def _():
        hist_a[0:rows_a, :] = jnp.zeros((rows_a, D_A), F32)
        hist_b[0:rows_b, :] = jnp.zeros((rows_b, D_B), F32)

    def time_major(v):
        return jnp.swapaxes(v.reshape(n_seq, steps, v.shape[-1]), 0, 1).reshape(v.shape)

    def sequence_major(v):
        return jnp.swapaxes(v.reshape(steps, n_seq, v.shape[-1]), 0, 1).reshape(v.shape)

    x = x_ref[...].reshape(t, d)
    mod = mod_ref[...]
    h = _per_sequence(_unit_rms(x), g1_ref[...] * (1.0 + mod[:, D_MODEL:2 * D_MODEL]), mod[:, 0:D_MODEL])
    hb = h.astype(BF16)

    def proj(lo, width):
        return _dot(hb, w_in_ref[:, lo:lo + width])

    hist_a[rows_a:rows_a + t, :] = time_major(proj(D_A, D_A) * proj(2 * D_A, D_A))
    conv_a = None
    for k in range(CONV_A_WIDTH):
        term = _tap(hist_a[k * n_seq:k * n_seq + t, :], caw_ref[k])
        conv_a = term if conv_a is None else conv_a + term
    b_a_conv = proj(0, D_A) * sequence_major(conv_a)
    na_ref[...] = hist_a[t:t + rows_a, :].reshape(na_ref.shape)

    hist_b[rows_b:rows_b + t, :] = time_major(proj(3 * D_A, D_B) * _sigmoid(proj(3 * D_A + D_B, D_B)))
    nb_ref[...] = hist_b[t:t + rows_b, :].reshape(nb_ref.shape)
    for r0 in range(0, t, CONV_ROWS):
        for c0 in range(0, D_B, CONV_LANES):
            cols = slice(c0, c0 + CONV_LANES)
            acc = None
            for k in range(CONV_B_WIDTH):
                o = r0 + k * n_seq
                term = _tap(hist_b[o:o + CONV_ROWS, cols], cbw_ref[k][:, cols])
                acc = term if acc is None else acc + term
            conv_b_ref[r0:r0 + CONV_ROWS, cols] = acc
    yb_in = _ln_silu(sequence_major(conv_b_ref[...]), cbb_ref[...], lng_ref[...], lnb_ref[...]).astype(BF16)

    hist_a[0:rows_a, :] = hist_a[t:t + rows_a, :]
    hist_b[0:rows_b, :] = hist_b[t:t + rows_b, :]

    _mixer_tail(x, hb, mod, b_a_conv, yb_in, g2_ref[...], w_in_ref, woa_ref, wob_ref, wo_ref,
                wr_ref, br_ref, x1_ref, h2_ref, route_ref, cnt_ref, _per_sequence)


def _mixer_sample_kernel(x_ref, mod_ref, g1_ref, g2_ref, w_in_ref, caw_ref, woa_ref, cbw_ref, cbb_ref,
                         lng_ref, lnb_ref, wob_ref, wo_ref, wr_ref, br_ref, sa_ref, sb_ref,
                         x1_ref, h2_ref, route_ref, cnt_ref, na_ref, nb_ref, yb_in_ref):
    n_seq = mod_ref.shape[0]
    steps = x_ref.shape[0] // n_seq
    x = x_ref[...]
    mod = mod_ref[...]
    h = _affine(_unit_rms(x), g1_ref[...] * (1.0 + mod[:, D_MODEL:2 * D_MODEL]), mod[:, 0:D_MODEL])
    hb = h.astype(BF16)

    def proj(lo, width):
        return _dot(hb, w_in_ref[:, lo:lo + width])

    def slab(v, j):
        return v[j * n_seq:(j + 1) * n_seq, :]

    u_a = proj(D_A, D_A) * proj(2 * D_A, D_A)
    full_a = [sa_ref[j] for j in range(CONV_A_WIDTH - 1)] + [slab(u_a, j) for j in range(steps)]
    conv_a = []
    for s in range(steps):
        acc = jnp.zeros((n_seq, D_A), F32)
        for k in range(CONV_A_WIDTH):
            acc = acc + _tap(full_a[s + k], caw_ref[k])
        conv_a.append(acc)
    b_a_conv = proj(0, D_A) * jnp.concatenate(conv_a, axis=0)
    for j in range(CONV_A_WIDTH - 1):
        na_ref[j] = full_a[steps + j]

    u_b = proj(3 * D_A, D_B) * _sigmoid(proj(3 * D_A + D_B, D_B))
    hist = CONV_B_WIDTH - 1

    def full_b(j):
        return sb_ref[j] if j < hist else slab(u_b, j - hist)

    bias, lng, lnb = cbb_ref[...], lng_ref[...], lnb_ref[...]
    for s in range(steps):
        acc = jnp.zeros((n_seq, D_B), F32)
        for k in range(CONV_B_WIDTH):
            acc = acc + _tap(full_b(s + k), cbw_ref[k])
        yb_in_ref[s * n_seq:(s + 1) * n_seq, :] = _ln_silu(acc, bias, lng, lnb).astype(BF16)
    for j in range(hist):
        nb_ref[j] = full_b(steps + j)

    _mixer_tail(x, hb, mod, b_a_conv, yb_in_ref[...], g2_ref[...], w_in_ref, woa_ref, wob_ref, wo_ref,
                wr_ref, br_ref, x1_ref, h2_ref, route_ref, cnt_ref, _affine)


def _moe_kernel(cnt_ref, x1p_ref, h2p_ref, routep_ref, gt2p_ref, x1s_ref, h2s_ref, routes_ref, gt2s_ref,
                w1_ref, w3_ref, w2_ref, gf_ref, yp_ref, ys_ref, h2sort_ref, cs_ref, acc_ref):
    shared = (w1_ref, w3_ref, w2_ref, gf_ref)
    scratch = (h2sort_ref, cs_ref, acc_ref)
    last = pl.num_programs(0) - 1

    @pl.when(pl.program_id(0) < last)
    def _():
        _moe_tile(cnt_ref, x1p_ref, h2p_ref, routep_ref, gt2p_ref, *shared, yp_ref, *scratch, affine=_per_sequence)

    @pl.when(pl.program_id(0) == last)
    def _():
        _moe_tile(cnt_ref, x1s_ref, h2s_ref, routes_ref, gt2s_ref, *shared, ys_ref, *scratch, affine=_affine)


def _moe_tile(cnt_ref, x1_ref, h2_ref, route_ref, gt2_ref, w1_ref, w3_ref, w2_ref, gf_ref, y_ref,
              h2s_ref, cs_ref, acc_ref, *, affine):
    def rows(ref):
        return ref[...].reshape(-1, ref.shape[-1])

    tp = h2s_ref.shape[0]
    tile = pl.program_id(0)
    route = rows(route_ref)
    t = route.shape[0]
    lane_i = lax.broadcasted_iota(jnp.int32, route.shape, 1)
    dest_row = jnp.transpose(route)[EXPERTS_PER_GROUP:EXPERTS_PER_GROUP + 1, :]
    slot = lax.broadcasted_iota(jnp.int32, (tp, t), 0).astype(F32)
    p_sort = jnp.where(slot == dest_row, 1.0, 0.0).astype(BF16)
    h2s_ref[...] = _dot(p_sort, rows(h2_ref)).astype(BF16)
    comb = jnp.where(lane_i < EXPERTS_PER_GROUP, route, 0.0)
    c_hi = comb.astype(BF16).astype(F32)
    c_mid = (comb - c_hi).astype(BF16).astype(F32)
    c_lo = comb - c_hi - c_mid
    packed = (c_hi + pltpu.roll(c_mid, EXPERTS_PER_GROUP, axis=1)
              + pltpu.roll(c_lo, 2 * EXPERTS_PER_GROUP, axis=1)).astype(BF16)
    moved = _dot(p_sort, packed)
    cs_ref[...] = (moved + pltpu.roll(moved, LANES - EXPERTS_PER_GROUP, axis=1)
                   + pltpu.roll(moved, LANES - 2 * EXPERTS_PER_GROUP, axis=1))
    acc_ref[...] = jnp.zeros(acc_ref.shape, F32)

    def expert_pass(rows, g, window, lo, end):
        off = pl.multiple_of(jnp.minimum(window, tp - rows), SORT_ALIGN)
        row_id = off + lax.broadcasted_iota(jnp.int32, (rows, 1), 0)
        valid = (row_id >= jnp.maximum(lo, window)) & (row_id < end)
        hb = h2s_ref[pl.ds(off, rows), :]
        c4 = jnp.where(valid, cs_ref[pl.ds(off, rows), :], 0.0)
        experts = [g * EXPERTS_PER_GROUP + e for e in range(EXPERTS_PER_GROUP)]
        gate = jnp.concatenate([_dot(hb, w1_ref[e]) for e in experts], axis=1)
        up = jnp.concatenate([_dot(hb, w3_ref[e]) for e in experts], axis=1)
        cexp = jnp.concatenate(
            [jnp.broadcast_to(c4[:, e:e + 1], (rows, D_EXPERT)) for e in range(EXPERTS_PER_GROUP)], axis=1)
        a = gate * _sigmoid(gate) * up * cexp
        acc_ref[pl.ds(off, rows), :] += _dot(a.astype(BF16), w2_ref[g])

    def group(g, start):
        end = start + cnt_ref[tile * N_GROUPS + g]
        base = lax.shift_right_logical(start, SORT_SHIFT) * SORT_ALIGN
        span = end - base

        def passes(rows):
            def run():
                def body(p, carry):
                    expert_pass(rows, g, base + p * rows, start, end)
                    return carry
                lax.fori_loop(0, jnp.where(end > start, lax.div(span + (rows - 1), rows), 0), body, 0)
            return run

        size_class = sum((span > rows).astype(jnp.int32) for rows in MOE_PASS_ROWS[:-1])
        lax.switch(size_class, [passes(rows) for rows in MOE_PASS_ROWS])
        return end

    lax.fori_loop(0, N_GROUPS, group, jnp.int32(0))

    dest_col = route[:, EXPERTS_PER_GROUP:EXPERTS_PER_GROUP + 1]
    p_back = jnp.where(lax.broadcasted_iota(jnp.int32, (t, tp), 1).astype(F32) == dest_col,
                       1.0, 0.0).astype(BF16)
    y = acc_ref[...]
    y_hi = y.astype(BF16)
    y_lo = (y - y_hi.astype(F32)).astype(BF16)
    y_tok = _dot(p_back, y_hi) + _dot(p_back, y_lo)
    x2 = rows(x1_ref) + affine(y_tok, gt2_ref[...])
    y_ref[...] = (_unit_rms(x2) * gf_ref[...]).reshape(y_ref.shape)


def _mod_kernel(c_ref, w_ref, b_ref, win_ref, op_ref, os_ref, winb_ref):
    winb_ref[...] = win_ref[...].astype(BF16)
    c = c_ref[...]
    a = (c * _sigmoid(c)).astype(BF16)
    mod = _dot(a, w_ref[...].astype(BF16)) + b_ref[...]
    n_p = op_ref.shape[0]
    op_ref[...] = mod[:n_p]
    os_ref[...] = mod[n_p:]


def _const_spec(shape):
    nd = len(shape)
    return pl.BlockSpec(shape, lambda *_: (0,) * nd, pipeline_mode=pl.Buffered(1))


def _modulation(c_all, n_p, w_ada, b_ada, w_in):
    n, d = c_all.shape
    n_out = w_ada.shape[1]
    steps = n_out // MOD_BLOCK
    cast_cols = w_in.shape[1] // steps
    assert n_p % SUBLANES == 0 and cast_cols * steps == w_in.shape[1] and cast_cols % LANES == 0
    cast_spec = pl.BlockSpec((w_in.shape[0], cast_cols), lambda j: (0, j))
    return pl.pallas_call(
        _mod_kernel,
        grid=(steps,),
        in_specs=[pl.BlockSpec((n, d), lambda j: (0, 0)),
                  pl.BlockSpec((d, MOD_BLOCK), lambda j: (0, j)),
                  pl.BlockSpec((1, MOD_BLOCK), lambda j: (0, j)),
                  cast_spec],
        out_specs=[pl.BlockSpec((n_p, MOD_BLOCK), lambda j: (0, j)),
                   pl.BlockSpec((n - n_p, MOD_BLOCK), lambda j: (0, j)),
                   cast_spec],
        out_shape=[jax.ShapeDtypeStruct((n_p, n_out), F32), jax.ShapeDtypeStruct((n - n_p, n_out), F32),
                   jax.ShapeDtypeStruct(w_in.shape, BF16)],
        compiler_params=pltpu.CompilerParams(dimension_semantics=("arbitrary",),
                                             vmem_limit_bytes=VMEM_LIMIT),
        name="modulation",
    )(c_all, w_ada, b_ada, w_in)


def _mixer_weight_specs(p):
    names = ("g1", "g2", "w_in", "caw", "woa", "cbw", "cbb", "lng", "lnb", "wob", "wo", "wr", "br")
    return [p[k] for k in names], [_const_spec(p[k].shape) for k in names]


def _mixer_prompt(x, mod, p, expert_weights):
    b, s, d = x.shape
    assert b == SUBLANES, "the time-major conv layout puts one sequence on each sublane"
    steps = SEQ_TILE // b
    tiles = s // steps
    weights, wspecs = _mixer_weight_specs(p)
    tile = lambda width: pl.BlockSpec((b, steps, width), lambda c: (0, c, 0))
    whole = lambda *shape: pl.BlockSpec(shape, lambda c: (0,) * len(shape))

    def cast_spec(w):
        experts, rows, cols = w.shape
        per_expert = tiles // experts
        assert per_expert * experts == tiles and rows % (per_expert * 2 * SUBLANES) == 0
        return pl.BlockSpec((1, rows // per_expert, cols), lambda c: (c // per_expert, lax.rem(c, per_expert), 0))

    cast_specs = [cast_spec(w) for w in expert_weights]
    return pl.pallas_call(
        _mixer_prompt_kernel,
        grid=(tiles,),
        in_specs=[tile(d), whole(b, N_MOD * d)] + wspecs + cast_specs,
        out_specs=[tile(d), tile(d), tile(ROUTER_LANES),
                   pl.BlockSpec((None, 1, LANES), lambda c: (c, 0, 0)),
                   whole(CONV_A_WIDTH - 1, b, D_A), whole(CONV_B_WIDTH - 1, b, D_B)] + cast_specs,
        out_shape=[jax.ShapeDtypeStruct((b, s, d), F32),
                   jax.ShapeDtypeStruct((b, s, d), BF16),
                   jax.ShapeDtypeStruct((b, s, ROUTER_LANES), F32),
                   jax.ShapeDtypeStruct((tiles, 1, LANES), jnp.int32),
                   jax.ShapeDtypeStruct((CONV_A_WIDTH - 1, b, D_A), F32),
                   jax.ShapeDtypeStruct((CONV_B_WIDTH - 1, b, D_B), F32)]
        + [jax.ShapeDtypeStruct(w.shape, BF16) for w in expert_weights],
        scratch_shapes=[pltpu.VMEM(((CONV_A_WIDTH - 1) * b + SEQ_TILE, D_A), F32),
                        pltpu.VMEM(((CONV_B_WIDTH - 1) * b + SEQ_TILE, D_B), F32),
                        pltpu.VMEM((SEQ_TILE, D_B), F32)],
        compiler_params=pltpu.CompilerParams(dimension_semantics=("arbitrary",),
                                             vmem_limit_bytes=VMEM_LIMIT),
        name="mixer_prompt",
    )(x, mod, *weights, *expert_weights)


def _mixer_sample(x_tm, mod, sa_tm, sb_tm, p):
    n, d = x_tm.shape
    weights, wspecs = _mixer_weight_specs(p)
    full = lambda a: pl.BlockSpec(a.shape, lambda i: (0,) * a.ndim)
    whole = lambda *shape: pl.BlockSpec(shape, lambda i: (0,) * len(shape))
    return pl.pallas_call(
        _mixer_sample_kernel,
        grid=(1,),
        in_specs=[full(x_tm), full(mod)] + wspecs + [full(sa_tm), full(sb_tm)],
        out_specs=[whole(n, d), whole(n, d), whole(n, ROUTER_LANES), whole(1, LANES),
                   full(sa_tm), full(sb_tm)],
        out_shape=[jax.ShapeDtypeStruct((n, d), F32),
                   jax.ShapeDtypeStruct((n, d), BF16),
                   jax.ShapeDtypeStruct((n, ROUTER_LANES), F32),
                   jax.ShapeDtypeStruct((1, LANES), jnp.int32),
                   jax.ShapeDtypeStruct(sa_tm.shape, F32),
                   jax.ShapeDtypeStruct(sb_tm.shape, F32)],
        scratch_shapes=[pltpu.VMEM((n, D_B), BF16)],
        compiler_params=pltpu.CompilerParams(dimension_semantics=("arbitrary",),
                                             vmem_limit_bytes=VMEM_LIMIT),
        name="mixer_sample",
    )(x_tm, mod, *weights, sa_tm, sb_tm)


def _moe(counts, prompt, mod_p, sample, mod_s, p):
    b, s, d = prompt[0].shape
    n_s = mod_s.shape[0]
    steps = SEQ_TILE // b
    tiles = s // steps
    assert sample[0].shape == (SEQ_TILE, d) and SEQ_TILE % n_s == 0
    weights = [p["w1"], p["w3"], p["w2"], p["gf"]]
    tile = lambda width: pl.BlockSpec((b, steps, width), lambda c, cnt: (0, jnp.minimum(c, tiles - 1), 0))
    whole = lambda rows, width: pl.BlockSpec((rows, width), lambda c, cnt: (0, 0), pipeline_mode=pl.Buffered(1))
    gate = lambda rows: pl.BlockSpec((rows, d), lambda c, cnt: (0, N_MOD - 1), pipeline_mode=pl.Buffered(1))
    grid_spec = pltpu.PrefetchScalarGridSpec(
        num_scalar_prefetch=1,
        grid=(tiles + 1,),
        in_specs=[tile(d), tile(d), tile(ROUTER_LANES), gate(b),
                  whole(SEQ_TILE, d), whole(SEQ_TILE, d), whole(SEQ_TILE, ROUTER_LANES), gate(n_s)]
        + [_const_spec(w.shape) for w in weights],
        out_specs=[tile(d), pl.BlockSpec((SEQ_TILE, d), lambda c, cnt: (0, 0))],
        scratch_shapes=[pltpu.VMEM((SEQ_TILE, d), BF16),
                        pltpu.VMEM((SEQ_TILE, ROUTER_LANES), F32),
                        pltpu.VMEM((SEQ_TILE, d), F32)],
    )
    return pl.pallas_call(
        _moe_kernel,
        grid_spec=grid_spec,
        out_shape=[jax.ShapeDtypeStruct((b, s, d), F32), jax.ShapeDtypeStruct((SEQ_TILE, d), F32)],
        compiler_params=pltpu.CompilerParams(dimension_semantics=("arbitrary",),
                                             vmem_limit_bytes=VMEM_LIMIT),
        name="moe",
    )(counts, *prompt, mod_p, *sample, mod_s, *weights)


def kernel(x_prompt, x_sample, c_prompt, c_sample, state_conv_a, state_conv_b, w_ada, b_ada, norm1_g,
           norm2_g, w_in, conv_a_w, w_out_a, conv_b_w, conv_b_bias, ln_b_g, ln_b_b, w_out_b, w_o,
           w_group, b_group, w_expert, b_expert, w1, w3, w2, final_norm_g):
    depth = w_ada.shape[0]
    assert depth == 1, "the MoE kernel fuses the final norm, so exactly one layer is supported"
    n_p, _, d = x_prompt.shape
    n_s, steps, _ = x_sample.shape
    c_all = jnp.concatenate([c_prompt, c_sample], axis=0)
    xp = x_prompt
    xs = x_sample.transpose(1, 0, 2).reshape(steps * n_s, d)
    gf = final_norm_g.reshape(1, d)
    pa, pb, sa, sb = [], [], [], []
    for l in range(depth):
        pad = ROUTER_LANES - N_EXPERTS - N_GROUPS
        wr = jnp.pad(jnp.concatenate([w_expert[l], w_group[l]], axis=1).T, ((0, pad), (0, 0)))
        br = jnp.pad(jnp.concatenate([b_expert[l], b_group[l]]), (0, pad))
        br = jnp.broadcast_to(br[:, None], (ROUTER_LANES, LANES))
        p = {
            "g1": norm1_g[l].reshape(1, d), "g2": norm2_g[l].reshape(1, d),
            "woa": w_out_a[l].astype(BF16),
            "caw": jnp.broadcast_to(conv_a_w[l][:, None, :], (CONV_A_WIDTH, SUBLANES, D_A)),
            "cbw": jnp.broadcast_to(conv_b_w[l][:, None, :], (CONV_B_WIDTH, SUBLANES, D_B)),
            "cbb": conv_b_bias[l].reshape(1, D_B),
            "lng": ln_b_g[l].reshape(1, D_B), "lnb": ln_b_b[l].reshape(1, D_B),
            "wob": w_out_b[l].astype(BF16), "wo": w_o[l].astype(BF16), "wr": wr.astype(BF16), "br": br,
            "gf": gf,
        }
        mod_p, mod_s, p["w_in"] = _modulation(c_all, n_p, w_ada[l], b_ada[l].reshape(1, N_MOD * d), w_in[l])

        x1p, h2p, route_p, cnt_p, na_p, nb_p, w1b, w3b, w2b = _mixer_prompt(
            xp, mod_p, p, (w1[l], w3[l], w2[l]))
        p.update(w1=w1b, w3=w3b, w2=w2b.reshape(N_GROUPS, EXPERTS_PER_GROUP * D_EXPERT, d))

        sa_tm = state_conv_a[l].transpose(1, 0, 2)
        sb_tm = state_conv_b[l].transpose(1, 0, 2)
        x1s, h2s, route_s, cnt_s, na_s, nb_s = _mixer_sample(xs, mod_s, sa_tm, sb_tm, p)
        counts = jnp.concatenate([cnt_p[:, 0, :N_GROUPS].reshape(-1), cnt_s[0, :N_GROUPS]])
        xp, xs = _moe(counts, (x1p, h2p, route_p), mod_p, (x1s, h2s, route_s), mod_s, p)
        pa.append(na_p.transpose(1, 0, 2))
        pb.append(nb_p.transpose(1, 0, 2))
        sa.append(na_s.transpose(1, 0, 2))
        sb.append(nb_s.transpose(1, 0, 2))
    y_sample = xs.reshape(steps, n_s, d).transpose(1, 0, 2)
    return (xp, y_sample, jnp.stack(pa), jnp.stack(pb), jnp.stack(sa), jnp.stack(sb))
```

```python
import jax
import jax.numpy as jnp
from jax import lax
from jax.experimental import pallas as pl
from jax.experimental.pallas import tpu as pltpu

D_MODEL = 1024
D_A = 512
D_B = 512
CONV_A_WIDTH = 3
CONV_B_WIDTH = 31
N_GROUPS = 4
EXPERTS_PER_GROUP = 4
N_EXPERTS = 16
D_EXPERT = 256
N_MOD = 6
EPS = 1e-6

LANES = 128
SUBLANES = 8
ROUTER_LANES = LANES
SEQ_TILE = 512
CONV_ROWS = 64
CONV_LANES = 2 * LANES
SORT_SHIFT = 4
SORT_ALIGN = 1 << SORT_SHIFT
MOE_PASS_ROWS = (128, 144, 160, 176, 192, 224, 256)
VMEM_LIMIT = 52 * 1024 * 1024
MOD_BLOCK = 2 * D_MODEL

BF16 = jnp.bfloat16
F32 = jnp.float32

assert 2 * EXPERTS_PER_GROUP == SUBLANES and N_GROUPS <= SUBLANES and N_EXPERTS == N_GROUPS * EXPERTS_PER_GROUP


def _dot(a, b):
    return jnp.dot(a, b, preferred_element_type=F32)


def _unit_rms(x):
    return x * lax.rsqrt(jnp.mean(x * x, axis=-1, keepdims=True) + EPS)


def _sigmoid(v):
    return 0.5 * jnp.tanh(0.5 * v) + 0.5


def _affine(v, scale, shift=None):
    r = scale.shape[0]
    if r != 1 and r != v.shape[0]:
        v3 = v.reshape(v.shape[0] // r, r, v.shape[-1])
        shift3 = None if shift is None else shift[None]
        return _affine(v3, scale[None], shift3).reshape(v.shape)
    out = v * scale
    return out if shift is None else out + shift


def _per_sequence(v, scale, shift=None):
    n = scale.shape[0]
    v3 = v.reshape(n, v.shape[0] // n, v.shape[-1])
    out = v3 * scale[:, None, :]
    if shift is not None:
        out = out + shift[:, None, :]
    return out.reshape(v.shape)


def _tap(v, w8):
    return (v.reshape(v.shape[0] // SUBLANES, SUBLANES, v.shape[-1]) * w8[None]).reshape(v.shape)


def _route(logits_t):
    t = logits_t.shape[1]
    neg = -jnp.inf
    row = lax.broadcasted_iota(jnp.int32, (SUBLANES, t), 0).astype(F32)
    live = row < float(EXPERTS_PER_GROUP)
    gl = jnp.where(live, logits_t[N_EXPERTS:N_EXPERTS + SUBLANES, :], neg)
    gmax = jnp.max(gl, axis=0, keepdims=True)
    g_idx = jnp.min(jnp.where(gl == gmax, row, float(SUBLANES)), axis=0, keepdims=True)
    p_g = 1.0 / jnp.sum(jnp.exp(gl - gmax), axis=0, keepdims=True)
    el = None
    for g in reversed(range(N_GROUPS)):
        slab = logits_t[(g // 2) * SUBLANES:(g // 2 + 1) * SUBLANES, :]
        cand = slab if g % 2 == 0 else pltpu.roll(slab, SUBLANES - EXPERTS_PER_GROUP, axis=0)
        el = cand if el is None else jnp.where(g_idx == float(g), cand, el)
    el = jnp.where(live, el, neg)
    v1 = jnp.max(el, axis=0, keepdims=True)
    i1 = jnp.min(jnp.where(el == v1, row, float(SUBLANES)), axis=0, keepdims=True)
    el2 = jnp.where(row == i1, neg, el)
    v2 = jnp.max(el2, axis=0, keepdims=True)
    i2 = jnp.min(jnp.where(el2 == v2, row, float(SUBLANES)), axis=0, keepdims=True)
    e21 = jnp.exp(v2 - v1)
    wt1 = p_g / (1.0 + e21)
    wt2 = p_g * e21 / (1.0 + e21)
    comb = jnp.where(row == i1, wt1, jnp.where(row == i2, wt2, 0.0))
    onehot = jnp.where(row == g_idx, 1.0, 0.0)
    onehot_b = jnp.concatenate([onehot, jnp.zeros_like(onehot)], axis=0).astype(BF16)
    earlier = lax.broadcasted_iota(jnp.int32, (t, t), 0) < lax.broadcasted_iota(jnp.int32, (t, t), 1)
    before = _dot(onehot_b, jnp.where(earlier, 1.0, 0.0).astype(BF16))[0:SUBLANES, :]
    rank = jnp.sum(before * onehot, axis=0, keepdims=True)
    count = jnp.sum(onehot, axis=1, keepdims=True)
    row1 = lax.broadcasted_iota(jnp.int32, (SUBLANES, 1), 0)
    start = jnp.zeros((SUBLANES, 1), F32)
    for g in range(N_GROUPS - 1):
        start = start + jnp.where(row1 > g, count[g:g + 1, :], 0.0)
    dest = jnp.sum(onehot * start, axis=0, keepdims=True) + rank
    route_t = jnp.concatenate([jnp.where(live, comb, dest), jnp.broadcast_to(dest, (LANES - SUBLANES, t))],
                              axis=0)
    onehot_all = jnp.concatenate([onehot, jnp.zeros((LANES - SUBLANES, t), F32)], axis=0).astype(BF16)
    count_row = lax.dot_general(jnp.ones((2 * SUBLANES, t), BF16), onehot_all, (((1,), (1,)), ((), ())),
                                preferred_element_type=F32)[0:1, :]
    return jnp.transpose(route_t), count_row


def _ln_silu(v, bias, g, b):
    v = v + bias
    mu = jnp.mean(v, axis=-1, keepdims=True)
    d = v - mu
    var = jnp.mean(d * d, axis=-1, keepdims=True)
    y = d * lax.rsqrt(var + EPS) * g + b
    return y * _sigmoid(y)


def _mixer_tail(x, hb, mod, b_a_conv, yb_in, g2, w_in_ref, woa_ref, wob_ref, wo_ref, wr_ref, br_ref,
                x1_ref, h2_ref, route_ref, cnt_ref, affine):
    y_a = _dot(b_a_conv.astype(BF16), woa_ref[...])
    y_b = _dot(yb_in, wob_ref[...])
    o = 3 * D_A + 2 * D_B
    mg_a = _dot(hb, w_in_ref[:, o:o + D_MODEL])
    mg_b = _dot(hb, w_in_ref[:, o + D_MODEL:o + 2 * D_MODEL])
    merged = _sigmoid(mg_a) * y_a + _sigmoid(mg_b) * y_b
    gt1 = mod[:, 2 * D_MODEL:3 * D_MODEL]
    x1 = x + affine(_dot(merged.astype(BF16), wo_ref[...]), gt1)
    sh2 = mod[:, 3 * D_MODEL:4 * D_MODEL]
    sc2 = mod[:, 4 * D_MODEL:5 * D_MODEL]
    h2 = affine(_unit_rms(x1), g2 * (1.0 + sc2), sh2).astype(BF16)
    logits_t = lax.dot_general(wr_ref[...], h2, (((1,), (1,)), ((), ())), preferred_element_type=F32)
    logits_t = logits_t + jnp.tile(br_ref[...], (1, h2.shape[0] // LANES))
    x1_ref[...] = x1.reshape(x1_ref.shape)
    h2_ref[...] = h2.reshape(h2_ref.shape)
    route, count = _route(logits_t)
    route_ref[...] = route.reshape(route_ref.shape)
    cnt_ref[...] = count.astype(jnp.int32)


def _mixer_prompt_kernel(x_ref, mod_ref, g1_ref, g2_ref, w_in_ref, caw_ref, woa_ref, cbw_ref, cbb_ref,
                         lng_ref, lnb_ref, wob_ref, wo_ref, wr_ref, br_ref,
                         w1f_ref, w3f_ref, w2f_ref,
                         x1_ref, h2_ref, route_ref, cnt_ref, na_ref, nb_ref, w1b_ref, w3b_ref, w2b_ref,
                         hist_a, hist_b, conv_b_ref):
    n_seq, steps, d = x_ref.shape
    t = n_seq * steps
    rows_a = (CONV_A_WIDTH - 1) * n_seq
    rows_b = (CONV_B_WIDTH - 1) * n_seq
    w1b_ref[...] = w1f_ref[...].astype(BF16)
    w3b_ref[...] = w3f_ref[...].astype(BF16)
    w2b_ref[...] = w2f_ref[...].astype(BF16)

    @pl.when(pl.program_id(0) == 0)
    def _():
        hist_a[0:rows_a, :] = jnp.zeros((rows_a, D_A), F32)
        hist_b[0:rows_b, :] = jnp.zeros((rows_b, D_B), F32)

    def time_major(v):
        return jnp.swapaxes(v.reshape(n_seq, steps, v.shape[-1]), 0, 1).reshape(v.shape)

    def sequence_major(v):
        return jnp.swapaxes(v.reshape(steps, n_seq, v.shape[-1]), 0, 1).reshape(v.shape)

    x = x_ref[...].reshape(t, d)
    mod = mod_ref[...]
    h = _per_sequence(_unit_rms(x), g1_ref[...] * (1.0 + mod[:, D_MODEL:2 * D_MODEL]), mod[:, 0:D_MODEL])
    hb = h.astype(BF16)

    def proj(lo, width):
        return _dot(hb, w_in_ref[:, lo:lo + width])

    hist_a[rows_a:rows_a + t, :] = time_major(proj(D_A, D_A) * proj(2 * D_A, D_A))
    conv_a = None
    for k in range(CONV_A_WIDTH):
        term = _tap(hist_a[k * n_seq:k * n_seq + t, :], caw_ref[k])
        conv_a = term if conv_a is None else conv_a + term
    b_a_conv = proj(0, D_A) * sequence_major(conv_a)
    na_ref[...] = hist_a[t:t + rows_a, :].reshape(na_ref.shape)

    hist_b[rows_b:rows_b + t, :] = time_major(proj(3 * D_A, D_B) * _sigmoid(proj(3 * D_A + D_B, D_B)))
    nb_ref[...] = hist_b[t:t + rows_b, :].reshape(nb_ref.shape)
    for r0 in range(0, t, CONV_ROWS):
        for c0 in range(0, D_B, CONV_LANES):
            cols = slice(c0, c0 + CONV_LANES)
            acc = None
            for k in range(CONV_B_WIDTH):
                o = r0 + k * n_seq
                term = _tap(hist_b[o:o + CONV_ROWS, cols], cbw_ref[k][:, cols])
                acc = term if acc is None else acc + term
            conv_b_ref[r0:r0 + CONV_ROWS, cols] = acc
    yb_in = _ln_silu(sequence_major(conv_b_ref[...]), cbb_ref[...], lng_ref[...], lnb_ref[...]).astype(BF16)

    hist_a[0:rows_a, :] = hist_a[t:t + rows_a, :]
    hist_b[0:rows_b, :] = hist_b[t:t + rows_b, :]

    _mixer_tail(x, hb, mod, b_a_conv, yb_in, g2_ref[...], w_in_ref, woa_ref, wob_ref, wo_ref,
                wr_ref, br_ref, x1_ref, h2_ref, route_ref, cnt_ref, _per_sequence)


def _mixer_sample_kernel(x_ref, mod_ref, g1_ref, g2_ref, w_in_ref, caw_ref, woa_ref, cbw_ref, cbb_ref,
                         lng_ref, lnb_ref, wob_ref, wo_ref, wr_ref, br_ref, sa_ref, sb_ref,
                         x1_ref, h2_ref, route_ref, cnt_ref, na_ref, nb_ref, yb_in_ref):
    n_seq = mod_ref.shape[0]
    steps = x_ref.shape[0] // n_seq
    x = x_ref[...]
    mod = mod_ref[...]
    h = _affine(_unit_rms(x), g1_ref[...] * (1.0 + mod[:, D_MODEL:2 * D_MODEL]), mod[:, 0:D_MODEL])
    hb = h.astype(BF16)

    def proj(lo, width):
        return _dot(hb, w_in_ref[:, lo:lo + width])

    def slab(v, j):
        return v[j * n_seq:(j + 1) * n_seq, :]

    u_a = proj(D_A, D_A) * proj(2 * D_A, D_A)
    full_a = [sa_ref[j] for j in range(CONV_A_WIDTH - 1)] + [slab(u_a, j) for j in range(steps)]
    conv_a = []
    for s in range(steps):
        acc = jnp.zeros((n_seq, D_A), F32)
        for k in range(CONV_A_WIDTH):
            acc = acc + _tap(full_a[s + k], caw_ref[k])
        conv_a.append(acc)
    b_a_conv = proj(0, D_A) * jnp.concatenate(conv_a, axis=0)
    for j in range(CONV_A_WIDTH - 1):
        na_ref[j] = full_a[steps + j]

    u_b = proj(3 * D_A, D_B) * _sigmoid(proj(3 * D_A + D_B, D_B))
    hist = CONV_B_WIDTH - 1

    def full_b(j):
        return sb_ref[j] if j < hist else slab(u_b, j - hist)

    bias, lng, lnb = cbb_ref[...], lng_ref[...], lnb_ref[...]
    for s in range(steps):
        acc = jnp.zeros((n_seq, D_B), F32)
        for k in range(CONV_B_WIDTH):
            acc = acc + _tap(full_b(s + k), cbw_ref[k])
        yb_in_ref[s * n_seq:(s + 1) * n_seq, :] = _ln_silu(acc, bias, lng, lnb).astype(BF16)
    for j in range(hist):
        nb_ref[j] = full_b(steps + j)

    _mixer_tail(x, hb, mod, b_a_conv, yb_in_ref[...], g2_ref[...], w_in_ref, woa_ref, wob_ref, wo_ref,
                wr_ref, br_ref, x1_ref, h2_ref, route_ref, cnt_ref, _affine)


def _moe_kernel(cnt_ref, x1p_ref, h2p_ref, routep_ref, gt2p_ref, x1s_ref, h2s_ref, routes_ref, gt2s_ref,
                w1_ref, w3_ref, w2_ref, gf_ref, yp_ref, ys_ref, h2sort_ref, cs_ref, acc_ref):
    is_sample = pl.program_id(0) == pl.num_programs(0) - 1
    prompt = jnp.logical_not(is_sample)
    pl.when(prompt)(lambda: _moe_sort(h2p_ref, routep_ref, h2sort_ref, cs_ref, acc_ref))
    pl.when(is_sample)(lambda: _moe_sort(h2s_ref, routes_ref, h2sort_ref, cs_ref, acc_ref))
    _moe_experts(cnt_ref, w1_ref, w3_ref, w2_ref, h2sort_ref, cs_ref, acc_ref)
    pl.when(prompt)(lambda: _moe_finish(x1p_ref, routep_ref, gt2p_ref, gf_ref, acc_ref, yp_ref, _per_sequence))
    pl.when(is_sample)(lambda: _moe_finish(x1s_ref, routes_ref, gt2s_ref, gf_ref, acc_ref, ys_ref, _affine))


def _rows(ref):
    return ref[...].reshape(-1, ref.shape[-1])


def _moe_sort(h2_ref, route_ref, h2s_ref, cs_ref, acc_ref):
    tp = h2s_ref.shape[0]
    route = _rows(route_ref)
    t = route.shape[0]
    lane_i = lax.broadcasted_iota(jnp.int32, route.shape, 1)
    dest_row = jnp.transpose(route)[EXPERTS_PER_GROUP:EXPERTS_PER_GROUP + 1, :]
    slot = lax.broadcasted_iota(jnp.int32, (tp, t), 0).astype(F32)
    p_sort = jnp.where(slot == dest_row, 1.0, 0.0).astype(BF16)
    h2s_ref[...] = _dot(p_sort, _rows(h2_ref)).astype(BF16)
    comb = jnp.where(lane_i < EXPERTS_PER_GROUP, route, 0.0)
    c_hi = comb.astype(BF16).astype(F32)
    c_mid = (comb - c_hi).astype(BF16).astype(F32)
    c_lo = comb - c_hi - c_mid
    packed = (c_hi + pltpu.roll(c_mid, EXPERTS_PER_GROUP, axis=1)
              + pltpu.roll(c_lo, 2 * EXPERTS_PER_GROUP, axis=1)).astype(BF16)
    moved = _dot(p_sort, packed)
    cs_ref[...] = (moved + pltpu.roll(moved, LANES - EXPERTS_PER_GROUP, axis=1)
                   + pltpu.roll(moved, LANES - 2 * EXPERTS_PER_GROUP, axis=1))
    acc_ref[...] = jnp.zeros(acc_ref.shape, F32)


def _moe_experts(cnt_ref, w1_ref, w3_ref, w2_ref, h2s_ref, cs_ref, acc_ref):
    tp = h2s_ref.shape[0]
    tile = pl.program_id(0)

    def expert_pass(rows, g, window, lo, end):
        off = pl.multiple_of(jnp.minimum(window, tp - rows), SORT_ALIGN)
        row_id = off + lax.broadcasted_iota(jnp.int32, (rows, 1), 0)
        valid = (row_id >= jnp.maximum(lo, window)) & (row_id < end)
        hb = h2s_ref[pl.ds(off, rows), :]
        c4 = jnp.where(valid, cs_ref[pl.ds(off, rows), :], 0.0)
        experts = [g * EXPERTS_PER_GROUP + e for e in range(EXPERTS_PER_GROUP)]
        gate = jnp.concatenate([_dot(hb, w1_ref[e]) for e in experts], axis=1)
        up = jnp.concatenate([_dot(hb, w3_ref[e]) for e in experts], axis=1)
        cexp = jnp.concatenate(
            [jnp.broadcast_to(c4[:, e:e + 1], (rows, D_EXPERT)) for e in range(EXPERTS_PER_GROUP)], axis=1)
        a = gate * _sigmoid(gate) * up * cexp
        acc_ref[pl.ds(off, rows), :] += _dot(a.astype(BF16), w2_ref[g])

    def group(g, start):
        end = start + cnt_ref[tile * N_GROUPS + g]
        base = lax.shift_right_logical(start, SORT_SHIFT) * SORT_ALIGN
        span = end - base

        def passes(rows):
            def run():
                def body(p, carry):
                    expert_pass(rows, g, base + p * rows, start, end)
                    return carry
                lax.fori_loop(0, jnp.where(end > start, lax.div(span + (rows - 1), rows), 0), body, 0)
            return run

        size_class = sum((span > rows).astype(jnp.int32) for rows in MOE_PASS_ROWS[:-1])
        lax.switch(size_class, [passes(rows) for rows in MOE_PASS_ROWS])
        return end

    lax.fori_loop(0, N_GROUPS, group, jnp.int32(0))


def _moe_finish(x1_ref, route_ref, gt2_ref, gf_ref, acc_ref, y_ref, affine):
    tp = acc_ref.shape[0]
    route = _rows(route_ref)
    t = route.shape[0]
    dest_col = route[:, EXPERTS_PER_GROUP:EXPERTS_PER_GROUP + 1]
    p_back = jnp.where(lax.broadcasted_iota(jnp.int32, (t, tp), 1).astype(F32) == dest_col,
                       1.0, 0.0).astype(BF16)
    y = acc_ref[...]
    y_hi = y.astype(BF16)
    y_lo = (y - y_hi.astype(F32)).astype(BF16)
    y_tok = _dot(p_back, y_hi) + _dot(p_back, y_lo)
    x2 = _rows(x1_ref) + affine(y_tok, gt2_ref[...])
    y_ref[...] = (_unit_rms(x2) * gf_ref[...]).reshape(y_ref.shape)


def _mod_kernel(c_ref, w_ref, b_ref, win_ref, op_ref, os_ref, winb_ref):
    winb_ref[...] = win_ref[...].astype(BF16)
    c = c_ref[...]
    a = (c * _sigmoid(c)).astype(BF16)
    mod = _dot(a, w_ref[...].astype(BF16)) + b_ref[...]
    n_p = op_ref.shape[0]
    op_ref[...] = mod[:n_p]
    os_ref[...] = mod[n_p:]


def _const_spec(shape):
    nd = len(shape)
    return pl.BlockSpec(shape, lambda *_: (0,) * nd, pipeline_mode=pl.Buffered(1))


def _modulation(c_all, n_p, w_ada, b_ada, w_in):
    n, d = c_all.shape
    n_out = w_ada.shape[1]
    steps = n_out // MOD_BLOCK
    cast_cols = w_in.shape[1] // steps
    assert n_p % SUBLANES == 0 and cast_cols * steps == w_in.shape[1] and cast_cols % LANES == 0
    cast_spec = pl.BlockSpec((w_in.shape[0], cast_cols), lambda j: (0, j))
    return pl.pallas_call(
        _mod_kernel,
        grid=(steps,),
        in_specs=[pl.BlockSpec((n, d), lambda j: (0, 0)),
                  pl.BlockSpec((d, MOD_BLOCK), lambda j: (0, j)),
                  pl.BlockSpec((1, MOD_BLOCK), lambda j: (0, j)),
                  cast_spec],
        out_specs=[pl.BlockSpec((n_p, MOD_BLOCK), lambda j: (0, j)),
                   pl.BlockSpec((n - n_p, MOD_BLOCK), lambda j: (0, j)),
                   cast_spec],
        out_shape=[jax.ShapeDtypeStruct((n_p, n_out), F32), jax.ShapeDtypeStruct((n - n_p, n_out), F32),
                   jax.ShapeDtypeStruct(w_in.shape, BF16)],
        compiler_params=pltpu.CompilerParams(dimension_semantics=("arbitrary",),
                                             vmem_limit_bytes=VMEM_LIMIT),
        name="modulation",
    )(c_all, w_ada, b_ada, w_in)


def _mixer_weight_specs(p):
    names = ("g1", "g2", "w_in", "caw", "woa", "cbw", "cbb", "lng", "lnb", "wob", "wo", "wr", "br")
    return [p[k] for k in names], [_const_spec(p[k].shape) for k in names]


def _mixer_prompt(x, mod, p, expert_weights):
    b, s, d = x.shape
    assert b == SUBLANES, "the time-major conv layout puts one sequence on each sublane"
    steps = SEQ_TILE // b
    tiles = s // steps
    weights, wspecs = _mixer_weight_specs(p)
    tile = lambda width: pl.BlockSpec((b, steps, width), lambda c: (0, c, 0))
    whole = lambda *shape: pl.BlockSpec(shape, lambda c: (0,) * len(shape))

    def cast_spec(w):
        experts, rows, cols = w.shape
        per_expert = tiles // experts
        assert per_expert * experts == tiles and rows % (per_expert * 2 * SUBLANES) == 0
        return pl.BlockSpec((1, rows // per_expert, cols), lambda c: (c // per_expert, lax.rem(c, per_expert), 0))

    cast_specs = [cast_spec(w) for w in expert_weights]
    return pl.pallas_call(
        _mixer_prompt_kernel,
        grid=(tiles,),
        in_specs=[tile(d), whole(b, N_MOD * d)] + wspecs + cast_specs,
        out_specs=[tile(d), tile(d), tile(ROUTER_LANES),
                   pl.BlockSpec((None, 1, LANES), lambda c: (c, 0, 0)),
                   whole(CONV_A_WIDTH - 1, b, D_A), whole(CONV_B_WIDTH - 1, b, D_B)] + cast_specs,
        out_shape=[jax.ShapeDtypeStruct((b, s, d), F32),
                   jax.ShapeDtypeStruct((b, s, d), BF16),
                   jax.ShapeDtypeStruct((b, s, ROUTER_LANES), F32),
                   jax.ShapeDtypeStruct((tiles, 1, LANES), jnp.int32),
                   jax.ShapeDtypeStruct((CONV_A_WIDTH - 1, b, D_A), F32),
                   jax.ShapeDtypeStruct((CONV_B_WIDTH - 1, b, D_B), F32)]
        + [jax.ShapeDtypeStruct(w.shape, BF16) for w in expert_weights],
        scratch_shapes=[pltpu.VMEM(((CONV_A_WIDTH - 1) * b + SEQ_TILE, D_A), F32),
                        pltpu.VMEM(((CONV_B_WIDTH - 1) * b + SEQ_TILE, D_B), F32),
                        pltpu.VMEM((SEQ_TILE, D_B), F32)],
        compiler_params=pltpu.CompilerParams(dimension_semantics=("arbitrary",),
                                             vmem_limit_bytes=VMEM_LIMIT),
        name="mixer_prompt",
    )(x, mod, *weights, *expert_weights)


def _mixer_sample(x_tm, mod, sa_tm, sb_tm, p):
    n, d = x_tm.shape
    weights, wspecs = _mixer_weight_specs(p)
    full = lambda a: pl.BlockSpec(a.shape, lambda i: (0,) * a.ndim)
    whole = lambda *shape: pl.BlockSpec(shape, lambda i: (0,) * len(shape))
    return pl.pallas_call(
        _mixer_sample_kernel,
        grid=(1,),
        in_specs=[full(x_tm), full(mod)] + wspecs + [full(sa_tm), full(sb_tm)],
        out_specs=[whole(n, d), whole(n, d), whole(n, ROUTER_LANES), whole(1, LANES),
                   full(sa_tm), full(sb_tm)],
        out_shape=[jax.ShapeDtypeStruct((n, d), F32),
                   jax.ShapeDtypeStruct((n, d), BF16),
                   jax.ShapeDtypeStruct((n, ROUTER_LANES), F32),
                   jax.ShapeDtypeStruct((1, LANES), jnp.int32),
                   jax.ShapeDtypeStruct(sa_tm.shape, F32),
                   jax.ShapeDtypeStruct(sb_tm.shape, F32)],
        scratch_shapes=[pltpu.VMEM((n, D_B), BF16)],
        compiler_params=pltpu.CompilerParams(dimension_semantics=("arbitrary",),
                                             vmem_limit_bytes=VMEM_LIMIT),
        name="mixer_sample",
    )(x_tm, mod, *weights, sa_tm, sb_tm)


def _moe(counts, prompt, mod_p, sample, mod_s, p):
    b, s, d = prompt[0].shape
    n_s = mod_s.shape[0]
    steps = SEQ_TILE // b
    tiles = s // steps
    assert sample[0].shape == (SEQ_TILE, d) and SEQ_TILE % n_s == 0
    weights = [p["w1"], p["w3"], p["w2"], p["gf"]]
    tile = lambda width: pl.BlockSpec((b, steps, width), lambda c, cnt: (0, jnp.minimum(c, tiles - 1), 0))
    whole = lambda rows, width: pl.BlockSpec((rows, width), lambda c, cnt: (0, 0), pipeline_mode=pl.Buffered(1))
    gate = lambda rows: pl.BlockSpec((rows, d), lambda c, cnt: (0, N_MOD - 1), pipeline_mode=pl.Buffered(1))
    grid_spec = pltpu.PrefetchScalarGridSpec(
        num_scalar_prefetch=1,
        grid=(tiles + 1,),
        in_specs=[tile(d), tile(d), tile(ROUTER_LANES), gate(b),
                  whole(SEQ_TILE, d), whole(SEQ_TILE, d), whole(SEQ_TILE, ROUTER_LANES), gate(n_s)]
        + [_const_spec(w.shape) for w in weights],
        out_specs=[tile(d), pl.BlockSpec((SEQ_TILE, d), lambda c, cnt: (0, 0))],
        scratch_shapes=[pltpu.VMEM((SEQ_TILE, d), BF16),
                        pltpu.VMEM((SEQ_TILE, ROUTER_LANES), F32),
                        pltpu.VMEM((SEQ_TILE, d), F32)],
    )
    return pl.pallas_call(
        _moe_kernel,
        grid_spec=grid_spec,
        out_shape=[jax.ShapeDtypeStruct((b, s, d), F32), jax.ShapeDtypeStruct((SEQ_TILE, d), F32)],
        compiler_params=pltpu.CompilerParams(dimension_semantics=("arbitrary",),
                                             vmem_limit_bytes=VMEM_LIMIT),
        name="moe",
    )(counts, *prompt, mod_p, *sample, mod_s, *weights)


def kernel(x_prompt, x_sample, c_prompt, c_sample, state_conv_a, state_conv_b, w_ada, b_ada, norm1_g,
           norm2_g, w_in, conv_a_w, w_out_a, conv_b_w, conv_b_bias, ln_b_g, ln_b_b, w_out_b, w_o,
           w_group, b_group, w_expert, b_expert, w1, w3, w2, final_norm_g):
    depth = w_ada.shape[0]
    assert depth == 1, "the MoE kernel fuses the final norm, so exactly one layer is supported"
    n_p, _, d = x_prompt.shape
    n_s, steps, _ = x_sample.shape
    c_all = jnp.concatenate([c_prompt, c_sample], axis=0)
    xp = x_prompt
    xs = x_sample.transpose(1, 0, 2).reshape(steps * n_s, d)
    gf = final_norm_g.reshape(1, d)
    pa, pb, sa, sb = [], [], [], []
    for l in range(depth):
        pad = ROUTER_LANES - N_EXPERTS - N_GROUPS
        wr = jnp.pad(jnp.concatenate([w_expert[l], w_group[l]], axis=1).T, ((0, pad), (0, 0)))
        br = jnp.pad(jnp.concatenate([b_expert[l], b_group[l]]), (0, pad))
        br = jnp.broadcast_to(br[:, None], (ROUTER_LANES, LANES))
        p = {
            "g1": norm1_g[l].reshape(1, d), "g2": norm2_g[l].reshape(1, d),
            "woa": w_out_a[l].astype(BF16),
            "caw": jnp.broadcast_to(conv_a_w[l][:, None, :], (CONV_A_WIDTH, SUBLANES, D_A)),
            "cbw": jnp.broadcast_to(conv_b_w[l][:, None, :], (CONV_B_WIDTH, SUBLANES, D_B)),
            "cbb": conv_b_bias[l].reshape(1, D_B),
            "lng": ln_b_g[l].reshape(1, D_B), "lnb": ln_b_b[l].reshape(1, D_B),
            "wob": w_out_b[l].astype(BF16), "wo": w_o[l].astype(BF16), "wr": wr.astype(BF16), "br": br,
            "gf": gf,
        }
        mod_p, mod_s, p["w_in"] = _modulation(c_all, n_p, w_ada[l], b_ada[l].reshape(1, N_MOD * d), w_in[l])

        x1p, h2p, route_p, cnt_p, na_p, nb_p, w1b, w3b, w2b = _mixer_prompt(
            xp, mod_p, p, (w1[l], w3[l], w2[l]))
        p.update(w1=w1b, w3=w3b, w2=w2b.reshape(N_GROUPS, EXPERTS_PER_GROUP * D_EXPERT, d))

        sa_tm = state_conv_a[l].transpose(1, 0, 2)
        sb_tm = state_conv_b[l].transpose(1, 0, 2)
        x1s, h2s, route_s, cnt_s, na_s, nb_s = _mixer_sample(xs, mod_s, sa_tm, sb_tm, p)
        counts = jnp.concatenate([cnt_p[:, 0, :N_GROUPS].reshape(-1), cnt_s[0, :N_GROUPS]])
        xp, xs = _moe(counts, (x1p, h2p, route_p), mod_p, (x1s, h2s, route_s), mod_s, p)
        pa.append(na_p.transpose(1, 0, 2))
        pb.append(nb_p.transpose(1, 0, 2))
        sa.append(na_s.transpose(1, 0, 2))
        sb.append(nb_s.transpose(1, 0, 2))
    y_sample = xs.reshape(steps, n_s, d).transpose(1, 0, 2)
    return (xp, y_sample, jnp.stack(pa), jnp.stack(pb), jnp.stack(sa), jnp.stack(sb))
```

```python
import jax
import jax.numpy as jnp
from jax import lax
from jax.experimental import pallas as pl
from jax.experimental.pallas import tpu as pltpu

D_MODEL = 1024
D_A = 512
D_B = 512
CONV_A_WIDTH = 3
CONV_B_WIDTH = 31
N_GROUPS = 4
EXPERTS_PER_GROUP = 4
N_EXPERTS = 16
D_EXPERT = 256
N_MOD = 6
EPS = 1e-6

LANES = 128
SUBLANES = 8
ROUTER_LANES = LANES
SEQ_TILE = 512
CONV_ROWS = 64
CONV_LANES = 2 * LANES
SORT_SHIFT = 4
SORT_ALIGN = 1 << SORT_SHIFT
MOE_PASS_ROWS = (128, 144, 160, 176, 192, 224, 256)
VMEM_LIMIT = 52 * 1024 * 1024
MOD_BLOCK = 2 * D_MODEL

BF16 = jnp.bfloat16
F32 = jnp.float32

assert 2 * EXPERTS_PER_GROUP == SUBLANES and N_GROUPS <= SUBLANES and N_EXPERTS == N_GROUPS * EXPERTS_PER_GROUP


def _dot(a, b):
    return jnp.dot(a, b, preferred_element_type=F32)


def _unit_rms(x):
    return x * lax.rsqrt(jnp.mean(x * x, axis=-1, keepdims=True) + EPS)


def _sigmoid(v):
    return 0.5 * jnp.tanh(0.5 * v) + 0.5


def _affine(v, scale, shift=None):
    r = scale.shape[0]
    if r != 1 and r != v.shape[0]:
        v3 = v.reshape(v.shape[0] // r, r, v.shape[-1])
        shift3 = None if shift is None else shift[None]
        return _affine(v3, scale[None], shift3).reshape(v.shape)
    out = v * scale
    return out if shift is None else out + shift


def _per_sequence(v, scale, shift=None):
    n = scale.shape[0]
    v3 = v.reshape(n, v.shape[0] // n, v.shape[-1])
    out = v3 * scale[:, None, :]
    if shift is not None:
        out = out + shift[:, None, :]
    return out.reshape(v.shape)


def _tap(v, w8):
    return (v.reshape(v.shape[0] // SUBLANES, SUBLANES, v.shape[-1]) * w8[None]).reshape(v.shape)


def _route(logits_t):
    t = logits_t.shape[1]
    neg = -jnp.inf
    row = lax.broadcasted_iota(jnp.int32, (SUBLANES, t), 0).astype(F32)
    live = row < float(EXPERTS_PER_GROUP)
    gl = jnp.where(live, logits_t[N_EXPERTS:N_EXPERTS + SUBLANES, :], neg)
    gmax = jnp.max(gl, axis=0, keepdims=True)
    g_idx = jnp.min(jnp.where(gl == gmax, row, float(SUBLANES)), axis=0, keepdims=True)
    p_g = 1.0 / jnp.sum(jnp.exp(gl - gmax), axis=0, keepdims=True)
    el = None
    for g in reversed(range(N_GROUPS)):
        slab = logits_t[(g // 2) * SUBLANES:(g // 2 + 1) * SUBLANES, :]
        cand = slab if g % 2 == 0 else pltpu.roll(slab, SUBLANES - EXPERTS_PER_GROUP, axis=0)
        el = cand if el is None else jnp.where(g_idx == float(g), cand, el)
    el = jnp.where(live, el, neg)
    v1 = jnp.max(el, axis=0, keepdims=True)
    i1 = jnp.min(jnp.where(el == v1, row, float(SUBLANES)), axis=0, keepdims=True)
    el2 = jnp.where(row == i1, neg, el)
    v2 = jnp.max(el2, axis=0, keepdims=True)
    i2 = jnp.min(jnp.where(el2 == v2, row, float(SUBLANES)), axis=0, keepdims=True)
    e21 = jnp.exp(v2 - v1)
    wt1 = p_g / (1.0 + e21)
    wt2 = p_g * e21 / (1.0 + e21)
    comb = jnp.where(row == i1, wt1, jnp.where(row == i2, wt2, 0.0))
    onehot = jnp.where(row == g_idx, 1.0, 0.0)
    onehot_b = jnp.concatenate([onehot, jnp.zeros_like(onehot)], axis=0).astype(BF16)
    earlier = lax.broadcasted_iota(jnp.int32, (t, t), 0) < lax.broadcasted_iota(jnp.int32, (t, t), 1)
    before = _dot(onehot_b, jnp.where(earlier, 1.0, 0.0).astype(BF16))[0:SUBLANES, :]
    rank = jnp.sum(before * onehot, axis=0, keepdims=True)
    count = jnp.sum(onehot, axis=1, keepdims=True)
    row1 = lax.broadcasted_iota(jnp.int32, (SUBLANES, 1), 0)
    start = jnp.zeros((SUBLANES, 1), F32)
    for g in range(N_GROUPS - 1):
        start = start + jnp.where(row1 > g, count[g:g + 1, :], 0.0)
    dest = jnp.sum(onehot * start, axis=0, keepdims=True) + rank
    route_t = jnp.concatenate([jnp.where(live, comb, dest), jnp.broadcast_to(dest, (LANES - SUBLANES, t))],
                              axis=0)
    onehot_all = jnp.concatenate([onehot, jnp.zeros((LANES - SUBLANES, t), F32)], axis=0).astype(BF16)
    count_row = lax.dot_general(jnp.ones((2 * SUBLANES, t), BF16), onehot_all, (((1,), (1,)), ((), ())),
                                preferred_element_type=F32)[0:1, :]
    return jnp.transpose(route_t), count_row


def _ln_silu(v, bias, g, b):
    v = v + bias
    mu = jnp.mean(v, axis=-1, keepdims=True)
    d = v - mu
    var = jnp.mean(d * d, axis=-1, keepdims=True)
    y = d * lax.rsqrt(var + EPS) * g + b
    return y * _sigmoid(y)


def _mixer_tail(x, hb, mod, b_a_conv, yb_in, g2, w_in_ref, woa_ref, wob_ref, wo_ref, wr_ref, br_ref,
                x1_ref, h2_ref, route_ref, cnt_ref, affine):
    y_a = _dot(b_a_conv.astype(BF16), woa_ref[...])
    y_b = _dot(yb_in, wob_ref[...])
    o = 3 * D_A + 2 * D_B
    mg_a = _dot(hb, w_in_ref[:, o:o + D_MODEL])
    mg_b = _dot(hb, w_in_ref[:, o + D_MODEL:o + 2 * D_MODEL])
    merged = _sigmoid(mg_a) * y_a + _sigmoid(mg_b) * y_b
    gt1 = mod[:, 2 * D_MODEL:3 * D_MODEL]
    x1 = x + affine(_dot(merged.astype(BF16), wo_ref[...]), gt1)
    sh2 = mod[:, 3 * D_MODEL:4 * D_MODEL]
    sc2 = mod[:, 4 * D_MODEL:5 * D_MODEL]
    h2 = affine(_unit_rms(x1), g2 * (1.0 + sc2), sh2).astype(BF16)
    logits_t = lax.dot_general(wr_ref[...], h2, (((1,), (1,)), ((), ())), preferred_element_type=F32)
    logits_t = logits_t + jnp.tile(br_ref[...], (1, h2.shape[0] // LANES))
    x1_ref[...] = x1.reshape(x1_ref.shape)
    h2_ref[...] = h2.reshape(h2_ref.shape)
    route, count = _route(logits_t)
    route_ref[...] = route.reshape(route_ref.shape)
    cnt_ref[...] = count.astype(jnp.int32)


def _mixer_prompt_kernel(x_ref, mod_ref, g1_ref, g2_ref, w_in_ref, caw_ref, woa_ref, cbw_ref, cbb_ref,
                         lng_ref, lnb_ref, wob_ref, wo_ref, wr_ref, br_ref,
                         w1f_ref, w3f_ref, w2f_ref,
                         x1_ref, h2_ref, route_ref, cnt_ref, na_ref, nb_ref, w1b_ref, w3b_ref, w2b_ref,
                         hist_a, hist_b, conv_b_ref):
    n_seq, steps, d = x_ref.shape
    t = n_seq * steps
    rows_a = (CONV_A_WIDTH - 1) * n_seq
    rows_b = (CONV_B_WIDTH - 1) * n_seq

    @pl.when(pl.program_id(0) == 0)
    def _():
        hist_a[0:rows_a, :] = jnp.zeros((rows_a, D_A), F32)
        hist_b[0:rows_b, :] = jnp.zeros((rows_b, D_B), F32)

    def time_major(v):
        return jnp.swapaxes(v.reshape(n_seq, steps, v.shape[-1]), 0, 1).reshape(v.shape)

    def sequence_major(v):
        return jnp.swapaxes(v.reshape(steps, n_seq, v.shape[-1]), 0, 1).reshape(v.shape)

    x = x_ref[...].reshape(t, d)
    mod = mod_ref[...]
    h = _per_sequence(_unit_rms(x), g1_ref[...] * (1.0 + mod[:, D_MODEL:2 * D_MODEL]), mod[:, 0:D_MODEL])
    hb = h.astype(BF16)

    def proj(lo, width):
        return _dot(hb, w_in_ref[:, lo:lo + width])

    hist_a[rows_a:rows_a + t, :] = time_major(proj(D_A, D_A) * proj(2 * D_A, D_A))
    conv_a = None
    for k in range(CONV_A_WIDTH):
        term = _tap(hist_a[k * n_seq:k * n_seq + t, :], caw_ref[k])
        conv_a = term if conv_a is None else conv_a + term
    b_a_conv = proj(0, D_A) * sequence_major(conv_a)
    na_ref[...] = hist_a[t:t + rows_a, :].reshape(na_ref.shape)

    hist_b[rows_b:rows_b + t, :] = time_major(proj(3 * D_A, D_B) * _sigmoid(proj(3 * D_A + D_B, D_B)))
    nb_ref[...] = hist_b[t:t + rows_b, :].reshape(nb_ref.shape)
    for r0 in range(0, t, CONV_ROWS):
        for c0 in range(0, D_B, CONV_LANES):
            cols = slice(c0, c0 + CONV_LANES)
            acc = None
            for k in range(CONV_B_WIDTH):
                o = r0 + k * n_seq
                term = _tap(hist_b[o:o + CONV_ROWS, cols], cbw_ref[k][:, cols])
                acc = term if acc is None else acc + term
            conv_b_ref[r0:r0 + CONV_ROWS, cols] = acc
    yb_in = _ln_silu(sequence_major(conv_b_ref[...]), cbb_ref[...], lng_ref[...], lnb_ref[...]).astype(BF16)

    hist_a[0:rows_a, :] = hist_a[t:t + rows_a, :]
    hist_b[0:rows_b, :] = hist_b[t:t + rows_b, :]

    _mixer_tail(x, hb, mod, b_a_conv, yb_in, g2_ref[...], w_in_ref, woa_ref, wob_ref, wo_ref,
                wr_ref, br_ref, x1_ref, h2_ref, route_ref, cnt_ref, _per_sequence)

    w1b_ref[...] = w1f_ref[...].astype(BF16)
    w3b_ref[...] = w3f_ref[...].astype(BF16)
    w2b_ref[...] = w2f_ref[...].astype(BF16)


def _mixer_sample_kernel(x_ref, mod_ref, g1_ref, g2_ref, w_in_ref, caw_ref, woa_ref, cbw_ref, cbb_ref,
                         lng_ref, lnb_ref, wob_ref, wo_ref, wr_ref, br_ref, sa_ref, sb_ref,
                         x1_ref, h2_ref, route_ref, cnt_ref, na_ref, nb_ref, yb_in_ref):
    n_seq = mod_ref.shape[0]
    steps = x_ref.shape[0] // n_seq
    x = x_ref[...]
    mod = mod_ref[...]
    h = _affine(_unit_rms(x), g1_ref[...] * (1.0 + mod[:, D_MODEL:2 * D_MODEL]), mod[:, 0:D_MODEL])
    hb = h.astype(BF16)

    def proj(lo, width):
        return _dot(hb, w_in_ref[:, lo:lo + width])

    def slab(v, j):
        return v[j * n_seq:(j + 1) * n_seq, :]

    u_a = proj(D_A, D_A) * proj(2 * D_A, D_A)
    full_a = [sa_ref[j] for j in range(CONV_A_WIDTH - 1)] + [slab(u_a, j) for j in range(steps)]
    conv_a = []
    for s in range(steps):
        acc = jnp.zeros((n_seq, D_A), F32)
        for k in range(CONV_A_WIDTH):
            acc = acc + _tap(full_a[s + k], caw_ref[k])
        conv_a.append(acc)
    b_a_conv = proj(0, D_A) * jnp.concatenate(conv_a, axis=0)
    for j in range(CONV_A_WIDTH - 1):
        na_ref[j] = full_a[steps + j]

    u_b = proj(3 * D_A, D_B) * _sigmoid(proj(3 * D_A + D_B, D_B))
    hist = CONV_B_WIDTH - 1

    def full_b(j):
        return sb_ref[j] if j < hist else slab(u_b, j - hist)

    bias, lng, lnb = cbb_ref[...], lng_ref[...], lnb_ref[...]
    for s in range(steps):
        acc = jnp.zeros((n_seq, D_B), F32)
        for k in range(CONV_B_WIDTH):
            acc = acc + _tap(full_b(s + k), cbw_ref[k])
        yb_in_ref[s * n_seq:(s + 1) * n_seq, :] = _ln_silu(acc, bias, lng, lnb).astype(BF16)
    for j in range(hist):
        nb_ref[j] = full_b(steps + j)

    _mixer_tail(x, hb, mod, b_a_conv, yb_in_ref[...], g2_ref[...], w_in_ref, woa_ref, wob_ref, wo_ref,
                wr_ref, br_ref, x1_ref, h2_ref, route_ref, cnt_ref, _affine)


def _moe_kernel(cnt_ref, x1p_ref, h2p_ref, routep_ref, gt2p_ref, x1s_ref, h2s_ref, routes_ref, gt2s_ref,
                w1_ref, w3_ref, w2_ref, gf_ref, yp_ref, ys_ref, h2sort_ref, cs_ref, acc_ref):
    is_sample = pl.program_id(0) == pl.num_programs(0) - 1
    prompt = jnp.logical_not(is_sample)
    pl.when(prompt)(lambda: _moe_sort(h2p_ref, routep_ref, h2sort_ref, cs_ref, acc_ref))
    pl.when(is_sample)(lambda: _moe_sort(h2s_ref, routes_ref, h2sort_ref, cs_ref, acc_ref))
    _moe_experts(cnt_ref, w1_ref, w3_ref, w2_ref, h2sort_ref, cs_ref, acc_ref)
    pl.when(prompt)(lambda: _moe_finish(x1p_ref, routep_ref, gt2p_ref, gf_ref, acc_ref, yp_ref, _per_sequence))
    pl.when(is_sample)(lambda: _moe_finish(x1s_ref, routes_ref, gt2s_ref, gf_ref, acc_ref, ys_ref, _affine))


def _rows(ref):
    return ref[...].reshape(-1, ref.shape[-1])


def _moe_sort(h2_ref, route_ref, h2s_ref, cs_ref, acc_ref):
    tp = h2s_ref.shape[0]
    route = _rows(route_ref)
    t = route.shape[0]
    lane_i = lax.broadcasted_iota(jnp.int32, route.shape, 1)
    dest_row = jnp.transpose(route)[EXPERTS_PER_GROUP:EXPERTS_PER_GROUP + 1, :]
    slot = lax.broadcasted_iota(jnp.int32, (tp, t), 0).astype(F32)
    p_sort = jnp.where(slot == dest_row, 1.0, 0.0).astype(BF16)
    h2s_ref[...] = _dot(p_sort, _rows(h2_ref)).astype(BF16)
    comb = jnp.where(lane_i < EXPERTS_PER_GROUP, route, 0.0)
    c_hi = comb.astype(BF16).astype(F32)
    c_mid = (comb - c_hi).astype(BF16).astype(F32)
    c_lo = comb - c_hi - c_mid
    packed = (c_hi + pltpu.roll(c_mid, EXPERTS_PER_GROUP, axis=1)
              + pltpu.roll(c_lo, 2 * EXPERTS_PER_GROUP, axis=1)).astype(BF16)
    moved = _dot(p_sort, packed)
    cs_ref[...] = (moved + pltpu.roll(moved, LANES - EXPERTS_PER_GROUP, axis=1)
                   + pltpu.roll(moved, LANES - 2 * EXPERTS_PER_GROUP, axis=1))
    acc_ref[...] = jnp.zeros(acc_ref.shape, F32)


def _moe_experts(cnt_ref, w1_ref, w3_ref, w2_ref, h2s_ref, cs_ref, acc_ref):
    tp = h2s_ref.shape[0]
    tile = pl.program_id(0)

    def expert_pass(rows, g, window, lo, end):
        off = pl.multiple_of(jnp.minimum(window, tp - rows), SORT_ALIGN)
        row_id = off + lax.broadcasted_iota(jnp.int32, (rows, 1), 0)
        valid = (row_id >= jnp.maximum(lo, window)) & (row_id < end)
        hb = h2s_ref[pl.ds(off, rows), :]
        c4 = jnp.where(valid, cs_ref[pl.ds(off, rows), :], 0.0)
        experts = [g * EXPERTS_PER_GROUP + e for e in range(EXPERTS_PER_GROUP)]
        gate = jnp.concatenate([_dot(hb, w1_ref[e]) for e in experts], axis=1)
        up = jnp.concatenate([_dot(hb, w3_ref[e]) for e in experts], axis=1)
        cexp = jnp.concatenate(
            [jnp.broadcast_to(c4[:, e:e + 1], (rows, D_EXPERT)) for e in range(EXPERTS_PER_GROUP)], axis=1)
        a = gate * _sigmoid(gate) * up * cexp
        acc_ref[pl.ds(off, rows), :] += _dot(a.astype(BF16), w2_ref[g])

    def group(g, start):
        end = start + cnt_ref[tile * N_GROUPS + g]
        base = lax.shift_right_logical(start, SORT_SHIFT) * SORT_ALIGN
        span = end - base

        def passes(rows):
            def run():
                def body(p, carry):
                    expert_pass(rows, g, base + p * rows, start, end)
                    return carry
                lax.fori_loop(0, jnp.where(end > start, lax.div(span + (rows - 1), rows), 0), body, 0)
            return run

        size_class = sum((span > rows).astype(jnp.int32) for rows in MOE_PASS_ROWS[:-1])
        lax.switch(size_class, [passes(rows) for rows in MOE_PASS_ROWS])
        return end

    lax.fori_loop(0, N_GROUPS, group, jnp.int32(0))


def _moe_finish(x1_ref, route_ref, gt2_ref, gf_ref, acc_ref, y_ref, affine):
    tp = acc_ref.shape[0]
    route = _rows(route_ref)
    t = route.shape[0]
    dest_col = route[:, EXPERTS_PER_GROUP:EXPERTS_PER_GROUP + 1]
    p_back = jnp.where(lax.broadcasted_iota(jnp.int32, (t, tp), 1).astype(F32) == dest_col,
                       1.0, 0.0).astype(BF16)
    y = acc_ref[...]
    y_hi = y.astype(BF16)
    y_lo = (y - y_hi.astype(F32)).astype(BF16)
    y_tok = _dot(p_back, y_hi) + _dot(p_back, y_lo)
    x2 = _rows(x1_ref) + affine(y_tok, gt2_ref[...])
    y_ref[...] = (_unit_rms(x2) * gf_ref[...]).reshape(y_ref.shape)


def _mod_kernel(c_ref, w_ref, b_ref, win_ref, op_ref, os_ref, winb_ref):
    winb_ref[...] = win_ref[...].astype(BF16)
    c = c_ref[...]
    a = (c * _sigmoid(c)).astype(BF16)
    mod = _dot(a, w_ref[...].astype(BF16)) + b_ref[...]
    n_p = op_ref.shape[0]
    op_ref[...] = mod[:n_p]
    os_ref[...] = mod[n_p:]


def _const_spec(shape):
    nd = len(shape)
    return pl.BlockSpec(shape, lambda *_: (0,) * nd, pipeline_mode=pl.Buffered(1))


def _modulation(c_all, n_p, w_ada, b_ada, w_in):
    n, d = c_all.shape
    n_out = w_ada.shape[1]
    steps = n_out // MOD_BLOCK
    cast_cols = w_in.shape[1] // steps
    assert n_p % SUBLANES == 0 and cast_cols * steps == w_in.shape[1] and cast_cols % LANES == 0
    cast_spec = pl.BlockSpec((w_in.shape[0], cast_cols), lambda j: (0, j))
    return pl.pallas_call(
        _mod_kernel,
        grid=(steps,),
        in_specs=[pl.BlockSpec((n, d), lambda j: (0, 0)),
                  pl.BlockSpec((d, MOD_BLOCK), lambda j: (0, j)),
                  pl.BlockSpec((1, MOD_BLOCK), lambda j: (0, j)),
                  cast_spec],
        out_specs=[pl.BlockSpec((n_p, MOD_BLOCK), lambda j: (0, j)),
                   pl.BlockSpec((n - n_p, MOD_BLOCK), lambda j: (0, j)),
                   cast_spec],
        out_shape=[jax.ShapeDtypeStruct((n_p, n_out), F32), jax.ShapeDtypeStruct((n - n_p, n_out), F32),
                   jax.ShapeDtypeStruct(w_in.shape, BF16)],
        compiler_params=pltpu.CompilerParams(dimension_semantics=("arbitrary",),
                                             vmem_limit_bytes=VMEM_LIMIT),
        name="modulation",
    )(c_all, w_ada, b_ada, w_in)


def _mixer_weight_specs(p):
    names = ("g1", "g2", "w_in", "caw", "woa", "cbw", "cbb", "lng", "lnb", "wob", "wo", "wr", "br")
    return [p[k] for k in names], [_const_spec(p[k].shape) for k in names]


def _mixer_prompt(x, mod, p, expert_weights):
    b, s, d = x.shape
    assert b == SUBLANES, "the time-major conv layout puts one sequence on each sublane"
    steps = SEQ_TILE // b
    tiles = s // steps
    weights, wspecs = _mixer_weight_specs(p)
    tile = lambda width: pl.BlockSpec((b, steps, width), lambda c: (0, c, 0))
    whole = lambda *shape: pl.BlockSpec(shape, lambda c: (0,) * len(shape))

    def cast_spec(w):
        experts, rows, cols = w.shape
        per_expert = tiles // experts
        assert per_expert * experts == tiles and rows % (per_expert * 2 * SUBLANES) == 0
        return pl.BlockSpec((1, rows // per_expert, cols), lambda c: (c // per_expert, lax.rem(c, per_expert), 0))

    cast_specs = [cast_spec(w) for w in expert_weights]
    return pl.pallas_call(
        _mixer_prompt_kernel,
        grid=(tiles,),
        in_specs=[tile(d), whole(b, N_MOD * d)] + wspecs + cast_specs,
        out_specs=[tile(d), tile(d), tile(ROUTER_LANES),
                   pl.BlockSpec((None, 1, LANES), lambda c: (c, 0, 0)),
                   whole(CONV_A_WIDTH - 1, b, D_A), whole(CONV_B_WIDTH - 1, b, D_B)] + cast_specs,
        out_shape=[jax.ShapeDtypeStruct((b, s, d), F32),
                   jax.ShapeDtypeStruct((b, s, d), BF16),
                   jax.ShapeDtypeStruct((b, s, ROUTER_LANES), F32),
                   jax.ShapeDtypeStruct((tiles, 1, LANES), jnp.int32),
                   jax.ShapeDtypeStruct((CONV_A_WIDTH - 1, b, D_A), F32),
                   jax.ShapeDtypeStruct((CONV_B_WIDTH - 1, b, D_B), F32)]
        + [jax.ShapeDtypeStruct(w.shape, BF16) for w in expert_weights],
        scratch_shapes=[pltpu.VMEM(((CONV_A_WIDTH - 1) * b + SEQ_TILE, D_A), F32),
                        pltpu.VMEM(((CONV_B_WIDTH - 1) * b + SEQ_TILE, D_B), F32),
                        pltpu.VMEM((SEQ_TILE, D_B), F32)],
        compiler_params=pltpu.CompilerParams(dimension_semantics=("arbitrary",),
                                             vmem_limit_bytes=VMEM_LIMIT),
        name="mixer_prompt",
    )(x, mod, *weights, *expert_weights)


def _mixer_sample(x_tm, mod, sa_tm, sb_tm, p):
    n, d = x_tm.shape
    weights, wspecs = _mixer_weight_specs(p)
    full = lambda a: pl.BlockSpec(a.shape, lambda i: (0,) * a.ndim)
    whole = lambda *shape: pl.BlockSpec(shape, lambda i: (0,) * len(shape))
    return pl.pallas_call(
        _mixer_sample_kernel,
        grid=(1,),
        in_specs=[full(x_tm), full(mod)] + wspecs + [full(sa_tm), full(sb_tm)],
        out_specs=[whole(n, d), whole(n, d), whole(n, ROUTER_LANES), whole(1, LANES),
                   full(sa_tm), full(sb_tm)],
        out_shape=[jax.ShapeDtypeStruct((n, d), F32),
                   jax.ShapeDtypeStruct((n, d), BF16),
                   jax.ShapeDtypeStruct((n, ROUTER_LANES), F32),
                   jax.ShapeDtypeStruct((1, LANES), jnp.int32),
                   jax.ShapeDtypeStruct(sa_tm.shape, F32),
                   jax.ShapeDtypeStruct(sb_tm.shape, F32)],
        scratch_shapes=[pltpu.VMEM((n, D_B), BF16)],
        compiler_params=pltpu.CompilerParams(dimension_semantics=("arbitrary",),
                                             vmem_limit_bytes=VMEM_LIMIT),
        name="mixer_sample",
    )(x_tm, mod, *weights, sa_tm, sb_tm)


def _moe(counts, prompt, mod_p, sample, mod_s, p):
    b, s, d = prompt[0].shape
    n_s = mod_s.shape[0]
    steps = SEQ_TILE // b
    tiles = s // steps
    assert sample[0].shape == (SEQ_TILE, d) and SEQ_TILE % n_s == 0
    weights = [p["w1"], p["w3"], p["w2"], p["gf"]]
    tile = lambda width: pl.BlockSpec((b, steps, width), lambda c, cnt: (0, jnp.minimum(c, tiles - 1), 0))
    whole = lambda rows, width: pl.BlockSpec((rows, width), lambda c, cnt: (0, 0), pipeline_mode=pl.Buffered(1))
    gate = lambda rows: pl.BlockSpec((rows, d), lambda c, cnt: (0, N_MOD - 1), pipeline_mode=pl.Buffered(1))
    grid_spec = pltpu.PrefetchScalarGridSpec(
        num_scalar_prefetch=1,
        grid=(tiles + 1,),
        in_specs=[tile(d), tile(d), tile(ROUTER_LANES), gate(b),
                  whole(SEQ_TILE, d), whole(SEQ_TILE, d), whole(SEQ_TILE, ROUTER_LANES), gate(n_s)]
        + [_const_spec(w.shape) for w in weights],
        out_specs=[tile(d), pl.BlockSpec((SEQ_TILE, d), lambda c, cnt: (0, 0))],
        scratch_shapes=[pltpu.VMEM((SEQ_TILE, d), BF16),
                        pltpu.VMEM((SEQ_TILE, ROUTER_LANES), F32),
                        pltpu.VMEM((SEQ_TILE, d), F32)],
    )
    return pl.pallas_call(
        _moe_kernel,
        grid_spec=grid_spec,
        out_shape=[jax.ShapeDtypeStruct((b, s, d), F32), jax.ShapeDtypeStruct((SEQ_TILE, d), F32)],
        compiler_params=pltpu.CompilerParams(dimension_semantics=("arbitrary",),
                                             vmem_limit_bytes=VMEM_LIMIT),
        name="moe",
    )(counts, *prompt, mod_p, *sample, mod_s, *weights)


def kernel(x_prompt, x_sample, c_prompt, c_sample, state_conv_a, state_conv_b, w_ada, b_ada, norm1_g,
           norm2_g, w_in, conv_a_w, w_out_a, conv_b_w, conv_b_bias, ln_b_g, ln_b_b, w_out_b, w_o,
           w_group, b_group, w_expert, b_expert, w1, w3, w2, final_norm_g):
    depth = w_ada.shape[0]
    assert depth == 1, "the MoE kernel fuses the final norm, so exactly one layer is supported"
    n_p, _, d = x_prompt.shape
    n_s, steps, _ = x_sample.shape
    c_all = jnp.concatenate([c_prompt, c_sample], axis=0)
    xp = x_prompt
    xs = x_sample.transpose(1, 0, 2).reshape(steps * n_s, d)
    gf = final_norm_g.reshape(1, d)
    pa, pb, sa, sb = [], [], [], []
    for l in range(depth):
        pad = ROUTER_LANES - N_EXPERTS - N_GROUPS
        wr = jnp.pad(jnp.concatenate([w_expert[l], w_group[l]], axis=1).T, ((0, pad), (0, 0)))
        br = jnp.pad(jnp.concatenate([b_expert[l], b_group[l]]), (0, pad))
        br = jnp.broadcast_to(br[:, None], (ROUTER_LANES, LANES))
        p = {
            "g1": norm1_g[l].reshape(1, d), "g2": norm2_g[l].reshape(1, d),
            "woa": w_out_a[l].astype(BF16),
            "caw": jnp.broadcast_to(conv_a_w[l][:, None, :], (CONV_A_WIDTH, SUBLANES, D_A)),
            "cbw": jnp.broadcast_to(conv_b_w[l][:, None, :], (CONV_B_WIDTH, SUBLANES, D_B)),
            "cbb": conv_b_bias[l].reshape(1, D_B),
            "lng": ln_b_g[l].reshape(1, D_B), "lnb": ln_b_b[l].reshape(1, D_B),
            "wob": w_out_b[l].astype(BF16), "wo": w_o[l].astype(BF16), "wr": wr.astype(BF16), "br": br,
            "gf": gf,
        }
        mod_p, mod_s, p["w_in"] = _modulation(c_all, n_p, w_ada[l], b_ada[l].reshape(1, N_MOD * d), w_in[l])

        x1p, h2p, route_p, cnt_p, na_p, nb_p, w1b, w3b, w2b = _mixer_prompt(
            xp, mod_p, p, (w1[l], w3[l], w2[l]))
        p.update(w1=w1b, w3=w3b, w2=w2b.reshape(N_GROUPS, EXPERTS_PER_GROUP * D_EXPERT, d))

        sa_tm = state_conv_a[l].transpose(1, 0, 2)
        sb_tm = state_conv_b[l].transpose(1, 0, 2)
        x1s, h2s, route_s, cnt_s, na_s, nb_s = _mixer_sample(xs, mod_s, sa_tm, sb_tm, p)
        counts = jnp.concatenate([cnt_p[:, 0, :N_GROUPS].reshape(-1), cnt_s[0, :N_GROUPS]])
        xp, xs = _moe(counts, (x1p, h2p, route_p), mod_p, (x1s, h2s, route_s), mod_s, p)
        pa.append(na_p.transpose(1, 0, 2))
        pb.append(nb_p.transpose(1, 0, 2))
        sa.append(na_s.transpose(1, 0, 2))
        sb.append(nb_s.transpose(1, 0, 2))
    y_sample = xs.reshape(steps, n_s, d).transpose(1, 0, 2)
    return (xp, y_sample, jnp.stack(pa), jnp.stack(pb), jnp.stack(sa), jnp.stack(sb))
```

```python
import jax
import jax.numpy as jnp
from jax import lax
from jax.experimental import pallas as pl
from jax.experimental.pallas import tpu as pltpu

D_MODEL = 1024
D_A = 512
D_B = 512
CONV_A_WIDTH = 3
CONV_B_WIDTH = 31
N_GROUPS = 4
EXPERTS_PER_GROUP = 4
N_EXPERTS = 16
D_EXPERT = 256
N_MOD = 6
EPS = 1e-6

LANES = 128
SUBLANES = 8
ROUTER_LANES = LANES
SEQ_TILE = 512
CONV_ROWS = 64
CONV_LANES = 2 * LANES
SORT_SHIFT = 3
SORT_ALIGN = 1 << SORT_SHIFT
MOE_PASS_ROWS = (128, 144, 160, 176, 192, 224, 256)
VMEM_LIMIT = 52 * 1024 * 1024
MOD_BLOCK = 2 * D_MODEL

BF16 = jnp.bfloat16
F32 = jnp.float32

assert 2 * EXPERTS_PER_GROUP == SUBLANES and N_GROUPS <= SUBLANES and N_EXPERTS == N_GROUPS * EXPERTS_PER_GROUP


def _dot(a, b):
    return jnp.dot(a, b, preferred_element_type=F32)


def _unit_rms(x):
    return x * lax.rsqrt(jnp.mean(x * x, axis=-1, keepdims=True) + EPS)


def _sigmoid(v):
    return 0.5 * jnp.tanh(0.5 * v) + 0.5


def _affine(v, scale, shift=None):
    r = scale.shape[0]
    if r != 1 and r != v.shape[0]:
        v3 = v.reshape(v.shape[0] // r, r, v.shape[-1])
        shift3 = None if shift is None else shift[None]
        return _affine(v3, scale[None], shift3).reshape(v.shape)
    out = v * scale
    return out if shift is None else out + shift


def _per_sequence(v, scale, shift=None):
    n = scale.shape[0]
    v3 = v.reshape(n, v.shape[0] // n, v.shape[-1])
    out = v3 * scale[:, None, :]
    if shift is not None:
        out = out + shift[:, None, :]
    return out.reshape(v.shape)


def _tap(v, w8):
    return (v.reshape(v.shape[0] // SUBLANES, SUBLANES, v.shape[-1]) * w8[None]).reshape(v.shape)


def _route(logits_t):
    t = logits_t.shape[1]
    neg = -jnp.inf
    row = lax.broadcasted_iota(jnp.int32, (SUBLANES, t), 0).astype(F32)
    live = row < float(EXPERTS_PER_GROUP)
    gl = jnp.where(live, logits_t[N_EXPERTS:N_EXPERTS + SUBLANES, :], neg)
    gmax = jnp.max(gl, axis=0, keepdims=True)
    g_idx = jnp.min(jnp.where(gl == gmax, row, float(SUBLANES)), axis=0, keepdims=True)
    p_g = 1.0 / jnp.sum(jnp.exp(gl - gmax), axis=0, keepdims=True)
    el = None
    for g in reversed(range(N_GROUPS)):
        slab = logits_t[(g // 2) * SUBLANES:(g // 2 + 1) * SUBLANES, :]
        cand = slab if g % 2 == 0 else pltpu.roll(slab, SUBLANES - EXPERTS_PER_GROUP, axis=0)
        el = cand if el is None else jnp.where(g_idx == float(g), cand, el)
    el = jnp.where(live, el, neg)
    v1 = jnp.max(el, axis=0, keepdims=True)
    i1 = jnp.min(jnp.where(el == v1, row, float(SUBLANES)), axis=0, keepdims=True)
    el2 = jnp.where(row == i1, neg, el)
    v2 = jnp.max(el2, axis=0, keepdims=True)
    i2 = jnp.min(jnp.where(el2 == v2, row, float(SUBLANES)), axis=0, keepdims=True)
    e21 = jnp.exp(v2 - v1)
    wt1 = p_g / (1.0 + e21)
    wt2 = p_g * e21 / (1.0 + e21)
    comb = jnp.where(row == i1, wt1, jnp.where(row == i2, wt2, 0.0))
    onehot = jnp.where(row == g_idx, 1.0, 0.0)
    onehot_b = jnp.concatenate([onehot, jnp.zeros_like(onehot)], axis=0).astype(BF16)
    earlier = lax.broadcasted_iota(jnp.int32, (t, t), 0) < lax.broadcasted_iota(jnp.int32, (t, t), 1)
    before = _dot(onehot_b, jnp.where(earlier, 1.0, 0.0).astype(BF16))[0:SUBLANES, :]
    rank = jnp.sum(before * onehot, axis=0, keepdims=True)
    count = jnp.sum(onehot, axis=1, keepdims=True)
    row1 = lax.broadcasted_iota(jnp.int32, (SUBLANES, 1), 0)
    start = jnp.zeros((SUBLANES, 1), F32)
    for g in range(N_GROUPS - 1):
        start = start + jnp.where(row1 > g, count[g:g + 1, :], 0.0)
    dest = jnp.sum(onehot * start, axis=0, keepdims=True) + rank
    route_t = jnp.concatenate([jnp.where(live, comb, dest), jnp.broadcast_to(dest, (LANES - SUBLANES, t))],
                              axis=0)
    onehot_all = jnp.concatenate([onehot, jnp.zeros((LANES - SUBLANES, t), F32)], axis=0).astype(BF16)
    count_row = lax.dot_general(jnp.ones((2 * SUBLANES, t), BF16), onehot_all, (((1,), (1,)), ((), ())),
                                preferred_element_type=F32)[0:1, :]
    return jnp.transpose(route_t), count_row


def _ln_silu(v, bias, g, b):
    v = v + bias
    mu = jnp.mean(v, axis=-1, keepdims=True)
    d = v - mu
    var = jnp.mean(d * d, axis=-1, keepdims=True)
    y = d * lax.rsqrt(var + EPS) * g + b
    return y * _sigmoid(y)


def _mixer_tail(x, hb, mod, b_a_conv, yb_in, g2, w_in_ref, woa_ref, wob_ref, wo_ref, wr_ref, br_ref,
                x1_ref, h2_ref, route_ref, cnt_ref, affine):
    y_a = _dot(b_a_conv.astype(BF16), woa_ref[...])
    y_b = _dot(yb_in, wob_ref[...])
    o = 3 * D_A + 2 * D_B
    mg_a = _dot(hb, w_in_ref[:, o:o + D_MODEL])
    mg_b = _dot(hb, w_in_ref[:, o + D_MODEL:o + 2 * D_MODEL])
    merged = _sigmoid(mg_a) * y_a + _sigmoid(mg_b) * y_b
    gt1 = mod[:, 2 * D_MODEL:3 * D_MODEL]
    x1 = x + affine(_dot(merged.astype(BF16), wo_ref[...]), gt1)
    sh2 = mod[:, 3 * D_MODEL:4 * D_MODEL]
    sc2 = mod[:, 4 * D_MODEL:5 * D_MODEL]
    h2 = affine(_unit_rms(x1), g2 * (1.0 + sc2), sh2).astype(BF16)
    logits_t = lax.dot_general(wr_ref[...], h2, (((1,), (1,)), ((), ())), preferred_element_type=F32)
    logits_t = logits_t + jnp.tile(br_ref[...], (1, h2.shape[0] // LANES))
    x1_ref[...] = x1.reshape(x1_ref.shape)
    h2_ref[...] = h2.reshape(h2_ref.shape)
    route, count = _route(logits_t)
    route_ref[...] = route.reshape(route_ref.shape)
    cnt_ref[...] = count.astype(jnp.int32)


def _mixer_prompt_kernel(x_ref, mod_ref, g1_ref, g2_ref, w_in_ref, caw_ref, woa_ref, cbw_ref, cbb_ref,
                         lng_ref, lnb_ref, wob_ref, wo_ref, wr_ref, br_ref,
                         w1f_ref, w3f_ref, w2f_ref,
                         x1_ref, h2_ref, route_ref, cnt_ref, na_ref, nb_ref, w1b_ref, w3b_ref, w2b_ref,
                         hist_a, hist_b, conv_b_ref):
    n_seq, steps, d = x_ref.shape
    t = n_seq * steps
    rows_a = (CONV_A_WIDTH - 1) * n_seq
    rows_b = (CONV_B_WIDTH - 1) * n_seq

    @pl.when(pl.program_id(0) == 0)
    def _():
        hist_a[0:rows_a, :] = jnp.zeros((rows_a, D_A), F32)
        hist_b[0:rows_b, :] = jnp.zeros((rows_b, D_B), F32)

    def time_major(v):
        return jnp.swapaxes(v.reshape(n_seq, steps, v.shape[-1]), 0, 1).reshape(v.shape)

    def sequence_major(v):
        return jnp.swapaxes(v.reshape(steps, n_seq, v.shape[-1]), 0, 1).reshape(v.shape)

    x = x_ref[...].reshape(t, d)
    mod = mod_ref[...]
    h = _per_sequence(_unit_rms(x), g1_ref[...] * (1.0 + mod[:, D_MODEL:2 * D_MODEL]), mod[:, 0:D_MODEL])
    hb = h.astype(BF16)

    def proj(lo, width):
        return _dot(hb, w_in_ref[:, lo:lo + width])

    hist_a[rows_a:rows_a + t, :] = time_major(proj(D_A, D_A) * proj(2 * D_A, D_A))
    conv_a = None
    for k in range(CONV_A_WIDTH):
        term = _tap(hist_a[k * n_seq:k * n_seq + t, :], caw_ref[k])
        conv_a = term if conv_a is None else conv_a + term
    b_a_conv = proj(0, D_A) * sequence_major(conv_a)
    na_ref[...] = hist_a[t:t + rows_a, :].reshape(na_ref.shape)

    hist_b[rows_b:rows_b + t, :] = time_major(proj(3 * D_A, D_B) * _sigmoid(proj(3 * D_A + D_B, D_B)))
    nb_ref[...] = hist_b[t:t + rows_b, :].reshape(nb_ref.shape)
    for r0 in range(0, t, CONV_ROWS):
        for c0 in range(0, D_B, CONV_LANES):
            cols = slice(c0, c0 + CONV_LANES)
            acc = None
            for k in range(CONV_B_WIDTH):
                o = r0 + k * n_seq
                term = _tap(hist_b[o:o + CONV_ROWS, cols], cbw_ref[k][:, cols])
                acc = term if acc is None else acc + term
            conv_b_ref[r0:r0 + CONV_ROWS, cols] = acc
    yb_in = _ln_silu(sequence_major(conv_b_ref[...]), cbb_ref[...], lng_ref[...], lnb_ref[...]).astype(BF16)

    hist_a[0:rows_a, :] = hist_a[t:t + rows_a, :]
    hist_b[0:rows_b, :] = hist_b[t:t + rows_b, :]

    _mixer_tail(x, hb, mod, b_a_conv, yb_in, g2_ref[...], w_in_ref, woa_ref, wob_ref, wo_ref,
                wr_ref, br_ref, x1_ref, h2_ref, route_ref, cnt_ref, _per_sequence)

    w1b_ref[...] = w1f_ref[...].astype(BF16)
    w3b_ref[...] = w3f_ref[...].astype(BF16)
    w2b_ref[...] = w2f_ref[...].astype(BF16)


def _mixer_sample_kernel(x_ref, mod_ref, g1_ref, g2_ref, w_in_ref, caw_ref, woa_ref, cbw_ref, cbb_ref,
                         lng_ref, lnb_ref, wob_ref, wo_ref, wr_ref, br_ref, sa_ref, sb_ref,
                         x1_ref, h2_ref, route_ref, cnt_ref, na_ref, nb_ref, yb_in_ref):
    n_seq = mod_ref.shape[0]
    steps = x_ref.shape[0] // n_seq
    x = x_ref[...]
    mod = mod_ref[...]
    h = _affine(_unit_rms(x), g1_ref[...] * (1.0 + mod[:, D_MODEL:2 * D_MODEL]), mod[:, 0:D_MODEL])
    hb = h.astype(BF16)

    def proj(lo, width):
        return _dot(hb, w_in_ref[:, lo:lo + width])

    def slab(v, j):
        return v[j * n_seq:(j + 1) * n_seq, :]

    u_a = proj(D_A, D_A) * proj(2 * D_A, D_A)
    full_a = [sa_ref[j] for j in range(CONV_A_WIDTH - 1)] + [slab(u_a, j) for j in range(steps)]
    conv_a = []
    for s in range(steps):
        acc = jnp.zeros((n_seq, D_A), F32)
        for k in range(CONV_A_WIDTH):
            acc = acc + _tap(full_a[s + k], caw_ref[k])
        conv_a.append(acc)
    b_a_conv = proj(0, D_A) * jnp.concatenate(conv_a, axis=0)
    for j in range(CONV_A_WIDTH - 1):
        na_ref[j] = full_a[steps + j]

    u_b = proj(3 * D_A, D_B) * _sigmoid(proj(3 * D_A + D_B, D_B))
    hist = CONV_B_WIDTH - 1

    def full_b(j):
        return sb_ref[j] if j < hist else slab(u_b, j - hist)

    bias, lng, lnb = cbb_ref[...], lng_ref[...], lnb_ref[...]
    for s in range(steps):
        acc = jnp.zeros((n_seq, D_B), F32)
        for k in range(CONV_B_WIDTH):
            acc = acc + _tap(full_b(s + k), cbw_ref[k])
        yb_in_ref[s * n_seq:(s + 1) * n_seq, :] = _ln_silu(acc, bias, lng, lnb).astype(BF16)
    for j in range(hist):
        nb_ref[j] = full_b(steps + j)

    _mixer_tail(x, hb, mod, b_a_conv, yb_in_ref[...], g2_ref[...], w_in_ref, woa_ref, wob_ref, wo_ref,
                wr_ref, br_ref, x1_ref, h2_ref, route_ref, cnt_ref, _affine)


def _moe_kernel(cnt_ref, x1p_ref, h2p_ref, routep_ref, gt2p_ref, x1s_ref, h2s_ref, routes_ref, gt2s_ref,
                w1_ref, w3_ref, w2_ref, gf_ref, yp_ref, ys_ref, h2sort_ref, cs_ref, acc_ref):
    is_sample = pl.program_id(0) == pl.num_programs(0) - 1
    prompt = jnp.logical_not(is_sample)
    pl.when(prompt)(lambda: _moe_sort(h2p_ref, routep_ref, h2sort_ref, cs_ref, acc_ref))
    pl.when(is_sample)(lambda: _moe_sort(h2s_ref, routes_ref, h2sort_ref, cs_ref, acc_ref))
    _moe_experts(cnt_ref, w1_ref, w3_ref, w2_ref, h2sort_ref, cs_ref, acc_ref)
    pl.when(prompt)(lambda: _moe_finish(x1p_ref, routep_ref, gt2p_ref, gf_ref, acc_ref, yp_ref, _per_sequence))
    pl.when(is_sample)(lambda: _moe_finish(x1s_ref, routes_ref, gt2s_ref, gf_ref, acc_ref, ys_ref, _affine))


def _rows(ref):
    return ref[...].reshape(-1, ref.shape[-1])


def _moe_sort(h2_ref, route_ref, h2s_ref, cs_ref, acc_ref):
    tp = h2s_ref.shape[0]
    route = _rows(route_ref)
    t = route.shape[0]
    lane_i = lax.broadcasted_iota(jnp.int32, route.shape, 1)
    dest_row = jnp.transpose(route)[EXPERTS_PER_GROUP:EXPERTS_PER_GROUP + 1, :]
    slot = lax.broadcasted_iota(jnp.int32, (tp, t), 0).astype(F32)
    p_sort = jnp.where(slot == dest_row, 1.0, 0.0).astype(BF16)
    h2s_ref[...] = _dot(p_sort, _rows(h2_ref))
    comb = jnp.where(lane_i < EXPERTS_PER_GROUP, route, 0.0)
    c_hi = comb.astype(BF16).astype(F32)
    c_mid = (comb - c_hi).astype(BF16).astype(F32)
    c_lo = comb - c_hi - c_mid
    packed = (c_hi + pltpu.roll(c_mid, EXPERTS_PER_GROUP, axis=1)
              + pltpu.roll(c_lo, 2 * EXPERTS_PER_GROUP, axis=1)).astype(BF16)
    moved = _dot(p_sort, packed)
    cs_ref[...] = (moved + pltpu.roll(moved, LANES - EXPERTS_PER_GROUP, axis=1)
                   + pltpu.roll(moved, LANES - 2 * EXPERTS_PER_GROUP, axis=1))
    acc_ref[...] = jnp.zeros(acc_ref.shape, F32)


def _moe_experts(cnt_ref, w1_ref, w3_ref, w2_ref, h2s_ref, cs_ref, acc_ref):
    tp = h2s_ref.shape[0]
    tile = pl.program_id(0)

    def expert_pass(rows, g, window, lo, end):
        off = pl.multiple_of(jnp.minimum(window, tp - rows), SORT_ALIGN)
        row_id = off + lax.broadcasted_iota(jnp.int32, (rows, 1), 0)
        valid = (row_id >= jnp.maximum(lo, window)) & (row_id < end)
        hb = h2s_ref[pl.ds(off, rows), :].astype(BF16)
        c4 = jnp.where(valid, cs_ref[pl.ds(off, rows), :], 0.0)
        experts = [g * EXPERTS_PER_GROUP + e for e in range(EXPERTS_PER_GROUP)]
        gate = jnp.concatenate([_dot(hb, w1_ref[e]) for e in experts], axis=1)
        up = jnp.concatenate([_dot(hb, w3_ref[e]) for e in experts], axis=1)
        cexp = jnp.concatenate(
            [jnp.broadcast_to(c4[:, e:e + 1], (rows, D_EXPERT)) for e in range(EXPERTS_PER_GROUP)], axis=1)
        a = gate * _sigmoid(gate) * up * cexp
        acc_ref[pl.ds(off, rows), :] += _dot(a.astype(BF16), w2_ref[g])

    def group(g, start):
        end = start + cnt_ref[tile * N_GROUPS + g]
        base = lax.shift_right_logical(start, SORT_SHIFT) * SORT_ALIGN
        span = end - base

        def passes(rows):
            def run():
                def body(p, carry):
                    expert_pass(rows, g, base + p * rows, start, end)
                    return carry
                lax.fori_loop(0, jnp.where(end > start, lax.div(span + (rows - 1), rows), 0), body, 0)
            return run

        size_class = sum((span > rows).astype(jnp.int32) for rows in MOE_PASS_ROWS[:-1])
        lax.switch(size_class, [passes(rows) for rows in MOE_PASS_ROWS])
        return end

    lax.fori_loop(0, N_GROUPS, group, jnp.int32(0))


def _moe_finish(x1_ref, route_ref, gt2_ref, gf_ref, acc_ref, y_ref, affine):
    tp = acc_ref.shape[0]
    route = _rows(route_ref)
    t = route.shape[0]
    dest_col = route[:, EXPERTS_PER_GROUP:EXPERTS_PER_GROUP + 1]
    p_back = jnp.where(lax.broadcasted_iota(jnp.int32, (t, tp), 1).astype(F32) == dest_col,
                       1.0, 0.0).astype(BF16)
    y = acc_ref[...]
    y_hi = y.astype(BF16)
    y_lo = (y - y_hi.astype(F32)).astype(BF16)
    y_tok = _dot(p_back, y_hi) + _dot(p_back, y_lo)
    x2 = _rows(x1_ref) + affine(y_tok, gt2_ref[...])
    y_ref[...] = (_unit_rms(x2) * gf_ref[...]).reshape(y_ref.shape)


def _mod_kernel(c_ref, w_ref, b_ref, win_ref, op_ref, os_ref, winb_ref):
    winb_ref[...] = win_ref[...].astype(BF16)
    c = c_ref[...]
    a = (c * _sigmoid(c)).astype(BF16)
    mod = _dot(a, w_ref[...].astype(BF16)) + b_ref[...]
    n_p = op_ref.shape[0]
    op_ref[...] = mod[:n_p]
    os_ref[...] = mod[n_p:]


def _const_spec(shape):
    nd = len(shape)
    return pl.BlockSpec(shape, lambda *_: (0,) * nd, pipeline_mode=pl.Buffered(1))


def _modulation(c_all, n_p, w_ada, b_ada, w_in):
    n, d = c_all.shape
    n_out = w_ada.shape[1]
    steps = n_out // MOD_BLOCK
    cast_cols = w_in.shape[1] // steps
    assert n_p % SUBLANES == 0 and cast_cols * steps == w_in.shape[1] and cast_cols % LANES == 0
    cast_spec = pl.BlockSpec((w_in.shape[0], cast_cols), lambda j: (0, j))
    return pl.pallas_call(
        _mod_kernel,
        grid=(steps,),
        in_specs=[pl.BlockSpec((n, d), lambda j: (0, 0)),
                  pl.BlockSpec((d, MOD_BLOCK), lambda j: (0, j)),
                  pl.BlockSpec((1, MOD_BLOCK), lambda j: (0, j)),
                  cast_spec],
        out_specs=[pl.BlockSpec((n_p, MOD_BLOCK), lambda j: (0, j)),
                   pl.BlockSpec((n - n_p, MOD_BLOCK), lambda j: (0, j)),
                   cast_spec],
        out_shape=[jax.ShapeDtypeStruct((n_p, n_out), F32), jax.ShapeDtypeStruct((n - n_p, n_out), F32),
                   jax.ShapeDtypeStruct(w_in.shape, BF16)],
        compiler_params=pltpu.CompilerParams(dimension_semantics=("arbitrary",),
                                             vmem_limit_bytes=VMEM_LIMIT),
        name="modulation",
    )(c_all, w_ada, b_ada, w_in)


def _mixer_weight_specs(p):
    names = ("g1", "g2", "w_in", "caw", "woa", "cbw", "cbb", "lng", "lnb", "wob", "wo", "wr", "br")
    return [p[k] for k in names], [_const_spec(p[k].shape) for k in names]


def _mixer_prompt(x, mod, p, expert_weights):
    b, s, d = x.shape
    assert b == SUBLANES, "the time-major conv layout puts one sequence on each sublane"
    steps = SEQ_TILE // b
    tiles = s // steps
    weights, wspecs = _mixer_weight_specs(p)
    tile = lambda width: pl.BlockSpec((b, steps, width), lambda c: (0, c, 0))
    whole = lambda *shape: pl.BlockSpec(shape, lambda c: (0,) * len(shape))

    def cast_spec(w):
        experts, rows, cols = w.shape
        per_expert = tiles // experts
        assert per_expert * experts == tiles and rows % (per_expert * 2 * SUBLANES) == 0
        return pl.BlockSpec((1, rows // per_expert, cols), lambda c: (c // per_expert, lax.rem(c, per_expert), 0))

    cast_specs = [cast_spec(w) for w in expert_weights]
    return pl.pallas_call(
        _mixer_prompt_kernel,
        grid=(tiles,),
        in_specs=[tile(d), whole(b, N_MOD * d)] + wspecs + cast_specs,
        out_specs=[tile(d), tile(d), tile(ROUTER_LANES),
                   pl.BlockSpec((None, 1, LANES), lambda c: (c, 0, 0)),
                   whole(CONV_A_WIDTH - 1, b, D_A), whole(CONV_B_WIDTH - 1, b, D_B)] + cast_specs,
        out_shape=[jax.ShapeDtypeStruct((b, s, d), F32),
                   jax.ShapeDtypeStruct((b, s, d), BF16),
                   jax.ShapeDtypeStruct((b, s, ROUTER_LANES), F32),
                   jax.ShapeDtypeStruct((tiles, 1, LANES), jnp.int32),
                   jax.ShapeDtypeStruct((CONV_A_WIDTH - 1, b, D_A), F32),
                   jax.ShapeDtypeStruct((CONV_B_WIDTH - 1, b, D_B), F32)]
        + [jax.ShapeDtypeStruct(w.shape, BF16) for w in expert_weights],
        scratch_shapes=[pltpu.VMEM(((CONV_A_WIDTH - 1) * b + SEQ_TILE, D_A), F32),
                        pltpu.VMEM(((CONV_B_WIDTH - 1) * b + SEQ_TILE, D_B), F32),
                        pltpu.VMEM((SEQ_TILE, D_B), F32)],
        compiler_params=pltpu.CompilerParams(dimension_semantics=("arbitrary",),
                                             vmem_limit_bytes=VMEM_LIMIT),
        name="mixer_prompt",
    )(x, mod, *weights, *expert_weights)


def _mixer_sample(x_tm, mod, sa_tm, sb_tm, p):
    n, d = x_tm.shape
    weights, wspecs = _mixer_weight_specs(p)
    full = lambda a: pl.BlockSpec(a.shape, lambda i: (0,) * a.ndim)
    whole = lambda *shape: pl.BlockSpec(shape, lambda i: (0,) * len(shape))
    return pl.pallas_call(
        _mixer_sample_kernel,
        grid=(1,),
        in_specs=[full(x_tm), full(mod)] + wspecs + [full(sa_tm), full(sb_tm)],
        out_specs=[whole(n, d), whole(n, d), whole(n, ROUTER_LANES), whole(1, LANES),
                   full(sa_tm), full(sb_tm)],
        out_shape=[jax.ShapeDtypeStruct((n, d), F32),
                   jax.ShapeDtypeStruct((n, d), BF16),
                   jax.ShapeDtypeStruct((n, ROUTER_LANES), F32),
                   jax.ShapeDtypeStruct((1, LANES), jnp.int32),
                   jax.ShapeDtypeStruct(sa_tm.shape, F32),
                   jax.ShapeDtypeStruct(sb_tm.shape, F32)],
        scratch_shapes=[pltpu.VMEM((n, D_B), BF16)],
        compiler_params=pltpu.CompilerParams(dimension_semantics=("arbitrary",),
                                             vmem_limit_bytes=VMEM_LIMIT),
        name="mixer_sample",
    )(x_tm, mod, *weights, sa_tm, sb_tm)


def _moe(counts, prompt, mod_p, sample, mod_s, p):
    b, s, d = prompt[0].shape
    n_s = mod_s.shape[0]
    steps = SEQ_TILE // b
    tiles = s // steps
    assert sample[0].shape == (SEQ_TILE, d) and SEQ_TILE % n_s == 0
    weights = [p["w1"], p["w3"], p["w2"], p["gf"]]
    tile = lambda width: pl.BlockSpec((b, steps, width), lambda c, cnt: (0, jnp.minimum(c, tiles - 1), 0))
    whole = lambda rows, width: pl.BlockSpec((rows, width), lambda c, cnt: (0, 0), pipeline_mode=pl.Buffered(1))
    gate = lambda rows: pl.BlockSpec((rows, d), lambda c, cnt: (0, N_MOD - 1), pipeline_mode=pl.Buffered(1))
    grid_spec = pltpu.PrefetchScalarGridSpec(
        num_scalar_prefetch=1,
        grid=(tiles + 1,),
        in_specs=[tile(d), tile(d), tile(ROUTER_LANES), gate(b),
                  whole(SEQ_TILE, d), whole(SEQ_TILE, d), whole(SEQ_TILE, ROUTER_LANES), gate(n_s)]
        + [_const_spec(w.shape) for w in weights],
        out_specs=[tile(d), pl.BlockSpec((SEQ_TILE, d), lambda c, cnt: (0, 0))],
        scratch_shapes=[pltpu.VMEM((SEQ_TILE, d), F32),
                        pltpu.VMEM((SEQ_TILE, ROUTER_LANES), F32),
                        pltpu.VMEM((SEQ_TILE, d), F32)],
    )
    return pl.pallas_call(
        _moe_kernel,
        grid_spec=grid_spec,
        out_shape=[jax.ShapeDtypeStruct((b, s, d), F32), jax.ShapeDtypeStruct((SEQ_TILE, d), F32)],
        compiler_params=pltpu.CompilerParams(dimension_semantics=("arbitrary",),
                                             vmem_limit_bytes=VMEM_LIMIT),
        name="moe",
    )(counts, *prompt, mod_p, *sample, mod_s, *weights)


def kernel(x_prompt, x_sample, c_prompt, c_sample, state_conv_a, state_conv_b, w_ada, b_ada, norm1_g,
           norm2_g, w_in, conv_a_w, w_out_a, conv_b_w, conv_b_bias, ln_b_g, ln_b_b, w_out_b, w_o,
           w_group, b_group, w_expert, b_expert, w1, w3, w2, final_norm_g):
    depth = w_ada.shape[0]
    assert depth == 1, "the MoE kernel fuses the final norm, so exactly one layer is supported"
    n_p, _, d = x_prompt.shape
    n_s, steps, _ = x_sample.shape
    c_all = jnp.concatenate([c_prompt, c_sample], axis=0)
    xp = x_prompt
    xs = x_sample.transpose(1, 0, 2).reshape(steps * n_s, d)
    gf = final_norm_g.reshape(1, d)
    pa, pb, sa, sb = [], [], [], []
    for l in range(depth):
        pad = ROUTER_LANES - N_EXPERTS - N_GROUPS
        wr = jnp.pad(jnp.concatenate([w_expert[l], w_group[l]], axis=1).T, ((0, pad), (0, 0)))
        br = jnp.pad(jnp.concatenate([b_expert[l], b_group[l]]), (0, pad))
        br = jnp.broadcast_to(br[:, None], (ROUTER_LANES, LANES))
        p = {
            "g1": norm1_g[l].reshape(1, d), "g2": norm2_g[l].reshape(1, d),
            "woa": w_out_a[l].astype(BF16),
            "caw": jnp.broadcast_to(conv_a_w[l][:, None, :], (CONV_A_WIDTH, SUBLANES, D_A)),
            "cbw": jnp.broadcast_to(conv_b_w[l][:, None, :], (CONV_B_WIDTH, SUBLANES, D_B)),
            "cbb": conv_b_bias[l].reshape(1, D_B),
            "lng": ln_b_g[l].reshape(1, D_B), "lnb": ln_b_b[l].reshape(1, D_B),
            "wob": w_out_b[l].astype(BF16), "wo": w_o[l].astype(BF16), "wr": wr.astype(BF16), "br": br,
            "gf": gf,
        }
        mod_p, mod_s, p["w_in"] = _modulation(c_all, n_p, w_ada[l], b_ada[l].reshape(1, N_MOD * d), w_in[l])

        x1p, h2p, route_p, cnt_p, na_p, nb_p, w1b, w3b, w2b = _mixer_prompt(
            xp, mod_p, p, (w1[l], w3[l], w2[l]))
        p.update(w1=w1b, w3=w3b, w2=w2b.reshape(N_GROUPS, EXPERTS_PER_GROUP * D_EXPERT, d))

        sa_tm = state_conv_a[l].transpose(1, 0, 2)
        sb_tm = state_conv_b[l].transpose(1, 0, 2)
        x1s, h2s, route_s, cnt_s, na_s, nb_s = _mixer_sample(xs, mod_s, sa_tm, sb_tm, p)
        counts = jnp.concatenate([cnt_p[:, 0, :N_GROUPS].reshape(-1), cnt_s[0, :N_GROUPS]])
        xp, xs = _moe(counts, (x1p, h2p, route_p), mod_p, (x1s, h2s, route_s), mod_s, p)
        pa.append(na_p.transpose(1, 0, 2))
        pb.append(nb_p.transpose(1, 0, 2))
        sa.append(na_s.transpose(1, 0, 2))
        sb.append(nb_s.transpose(1, 0, 2))
    y_sample = xs.reshape(steps, n_s, d).transpose(1, 0, 2)
    return (xp, y_sample, jnp.stack(pa), jnp.stack(pb), jnp.stack(sa), jnp.stack(sb))
```

```python
import jax
import jax.numpy as jnp
from jax import lax
from jax.experimental import pallas as pl
from jax.experimental.pallas import tpu as pltpu

D_MODEL = 1024
D_A = 512
D_B = 512
CONV_A_WIDTH = 3
CONV_B_WIDTH = 31
N_GROUPS = 4
EXPERTS_PER_GROUP = 4
N_EXPERTS = 16
D_EXPERT = 256
N_MOD = 6
EPS = 1e-6

LANES = 128
SUBLANES = 8
ROUTER_LANES = LANES
SEQ_TILE = 512
CONV_ROWS = 64
CONV_LANES = 2 * LANES
SORT_SHIFT = 3
SORT_ALIGN = 1 << SORT_SHIFT
MOE_PASS_ROWS = (128, 144, 160, 176, 192, 224, 256)
VMEM_LIMIT = 52 * 1024 * 1024
MOD_BLOCK = 2 * D_MODEL

BF16 = jnp.bfloat16
F32 = jnp.float32

assert 2 * EXPERTS_PER_GROUP == SUBLANES and N_GROUPS <= SUBLANES and N_EXPERTS == N_GROUPS * EXPERTS_PER_GROUP


def _dot(a, b):
    return jnp.dot(a, b, preferred_element_type=F32)


def _unit_rms(x):
    return x * lax.rsqrt(jnp.mean(x * x, axis=-1, keepdims=True) + EPS)


def _sigmoid(v):
    return 0.5 * jnp.tanh(0.5 * v) + 0.5


def _affine(v, scale, shift=None):
    r = scale.shape[0]
    if r != 1 and r != v.shape[0]:
        v3 = v.reshape(v.shape[0] // r, r, v.shape[-1])
        shift3 = None if shift is None else shift[None]
        return _affine(v3, scale[None], shift3).reshape(v.shape)
    out = v * scale
    return out if shift is None else out + shift


def _per_sequence(v, scale, shift=None):
    n = scale.shape[0]
    v3 = v.reshape(n, v.shape[0] // n, v.shape[-1])
    out = v3 * scale[:, None, :]
    if shift is not None:
        out = out + shift[:, None, :]
    return out.reshape(v.shape)


def _tap(v, w8):
    return (v.reshape(v.shape[0] // SUBLANES, SUBLANES, v.shape[-1]) * w8[None]).reshape(v.shape)


def _route(logits_t):
    t = logits_t.shape[1]
    neg = -jnp.inf
    row = lax.broadcasted_iota(jnp.int32, (SUBLANES, t), 0).astype(F32)
    live = row < float(EXPERTS_PER_GROUP)
    gl = jnp.where(live, logits_t[N_EXPERTS:N_EXPERTS + SUBLANES, :], neg)
    gmax = jnp.max(gl, axis=0, keepdims=True)
    g_idx = jnp.min(jnp.where(gl == gmax, row, float(SUBLANES)), axis=0, keepdims=True)
    p_g = 1.0 / jnp.sum(jnp.exp(gl - gmax), axis=0, keepdims=True)
    el = None
    for g in reversed(range(N_GROUPS)):
        slab = logits_t[(g // 2) * SUBLANES:(g // 2 + 1) * SUBLANES, :]
        cand = slab if g % 2 == 0 else pltpu.roll(slab, SUBLANES - EXPERTS_PER_GROUP, axis=0)
        el = cand if el is None else jnp.where(g_idx == float(g), cand, el)
    el = jnp.where(live, el, neg)
    v1 = jnp.max(el, axis=0, keepdims=True)
    i1 = jnp.min(jnp.where(el == v1, row, float(SUBLANES)), axis=0, keepdims=True)
    el2 = jnp.where(row == i1, neg, el)
    v2 = jnp.max(el2, axis=0, keepdims=True)
    i2 = jnp.min(jnp.where(el2 == v2, row, float(SUBLANES)), axis=0, keepdims=True)
    e21 = jnp.exp(v2 - v1)
    wt1 = p_g / (1.0 + e21)
    wt2 = p_g * e21 / (1.0 + e21)
    comb = jnp.where(row == i1, wt1, jnp.where(row == i2, wt2, 0.0))
    onehot = jnp.where(row == g_idx, 1.0, 0.0)
    onehot_b = jnp.concatenate([onehot, jnp.zeros_like(onehot)], axis=0).astype(BF16)
    earlier = lax.broadcasted_iota(jnp.int32, (t, t), 0) < lax.broadcasted_iota(jnp.int32, (t, t), 1)
    before = _dot(onehot_b, jnp.where(earlier, 1.0, 0.0).astype(BF16))[0:SUBLANES, :]
    rank = jnp.sum(before * onehot, axis=0, keepdims=True)
    count = jnp.sum(onehot, axis=1, keepdims=True)
    row1 = lax.broadcasted_iota(jnp.int32, (SUBLANES, 1), 0)
    start = jnp.zeros((SUBLANES, 1), F32)
    for g in range(N_GROUPS - 1):
        start = start + jnp.where(row1 > g, count[g:g + 1, :], 0.0)
    dest = jnp.sum(onehot * start, axis=0, keepdims=True) + rank
    route_t = jnp.concatenate([jnp.where(live, comb, dest), jnp.broadcast_to(dest, (LANES - SUBLANES, t))],
                              axis=0)
    onehot_all = jnp.concatenate([onehot, jnp.zeros((LANES - SUBLANES, t), F32)], axis=0).astype(BF16)
    count_row = lax.dot_general(jnp.ones((2 * SUBLANES, t), BF16), onehot_all, (((1,), (1,)), ((), ())),
                                preferred_element_type=F32)[0:1, :]
    return jnp.transpose(route_t), count_row


def _ln_silu(v, bias, g, b):
    v = v + bias
    mu = jnp.mean(v, axis=-1, keepdims=True)
    d = v - mu
    var = jnp.mean(d * d, axis=-1, keepdims=True)
    y = d * lax.rsqrt(var + EPS) * g + b
    return y * _sigmoid(y)


def _mixer_tail(x, hb, mod, b_a_conv, yb_in, g2, w_in_ref, woa_ref, wob_ref, wo_ref, wr_ref, br_ref,
                x1_ref, h2_ref, route_ref, cnt_ref, affine):
    y_a = _dot(b_a_conv.astype(BF16), woa_ref[...])
    y_b = _dot(yb_in, wob_ref[...])
    o = 3 * D_A + 2 * D_B
    mg_a = _dot(hb, w_in_ref[:, o:o + D_MODEL])
    mg_b = _dot(hb, w_in_ref[:, o + D_MODEL:o + 2 * D_MODEL])
    merged = _sigmoid(mg_a) * y_a + _sigmoid(mg_b) * y_b
    gt1 = mod[:, 2 * D_MODEL:3 * D_MODEL]
    x1 = x + affine(_dot(merged.astype(BF16), wo_ref[...]), gt1)
    sh2 = mod[:, 3 * D_MODEL:4 * D_MODEL]
    sc2 = mod[:, 4 * D_MODEL:5 * D_MODEL]
    h2 = affine(_unit_rms(x1), g2 * (1.0 + sc2), sh2).astype(BF16)
    logits_t = lax.dot_general(wr_ref[...], h2, (((1,), (1,)), ((), ())), preferred_element_type=F32)
    logits_t = logits_t + jnp.tile(br_ref[...], (1, h2.shape[0] // LANES))
    x1_ref[...] = x1.reshape(x1_ref.shape)
    h2_ref[...] = h2.reshape(h2_ref.shape)
    route, count = _route(logits_t)
    route_ref[...] = route.reshape(route_ref.shape)
    cnt_ref[...] = count.astype(jnp.int32)


def _mixer_prompt_kernel(x_ref, mod_ref, g1_ref, g2_ref, w_in_ref, caw_ref, woa_ref, cbw_ref, cbb_ref,
                         lng_ref, lnb_ref, wob_ref, wo_ref, wr_ref, br_ref,
                         w1f_ref, w3f_ref, w2f_ref,
                         x1_ref, h2_ref, route_ref, cnt_ref, na_ref, nb_ref, w1b_ref, w3b_ref, w2b_ref,
                         hist_a, hist_b, conv_b_ref):
    n_seq, steps, d = x_ref.shape
    t = n_seq * steps
    rows_a = (CONV_A_WIDTH - 1) * n_seq
    rows_b = (CONV_B_WIDTH - 1) * n_seq

    @pl.when(pl.program_id(0) == 0)
    def _():
        hist_a[0:rows_a, :] = jnp.zeros((rows_a, D_A), F32)
        hist_b[0:rows_b, :] = jnp.zeros((rows_b, D_B), F32)

    def time_major(v):
        return jnp.swapaxes(v.reshape(n_seq, steps, v.shape[-1]), 0, 1).reshape(v.shape)

    def sequence_major(v):
        return jnp.swapaxes(v.reshape(steps, n_seq, v.shape[-1]), 0, 1).reshape(v.shape)

    x = x_ref[...].reshape(t, d)
    mod = mod_ref[...]
    h = _per_sequence(_unit_rms(x), g1_ref[...] * (1.0 + mod[:, D_MODEL:2 * D_MODEL]), mod[:, 0:D_MODEL])
    hb = h.astype(BF16)

    def proj(lo, width):
        return _dot(hb, w_in_ref[:, lo:lo + width])

    hist_a[rows_a:rows_a + t, :] = time_major(proj(D_A, D_A) * proj(2 * D_A, D_A))
    conv_a = None
    for k in range(CONV_A_WIDTH):
        term = _tap(hist_a[k * n_seq:k * n_seq + t, :], caw_ref[k])
        conv_a = term if conv_a is None else conv_a + term
    b_a_conv = proj(0, D_A) * sequence_major(conv_a)
    na_ref[...] = hist_a[t:t + rows_a, :].reshape(na_ref.shape)

    hist_b[rows_b:rows_b + t, :] = time_major(proj(3 * D_A, D_B) * _sigmoid(proj(3 * D_A + D_B, D_B)))
    nb_ref[...] = hist_b[t:t + rows_b, :].reshape(nb_ref.shape)
    for r0 in range(0, t, CONV_ROWS):
        for c0 in range(0, D_B, CONV_LANES):
            cols = slice(c0, c0 + CONV_LANES)
            acc = None
            for k in range(CONV_B_WIDTH):
                o = r0 + k * n_seq
                term = _tap(hist_b[o:o + CONV_ROWS, cols], cbw_ref[k][:, cols])
                acc = term if acc is None else acc + term
            conv_b_ref[r0:r0 + CONV_ROWS, cols] = acc
    yb_in = _ln_silu(sequence_major(conv_b_ref[...]), cbb_ref[...], lng_ref[...], lnb_ref[...]).astype(BF16)

    hist_a[0:rows_a, :] = hist_a[t:t + rows_a, :]
    hist_b[0:rows_b, :] = hist_b[t:t + rows_b, :]

    _mixer_tail(x, hb, mod, b_a_conv, yb_in, g2_ref[...], w_in_ref, woa_ref, wob_ref, wo_ref,
                wr_ref, br_ref, x1_ref, h2_ref, route_ref, cnt_ref, _per_sequence)

    w1b_ref[...] = w1f_ref[...].astype(BF16)
    w3b_ref[...] = w3f_ref[...].astype(BF16)
    w2b_ref[...] = w2f_ref[...].astype(BF16)


def _mixer_sample_kernel(x_ref, mod_ref, g1_ref, g2_ref, w_in_ref, caw_ref, woa_ref, cbw_ref, cbb_ref,
                         lng_ref, lnb_ref, wob_ref, wo_ref, wr_ref, br_ref, sa_ref, sb_ref,
                         x1_ref, h2_ref, route_ref, cnt_ref, na_ref, nb_ref, yb_in_ref):
    n_seq = mod_ref.shape[0]
    steps = x_ref.shape[0] // n_seq
    x = x_ref[...]
    mod = mod_ref[...]
    h = _affine(_unit_rms(x), g1_ref[...] * (1.0 + mod[:, D_MODEL:2 * D_MODEL]), mod[:, 0:D_MODEL])
    hb = h.astype(BF16)

    def proj(lo, width):
        return _dot(hb, w_in_ref[:, lo:lo + width])

    def slab(v, j):
        return v[j * n_seq:(j + 1) * n_seq, :]

    u_a = proj(D_A, D_A) * proj(2 * D_A, D_A)
    full_a = [sa_ref[j] for j in range(CONV_A_WIDTH - 1)] + [slab(u_a, j) for j in range(steps)]
    conv_a = []
    for s in range(steps):
        acc = jnp.zeros((n_seq, D_A), F32)
        for k in range(CONV_A_WIDTH):
            acc = acc + _tap(full_a[s + k], caw_ref[k])
        conv_a.append(acc)
    b_a_conv = proj(0, D_A) * jnp.concatenate(conv_a, axis=0)
    for j in range(CONV_A_WIDTH - 1):
        na_ref[j] = full_a[steps + j]

    u_b = proj(3 * D_A, D_B) * _sigmoid(proj(3 * D_A + D_B, D_B))
    hist = CONV_B_WIDTH - 1

    def full_b(j):
        return sb_ref[j] if j < hist else slab(u_b, j - hist)

    bias, lng, lnb = cbb_ref[...], lng_ref[...], lnb_ref[...]
    for s in range(steps):
        acc = jnp.zeros((n_seq, D_B), F32)
        for k in range(CONV_B_WIDTH):
            acc = acc + _tap(full_b(s + k), cbw_ref[k])
        yb_in_ref[s * n_seq:(s + 1) * n_seq, :] = _ln_silu(acc, bias, lng, lnb).astype(BF16)
    for j in range(hist):
        nb_ref[j] = full_b(steps + j)

    _mixer_tail(x, hb, mod, b_a_conv, yb_in_ref[...], g2_ref[...], w_in_ref, woa_ref, wob_ref, wo_ref,
                wr_ref, br_ref, x1_ref, h2_ref, route_ref, cnt_ref, _affine)


def _moe_kernel(cnt_ref, x1p_ref, h2p_ref, routep_ref, gt2p_ref, x1s_ref, h2s_ref, routes_ref, gt2s_ref,
                w1_hbm, w3_hbm, w2_hbm, gf_ref, yp_ref, ys_ref, h2sort_ref, cs_ref, acc_ref,
                w1_ref, w3_ref, w2_ref, sem):
    def weight_copies(g):
        experts = pl.ds(g * EXPERTS_PER_GROUP, EXPERTS_PER_GROUP)
        return (pltpu.make_async_copy(w1_hbm.at[experts], w1_ref.at[experts], sem.at[0, g]),
                pltpu.make_async_copy(w3_hbm.at[experts], w3_ref.at[experts], sem.at[1, g]),
                pltpu.make_async_copy(w2_hbm.at[g], w2_ref.at[g], sem.at[2, g]))

    first = pl.program_id(0) == 0

    @pl.when(first)
    def _():
        for g in range(N_GROUPS):
            for copy in weight_copies(g):
                copy.start()

    def await_weights(g):
        @pl.when(first)
        def _():
            for copy in weight_copies(g):
                copy.wait()

    is_sample = pl.program_id(0) == pl.num_programs(0) - 1
    prompt = jnp.logical_not(is_sample)
    pl.when(prompt)(lambda: _moe_sort(h2p_ref, routep_ref, h2sort_ref, cs_ref, acc_ref))
    pl.when(is_sample)(lambda: _moe_sort(h2s_ref, routes_ref, h2sort_ref, cs_ref, acc_ref))
    _moe_experts(cnt_ref, w1_ref, w3_ref, w2_ref, h2sort_ref, cs_ref, acc_ref, await_weights)
    pl.when(prompt)(lambda: _moe_finish(x1p_ref, routep_ref, gt2p_ref, gf_ref, acc_ref, yp_ref, _per_sequence))
    pl.when(is_sample)(lambda: _moe_finish(x1s_ref, routes_ref, gt2s_ref, gf_ref, acc_ref, ys_ref, _affine))


def _rows(ref):
    return ref[...].reshape(-1, ref.shape[-1])


def _moe_sort(h2_ref, route_ref, h2s_ref, cs_ref, acc_ref):
    tp = h2s_ref.shape[0]
    route = _rows(route_ref)
    t = route.shape[0]
    lane_i = lax.broadcasted_iota(jnp.int32, route.shape, 1)
    dest_row = jnp.transpose(route)[EXPERTS_PER_GROUP:EXPERTS_PER_GROUP + 1, :]
    slot = lax.broadcasted_iota(jnp.int32, (tp, t), 0).astype(F32)
    p_sort = jnp.where(slot == dest_row, 1.0, 0.0).astype(BF16)
    h2s_ref[...] = _dot(p_sort, _rows(h2_ref))
    comb = jnp.where(lane_i < EXPERTS_PER_GROUP, route, 0.0)
    c_hi = comb.astype(BF16).astype(F32)
    c_mid = (comb - c_hi).astype(BF16).astype(F32)
    c_lo = comb - c_hi - c_mid
    packed = (c_hi + pltpu.roll(c_mid, EXPERTS_PER_GROUP, axis=1)
              + pltpu.roll(c_lo, 2 * EXPERTS_PER_GROUP, axis=1)).astype(BF16)
    moved = _dot(p_sort, packed)
    cs_ref[...] = (moved + pltpu.roll(moved, LANES - EXPERTS_PER_GROUP, axis=1)
                   + pltpu.roll(moved, LANES - 2 * EXPERTS_PER_GROUP, axis=1))
    acc_ref[...] = jnp.zeros(acc_ref.shape, F32)


def _moe_experts(cnt_ref, w1_ref, w3_ref, w2_ref, h2s_ref, cs_ref, acc_ref, await_weights):
    tp = h2s_ref.shape[0]
    tile = pl.program_id(0)

    def expert_pass(rows, g, window, lo, end):
        off = pl.multiple_of(jnp.minimum(window, tp - rows), SORT_ALIGN)
        row_id = off + lax.broadcasted_iota(jnp.int32, (rows, 1), 0)
        valid = (row_id >= jnp.maximum(lo, window)) & (row_id < end)
        hb = h2s_ref[pl.ds(off, rows), :].astype(BF16)
        c4 = jnp.where(valid, cs_ref[pl.ds(off, rows), :], 0.0)
        experts = [g * EXPERTS_PER_GROUP + e for e in range(EXPERTS_PER_GROUP)]
        gate = jnp.concatenate([_dot(hb, w1_ref[e]) for e in experts], axis=1)
        up = jnp.concatenate([_dot(hb, w3_ref[e]) for e in experts], axis=1)
        cexp = jnp.concatenate(
            [jnp.broadcast_to(c4[:, e:e + 1], (rows, D_EXPERT)) for e in range(EXPERTS_PER_GROUP)], axis=1)
        a = gate * _sigmoid(gate) * up * cexp
        acc_ref[pl.ds(off, rows), :] += _dot(a.astype(BF16), w2_ref[g])

    def group(g, start):
        await_weights(g)
        end = start + cnt_ref[tile * N_GROUPS + g]
        base = lax.shift_right_logical(start, SORT_SHIFT) * SORT_ALIGN
        span = end - base

        def passes(rows):
            def run():
                def body(p, carry):
                    expert_pass(rows, g, base + p * rows, start, end)
                    return carry
                lax.fori_loop(0, jnp.where(end > start, lax.div(span + (rows - 1), rows), 0), body, 0)
            return run

        size_class = sum((span > rows).astype(jnp.int32) for rows in MOE_PASS_ROWS[:-1])
        lax.switch(size_class, [passes(rows) for rows in MOE_PASS_ROWS])
        return end

    lax.fori_loop(0, N_GROUPS, group, jnp.int32(0))


def _moe_finish(x1_ref, route_ref, gt2_ref, gf_ref, acc_ref, y_ref, affine):
    tp = acc_ref.shape[0]
    route = _rows(route_ref)
    t = route.shape[0]
    dest_col = route[:, EXPERTS_PER_GROUP:EXPERTS_PER_GROUP + 1]
    p_back = jnp.where(lax.broadcasted_iota(jnp.int32, (t, tp), 1).astype(F32) == dest_col,
                       1.0, 0.0).astype(BF16)
    y = acc_ref[...]
    y_hi = y.astype(BF16)
    y_lo = (y - y_hi.astype(F32)).astype(BF16)
    y_tok = _dot(p_back, y_hi) + _dot(p_back, y_lo)
    x2 = _rows(x1_ref) + affine(y_tok, gt2_ref[...])
    y_ref[...] = (_unit_rms(x2) * gf_ref[...]).reshape(y_ref.shape)


def _mod_kernel(c_ref, w_ref, b_ref, win_ref, op_ref, os_ref, winb_ref):
    winb_ref[...] = win_ref[...].astype(BF16)
    c = c_ref[...]
    a = (c * _sigmoid(c)).astype(BF16)
    mod = _dot(a, w_ref[...].astype(BF16)) + b_ref[...]
    n_p = op_ref.shape[0]
    op_ref[...] = mod[:n_p]
    os_ref[...] = mod[n_p:]


def _const_spec(shape):
    nd = len(shape)
    return pl.BlockSpec(shape, lambda *_: (0,) * nd, pipeline_mode=pl.Buffered(1))


def _modulation(c_all, n_p, w_ada, b_ada, w_in):
    n, d = c_all.shape
    n_out = w_ada.shape[1]
    steps = n_out // MOD_BLOCK
    cast_cols = w_in.shape[1] // steps
    assert n_p % SUBLANES == 0 and cast_cols * steps == w_in.shape[1] and cast_cols % LANES == 0
    cast_spec = pl.BlockSpec((w_in.shape[0], cast_cols), lambda j: (0, j))
    return pl.pallas_call(
        _mod_kernel,
        grid=(steps,),
        in_specs=[pl.BlockSpec((n, d), lambda j: (0, 0)),
                  pl.BlockSpec((d, MOD_BLOCK), lambda j: (0, j)),
                  pl.BlockSpec((1, MOD_BLOCK), lambda j: (0, j)),
                  cast_spec],
        out_specs=[pl.BlockSpec((n_p, MOD_BLOCK), lambda j: (0, j)),
                   pl.BlockSpec((n - n_p, MOD_BLOCK), lambda j: (0, j)),
                   cast_spec],
        out_shape=[jax.ShapeDtypeStruct((n_p, n_out), F32), jax.ShapeDtypeStruct((n - n_p, n_out), F32),
                   jax.ShapeDtypeStruct(w_in.shape, BF16)],
        compiler_params=pltpu.CompilerParams(dimension_semantics=("arbitrary",),
                                             vmem_limit_bytes=VMEM_LIMIT),
        name="modulation",
    )(c_all, w_ada, b_ada, w_in)


def _mixer_weight_specs(p):
    names = ("g1", "g2", "w_in", "caw", "woa", "cbw", "cbb", "lng", "lnb", "wob", "wo", "wr", "br")
    return [p[k] for k in names], [_const_spec(p[k].shape) for k in names]


def _mixer_prompt(x, mod, p, expert_weights):
    b, s, d = x.shape
    assert b == SUBLANES, "the time-major conv layout puts one sequence on each sublane"
    steps = SEQ_TILE // b
    tiles = s // steps
    weights, wspecs = _mixer_weight_specs(p)
    tile = lambda width: pl.BlockSpec((b, steps, width), lambda c: (0, c, 0))
    whole = lambda *shape: pl.BlockSpec(shape, lambda c: (0,) * len(shape))

    def cast_spec(w):
        experts, rows, cols = w.shape
        per_expert = tiles // experts
        assert per_expert * experts == tiles and rows % (per_expert * 2 * SUBLANES) == 0
        return pl.BlockSpec((1, rows // per_expert, cols), lambda c: (c // per_expert, lax.rem(c, per_expert), 0))

    cast_specs = [cast_spec(w) for w in expert_weights]
    return pl.pallas_call(
        _mixer_prompt_kernel,
        grid=(tiles,),
        in_specs=[tile(d), whole(b, N_MOD * d)] + wspecs + cast_specs,
        out_specs=[tile(d), tile(d), tile(ROUTER_LANES),
                   pl.BlockSpec((None, 1, LANES), lambda c: (c, 0, 0)),
                   whole(CONV_A_WIDTH - 1, b, D_A), whole(CONV_B_WIDTH - 1, b, D_B)] + cast_specs,
        out_shape=[jax.ShapeDtypeStruct((b, s, d), F32),
                   jax.ShapeDtypeStruct((b, s, d), BF16),
                   jax.ShapeDtypeStruct((b, s, ROUTER_LANES), F32),
                   jax.ShapeDtypeStruct((tiles, 1, LANES), jnp.int32),
                   jax.ShapeDtypeStruct((CONV_A_WIDTH - 1, b, D_A), F32),
                   jax.ShapeDtypeStruct((CONV_B_WIDTH - 1, b, D_B), F32)]
        + [jax.ShapeDtypeStruct(w.shape, BF16) for w in expert_weights],
        scratch_shapes=[pltpu.VMEM(((CONV_A_WIDTH - 1) * b + SEQ_TILE, D_A), F32),
                        pltpu.VMEM(((CONV_B_WIDTH - 1) * b + SEQ_TILE, D_B), F32),
                        pltpu.VMEM((SEQ_TILE, D_B), F32)],
        compiler_params=pltpu.CompilerParams(dimension_semantics=("arbitrary",),
                                             vmem_limit_bytes=VMEM_LIMIT),
        name="mixer_prompt",
    )(x, mod, *weights, *expert_weights)


def _mixer_sample(x_tm, mod, sa_tm, sb_tm, p):
    n, d = x_tm.shape
    weights, wspecs = _mixer_weight_specs(p)
    full = lambda a: pl.BlockSpec(a.shape, lambda i: (0,) * a.ndim)
    whole = lambda *shape: pl.BlockSpec(shape, lambda i: (0,) * len(shape))
    return pl.pallas_call(
        _mixer_sample_kernel,
        grid=(1,),
        in_specs=[full(x_tm), full(mod)] + wspecs + [full(sa_tm), full(sb_tm)],
        out_specs=[whole(n, d), whole(n, d), whole(n, ROUTER_LANES), whole(1, LANES),
                   full(sa_tm), full(sb_tm)],
        out_shape=[jax.ShapeDtypeStruct((n, d), F32),
                   jax.ShapeDtypeStruct((n, d), BF16),
                   jax.ShapeDtypeStruct((n, ROUTER_LANES), F32),
                   jax.ShapeDtypeStruct((1, LANES), jnp.int32),
                   jax.ShapeDtypeStruct(sa_tm.shape, F32),
                   jax.ShapeDtypeStruct(sb_tm.shape, F32)],
        scratch_shapes=[pltpu.VMEM((n, D_B), BF16)],
        compiler_params=pltpu.CompilerParams(dimension_semantics=("arbitrary",),
                                             vmem_limit_bytes=VMEM_LIMIT),
        name="mixer_sample",
    )(x_tm, mod, *weights, sa_tm, sb_tm)


def _moe(counts, prompt, mod_p, sample, mod_s, p):
    b, s, d = prompt[0].shape
    n_s = mod_s.shape[0]
    steps = SEQ_TILE // b
    tiles = s // steps
    assert sample[0].shape == (SEQ_TILE, d) and SEQ_TILE % n_s == 0
    experts = [p["w1"], p["w3"], p["w2"]]
    in_hbm = pl.BlockSpec(memory_space=pl.ANY)
    tile = lambda width: pl.BlockSpec((b, steps, width), lambda c, cnt: (0, jnp.minimum(c, tiles - 1), 0))
    whole = lambda rows, width: pl.BlockSpec((rows, width), lambda c, cnt: (0, 0), pipeline_mode=pl.Buffered(1))
    gate = lambda rows: pl.BlockSpec((rows, d), lambda c, cnt: (0, N_MOD - 1), pipeline_mode=pl.Buffered(1))
    grid_spec = pltpu.PrefetchScalarGridSpec(
        num_scalar_prefetch=1,
        grid=(tiles + 1,),
        in_specs=[tile(d), tile(d), tile(ROUTER_LANES), gate(b),
                  whole(SEQ_TILE, d), whole(SEQ_TILE, d), whole(SEQ_TILE, ROUTER_LANES), gate(n_s)]
        + [in_hbm] * len(experts) + [_const_spec(p["gf"].shape)],
        out_specs=[tile(d), pl.BlockSpec((SEQ_TILE, d), lambda c, cnt: (0, 0))],
        scratch_shapes=[pltpu.VMEM((SEQ_TILE, d), F32),
                        pltpu.VMEM((SEQ_TILE, ROUTER_LANES), F32),
                        pltpu.VMEM((SEQ_TILE, d), F32)]
        + [pltpu.VMEM(w.shape, w.dtype) for w in experts]
        + [pltpu.SemaphoreType.DMA((len(experts), N_GROUPS))],
    )
    return pl.pallas_call(
        _moe_kernel,
        grid_spec=grid_spec,
        out_shape=[jax.ShapeDtypeStruct((b, s, d), F32), jax.ShapeDtypeStruct((SEQ_TILE, d), F32)],
        compiler_params=pltpu.CompilerParams(dimension_semantics=("arbitrary",),
                                             vmem_limit_bytes=VMEM_LIMIT),
        name="moe",
    )(counts, *prompt, mod_p, *sample, mod_s, *experts, p["gf"])


def kernel(x_prompt, x_sample, c_prompt, c_sample, state_conv_a, state_conv_b, w_ada, b_ada, norm1_g,
           norm2_g, w_in, conv_a_w, w_out_a, conv_b_w, conv_b_bias, ln_b_g, ln_b_b, w_out_b, w_o,
           w_group, b_group, w_expert, b_expert, w1, w3, w2, final_norm_g):
    depth = w_ada.shape[0]
    assert depth == 1, "the MoE kernel fuses the final norm, so exactly one layer is supported"
    n_p, _, d = x_prompt.shape
    n_s, steps, _ = x_sample.shape
    c_all = jnp.concatenate([c_prompt, c_sample], axis=0)
    xp = x_prompt
    xs = x_sample.transpose(1, 0, 2).reshape(steps * n_s, d)
    gf = final_norm_g.reshape(1, d)
    pa, pb, sa, sb = [], [], [], []
    for l in range(depth):
        pad = ROUTER_LANES - N_EXPERTS - N_GROUPS
        wr = jnp.pad(jnp.concatenate([w_expert[l], w_group[l]], axis=1).T, ((0, pad), (0, 0)))
        br = jnp.pad(jnp.concatenate([b_expert[l], b_group[l]]), (0, pad))
        br = jnp.broadcast_to(br[:, None], (ROUTER_LANES, LANES))
        p = {
            "g1": norm1_g[l].reshape(1, d), "g2": norm2_g[l].reshape(1, d),
            "woa": w_out_a[l].astype(BF16),
            "caw": jnp.broadcast_to(conv_a_w[l][:, None, :], (CONV_A_WIDTH, SUBLANES, D_A)),
            "cbw": jnp.broadcast_to(conv_b_w[l][:, None, :], (CONV_B_WIDTH, SUBLANES, D_B)),
            "cbb": conv_b_bias[l].reshape(1, D_B),
            "lng": ln_b_g[l].reshape(1, D_B), "lnb": ln_b_b[l].reshape(1, D_B),
            "wob": w_out_b[l].astype(BF16), "wo": w_o[l].astype(BF16), "wr": wr.astype(BF16), "br": br,
            "gf": gf,
        }
        mod_p, mod_s, p["w_in"] = _modulation(c_all, n_p, w_ada[l], b_ada[l].reshape(1, N_MOD * d), w_in[l])

        x1p, h2p, route_p, cnt_p, na_p, nb_p, w1b, w3b, w2b = _mixer_prompt(
            xp, mod_p, p, (w1[l], w3[l], w2[l]))
        p.update(w1=w1b, w3=w3b, w2=w2b.reshape(N_GROUPS, EXPERTS_PER_GROUP * D_EXPERT, d))

        sa_tm = state_conv_a[l].transpose(1, 0, 2)
        sb_tm = state_conv_b[l].transpose(1, 0, 2)
        x1s, h2s, route_s, cnt_s, na_s, nb_s = _mixer_sample(xs, mod_s, sa_tm, sb_tm, p)
        counts = jnp.concatenate([cnt_p[:, 0, :N_GROUPS].reshape(-1), cnt_s[0, :N_GROUPS]])
        xp, xs = _moe(counts, (x1p, h2p, route_p), mod_p, (x1s, h2s, route_s), mod_s, p)
        pa.append(na_p.transpose(1, 0, 2))
        pb.append(nb_p.transpose(1, 0, 2))
        sa.append(na_s.transpose(1, 0, 2))
        sb.append(nb_s.transpose(1, 0, 2))
    y_sample = xs.reshape(steps, n_s, d).transpose(1, 0, 2)
    return (xp, y_sample, jnp.stack(pa), jnp.stack(pb), jnp.stack(sa), jnp.stack(sb))
```

```python
import jax
import jax.numpy as jnp
from jax import lax
from jax.experimental import pallas as pl
from jax.experimental.pallas import tpu as pltpu

D_MODEL = 1024
D_A = 512
D_B = 512
CONV_A_WIDTH = 3
CONV_B_WIDTH = 31
N_GROUPS = 4
EXPERTS_PER_GROUP = 4
N_EXPERTS = 16
D_EXPERT = 256
N_MOD = 6
EPS = 1e-6

LANES = 128
SUBLANES = 8
ROUTER_LANES = LANES
SEQ_TILE = 512
CONV_ROWS = 64
CONV_LANES = 2 * LANES
SORT_SHIFT = 3
SORT_ALIGN = 1 << SORT_SHIFT
MOE_PASS_ROWS = (128, 144, 160, 176, 192, 224, 256)
VMEM_LIMIT = 52 * 1024 * 1024
MOD_BLOCK = 2 * D_MODEL

BF16 = jnp.bfloat16
F32 = jnp.float32

assert 2 * EXPERTS_PER_GROUP == SUBLANES and N_GROUPS <= SUBLANES and N_EXPERTS == N_GROUPS * EXPERTS_PER_GROUP


def _dot(a, b):
    return jnp.dot(a, b, preferred_element_type=F32)


def _unit_rms(x):
    return x * lax.rsqrt(jnp.mean(x * x, axis=-1, keepdims=True) + EPS)


def _sigmoid(v):
    return 0.5 * jnp.tanh(0.5 * v) + 0.5


def _affine(v, scale, shift=None):
    r = scale.shape[0]
    if r != 1 and r != v.shape[0]:
        v3 = v.reshape(v.shape[0] // r, r, v.shape[-1])
        shift3 = None if shift is None else shift[None]
        return _affine(v3, scale[None], shift3).reshape(v.shape)
    out = v * scale
    return out if shift is None else out + shift


def _per_sequence(v, scale, shift=None):
    n = scale.shape[0]
    v3 = v.reshape(n, v.shape[0] // n, v.shape[-1])
    out = v3 * scale[:, None, :]
    if shift is not None:
        out = out + shift[:, None, :]
    return out.reshape(v.shape)


def _tap(v, w8):
    return (v.reshape(v.shape[0] // SUBLANES, SUBLANES, v.shape[-1]) * w8[None]).reshape(v.shape)


def _route(logits_t):
    t = logits_t.shape[1]
    neg = -jnp.inf
    row = lax.broadcasted_iota(jnp.int32, (SUBLANES, t), 0).astype(F32)
    live = row < float(EXPERTS_PER_GROUP)
    gl = jnp.where(live, logits_t[N_EXPERTS:N_EXPERTS + SUBLANES, :], neg)
    gmax = jnp.max(gl, axis=0, keepdims=True)
    g_idx = jnp.min(jnp.where(gl == gmax, row, float(SUBLANES)), axis=0, keepdims=True)
    p_g = 1.0 / jnp.sum(jnp.exp(gl - gmax), axis=0, keepdims=True)
    el = None
    for g in reversed(range(N_GROUPS)):
        slab = logits_t[(g // 2) * SUBLANES:(g // 2 + 1) * SUBLANES, :]
        cand = slab if g % 2 == 0 else pltpu.roll(slab, SUBLANES - EXPERTS_PER_GROUP, axis=0)
        el = cand if el is None else jnp.where(g_idx == float(g), cand, el)
    el = jnp.where(live, el, neg)
    v1 = jnp.max(el, axis=0, keepdims=True)
    i1 = jnp.min(jnp.where(el == v1, row, float(SUBLANES)), axis=0, keepdims=True)
    el2 = jnp.where(row == i1, neg, el)
    v2 = jnp.max(el2, axis=0, keepdims=True)
    i2 = jnp.min(jnp.where(el2 == v2, row, float(SUBLANES)), axis=0, keepdims=True)
    e21 = jnp.exp(v2 - v1)
    wt1 = p_g / (1.0 + e21)
    wt2 = p_g * e21 / (1.0 + e21)
    comb = jnp.where(row == i1, wt1, jnp.where(row == i2, wt2, 0.0))
    onehot = jnp.where(row == g_idx, 1.0, 0.0)
    onehot_b = jnp.concatenate([onehot, jnp.zeros_like(onehot)], axis=0).astype(BF16)
    earlier = lax.broadcasted_iota(jnp.int32, (t, t), 0) < lax.broadcasted_iota(jnp.int32, (t, t), 1)
    before = _dot(onehot_b, jnp.where(earlier, 1.0, 0.0).astype(BF16))[0:SUBLANES, :]
    rank = jnp.sum(before * onehot, axis=0, keepdims=True)
    count = jnp.sum(onehot, axis=1, keepdims=True)
    row1 = lax.broadcasted_iota(jnp.int32, (SUBLANES, 1), 0)
    start = jnp.zeros((SUBLANES, 1), F32)
    for g in range(N_GROUPS - 1):
        start = start + jnp.where(row1 > g, count[g:g + 1, :], 0.0)
    dest = jnp.sum(onehot * start, axis=0, keepdims=True) + rank
    route_t = jnp.concatenate([jnp.where(live, comb, dest), jnp.broadcast_to(dest, (LANES - SUBLANES, t))],
                              axis=0)
    onehot_all = jnp.concatenate([onehot, jnp.zeros((LANES - SUBLANES, t), F32)], axis=0).astype(BF16)
    count_row = lax.dot_general(jnp.ones((2 * SUBLANES, t), BF16), onehot_all, (((1,), (1,)), ((), ())),
                                preferred_element_type=F32)[0:1, :]
    return jnp.transpose(route_t), count_row


def _ln_silu(v, bias, g, b):
    v = v + bias
    mu = jnp.mean(v, axis=-1, keepdims=True)
    d = v - mu
    var = jnp.mean(d * d, axis=-1, keepdims=True)
    y = d * lax.rsqrt(var + EPS) * g + b
    return y * _sigmoid(y)


def _mixer_tail(x, hb, mod, b_a_conv, yb_in, g2, w_in_ref, woa_ref, wob_ref, wo_ref, wr_ref, br_ref,
                x1_ref, h2_ref, route_ref, cnt_ref, affine):
    y_a = _dot(b_a_conv.astype(BF16), woa_ref[...])
    y_b = _dot(yb_in, wob_ref[...])
    o = 3 * D_A + 2 * D_B
    mg_a = _dot(hb, w_in_ref[:, o:o + D_MODEL])
    mg_b = _dot(hb, w_in_ref[:, o + D_MODEL:o + 2 * D_MODEL])
    merged = _sigmoid(mg_a) * y_a + _sigmoid(mg_b) * y_b
    gt1 = mod[:, 2 * D_MODEL:3 * D_MODEL]
    x1 = x + affine(_dot(merged.astype(BF16), wo_ref[...]), gt1)
    sh2 = mod[:, 3 * D_MODEL:4 * D_MODEL]
    sc2 = mod[:, 4 * D_MODEL:5 * D_MODEL]
    h2 = affine(_unit_rms(x1), g2 * (1.0 + sc2), sh2).astype(BF16)
    logits_t = lax.dot_general(wr_ref[...], h2, (((1,), (1,)), ((), ())), preferred_element_type=F32)
    logits_t = logits_t + jnp.tile(br_ref[...], (1, h2.shape[0] // LANES))
    x1_ref[...] = x1.reshape(x1_ref.shape)
    h2_ref[...] = h2.reshape(h2_ref.shape)
    route, count = _route(logits_t)
    route_ref[...] = route.reshape(route_ref.shape)
    cnt_ref[...] = count.astype(jnp.int32)


def _mixer_prompt_kernel(x_ref, mod_ref, g1_ref, g2_ref, w_in_ref, caw_ref, woa_ref, cbw_ref, cbb_ref,
                         lng_ref, lnb_ref, wob_ref, wo_ref, wr_ref, br_ref,
                         w1f_ref, w3f_ref, w2f_ref,
                         x1_ref, h2_ref, route_ref, cnt_ref, na_ref, nb_ref, w1b_ref, w3b_ref, w2b_ref,
                         hist_a, hist_b, conv_b_ref):
    n_seq, steps, d = x_ref.shape
    t = n_seq * steps
    rows_a = (CONV_A_WIDTH - 1) * n_seq
    rows_b = (CONV_B_WIDTH - 1) * n_seq

    @pl.when(pl.program_id(0) == 0)
    def _():
        hist_a[0:rows_a, :] = jnp.zeros((rows_a, D_A), F32)
        hist_b[0:rows_b, :] = jnp.zeros((rows_b, D_B), F32)

    def time_major(v):
        return jnp.swapaxes(v.reshape(n_seq, steps, v.shape[-1]), 0, 1).reshape(v.shape)

    def sequence_major(v):
        return jnp.swapaxes(v.reshape(steps, n_seq, v.shape[-1]), 0, 1).reshape(v.shape)

    x = x_ref[...].reshape(t, d)
    mod = mod_ref[...]
    h = _per_sequence(_unit_rms(x), g1_ref[...] * (1.0 + mod[:, D_MODEL:2 * D_MODEL]), mod[:, 0:D_MODEL])
    hb = h.astype(BF16)

    def proj(lo, width):
        return _dot(hb, w_in_ref[:, lo:lo + width])

    hist_a[rows_a:rows_a + t, :] = time_major(proj(D_A, D_A) * proj(2 * D_A, D_A))
    conv_a = None
    for k in range(CONV_A_WIDTH):
        term = _tap(hist_a[k * n_seq:k * n_seq + t, :], caw_ref[k])
        conv_a = term if conv_a is None else conv_a + term
    b_a_conv = proj(0, D_A) * sequence_major(conv_a)
    na_ref[...] = hist_a[t:t + rows_a, :].reshape(na_ref.shape)

    hist_b[rows_b:rows_b + t, :] = time_major(proj(3 * D_A, D_B) * _sigmoid(proj(3 * D_A + D_B, D_B)))
    nb_ref[...] = hist_b[t:t + rows_b, :].reshape(nb_ref.shape)
    for r0 in range(0, t, CONV_ROWS):
        for c0 in range(0, D_B, CONV_LANES):
            cols = slice(c0, c0 + CONV_LANES)
            acc = None
            for k in range(CONV_B_WIDTH):
                o = r0 + k * n_seq
                term = _tap(hist_b[o:o + CONV_ROWS, cols], cbw_ref[k][:, cols])
                acc = term if acc is None else acc + term
            conv_b_ref[r0:r0 + CONV_ROWS, cols] = acc
    yb_in = _ln_silu(sequence_major(conv_b_ref[...]), cbb_ref[...], lng_ref[...], lnb_ref[...]).astype(BF16)

    hist_a[0:rows_a, :] = hist_a[t:t + rows_a, :]
    hist_b[0:rows_b, :] = hist_b[t:t + rows_b, :]

    _mixer_tail(x, hb, mod, b_a_conv, yb_in, g2_ref[...], w_in_ref, woa_ref, wob_ref, wo_ref,
                wr_ref, br_ref, x1_ref, h2_ref, route_ref, cnt_ref, _per_sequence)

    w1b_ref[...] = w1f_ref[...].astype(BF16)
    w3b_ref[...] = w3f_ref[...].astype(BF16)
    w2b_ref[...] = w2f_ref[...].astype(BF16)


def _mixer_sample_kernel(x_ref, mod_ref, g1_ref, g2_ref, w_in_ref, caw_ref, woa_ref, cbw_ref, cbb_ref,
                         lng_ref, lnb_ref, wob_ref, wo_ref, wr_ref, br_ref, sa_ref, sb_hbm,
                         x1_ref, h2_ref, route_ref, cnt_ref, na_ref, nb_hbm, yb_in_ref, sb_ref, ub_ref, sem):
    n_seq = mod_ref.shape[0]
    steps = x_ref.shape[0] // n_seq
    hist = CONV_B_WIDTH - 1
    load_state = pltpu.make_async_copy(sb_hbm, sb_ref, sem.at[0])
    carry_state = pltpu.make_async_copy(sb_hbm.at[pl.ds(steps, hist - steps)],
                                        nb_hbm.at[pl.ds(0, hist - steps)], sem.at[1])
    append_state = pltpu.make_async_copy(ub_ref, nb_hbm.at[pl.ds(hist - steps, steps)], sem.at[2])
    load_state.start()
    carry_state.start()
    x = x_ref[...]
    mod = mod_ref[...]
    h = _affine(_unit_rms(x), g1_ref[...] * (1.0 + mod[:, D_MODEL:2 * D_MODEL]), mod[:, 0:D_MODEL])
    hb = h.astype(BF16)

    def proj(lo, width):
        return _dot(hb, w_in_ref[:, lo:lo + width])

    def slab(v, j):
        return v[j * n_seq:(j + 1) * n_seq, :]

    u_a = proj(D_A, D_A) * proj(2 * D_A, D_A)
    full_a = [sa_ref[j] for j in range(CONV_A_WIDTH - 1)] + [slab(u_a, j) for j in range(steps)]
    conv_a = []
    for s in range(steps):
        acc = jnp.zeros((n_seq, D_A), F32)
        for k in range(CONV_A_WIDTH):
            acc = acc + _tap(full_a[s + k], caw_ref[k])
        conv_a.append(acc)
    b_a_conv = proj(0, D_A) * jnp.concatenate(conv_a, axis=0)
    for j in range(CONV_A_WIDTH - 1):
        na_ref[j] = full_a[steps + j]

    u_b = proj(3 * D_A, D_B) * _sigmoid(proj(3 * D_A + D_B, D_B))
    ub_ref[...] = u_b.reshape(ub_ref.shape)
    append_state.start()
    load_state.wait()

    def full_b(j):
        return sb_ref[j] if j < hist else slab(u_b, j - hist)

    bias, lng, lnb = cbb_ref[...], lng_ref[...], lnb_ref[...]
    for s in range(steps):
        acc = jnp.zeros((n_seq, D_B), F32)
        for k in range(CONV_B_WIDTH):
            acc = acc + _tap(full_b(s + k), cbw_ref[k])
        yb_in_ref[s * n_seq:(s + 1) * n_seq, :] = _ln_silu(acc, bias, lng, lnb).astype(BF16)

    _mixer_tail(x, hb, mod, b_a_conv, yb_in_ref[...], g2_ref[...], w_in_ref, woa_ref, wob_ref, wo_ref,
                wr_ref, br_ref, x1_ref, h2_ref, route_ref, cnt_ref, _affine)
    carry_state.wait()
    append_state.wait()


def _moe_kernel(cnt_ref, x1p_ref, h2p_ref, routep_ref, gt2p_ref, x1s_ref, h2s_ref, routes_ref, gt2s_ref,
                w1_hbm, w3_hbm, w2_hbm, gf_ref, yp_ref, ys_ref, h2sort_ref, cs_ref, acc_ref,
                w1_ref, w3_ref, w2_ref, sem):
    def weight_copies(g):
        experts = pl.ds(g * EXPERTS_PER_GROUP, EXPERTS_PER_GROUP)
        return (pltpu.make_async_copy(w1_hbm.at[experts], w1_ref.at[experts], sem.at[0, g]),
                pltpu.make_async_copy(w3_hbm.at[experts], w3_ref.at[experts], sem.at[1, g]),
                pltpu.make_async_copy(w2_hbm.at[g], w2_ref.at[g], sem.at[2, g]))

    first = pl.program_id(0) == 0

    @pl.when(first)
    def _():
        for g in range(N_GROUPS):
            for copy in weight_copies(g):
                copy.start()

    def await_weights(g):
        @pl.when(first)
        def _():
            for copy in weight_copies(g):
                copy.wait()

    is_sample = pl.program_id(0) == pl.num_programs(0) - 1
    prompt = jnp.logical_not(is_sample)
    pl.when(prompt)(lambda: _moe_sort(h2p_ref, routep_ref, h2sort_ref, cs_ref, acc_ref))
    pl.when(is_sample)(lambda: _moe_sort(h2s_ref, routes_ref, h2sort_ref, cs_ref, acc_ref))
    _moe_experts(cnt_ref, w1_ref, w3_ref, w2_ref, h2sort_ref, cs_ref, acc_ref, await_weights)
    pl.when(prompt)(lambda: _moe_finish(x1p_ref, routep_ref, gt2p_ref, gf_ref, acc_ref, yp_ref, _per_sequence))
    pl.when(is_sample)(lambda: _moe_finish(x1s_ref, routes_ref, gt2s_ref, gf_ref, acc_ref, ys_ref, _affine))


def _rows(ref):
    return ref[...].reshape(-1, ref.shape[-1])


def _moe_sort(h2_ref, route_ref, h2s_ref, cs_ref, acc_ref):
    tp = h2s_ref.shape[0]
    route = _rows(route_ref)
    t = route.shape[0]
    lane_i = lax.broadcasted_iota(jnp.int32, route.shape, 1)
    dest_row = jnp.transpose(route)[EXPERTS_PER_GROUP:EXPERTS_PER_GROUP + 1, :]
    slot = lax.broadcasted_iota(jnp.int32, (tp, t), 0).astype(F32)
    p_sort = jnp.where(slot == dest_row, 1.0, 0.0).astype(BF16)
    h2s_ref[...] = _dot(p_sort, _rows(h2_ref))
    comb = jnp.where(lane_i < EXPERTS_PER_GROUP, route, 0.0)
    c_hi = comb.astype(BF16).astype(F32)
    c_mid = (comb - c_hi).astype(BF16).astype(F32)
    c_lo = comb - c_hi - c_mid
    packed = (c_hi + pltpu.roll(c_mid, EXPERTS_PER_GROUP, axis=1)
              + pltpu.roll(c_lo, 2 * EXPERTS_PER_GROUP, axis=1)).astype(BF16)
    moved = _dot(p_sort, packed)
    cs_ref[...] = (moved + pltpu.roll(moved, LANES - EXPERTS_PER_GROUP, axis=1)
                   + pltpu.roll(moved, LANES - 2 * EXPERTS_PER_GROUP, axis=1))
    acc_ref[...] = jnp.zeros(acc_ref.shape, F32)


def _moe_experts(cnt_ref, w1_ref, w3_ref, w2_ref, h2s_ref, cs_ref, acc_ref, await_weights):
    tp = h2s_ref.shape[0]
    tile = pl.program_id(0)

    def expert_pass(rows, g, window, lo, end):
        off = pl.multiple_of(jnp.minimum(window, tp - rows), SORT_ALIGN)
        row_id = off + lax.broadcasted_iota(jnp.int32, (rows, 1), 0)
        valid = (row_id >= jnp.maximum(lo, window)) & (row_id < end)
        hb = h2s_ref[pl.ds(off, rows), :].astype(BF16)
        c4 = jnp.where(valid, cs_ref[pl.ds(off, rows), :], 0.0)
        experts = [g * EXPERTS_PER_GROUP + e for e in range(EXPERTS_PER_GROUP)]
        gate = jnp.concatenate([_dot(hb, w1_ref[e]) for e in experts], axis=1)
        up = jnp.concatenate([_dot(hb, w3_ref[e]) for e in experts], axis=1)
        cexp = jnp.concatenate(
            [jnp.broadcast_to(c4[:, e:e + 1], (rows, D_EXPERT)) for e in range(EXPERTS_PER_GROUP)], axis=1)
        a = gate * _sigmoid(gate) * up * cexp
        acc_ref[pl.ds(off, rows), :] += _dot(a.astype(BF16), w2_ref[g])

    def group(g, start):
        await_weights(g)
        end = start + cnt_ref[tile * N_GROUPS + g]
        base = lax.shift_right_logical(start, SORT_SHIFT) * SORT_ALIGN
        span = end - base

        def passes(rows):
            def run():
                def body(p, carry):
                    expert_pass(rows, g, base + p * rows, start, end)
                    return carry
                lax.fori_loop(0, jnp.where(end > start, lax.div(span + (rows - 1), rows), 0), body, 0)
            return run

        size_class = sum((span > rows).astype(jnp.int32) for rows in MOE_PASS_ROWS[:-1])
        lax.switch(size_class, [passes(rows) for rows in MOE_PASS_ROWS])
        return end

    lax.fori_loop(0, N_GROUPS, group, jnp.int32(0))


def _moe_finish(x1_ref, route_ref, gt2_ref, gf_ref, acc_ref, y_ref, affine):
    tp = acc_ref.shape[0]
    route = _rows(route_ref)
    t = route.shape[0]
    dest_col = route[:, EXPERTS_PER_GROUP:EXPERTS_PER_GROUP + 1]
    p_back = jnp.where(lax.broadcasted_iota(jnp.int32, (t, tp), 1).astype(F32) == dest_col,
                       1.0, 0.0).astype(BF16)
    y = acc_ref[...]
    y_hi = y.astype(BF16)
    y_lo = (y - y_hi.astype(F32)).astype(BF16)
    y_tok = _dot(p_back, y_hi) + _dot(p_back, y_lo)
    x2 = _rows(x1_ref) + affine(y_tok, gt2_ref[...])
    y_ref[...] = (_unit_rms(x2) * gf_ref[...]).reshape(y_ref.shape)


def _mod_kernel(c_ref, w_ref, b_ref, win_ref, op_ref, os_ref, winb_ref):
    winb_ref[...] = win_ref[...].astype(BF16)
    c = c_ref[...]
    a = (c * _sigmoid(c)).astype(BF16)
    mod = _dot(a, w_ref[...].astype(BF16)) + b_ref[...]
    n_p = op_ref.shape[0]
    op_ref[...] = mod[:n_p]
    os_ref[...] = mod[n_p:]


def _const_spec(shape):
    nd = len(shape)
    return pl.BlockSpec(shape, lambda *_: (0,) * nd, pipeline_mode=pl.Buffered(1))


def _modulation(c_all, n_p, w_ada, b_ada, w_in):
    n, d = c_all.shape
    n_out = w_ada.shape[1]
    steps = n_out // MOD_BLOCK
    cast_cols = w_in.shape[1] // steps
    assert n_p % SUBLANES == 0 and cast_cols * steps == w_in.shape[1] and cast_cols % LANES == 0
    cast_spec = pl.BlockSpec((w_in.shape[0], cast_cols), lambda j: (0, j))
    return pl.pallas_call(
        _mod_kernel,
        grid=(steps,),
        in_specs=[pl.BlockSpec((n, d), lambda j: (0, 0)),
                  pl.BlockSpec((d, MOD_BLOCK), lambda j: (0, j)),
                  pl.BlockSpec((1, MOD_BLOCK), lambda j: (0, j)),
                  cast_spec],
        out_specs=[pl.BlockSpec((n_p, MOD_BLOCK), lambda j: (0, j)),
                   pl.BlockSpec((n - n_p, MOD_BLOCK), lambda j: (0, j)),
                   cast_spec],
        out_shape=[jax.ShapeDtypeStruct((n_p, n_out), F32), jax.ShapeDtypeStruct((n - n_p, n_out), F32),
                   jax.ShapeDtypeStruct(w_in.shape, BF16)],
        compiler_params=pltpu.CompilerParams(dimension_semantics=("arbitrary",),
                                             vmem_limit_bytes=VMEM_LIMIT),
        name="modulation",
    )(c_all, w_ada, b_ada, w_in)


def _mixer_weight_specs(p):
    names = ("g1", "g2", "w_in", "caw", "woa", "cbw", "cbb", "lng", "lnb", "wob", "wo", "wr", "br")
    return [p[k] for k in names], [_const_spec(p[k].shape) for k in names]


def _mixer_prompt(x, mod, p, expert_weights):
    b, s, d = x.shape
    assert b == SUBLANES, "the time-major conv layout puts one sequence on each sublane"
    steps = SEQ_TILE // b
    tiles = s // steps
    weights, wspecs = _mixer_weight_specs(p)
    tile = lambda width: pl.BlockSpec((b, steps, width), lambda c: (0, c, 0))
    whole = lambda *shape: pl.BlockSpec(shape, lambda c: (0,) * len(shape))

    def cast_spec(w):
        experts, rows, cols = w.shape
        per_expert = tiles // experts
        assert per_expert * experts == tiles and rows % (per_expert * 2 * SUBLANES) == 0
        return pl.BlockSpec((1, rows // per_expert, cols), lambda c: (c // per_expert, lax.rem(c, per_expert), 0))

    cast_specs = [cast_spec(w) for w in expert_weights]
    return pl.pallas_call(
        _mixer_prompt_kernel,
        grid=(tiles,),
        in_specs=[tile(d), whole(b, N_MOD * d)] + wspecs + cast_specs,
        out_specs=[tile(d), tile(d), tile(ROUTER_LANES),
                   pl.BlockSpec((None, 1, LANES), lambda c: (c, 0, 0)),
                   whole(CONV_A_WIDTH - 1, b, D_A), whole(CONV_B_WIDTH - 1, b, D_B)] + cast_specs,
        out_shape=[jax.ShapeDtypeStruct((b, s, d), F32),
                   jax.ShapeDtypeStruct((b, s, d), BF16),
                   jax.ShapeDtypeStruct((b, s, ROUTER_LANES), F32),
                   jax.ShapeDtypeStruct((tiles, 1, LANES), jnp.int32),
                   jax.ShapeDtypeStruct((CONV_A_WIDTH - 1, b, D_A), F32),
                   jax.ShapeDtypeStruct((CONV_B_WIDTH - 1, b, D_B), F32)]
        + [jax.ShapeDtypeStruct(w.shape, BF16) for w in expert_weights],
        scratch_shapes=[pltpu.VMEM(((CONV_A_WIDTH - 1) * b + SEQ_TILE, D_A), F32),
                        pltpu.VMEM(((CONV_B_WIDTH - 1) * b + SEQ_TILE, D_B), F32),
                        pltpu.VMEM((SEQ_TILE, D_B), F32)],
        compiler_params=pltpu.CompilerParams(dimension_semantics=("arbitrary",),
                                             vmem_limit_bytes=VMEM_LIMIT),
        name="mixer_prompt",
    )(x, mod, *weights, *expert_weights)


def _mixer_sample(x_tm, mod, sa_tm, sb_tm, p):
    n, d = x_tm.shape
    weights, wspecs = _mixer_weight_specs(p)
    full = lambda a: pl.BlockSpec(a.shape, lambda i: (0,) * a.ndim)
    whole = lambda *shape: pl.BlockSpec(shape, lambda i: (0,) * len(shape))
    in_hbm = pl.BlockSpec(memory_space=pl.ANY)
    hist, n_seq, _ = sb_tm.shape
    steps = n // n_seq
    assert steps < hist, "the surviving history steps are copied state to state"
    return pl.pallas_call(
        _mixer_sample_kernel,
        grid=(1,),
        in_specs=[full(x_tm), full(mod)] + wspecs + [full(sa_tm), in_hbm],
        out_specs=[whole(n, d), whole(n, d), whole(n, ROUTER_LANES), whole(1, LANES),
                   full(sa_tm), in_hbm],
        out_shape=[jax.ShapeDtypeStruct((n, d), F32),
                   jax.ShapeDtypeStruct((n, d), BF16),
                   jax.ShapeDtypeStruct((n, ROUTER_LANES), F32),
                   jax.ShapeDtypeStruct((1, LANES), jnp.int32),
                   jax.ShapeDtypeStruct(sa_tm.shape, F32),
                   jax.ShapeDtypeStruct(sb_tm.shape, F32)],
        scratch_shapes=[pltpu.VMEM((n, D_B), BF16), pltpu.VMEM(sb_tm.shape, F32),
                        pltpu.VMEM((steps, n_seq, D_B), F32), pltpu.SemaphoreType.DMA((3,))],
        compiler_params=pltpu.CompilerParams(dimension_semantics=("arbitrary",),
                                             vmem_limit_bytes=VMEM_LIMIT),
        name="mixer_sample",
    )(x_tm, mod, *weights, sa_tm, sb_tm)


def _moe(counts, prompt, mod_p, sample, mod_s, p):
    b, s, d = prompt[0].shape
    n_s = mod_s.shape[0]
    steps = SEQ_TILE // b
    tiles = s // steps
    assert sample[0].shape == (SEQ_TILE, d) and SEQ_TILE % n_s == 0
    experts = [p["w1"], p["w3"], p["w2"]]
    in_hbm = pl.BlockSpec(memory_space=pl.ANY)
    tile = lambda width: pl.BlockSpec((b, steps, width), lambda c, cnt: (0, jnp.minimum(c, tiles - 1), 0))
    whole = lambda rows, width: pl.BlockSpec((rows, width), lambda c, cnt: (0, 0), pipeline_mode=pl.Buffered(1))
    gate = lambda rows: pl.BlockSpec((rows, d), lambda c, cnt: (0, N_MOD - 1), pipeline_mode=pl.Buffered(1))
    grid_spec = pltpu.PrefetchScalarGridSpec(
        num_scalar_prefetch=1,
        grid=(tiles + 1,),
        in_specs=[tile(d), tile(d), tile(ROUTER_LANES), gate(b),
                  whole(SEQ_TILE, d), whole(SEQ_TILE, d), whole(SEQ_TILE, ROUTER_LANES), gate(n_s)]
        + [in_hbm] * len(experts) + [_const_spec(p["gf"].shape)],
        out_specs=[tile(d), pl.BlockSpec((SEQ_TILE, d), lambda c, cnt: (0, 0))],
        scratch_shapes=[pltpu.VMEM((SEQ_TILE, d), F32),
                        pltpu.VMEM((SEQ_TILE, ROUTER_LANES), F32),
                        pltpu.VMEM((SEQ_TILE, d), F32)]
        + [pltpu.VMEM(w.shape, w.dtype) for w in experts]
        + [pltpu.SemaphoreType.DMA((len(experts), N_GROUPS))],
    )
    return pl.pallas_call(
        _moe_kernel,
        grid_spec=grid_spec,
        out_shape=[jax.ShapeDtypeStruct((b, s, d), F32), jax.ShapeDtypeStruct((SEQ_TILE, d), F32)],
        compiler_params=pltpu.CompilerParams(dimension_semantics=("arbitrary",),
                                             vmem_limit_bytes=VMEM_LIMIT),
        name="moe",
    )(counts, *prompt, mod_p, *sample, mod_s, *experts, p["gf"])


def kernel(x_prompt, x_sample, c_prompt, c_sample, state_conv_a, state_conv_b, w_ada, b_ada, norm1_g,
           norm2_g, w_in, conv_a_w, w_out_a, conv_b_w, conv_b_bias, ln_b_g, ln_b_b, w_out_b, w_o,
           w_group, b_group, w_expert, b_expert, w1, w3, w2, final_norm_g):
    depth = w_ada.shape[0]
    assert depth == 1, "the MoE kernel fuses the final norm, so exactly one layer is supported"
    n_p, _, d = x_prompt.shape
    n_s, steps, _ = x_sample.shape
    c_all = jnp.concatenate([c_prompt, c_sample], axis=0)
    xp = x_prompt
    xs = x_sample.transpose(1, 0, 2).reshape(steps * n_s, d)
    gf = final_norm_g.reshape(1, d)
    pa, pb, sa, sb = [], [], [], []
    for l in range(depth):
        pad = ROUTER_LANES - N_EXPERTS - N_GROUPS
        wr = jnp.pad(jnp.concatenate([w_expert[l], w_group[l]], axis=1).T, ((0, pad), (0, 0)))
        br = jnp.pad(jnp.concatenate([b_expert[l], b_group[l]]), (0, pad))
        br = jnp.broadcast_to(br[:, None], (ROUTER_LANES, LANES))
        p = {
            "g1": norm1_g[l].reshape(1, d), "g2": norm2_g[l].reshape(1, d),
            "woa": w_out_a[l].astype(BF16),
            "caw": jnp.broadcast_to(conv_a_w[l][:, None, :], (CONV_A_WIDTH, SUBLANES, D_A)),
            "cbw": jnp.broadcast_to(conv_b_w[l][:, None, :], (CONV_B_WIDTH, SUBLANES, D_B)),
            "cbb": conv_b_bias[l].reshape(1, D_B),
            "lng": ln_b_g[l].reshape(1, D_B), "lnb": ln_b_b[l].reshape(1, D_B),
            "wob": w_out_b[l].astype(BF16), "wo": w_o[l].astype(BF16), "wr": wr.astype(BF16), "br": br,
            "gf": gf,
        }
        mod_p, mod_s, p["w_in"] = _modulation(c_all, n_p, w_ada[l], b_ada[l].reshape(1, N_MOD * d), w_in[l])

        x1p, h2p, route_p, cnt_p, na_p, nb_p, w1b, w3b, w2b = _mixer_prompt(
            xp, mod_p, p, (w1[l], w3[l], w2[l]))
        p.update(w1=w1b, w3=w3b, w2=w2b.reshape(N_GROUPS, EXPERTS_PER_GROUP * D_EXPERT, d))

        sa_tm = state_conv_a[l].transpose(1, 0, 2)
        sb_tm = state_conv_b[l].transpose(1, 0, 2)
        x1s, h2s, route_s, cnt_s, na_s, nb_s = _mixer_sample(xs, mod_s, sa_tm, sb_tm, p)
        counts = jnp.concatenate([cnt_p[:, 0, :N_GROUPS].reshape(-1), cnt_s[0, :N_GROUPS]])
        xp, xs = _moe(counts, (x1p, h2p, route_p), mod_p, (x1s, h2s, route_s), mod_s, p)
        pa.append(na_p.transpose(1, 0, 2))
        pb.append(nb_p.transpose(1, 0, 2))
        sa.append(na_s.transpose(1, 0, 2))
        sb.append(nb_s.transpose(1, 0, 2))
    y_sample = xs.reshape(steps, n_s, d).transpose(1, 0, 2)
    return (xp, y_sample, jnp.stack(pa), jnp.stack(pb), jnp.stack(sa), jnp.stack(sb))
```

```python
import jax
import jax.numpy as jnp
from jax import lax
from jax.experimental import pallas as pl
from jax.experimental.pallas import tpu as pltpu

D_MODEL = 1024
D_A = 512
D_B = 512
CONV_A_WIDTH = 3
CONV_B_WIDTH = 31
N_GROUPS = 4
EXPERTS_PER_GROUP = 4
N_EXPERTS = 16
D_EXPERT = 256
N_MOD = 6
EPS = 1e-6

LANES = 128
SUBLANES = 8
ROUTER_LANES = LANES
SEQ_TILE = 512
CONV_ROWS = 64
CONV_LANES = 2 * LANES
SORT_SHIFT = 3
SORT_ALIGN = 1 << SORT_SHIFT
MOE_PASS_ROWS = (128, 144, 160, 176, 192, 224, 256)
VMEM_LIMIT = 52 * 1024 * 1024
MOD_BLOCK = 2 * D_MODEL

BF16 = jnp.bfloat16
F32 = jnp.float32

assert 2 * EXPERTS_PER_GROUP == SUBLANES and N_GROUPS <= SUBLANES and N_EXPERTS == N_GROUPS * EXPERTS_PER_GROUP


def _dot(a, b):
    return jnp.dot(a, b, preferred_element_type=F32)


def _unit_rms(x):
    return x * lax.rsqrt(jnp.mean(x * x, axis=-1, keepdims=True) + EPS)


def _sigmoid(v):
    return 0.5 * jnp.tanh(0.5 * v) + 0.5


def _affine(v, scale, shift=None):
    r = scale.shape[0]
    if r != 1 and r != v.shape[0]:
        v3 = v.reshape(v.shape[0] // r, r, v.shape[-1])
        shift3 = None if shift is None else shift[None]
        return _affine(v3, scale[None], shift3).reshape(v.shape)
    out = v * scale
    return out if shift is None else out + shift


def _per_sequence(v, scale, shift=None):
    n = scale.shape[0]
    v3 = v.reshape(n, v.shape[0] // n, v.shape[-1])
    out = v3 * scale[:, None, :]
    if shift is not None:
        out = out + shift[:, None, :]
    return out.reshape(v.shape)


def _tap(v, w8):
    return (v.reshape(v.shape[0] // SUBLANES, SUBLANES, v.shape[-1]) * w8[None]).reshape(v.shape)


def _route(logits_t):
    t = logits_t.shape[1]
    neg = -jnp.inf
    row = lax.broadcasted_iota(jnp.int32, (SUBLANES, t), 0).astype(F32)
    live = row < float(EXPERTS_PER_GROUP)
    gl = jnp.where(live, logits_t[N_EXPERTS:N_EXPERTS + SUBLANES, :], neg)
    gmax = jnp.max(gl, axis=0, keepdims=True)
    g_idx = jnp.min(jnp.where(gl == gmax, row, float(SUBLANES)), axis=0, keepdims=True)
    p_g = 1.0 / jnp.sum(jnp.exp(gl - gmax), axis=0, keepdims=True)
    el = None
    for g in reversed(range(N_GROUPS)):
        slab = logits_t[(g // 2) * SUBLANES:(g // 2 + 1) * SUBLANES, :]
        cand = slab if g % 2 == 0 else pltpu.roll(slab, SUBLANES - EXPERTS_PER_GROUP, axis=0)
        el = cand if el is None else jnp.where(g_idx == float(g), cand, el)
    el = jnp.where(live, el, neg)
    v1 = jnp.max(el, axis=0, keepdims=True)
    i1 = jnp.min(jnp.where(el == v1, row, float(SUBLANES)), axis=0, keepdims=True)
    el2 = jnp.where(row == i1, neg, el)
    v2 = jnp.max(el2, axis=0, keepdims=True)
    i2 = jnp.min(jnp.where(el2 == v2, row, float(SUBLANES)), axis=0, keepdims=True)
    e21 = jnp.exp(v2 - v1)
    wt1 = p_g / (1.0 + e21)
    wt2 = p_g * e21 / (1.0 + e21)
    comb = jnp.where(row == i1, wt1, jnp.where(row == i2, wt2, 0.0))
    onehot = jnp.where(row == g_idx, 1.0, 0.0)
    onehot_b = jnp.concatenate([onehot, jnp.zeros_like(onehot)], axis=0).astype(BF16)
    earlier = lax.broadcasted_iota(jnp.int32, (t, t), 0) < lax.broadcasted_iota(jnp.int32, (t, t), 1)
    before = _dot(onehot_b, jnp.where(earlier, 1.0, 0.0).astype(BF16))[0:SUBLANES, :]
    rank = jnp.sum(before * onehot, axis=0, keepdims=True)
    count = jnp.sum(onehot, axis=1, keepdims=True)
    row1 = lax.broadcasted_iota(jnp.int32, (SUBLANES, 1), 0)
    start = jnp.zeros((SUBLANES, 1), F32)
    for g in range(N_GROUPS - 1):
        start = start + jnp.where(row1 > g, count[g:g + 1, :], 0.0)
    dest = jnp.sum(onehot * start, axis=0, keepdims=True) + rank
    route_t = jnp.concatenate([jnp.where(live, comb, dest), jnp.broadcast_to(dest, (LANES - SUBLANES, t))],
                              axis=0)
    onehot_all = jnp.concatenate([onehot, jnp.zeros((LANES - SUBLANES, t), F32)], axis=0).astype(BF16)
    count_row = lax.dot_general(jnp.ones((2 * SUBLANES, t), BF16), onehot_all, (((1,), (1,)), ((), ())),
                                preferred_element_type=F32)[0:1, :]
    return jnp.transpose(route_t), count_row


def _ln_silu(v, bias, g, b):
    v = v + bias
    mu = jnp.mean(v, axis=-1, keepdims=True)
    d = v - mu
    var = jnp.mean(d * d, axis=-1, keepdims=True)
    y = d * lax.rsqrt(var + EPS) * g + b
    return y * _sigmoid(y)


def _mixer_tail(x, hb, mod, b_a_conv, yb_in, g2, w_in_ref, woa_ref, wob_ref, wo_ref, wr_ref, br_ref,
                x1_ref, h2_ref, route_ref, cnt_ref, affine):
    y_a = _dot(b_a_conv.astype(BF16), woa_ref[...])
    y_b = _dot(yb_in, wob_ref[...])
    o = 3 * D_A + 2 * D_B
    mg_a = _dot(hb, w_in_ref[:, o:o + D_MODEL])
    mg_b = _dot(hb, w_in_ref[:, o + D_MODEL:o + 2 * D_MODEL])
    merged = _sigmoid(mg_a) * y_a + _sigmoid(mg_b) * y_b
    gt1 = mod[:, 2 * D_MODEL:3 * D_MODEL]
    x1 = x + affine(_dot(merged.astype(BF16), wo_ref[...]), gt1)
    sh2 = mod[:, 3 * D_MODEL:4 * D_MODEL]
    sc2 = mod[:, 4 * D_MODEL:5 * D_MODEL]
    h2 = affine(_unit_rms(x1), g2 * (1.0 + sc2), sh2).astype(BF16)
    logits_t = lax.dot_general(wr_ref[...], h2, (((1,), (1,)), ((), ())), preferred_element_type=F32)
    logits_t = logits_t + jnp.tile(br_ref[...], (1, h2.shape[0] // LANES))
    x1_ref[...] = x1.reshape(x1_ref.shape)
    h2_ref[...] = h2.reshape(h2_ref.shape)
    route, count = _route(logits_t)
    route_ref[...] = route.reshape(route_ref.shape)
    cnt_ref[...] = count.astype(jnp.int32)


def _mixer_prompt_kernel(x_ref, mod_ref, g1_ref, g2_ref, w_in_ref, caw_ref, woa_ref, cbw_ref, cbb_ref,
                         lng_ref, lnb_ref, wob_ref, wo_ref, wr_ref, br_ref,
                         w1f_ref, w3f_ref, w2f_ref,
                         x1_ref, h2_ref, route_ref, cnt_ref, na_ref, nb_ref, w1b_ref, w3b_ref, w2b_ref,
                         hist_a, hist_b, conv_b_ref):
    n_seq, steps, d = x_ref.shape
    t = n_seq * steps
    rows_a = (CONV_A_WIDTH - 1) * n_seq
    rows_b = (CONV_B_WIDTH - 1) * n_seq

    @pl.when(pl.program_id(0) == 0)
    def _():
        hist_a[0:rows_a, :] = jnp.zeros((rows_a, D_A), F32)
        hist_b[0:rows_b, :] = jnp.zeros((rows_b, D_B), F32)

    def time_major(v):
        return jnp.swapaxes(v.reshape(n_seq, steps, v.shape[-1]), 0, 1).reshape(v.shape)

    def sequence_major(v):
        return jnp.swapaxes(v.reshape(steps, n_seq, v.shape[-1]), 0, 1).reshape(v.shape)

    x = x_ref[...].reshape(t, d)
    mod = mod_ref[...]
    h = _per_sequence(_unit_rms(x), g1_ref[...] * (1.0 + mod[:, D_MODEL:2 * D_MODEL]), mod[:, 0:D_MODEL])
    hb = h.astype(BF16)

    def proj(lo, width):
        return _dot(hb, w_in_ref[:, lo:lo + width])

    hist_a[rows_a:rows_a + t, :] = time_major(proj(D_A, D_A) * proj(2 * D_A, D_A))
    conv_a = None
    for k in range(CONV_A_WIDTH):
        term = _tap(hist_a[k * n_seq:k * n_seq + t, :], caw_ref[k])
        conv_a = term if conv_a is None else conv_a + term
    b_a_conv = proj(0, D_A) * sequence_major(conv_a)
    na_ref[...] = hist_a[t:t + rows_a, :].reshape(na_ref.shape)

    hist_b[rows_b:rows_b + t, :] = time_major(proj(3 * D_A, D_B) * _sigmoid(proj(3 * D_A + D_B, D_B)))
    nb_ref[...] = hist_b[t:t + rows_b, :].reshape(nb_ref.shape)
    for r0 in range(0, t, CONV_ROWS):
        for c0 in range(0, D_B, CONV_LANES):
            cols = slice(c0, c0 + CONV_LANES)
            acc = None
            for k in range(CONV_B_WIDTH):
                o = r0 + k * n_seq
                term = _tap(hist_b[o:o + CONV_ROWS, cols], cbw_ref[k][:, cols])
                acc = term if acc is None else acc + term
            conv_b_ref[r0:r0 + CONV_ROWS, cols] = acc
    yb_in = _ln_silu(sequence_major(conv_b_ref[...]), cbb_ref[...], lng_ref[...], lnb_ref[...]).astype(BF16)

    hist_a[0:rows_a, :] = hist_a[t:t + rows_a, :]
    hist_b[0:rows_b, :] = hist_b[t:t + rows_b, :]

    _mixer_tail(x, hb, mod, b_a_conv, yb_in, g2_ref[...], w_in_ref, woa_ref, wob_ref, wo_ref,
                wr_ref, br_ref, x1_ref, h2_ref, route_ref, cnt_ref, _per_sequence)

    w1b_ref[...] = w1f_ref[...].astype(BF16)
    w3b_ref[...] = w3f_ref[...].astype(BF16)
    w2b_ref[...] = w2f_ref[...].astype(BF16)


def _mixer_sample_kernel(x_ref, mod_ref, g1_ref, g2_ref, w_in_ref, caw_ref, woa_ref, cbw_ref, cbb_ref,
                         lng_ref, lnb_ref, wob_ref, wo_ref, wr_ref, br_ref, sa_ref, sb_hbm,
                         x1_ref, h2_ref, route_ref, cnt_ref, na_ref, nb_hbm, yb_in_ref, sb_ref, ub_ref, sem):
    n_seq = mod_ref.shape[0]
    steps = x_ref.shape[0] // n_seq
    hist = CONV_B_WIDTH - 1
    load_state = pltpu.make_async_copy(sb_hbm, sb_ref, sem.at[0])
    carry_state = pltpu.make_async_copy(sb_ref.at[pl.ds(steps, hist - steps)],
                                        nb_hbm.at[pl.ds(0, hist - steps)], sem.at[1])
    append_state = pltpu.make_async_copy(ub_ref, nb_hbm.at[pl.ds(hist - steps, steps)], sem.at[2])
    load_state.start()
    x = x_ref[...]
    mod = mod_ref[...]
    h = _affine(_unit_rms(x), g1_ref[...] * (1.0 + mod[:, D_MODEL:2 * D_MODEL]), mod[:, 0:D_MODEL])
    hb = h.astype(BF16)

    def proj(lo, width):
        return _dot(hb, w_in_ref[:, lo:lo + width])

    def slab(v, j):
        return v[j * n_seq:(j + 1) * n_seq, :]

    u_a = proj(D_A, D_A) * proj(2 * D_A, D_A)
    full_a = [sa_ref[j] for j in range(CONV_A_WIDTH - 1)] + [slab(u_a, j) for j in range(steps)]
    conv_a = []
    for s in range(steps):
        acc = jnp.zeros((n_seq, D_A), F32)
        for k in range(CONV_A_WIDTH):
            acc = acc + _tap(full_a[s + k], caw_ref[k])
        conv_a.append(acc)
    b_a_conv = proj(0, D_A) * jnp.concatenate(conv_a, axis=0)
    for j in range(CONV_A_WIDTH - 1):
        na_ref[j] = full_a[steps + j]

    u_b = proj(3 * D_A, D_B) * _sigmoid(proj(3 * D_A + D_B, D_B))
    ub_ref[...] = u_b.reshape(ub_ref.shape)
    append_state.start()
    load_state.wait()
    carry_state.start()

    def full_b(j):
        return sb_ref[j] if j < hist else slab(u_b, j - hist)

    bias, lng, lnb = cbb_ref[...], lng_ref[...], lnb_ref[...]
    for s in range(steps):
        acc = jnp.zeros((n_seq, D_B), F32)
        for k in range(CONV_B_WIDTH):
            acc = acc + _tap(full_b(s + k), cbw_ref[k])
        yb_in_ref[s * n_seq:(s + 1) * n_seq, :] = _ln_silu(acc, bias, lng, lnb).astype(BF16)

    _mixer_tail(x, hb, mod, b_a_conv, yb_in_ref[...], g2_ref[...], w_in_ref, woa_ref, wob_ref, wo_ref,
                wr_ref, br_ref, x1_ref, h2_ref, route_ref, cnt_ref, _affine)
    carry_state.wait()
    append_state.wait()


def _moe_kernel(cnt_ref, x1p_ref, h2p_ref, routep_ref, gt2p_ref, x1s_ref, h2s_ref, routes_ref, gt2s_ref,
                w1_hbm, w3_hbm, w2_hbm, gf_ref, yp_ref, ys_ref, h2sort_ref, cs_ref, acc_ref,
                w1_ref, w3_ref, w2_ref, sem):
    def weight_copies(g):
        experts = pl.ds(g * EXPERTS_PER_GROUP, EXPERTS_PER_GROUP)
        return (pltpu.make_async_copy(w1_hbm.at[experts], w1_ref.at[experts], sem.at[0, g]),
                pltpu.make_async_copy(w3_hbm.at[experts], w3_ref.at[experts], sem.at[1, g]),
                pltpu.make_async_copy(w2_hbm.at[g], w2_ref.at[g], sem.at[2, g]))

    first = pl.program_id(0) == 0

    @pl.when(first)
    def _():
        for g in range(N_GROUPS):
            for copy in weight_copies(g):
                copy.start()

    def await_weights(g):
        @pl.when(first)
        def _():
            for copy in weight_copies(g):
                copy.wait()

    is_sample = pl.program_id(0) == pl.num_programs(0) - 1
    prompt = jnp.logical_not(is_sample)
    pl.when(prompt)(lambda: _moe_sort(h2p_ref, routep_ref, h2sort_ref, cs_ref, acc_ref))
    pl.when(is_sample)(lambda: _moe_sort(h2s_ref, routes_ref, h2sort_ref, cs_ref, acc_ref))
    _moe_experts(cnt_ref, w1_ref, w3_ref, w2_ref, h2sort_ref, cs_ref, acc_ref, await_weights)
    pl.when(prompt)(lambda: _moe_finish(x1p_ref, routep_ref, gt2p_ref, gf_ref, acc_ref, yp_ref, _per_sequence))
    pl.when(is_sample)(lambda: _moe_finish(x1s_ref, routes_ref, gt2s_ref, gf_ref, acc_ref, ys_ref, _affine))


def _rows(ref):
    return ref[...].reshape(-1, ref.shape[-1])


def _moe_sort(h2_ref, route_ref, h2s_ref, cs_ref, acc_ref):
    tp = h2s_ref.shape[0]
    route = _rows(route_ref)
    t = route.shape[0]
    lane_i = lax.broadcasted_iota(jnp.int32, route.shape, 1)
    dest_row = jnp.transpose(route)[EXPERTS_PER_GROUP:EXPERTS_PER_GROUP + 1, :]
    slot = lax.broadcasted_iota(jnp.int32, (tp, t), 0).astype(F32)
    p_sort = jnp.where(slot == dest_row, 1.0, 0.0).astype(BF16)
    h2s_ref[...] = _dot(p_sort, _rows(h2_ref))
    comb = jnp.where(lane_i < EXPERTS_PER_GROUP, route, 0.0)
    c_hi = comb.astype(BF16).astype(F32)
    c_mid = (comb - c_hi).astype(BF16).astype(F32)
    c_lo = comb - c_hi - c_mid
    packed = (c_hi + pltpu.roll(c_mid, EXPERTS_PER_GROUP, axis=1)
              + pltpu.roll(c_lo, 2 * EXPERTS_PER_GROUP, axis=1)).astype(BF16)
    moved = _dot(p_sort, packed)
    cs_ref[...] = (moved + pltpu.roll(moved, LANES - EXPERTS_PER_GROUP, axis=1)
                   + pltpu.roll(moved, LANES - 2 * EXPERTS_PER_GROUP, axis=1))
    acc_ref[...] = jnp.zeros(acc_ref.shape, F32)


def _moe_experts(cnt_ref, w1_ref, w3_ref, w2_ref, h2s_ref, cs_ref, acc_ref, await_weights):
    tp = h2s_ref.shape[0]
    tile = pl.program_id(0)

    def expert_pass(rows, g, window, lo, end):
        off = pl.multiple_of(jnp.minimum(window, tp - rows), SORT_ALIGN)
        row_id = off + lax.broadcasted_iota(jnp.int32, (rows, 1), 0)
        valid = (row_id >= jnp.maximum(lo, window)) & (row_id < end)
        hb = h2s_ref[pl.ds(off, rows), :].astype(BF16)
        c4 = jnp.where(valid, cs_ref[pl.ds(off, rows), :], 0.0)
        experts = [g * EXPERTS_PER_GROUP + e for e in range(EXPERTS_PER_GROUP)]
        gate = jnp.concatenate([_dot(hb, w1_ref[e]) for e in experts], axis=1)
        up = jnp.concatenate([_dot(hb, w3_ref[e]) for e in experts], axis=1)
        cexp = jnp.concatenate(
            [jnp.broadcast_to(c4[:, e:e + 1], (rows, D_EXPERT)) for e in range(EXPERTS_PER_GROUP)], axis=1)
        a = gate * _sigmoid(gate) * up * cexp
        acc_ref[pl.ds(off, rows), :] += _dot(a.astype(BF16), w2_ref[g])

    def group(g, start):
        await_weights(g)
        end = start + cnt_ref[tile * N_GROUPS + g]
        base = lax.shift_right_logical(start, SORT_SHIFT) * SORT_ALIGN
        span = end - base

        def passes(rows):
            def run():
                def body(p, carry):
                    expert_pass(rows, g, base + p * rows, start, end)
                    return carry
                lax.fori_loop(0, jnp.where(end > start, lax.div(span + (rows - 1), rows), 0), body, 0)
            return run

        size_class = sum((span > rows).astype(jnp.int32) for rows in MOE_PASS_ROWS[:-1])
        lax.switch(size_class, [passes(rows) for rows in MOE_PASS_ROWS])
        return end

    lax.fori_loop(0, N_GROUPS, group, jnp.int32(0))


def _moe_finish(x1_ref, route_ref, gt2_ref, gf_ref, acc_ref, y_ref, affine):
    tp = acc_ref.shape[0]
    route = _rows(route_ref)
    t = route.shape[0]
    dest_col = route[:, EXPERTS_PER_GROUP:EXPERTS_PER_GROUP + 1]
    p_back = jnp.where(lax.broadcasted_iota(jnp.int32, (t, tp), 1).astype(F32) == dest_col,
                       1.0, 0.0).astype(BF16)
    y = acc_ref[...]
    y_hi = y.astype(BF16)
    y_lo = (y - y_hi.astype(F32)).astype(BF16)
    y_tok = _dot(p_back, y_hi) + _dot(p_back, y_lo)
    x2 = _rows(x1_ref) + affine(y_tok, gt2_ref[...])
    y_ref[...] = (_unit_rms(x2) * gf_ref[...]).reshape(y_ref.shape)


def _mod_kernel(c_ref, w_ref, b_ref, win_ref, op_ref, os_ref, winb_ref):
    winb_ref[...] = win_ref[...].astype(BF16)
    c = c_ref[...]
    a = (c * _sigmoid(c)).astype(BF16)
    mod = _dot(a, w_ref[...].astype(BF16)) + b_ref[...]
    n_p = op_ref.shape[0]
    op_ref[...] = mod[:n_p]
    os_ref[...] = mod[n_p:]


def _const_spec(shape):
    nd = len(shape)
    return pl.BlockSpec(shape, lambda *_: (0,) * nd, pipeline_mode=pl.Buffered(1))


def _modulation(c_all, n_p, w_ada, b_ada, w_in):
    n, d = c_all.shape
    n_out = w_ada.shape[1]
    steps = n_out // MOD_BLOCK
    cast_cols = w_in.shape[1] // steps
    assert n_p % SUBLANES == 0 and cast_cols * steps == w_in.shape[1] and cast_cols % LANES == 0
    cast_spec = pl.BlockSpec((w_in.shape[0], cast_cols), lambda j: (0, j))
    return pl.pallas_call(
        _mod_kernel,
        grid=(steps,),
        in_specs=[pl.BlockSpec((n, d), lambda j: (0, 0)),
                  pl.BlockSpec((d, MOD_BLOCK), lambda j: (0, j)),
                  pl.BlockSpec((1, MOD_BLOCK), lambda j: (0, j)),
                  cast_spec],
        out_specs=[pl.BlockSpec((n_p, MOD_BLOCK), lambda j: (0, j)),
                   pl.BlockSpec((n - n_p, MOD_BLOCK), lambda j: (0, j)),
                   cast_spec],
        out_shape=[jax.ShapeDtypeStruct((n_p, n_out), F32), jax.ShapeDtypeStruct((n - n_p, n_out), F32),
                   jax.ShapeDtypeStruct(w_in.shape, BF16)],
        compiler_params=pltpu.CompilerParams(dimension_semantics=("arbitrary",),
                                             vmem_limit_bytes=VMEM_LIMIT),
        name="modulation",
    )(c_all, w_ada, b_ada, w_in)


def _mixer_weight_specs(p):
    names = ("g1", "g2", "w_in", "caw", "woa", "cbw", "cbb", "lng", "lnb", "wob", "wo", "wr", "br")
    return [p[k] for k in names], [_const_spec(p[k].shape) for k in names]


def _mixer_prompt(x, mod, p, expert_weights):
    b, s, d = x.shape
    assert b == SUBLANES, "the time-major conv layout puts one sequence on each sublane"
    steps = SEQ_TILE // b
    tiles = s // steps
    weights, wspecs = _mixer_weight_specs(p)
    tile = lambda width: pl.BlockSpec((b, steps, width), lambda c: (0, c, 0))
    whole = lambda *shape: pl.BlockSpec(shape, lambda c: (0,) * len(shape))

    def cast_spec(w):
        experts, rows, cols = w.shape
        per_expert = tiles // experts
        assert per_expert * experts == tiles and rows % (per_expert * 2 * SUBLANES) == 0
        return pl.BlockSpec((1, rows // per_expert, cols), lambda c: (c // per_expert, lax.rem(c, per_expert), 0))

    cast_specs = [cast_spec(w) for w in expert_weights]
    return pl.pallas_call(
        _mixer_prompt_kernel,
        grid=(tiles,),
        in_specs=[tile(d), whole(b, N_MOD * d)] + wspecs + cast_specs,
        out_specs=[tile(d), tile(d), tile(ROUTER_LANES),
                   pl.BlockSpec((None, 1, LANES), lambda c: (c, 0, 0)),
                   whole(CONV_A_WIDTH - 1, b, D_A), whole(CONV_B_WIDTH - 1, b, D_B)] + cast_specs,
        out_shape=[jax.ShapeDtypeStruct((b, s, d), F32),
                   jax.ShapeDtypeStruct((b, s, d), BF16),
                   jax.ShapeDtypeStruct((b, s, ROUTER_LANES), F32),
                   jax.ShapeDtypeStruct((tiles, 1, LANES), jnp.int32),
                   jax.ShapeDtypeStruct((CONV_A_WIDTH - 1, b, D_A), F32),
                   jax.ShapeDtypeStruct((CONV_B_WIDTH - 1, b, D_B), F32)]
        + [jax.ShapeDtypeStruct(w.shape, BF16) for w in expert_weights],
        scratch_shapes=[pltpu.VMEM(((CONV_A_WIDTH - 1) * b + SEQ_TILE, D_A), F32),
                        pltpu.VMEM(((CONV_B_WIDTH - 1) * b + SEQ_TILE, D_B), F32),
                        pltpu.VMEM((SEQ_TILE, D_B), F32)],
        compiler_params=pltpu.CompilerParams(dimension_semantics=("arbitrary",),
                                             vmem_limit_bytes=VMEM_LIMIT),
        name="mixer_prompt",
    )(x, mod, *weights, *expert_weights)


def _mixer_sample(x_tm, mod, sa_tm, sb_tm, p):
    n, d = x_tm.shape
    weights, wspecs = _mixer_weight_specs(p)
    full = lambda a: pl.BlockSpec(a.shape, lambda i: (0,) * a.ndim)
    whole = lambda *shape: pl.BlockSpec(shape, lambda i: (0,) * len(shape))
    in_hbm = pl.BlockSpec(memory_space=pl.ANY)
    hist, n_seq, _ = sb_tm.shape
    steps = n // n_seq
    assert steps < hist, "the surviving history steps are copied state to state"
    return pl.pallas_call(
        _mixer_sample_kernel,
        grid=(1,),
        in_specs=[full(x_tm), full(mod)] + wspecs + [full(sa_tm), in_hbm],
        out_specs=[whole(n, d), whole(n, d), whole(n, ROUTER_LANES), whole(1, LANES),
                   full(sa_tm), in_hbm],
        out_shape=[jax.ShapeDtypeStruct((n, d), F32),
                   jax.ShapeDtypeStruct((n, d), BF16),
                   jax.ShapeDtypeStruct((n, ROUTER_LANES), F32),
                   jax.ShapeDtypeStruct((1, LANES), jnp.int32),
                   jax.ShapeDtypeStruct(sa_tm.shape, F32),
                   jax.ShapeDtypeStruct(sb_tm.shape, F32)],
        scratch_shapes=[pltpu.VMEM((n, D_B), BF16), pltpu.VMEM(sb_tm.shape, F32),
                        pltpu.VMEM((steps, n_seq, D_B), F32), pltpu.SemaphoreType.DMA((3,))],
        compiler_params=pltpu.CompilerParams(dimension_semantics=("arbitrary",),
                                             vmem_limit_bytes=VMEM_LIMIT),
        name="mixer_sample",
    )(x_tm, mod, *weights, sa_tm, sb_tm)


def _moe(counts, prompt, mod_p, sample, mod_s, p):
    b, s, d = prompt[0].shape
    n_s = mod_s.shape[0]
    steps = SEQ_TILE // b
    tiles = s // steps
    assert sample[0].shape == (SEQ_TILE, d) and SEQ_TILE % n_s == 0
    experts = [p["w1"], p["w3"], p["w2"]]
    in_hbm = pl.BlockSpec(memory_space=pl.ANY)
    tile = lambda width: pl.BlockSpec((b, steps, width), lambda c, cnt: (0, jnp.minimum(c, tiles - 1), 0))
    whole = lambda rows, width: pl.BlockSpec((rows, width), lambda c, cnt: (0, 0), pipeline_mode=pl.Buffered(1))
    gate = lambda rows: pl.BlockSpec((rows, d), lambda c, cnt: (0, N_MOD - 1), pipeline_mode=pl.Buffered(1))
    grid_spec = pltpu.PrefetchScalarGridSpec(
        num_scalar_prefetch=1,
        grid=(tiles + 1,),
        in_specs=[tile(d), tile(d), tile(ROUTER_LANES), gate(b),
                  whole(SEQ_TILE, d), whole(SEQ_TILE, d), whole(SEQ_TILE, ROUTER_LANES), gate(n_s)]
        + [in_hbm] * len(experts) + [_const_spec(p["gf"].shape)],
        out_specs=[tile(d), pl.BlockSpec((SEQ_TILE, d), lambda c, cnt: (0, 0))],
        scratch_shapes=[pltpu.VMEM((SEQ_TILE, d), F32),
                        pltpu.VMEM((SEQ_TILE, ROUTER_LANES), F32),
                        pltpu.VMEM((SEQ_TILE, d), F32)]
        + [pltpu.VMEM(w.shape, w.dtype) for w in experts]
        + [pltpu.SemaphoreType.DMA((len(experts), N_GROUPS))],
    )
    return pl.pallas_call(
        _moe_kernel,
        grid_spec=grid_spec,
        out_shape=[jax.ShapeDtypeStruct((b, s, d), F32), jax.ShapeDtypeStruct((SEQ_TILE, d), F32)],
        compiler_params=pltpu.CompilerParams(dimension_semantics=("arbitrary",),
                                             vmem_limit_bytes=VMEM_LIMIT),
        name="moe",
    )(counts, *prompt, mod_p, *sample, mod_s, *experts, p["gf"])


def kernel(x_prompt, x_sample, c_prompt, c_sample, state_conv_a, state_conv_b, w_ada, b_ada, norm1_g,
           norm2_g, w_in, conv_a_w, w_out_a, conv_b_w, conv_b_bias, ln_b_g, ln_b_b, w_out_b, w_o,
           w_group, b_group, w_expert, b_expert, w1, w3, w2, final_norm_g):
    depth = w_ada.shape[0]
    assert depth == 1, "the MoE kernel fuses the final norm, so exactly one layer is supported"
    n_p, _, d = x_prompt.shape
    n_s, steps, _ = x_sample.shape
    c_all = jnp.concatenate([c_prompt, c_sample], axis=0)
    xp = x_prompt
    xs = x_sample.transpose(1, 0, 2).reshape(steps * n_s, d)
    gf = final_norm_g.reshape(1, d)
    pa, pb, sa, sb = [], [], [], []
    for l in range(depth):
        pad = ROUTER_LANES - N_EXPERTS - N_GROUPS
        wr = jnp.pad(jnp.concatenate([w_expert[l], w_group[l]], axis=1).T, ((0, pad), (0, 0)))
        br = jnp.pad(jnp.concatenate([b_expert[l], b_group[l]]), (0, pad))
        br = jnp.broadcast_to(br[:, None], (ROUTER_LANES, LANES))
        p = {
            "g1": norm1_g[l].reshape(1, d), "g2": norm2_g[l].reshape(1, d),
            "woa": w_out_a[l].astype(BF16),
            "caw": jnp.broadcast_to(conv_a_w[l][:, None, :], (CONV_A_WIDTH, SUBLANES, D_A)),
            "cbw": jnp.broadcast_to(conv_b_w[l][:, None, :], (CONV_B_WIDTH, SUBLANES, D_B)),
            "cbb": conv_b_bias[l].reshape(1, D_B),
            "lng": ln_b_g[l].reshape(1, D_B), "lnb": ln_b_b[l].reshape(1, D_B),
            "wob": w_out_b[l].astype(BF16), "wo": w_o[l].astype(BF16), "wr": wr.astype(BF16), "br": br,
            "gf": gf,
        }
        mod_p, mod_s, p["w_in"] = _modulation(c_all, n_p, w_ada[l], b_ada[l].reshape(1, N_MOD * d), w_in[l])

        x1p, h2p, route_p, cnt_p, na_p, nb_p, w1b, w3b, w2b = _mixer_prompt(
            xp, mod_p, p, (w1[l], w3[l], w2[l]))
        p.update(w1=w1b, w3=w3b, w2=w2b.reshape(N_GROUPS, EXPERTS_PER_GROUP * D_EXPERT, d))

        sa_tm = state_conv_a[l].transpose(1, 0, 2)
        sb_tm = state_conv_b[l].transpose(1, 0, 2)
        x1s, h2s, route_s, cnt_s, na_s, nb_s = _mixer_sample(xs, mod_s, sa_tm, sb_tm, p)
        counts = jnp.concatenate([cnt_p[:, 0, :N_GROUPS].reshape(-1), cnt_s[0, :N_GROUPS]])
        xp, xs = _moe(counts, (x1p, h2p, route_p), mod_p, (x1s, h2s, route_s), mod_s, p)
        pa.append(na_p.transpose(1, 0, 2))
        pb.append(nb_p.transpose(1, 0, 2))
        sa.append(na_s.transpose(1, 0, 2))
        sb.append(nb_s.transpose(1, 0, 2))
    y_sample = xs.reshape(steps, n_s, d).transpose(1, 0, 2)
    return (xp, y_sample, jnp.stack(pa), jnp.stack(pb), jnp.stack(sa), jnp.stack(sb))
```
